```python
import math
import jax, jax.numpy as jnp
from jax import lax
import numpy as np

D_MODEL = 1024
BATCH = 1
SEQ = 16384
DEPTH = 2
DEC_BATCH = 8
DEC_SEQ = 16
PAST_LEN = 2048

CHUNK = 64
N_META = 16
Q_BLOCK = 128
LN_EPS = 1e-5
RMS_EPS = 1e-6
ALPHA = (2 * DEPTH) ** 0.25
BETA = (8 * DEPTH) ** -0.25
N_EVEN = (DEPTH + 1) // 2
N_ODD = DEPTH // 2
MLA_HEADS = 8
D_NOPE = 64
D_ROPE = 32
D_V = 64
D_C = 256
D_CQ = 384
ROPE_BASE = 10000.0
MLA_SCALE = (D_NOPE + D_ROPE) ** -0.5
D_RNN = 512
LRU_BLOCKS = 8
LRU_BLOCK_W = D_RNN // LRU_BLOCKS
CONV_W = 4
LRU_C = 8.0
OFF_CKV = D_CQ
OFF_KR = OFF_CKV + D_C
OFF_XR = OFF_KR + D_ROPE
OFF_Y = OFF_XR + D_RNN
D_IN_EVEN = OFF_Y + D_RNN
D_MIX_EVEN = MLA_HEADS * D_V + D_RNN
RWKV_HEAD = 64
RWKV_HEADS = D_MODEL // RWKV_HEAD
D_DECAY_LORA = 64
D_AAA_LORA = 64
D_GATE_LORA = 128
DECAY_SCALE = math.exp(-0.5)
GN_EPS = 64e-5
N_EXPERTS = 16
N_GROUPS = 4
EXPERTS_PER_GROUP = N_EXPERTS // N_GROUPS
TOP_K = 2
D_EXPERT = 512

kernel_name = 'hybrid_mla_rglru_rwkv7_moe_stream_step'


def _layer_norm(x, g, b):
    xf = x.astype(jnp.float32)
    mu = jnp.mean(xf, axis=-1, keepdims=True)
    var = jnp.mean(jnp.square(xf - mu), axis=-1, keepdims=True)
    return ((xf - mu) * lax.rsqrt(var + LN_EPS) * g + b).astype(x.dtype)


def _rms_norm(x, g):
    xf = x.astype(jnp.float32)
    return (xf * lax.rsqrt(jnp.mean(xf * xf, axis=-1, keepdims=True) + RMS_EPS) * g).astype(x.dtype)


def _rope(x, pos):
    half = x.shape[-1] // 2
    freq = ROPE_BASE ** (-jnp.arange(half, dtype=jnp.float32) / half)
    ang = pos.astype(jnp.float32)[:, None] * freq[None, :]
    cos = jnp.cos(ang)[None, :, None, :]
    sin = jnp.sin(ang)[None, :, None, :]
    x1 = x[..., :half].astype(jnp.float32)
    x2 = x[..., half:].astype(jnp.float32)
    return jnp.concatenate([x1 * cos - x2 * sin, x1 * sin + x2 * cos], axis=-1).astype(x.dtype)


def _chunk_ids(n):
    i = jnp.arange(n)
    return jnp.where(i < N_META, -1, (i - N_META) // CHUNK)


def _mla_latent(ckv_raw, kr_raw, pos, g_kv):
    ckv = _rms_norm(ckv_raw, g_kv)
    krope = _rope(kr_raw[:, :, None, :], pos)[:, :, 0]
    return ckv, krope


def _mla_keys(ckv, ev):
    k_nope = jnp.einsum('bkc,chd->bkhd', ckv, ev['w_uk'])
    v = jnp.einsum('bkc,chd->bkhd', ckv, ev['w_uv'])
    return k_nope, v


def _mla_attend(q_nope, q_rope, k_nope, krope, v, mask):
    s = (jnp.einsum('bqhd,bkhd->bhqk', q_nope, k_nope)
         + jnp.einsum('bqhd,bkd->bhqk', q_rope, krope)).astype(jnp.float32) * MLA_SCALE
    if mask is not None:
        s = jnp.where(mask, s, -jnp.inf)
    p = jax.nn.softmax(s, axis=-1).astype(v.dtype)
    return jnp.einsum('bhqk,bkhd->bqhd', p, v)


def _mla_attention_blocked(q_nope, q_rope, ckv, krope, ev):
    B, T = q_nope.shape[0], q_nope.shape[1]
    k_nope, v = _mla_keys(ckv, ev)
    nb = -(-T // Q_BLOCK)
    pad = nb * Q_BLOCK - T

    def to_blocks(t):
        t = jnp.pad(t, ((0, 0), (0, pad), (0, 0), (0, 0)))
        return t.reshape((B, nb, Q_BLOCK) + t.shape[2:]).swapaxes(0, 1)

    q_cid = _chunk_ids(nb * Q_BLOCK).reshape(nb, Q_BLOCK)
    k_cid = _chunk_ids(T)

    def one_block(args):
        qn, qr, cid = args
        return _mla_attend(qn, qr, k_nope, krope, v, cid[:, None] >= k_cid[None, :])

    out = lax.map(one_block, (to_blocks(q_nope), to_blocks(q_rope), q_cid))
    return out.swapaxes(0, 1).reshape(B, nb * Q_BLOCK, MLA_HEADS, D_V)[:, :T]


def _linear_combine(left, right):
    a_l, b_l = left
    a_r, b_r = right
    return a_r * a_l, a_r * b_l + b_r


def _rglru_branch(xr, yg, conv_buf, h0, ev):
    B, L, _ = xr.shape
    xpad = jnp.concatenate([conv_buf.astype(xr.dtype), xr], axis=1)
    xc = ev['conv_b'] + sum(ev['conv_w'][k] * xpad[:, k:k + L] for k in range(CONV_W))
    new_buf = xpad[:, L:]
    xcb = xc.reshape(B, L, LRU_BLOCKS, LRU_BLOCK_W)
    r = jax.nn.sigmoid(jnp.einsum('blnc,ncd->blnd', xcb, ev['w_rg'])
                       + ev['b_rg'].reshape(LRU_BLOCKS, LRU_BLOCK_W)).reshape(B, L, D_RNN)
    i = jax.nn.sigmoid(jnp.einsum('blnc,ncd->blnd', xcb, ev['w_ig'])
                       + ev['b_ig'].reshape(LRU_BLOCKS, LRU_BLOCK_W)).reshape(B, L, D_RNN)
    log_a = -LRU_C * r.astype(jnp.float32) * jax.nn.softplus(-ev['lam'].astype(jnp.float32))
    a = jnp.exp(log_a)
    b = jnp.sqrt(-jnp.expm1(2.0 * log_a)) * (i * xc).astype(jnp.float32)
    b = b.at[:, 0].add(a[:, 0] * h0.astype(jnp.float32))
    _, h = lax.associative_scan(_linear_combine, (a, b), axis=1)
    out = h.astype(xr.dtype) * jax.nn.gelu(yg)
    return out, new_buf, h[:, -1].astype(h0.dtype)


def _even_mixer(x, pos, prefix, conv_buf, h0, ev):
    B, L, _ = x.shape
    u = x @ ev['w_in']
    cq = _rms_norm(u[..., :OFF_CKV], ev['g_q'])
    q = jnp.einsum('blc,chd->blhd', cq, ev['w_uq'])
    q_nope = q[..., :D_NOPE]
    q_rope = _rope(q[..., D_NOPE:], pos)
    ckv, krope = _mla_latent(u[..., OFF_CKV:OFF_KR], u[..., OFF_KR:OFF_XR], pos, ev['g_kv'])
    if prefix is None:
        attn = _mla_attention_blocked(q_nope, q_rope, ckv, krope, ev)
    else:
        all_ckv = jnp.concatenate([prefix[0], ckv], axis=1)
        all_kr = jnp.concatenate([prefix[1], krope], axis=1)
        k_nope, v = _mla_keys(all_ckv, ev)
        attn = _mla_attend(q_nope, q_rope, k_nope, all_kr, v, None)
    rnn, new_buf, h_last = _rglru_branch(u[..., OFF_XR:OFF_Y], u[..., OFF_Y:], conv_buf, h0, ev)
    mixed = jnp.concatenate([attn.reshape(B, L, MLA_HEADS * D_V), rnn], axis=-1)
    return mixed @ ev['w_out'], ckv, krope, new_buf, h_last


def _wkv7_scan(r, w, k, v, kk, a, S0):
    def step(S, inp):
        r_t, w_t, k_t, v_t, kk_t, a_t = inp
        sa = jnp.einsum('bhij,bhj->bhi', S, -kk_t)
        S = (S * w_t[:, :, None, :] + sa[..., None] * (kk_t * a_t)[:, :, None, :]
             + v_t[..., None] * k_t[:, :, None, :])
        return S, jnp.einsum('bhij,bhj->bhi', S, r_t)

    xs = tuple(jnp.moveaxis(t, 1, 0) for t in (r, w, k, v, kk, a))
    S, o = lax.scan(step, S0.astype(jnp.float32), xs)
    return S, jnp.moveaxis(o, 0, 1)


def _rwkv7_mixer(x, shift0, S0, od):
    B, L, D = x.shape
    x_prev = jnp.concatenate([shift0[:, None, :].astype(x.dtype), x[:, :-1]], axis=1)
    xx = x_prev - x
    mu = od['mu']
    xr, xw, xk, xv, xa, xg = (x + xx * mu[n] for n in range(6))
    r = xr @ od['w_r']
    k = xk @ od['w_k']
    v = xv @ od['w_v']
    log_w = -DECAY_SCALE * jax.nn.sigmoid((od['w0'] + jnp.tanh(xw @ od['w1']) @ od['w2']).astype(jnp.float32))
    a = jax.nn.sigmoid(od['a0'] + (xa @ od['a1']) @ od['a2'])
    g = jax.nn.sigmoid(xg @ od['g1']) @ od['g2']

    def heads(t):
        return t.reshape(B, L, RWKV_HEADS, RWKV_HEAD).astype(jnp.float32)

    kk = heads(k * od['k_k'])
    kk = kk / jnp.maximum(jnp.linalg.norm(kk, axis=-1, keepdims=True), 1e-12)
    k = k * (1 + (a - 1) * od['k_a'])
    rh, kh, vh, ah = heads(r), heads(k), heads(v), heads(a)
    S_last, o = _wkv7_scan(rh, heads(jnp.exp(log_w)), kh, vh, kk, ah, S0)
    o_mu = jnp.mean(o, axis=-1, keepdims=True)
    o_var = jnp.mean(jnp.square(o - o_mu), axis=-1, keepdims=True)
    o = ((o - o_mu) * lax.rsqrt(o_var + GN_EPS)).reshape(B, L, D) * od['ln_g'] + od['ln_b']
    o = o + (jnp.sum(rh * kh * od['r_k'], axis=-1, keepdims=True) * vh).reshape(B, L, D)
    out = (o.astype(x.dtype) * g) @ od['w_o']
    return out, x[:, -1], S_last.astype(S0.dtype)


def _moe(x, router_w, router_b, w_gate, w_up, w_down):
    s = jax.nn.sigmoid(jnp.einsum('bld,de->ble', x, router_w).astype(jnp.float32))
    sel = s + router_b.astype(jnp.float32)
    grp = sel.reshape(sel.shape[:-1] + (N_GROUPS, EXPERTS_PER_GROUP))
    g_score = jnp.sum(lax.top_k(grp, TOP_K)[0], axis=-1)
    g_best = jnp.argmax(g_score, axis=-1)
    in_group = (jnp.arange(N_EXPERTS) // EXPERTS_PER_GROUP) == g_best[..., None]
    _, idx = lax.top_k(jnp.where(in_group, sel, -jnp.inf), TOP_K)
    w_sel = jnp.take_along_axis(s, idx, axis=-1)
    w_sel = w_sel / jnp.sum(w_sel, axis=-1, keepdims=True)
    gate = jnp.sum(jax.nn.one_hot(idx, N_EXPERTS, dtype=jnp.float32) * w_sel[..., None], axis=-2)
    h = jax.nn.silu(jnp.einsum('bld,edf->blef', x, w_gate)) * jnp.einsum('bld,edf->blef', x, w_up)
    return jnp.einsum('blef,efd->bld', h * gate[..., None].astype(h.dtype), w_down)


def _post_block(x, mix, ln_g, ln_b, router_w, router_b, w_gate, w_up, w_down):
    x = _layer_norm(ALPHA * x + mix, ln_g[0], ln_b[0])
    return _layer_norm(ALPHA * x + _moe(x, router_w, router_b, w_gate, w_up, w_down), ln_g[1], ln_b[1])


def _meta_needed_after(layer):
    return any(l % 2 == 0 for l in range(layer + 1, DEPTH))


def setup_inputs(seed: int = 0) -> dict:
    keys = iter(jax.random.split(jax.random.key(seed), 64))

    def nrm(shape, scale=1.0):
        return scale * jax.random.normal(next(keys), shape, jnp.float32)

    def gain(shape):
        return 1.0 + nrm(shape, 0.05)

    lru_a = jax.random.uniform(next(keys), (N_EVEN, D_RNN), jnp.float32, 0.9, 0.999)
    return {
        'x_prompt': nrm((BATCH, SEQ, D_MODEL)),
        'x_sample': nrm((DEC_BATCH, DEC_SEQ, D_MODEL)),
        'cache_ckv': nrm((N_EVEN, DEC_BATCH, PAST_LEN, D_C)),
        'cache_krope': nrm((N_EVEN, DEC_BATCH, PAST_LEN, D_ROPE)),
        'state_conv': nrm((N_EVEN, DEC_BATCH, CONV_W - 1, D_RNN)),
        'state_lru': nrm((N_EVEN, DEC_BATCH, D_RNN), 0.5),
        'state_shift': nrm((N_ODD, DEC_BATCH, D_MODEL)),
        'state_wkv': nrm((N_ODD, DEC_BATCH, RWKV_HEADS, RWKV_HEAD, RWKV_HEAD), 0.3),
        'meta_tokens': nrm((N_META, D_MODEL)),
        'ev_w_in': nrm((N_EVEN, D_MODEL, D_IN_EVEN), D_MODEL ** -0.5),
        'ev_g_q': gain((N_EVEN, D_CQ)),
        'ev_w_uq': nrm((N_EVEN, D_CQ, MLA_HEADS, D_NOPE + D_ROPE), D_CQ ** -0.5),
        'ev_g_kv': gain((N_EVEN, D_C)),
        'ev_w_uk': nrm((N_EVEN, D_C, MLA_HEADS, D_NOPE), D_C ** -0.5),
        'ev_w_uv': nrm((N_EVEN, D_C, MLA_HEADS, D_V), D_C ** -0.5),
        'ev_conv_w': nrm((N_EVEN, CONV_W, D_RNN), CONV_W ** -0.5),
        'ev_conv_b': nrm((N_EVEN, D_RNN), 0.02),
        'ev_w_rg': nrm((N_EVEN, LRU_BLOCKS, LRU_BLOCK_W, LRU_BLOCK_W), LRU_BLOCK_W ** -0.5),
        'ev_b_rg': nrm((N_EVEN, D_RNN), 0.02),
        'ev_w_ig': nrm((N_EVEN, LRU_BLOCKS, LRU_BLOCK_W, LRU_BLOCK_W), LRU_BLOCK_W ** -0.5),
        'ev_b_ig': nrm((N_EVEN, D_RNN), 0.02),
        'ev_lru_lambda': jnp.log(lru_a) - jnp.log1p(-lru_a),
        'ev_w_out': nrm((N_EVEN, D_MIX_EVEN, D_MODEL), BETA * D_MIX_EVEN ** -0.5),
        'od_mu': jax.random.uniform(next(keys), (N_ODD, 6, D_MODEL), jnp.float32),
        'od_w_r': nrm((N_ODD, D_MODEL, D_MODEL), D_MODEL ** -0.5),
        'od_w_k': nrm((N_ODD, D_MODEL, D_MODEL), D_MODEL ** -0.5),
        'od_w_v': nrm((N_ODD, D_MODEL, D_MODEL), D_MODEL ** -0.5),
        'od_w0': nrm((N_ODD, D_MODEL), 1.0),
        'od_w1': nrm((N_ODD, D_MODEL, D_DECAY_LORA), D_MODEL ** -0.5),
        'od_w2': nrm((N_ODD, D_DECAY_LORA, D_MODEL), 0.5 * D_DECAY_LORA ** -0.5),
        'od_a0': nrm((N_ODD, D_MODEL), 0.1),
        'od_a1': nrm((N_ODD, D_MODEL, D_AAA_LORA), D_MODEL ** -0.5),
        'od_a2': nrm((N_ODD, D_AAA_LORA, D_MODEL), 0.5 * D_AAA_LORA ** -0.5),
        'od_g1': nrm((N_ODD, D_MODEL, D_GATE_LORA), D_MODEL ** -0.5),
        'od_g2': nrm((N_ODD, D_GATE_LORA, D_MODEL), D_GATE_LORA ** -0.5),
        'od_k_k': 0.85 + nrm((N_ODD, D_MODEL), 0.02),
        'od_k_a': 1.0 + nrm((N_ODD, D_MODEL), 0.02),
        'od_r_k': nrm((N_ODD, RWKV_HEADS, RWKV_HEAD), 0.1),
        'od_ln_g': gain((N_ODD, D_MODEL)),
        'od_ln_b': nrm((N_ODD, D_MODEL), 0.02),
        'od_w_o': nrm((N_ODD, D_MODEL, D_MODEL), BETA * D_MODEL ** -0.5),
        'ln_g': gain((DEPTH, 2, D_MODEL)),
        'ln_b': nrm((DEPTH, 2, D_MODEL), 0.02),
        'router_w': nrm((D_MODEL, N_EXPERTS), D_MODEL ** -0.5),
        'router_b': nrm((N_EXPERTS,), 0.01),
        'exp_w_gate': nrm((DEPTH, N_EXPERTS, D_MODEL, D_EXPERT), D_MODEL ** -0.5),
        'exp_w_up': nrm((DEPTH, N_EXPERTS, D_MODEL, D_EXPERT), D_MODEL ** -0.5),
        'exp_w_down': nrm((DEPTH, N_EXPERTS, D_EXPERT, D_MODEL), BETA * D_EXPERT ** -0.5),
    }


def reference(x_prompt, x_sample, cache_ckv, cache_krope, state_conv, state_lru, state_shift, state_wkv,
              meta_tokens, ev_w_in, ev_g_q, ev_w_uq, ev_g_kv, ev_w_uk, ev_w_uv, ev_conv_w, ev_conv_b,
              ev_w_rg, ev_b_rg, ev_w_ig, ev_b_ig, ev_lru_lambda, ev_w_out, od_mu, od_w_r, od_w_k, od_w_v,
              od_w0, od_w1, od_w2, od_a0, od_a1, od_a2, od_g1, od_g2, od_k_k, od_k_a, od_r_k, od_ln_g,
              od_ln_b, od_w_o, ln_g, ln_b, router_w, router_b, exp_w_gate, exp_w_up, exp_w_down):
    B_p = x_prompt.shape[0]
    B_s, L_s = x_sample.shape[0], x_sample.shape[1]
    past = cache_ckv.shape[2]
    T = N_META + x_prompt.shape[1]
    dt = x_prompt.dtype
    x_p = jnp.concatenate([jnp.broadcast_to(meta_tokens[None].astype(dt), (B_p, N_META, D_MODEL)), x_prompt], axis=1)
    x_s = x_sample
    x_m = meta_tokens[None]
    pos_p = jnp.arange(T)
    pos_s = N_META + past + jnp.arange(L_s)
    pos_m = jnp.arange(N_META)

    ckv_p, kr_p, conv_p, lru_p, shift_p, wkv_p = [], [], [], [], [], []
    ckv_s, kr_s, conv_s, lru_s, shift_s, wkv_s = [], [], [], [], [], []
    for layer in range(DEPTH):
        lp = {'ln_g': ln_g[layer], 'ln_b': ln_b[layer], 'router_w': router_w, 'router_b': router_b,
              'w_gate': exp_w_gate[layer], 'w_up': exp_w_up[layer], 'w_down': exp_w_down[layer]}
        if layer % 2 == 0:
            i = layer // 2
            ev = {'w_in': ev_w_in[i], 'g_q': ev_g_q[i], 'w_uq': ev_w_uq[i], 'g_kv': ev_g_kv[i],
                  'w_uk': ev_w_uk[i], 'w_uv': ev_w_uv[i], 'conv_w': ev_conv_w[i], 'conv_b': ev_conv_b[i],
                  'w_rg': ev_w_rg[i], 'b_rg': ev_b_rg[i], 'w_ig': ev_w_ig[i], 'b_ig': ev_b_ig[i],
                  'lam': ev_lru_lambda[i], 'w_out': ev_w_out[i]}
            um = x_m @ ev['w_in'][:, OFF_CKV:OFF_XR]
            m_ckv, m_kr = _mla_latent(um[..., :D_C], um[..., D_C:], pos_m, ev['g_kv'])
            prefix = (jnp.concatenate([jnp.broadcast_to(m_ckv, (B_s, N_META, D_C)).astype(cache_ckv.dtype), cache_ckv[i]], axis=1),
                      jnp.concatenate([jnp.broadcast_to(m_kr, (B_s, N_META, D_ROPE)).astype(cache_krope.dtype), cache_krope[i]], axis=1))
            mix_p, c_p, k_p, b_p, h_p = _even_mixer(
                x_p, pos_p, None, jnp.zeros((B_p, CONV_W - 1, D_RNN), dt), jnp.zeros((B_p, D_RNN), dt), ev)
            mix_s, c_s, k_s, b_s, h_s = _even_mixer(x_s, pos_s, prefix, state_conv[i], state_lru[i], ev)
            if _meta_needed_after(layer):
                mix_m = _even_mixer(x_m, pos_m, None, jnp.zeros((1, CONV_W - 1, D_RNN), x_m.dtype),
                                    jnp.zeros((1, D_RNN), x_m.dtype), ev)[0]
                x_m = _post_block(x_m, mix_m, **lp)
            ckv_p.append(c_p); kr_p.append(k_p); conv_p.append(b_p); lru_p.append(h_p)
            ckv_s.append(c_s); kr_s.append(k_s); conv_s.append(b_s); lru_s.append(h_s)
        else:
            j = layer // 2
            od = {'mu': od_mu[j], 'w_r': od_w_r[j], 'w_k': od_w_k[j], 'w_v': od_w_v[j], 'w0': od_w0[j],
                  'w1': od_w1[j], 'w2': od_w2[j], 'a0': od_a0[j], 'a1': od_a1[j], 'a2': od_a2[j],
                  'g1': od_g1[j], 'g2': od_g2[j], 'k_k': od_k_k[j], 'k_a': od_k_a[j], 'r_k': od_r_k[j],
                  'ln_g': od_ln_g[j], 'ln_b': od_ln_b[j], 'w_o': od_w_o[j]}
            mix_p, sh_p, S_p = _rwkv7_mixer(
                x_p, jnp.zeros((B_p, D_MODEL), dt), jnp.zeros((B_p, RWKV_HEADS, RWKV_HEAD, RWKV_HEAD), dt), od)
            mix_s, sh_s, S_s = _rwkv7_mixer(x_s, state_shift[j], state_wkv[j], od)
            if _meta_needed_after(layer):
                mix_m = _rwkv7_mixer(x_m, jnp.zeros((1, D_MODEL), x_m.dtype),
                                     jnp.zeros((1, RWKV_HEADS, RWKV_HEAD, RWKV_HEAD), x_m.dtype), od)[0]
                x_m = _post_block(x_m, mix_m, **lp)
            shift_p.append(sh_p); wkv_p.append(S_p)
            shift_s.append(sh_s); wkv_s.append(S_s)
        x_p = _post_block(x_p, mix_p, **lp)
        x_s = _post_block(x_s, mix_s, **lp)

    y_prompt = x_p[:, N_META:]
    y_sample = x_s
    return (y_prompt, y_sample,
            jnp.stack(ckv_p), jnp.stack(kr_p), jnp.stack(conv_p), jnp.stack(lru_p),
            jnp.stack(shift_p), jnp.stack(wkv_p),
            jnp.stack(ckv_s), jnp.stack(kr_s), jnp.stack(conv_s), jnp.stack(lru_s),
            jnp.stack(shift_s), jnp.stack(wkv_s))
```

```python
import functools

import numpy as np
import jax
import jax.numpy as jnp
from jax import lax
from jax.experimental import pallas as pl
from jax.experimental.pallas import tpu as pltpu

F32 = jnp.float32
BF16 = jnp.bfloat16

D_MODEL = 1024
N_META = 16
CHUNK = 64
CHUNK_SHIFT = 6
LN_EPS = 1e-5
RMS_EPS = 1e-6
DEPTH = 2
ALPHA = (2 * DEPTH) ** 0.25
MLA_HEADS = 8
D_NOPE = 64
D_ROPE = 32
D_V = 64
D_C = 256
D_CQ = 384
ROPE_BASE = 10000.0
MLA_SCALE = (D_NOPE + D_ROPE) ** -0.5
D_RNN = 512
LRU_BLOCKS = 8
LRU_BLOCK_W = D_RNN // LRU_BLOCKS
CONV_W = 4
LRU_C = 8.0
RWKV_HEAD = 64
RWKV_HEADS = D_MODEL // RWKV_HEAD
DECAY_SCALE = float(np.exp(-0.5))
GN_EPS = 64e-5
N_EXPERTS = 16
N_GROUPS = 4
EXPERTS_PER_GROUP = N_EXPERTS // N_GROUPS
D_EXPERT = 512

LANES = 128
SUBLANES = 8
HEAD_SLOT = LANES
D_ATT = MLA_HEADS * HEAD_SLOT
PAD_FRONT = CHUNK - N_META
ROW0 = PAD_FRONT + N_META
NEG = -1e30
VMEM_LIMIT = 56 * 1024 * 1024

C_CQ = 0
C_CKV = D_CQ
C_XR = C_CKV + D_C
C_YG = C_XR + D_RNN
C_KR = C_YG + D_RNN
N_COL = C_KR + LANES


def _cparams(sem):
    return pltpu.CompilerParams(dimension_semantics=sem, vmem_limit_bytes=VMEM_LIMIT)


def _row_tile(n, cap):
    for t in (1024, 512, 256, 128, 64, 32, 16, 8):
        if t <= cap and n % t == 0:
            return t
    return n


def _full(shape):
    zeros = (0,) * len(shape)
    return pl.BlockSpec(shape, lambda *_: zeros)


def _ln(x, g, b):
    mu = jnp.mean(x, axis=-1, keepdims=True)
    xc = x - mu
    var = jnp.mean(xc * xc, axis=-1, keepdims=True)
    return xc * lax.rsqrt(var + LN_EPS) * g + b


def _bdot(a, b):
    return jnp.dot(a.astype(BF16), b.astype(BF16), preferred_element_type=F32)


def _split2(x):
    hi = x.astype(BF16)
    return hi, (x - hi.astype(F32)).astype(BF16)


def _split3(x):
    hi = x.astype(BF16)
    r1 = x - hi.astype(F32)
    mid = r1.astype(BF16)
    return hi, mid, (r1 - mid.astype(F32)).astype(BF16)


_NN = ((1,), (0,))
_NT = ((1,), (1,))
_TN = ((0,), (0,))


def _mm(a, b, dims, exact):
    dn = (dims, ((), ()))
    if not exact:
        return lax.dot_general(a.astype(BF16), b.astype(BF16), dn, preferred_element_type=F32)
    ah, al = _split2(a)
    bh, bl = _split2(b)
    return (lax.dot_general(ah, bh, dn, preferred_element_type=F32)
            + lax.dot_general(al, bh, dn, preferred_element_type=F32)
            + lax.dot_general(ah, bl, dn, preferred_element_type=F32))


def _wdot(a, w):
    return _mm(a, w, _NN, exact=(w.dtype == F32))


def _act_dtype(w):
    return F32 if w.dtype == F32 else BF16


def _rope_slot(x, c, sa, sb):
    return x * c + pltpu.roll(x, LANES - D_ROPE // 2, 1) * sa + pltpu.roll(x, D_ROPE // 2, 1) * sb


def _even_proj_kernel(x_ref, w1_ref, gq_ref, wuq_ref, gkv_ref, c_ref, sa_ref, sb_ref,
                      q_ref, ckv_ref, kr_ref, xr_ref, yg_ref):
    u = _wdot(x_ref[...], w1_ref[...])
    cq = u[:, C_CQ:C_CQ + D_CQ]
    cq = cq * lax.rsqrt(jnp.mean(cq * cq, axis=-1, keepdims=True) + RMS_EPS) * gq_ref[...]
    q = _wdot(cq, wuq_ref[...])
    c, sa, sb = c_ref[...], sa_ref[...], sb_ref[...]
    for h in range(MLA_HEADS):
        sl = slice(h * HEAD_SLOT, (h + 1) * HEAD_SLOT)
        q_ref[:, sl] = (_rope_slot(q[:, sl], c, sa, sb) * MLA_SCALE).astype(q_ref.dtype)
    ckv = u[:, C_CKV:C_CKV + D_C]
    ckv_ref[...] = ckv * lax.rsqrt(jnp.mean(ckv * ckv, axis=-1, keepdims=True) + RMS_EPS) * gkv_ref[...]
    kr_ref[...] = _rope_slot(u[:, C_KR:C_KR + LANES], c, sa, sb)
    xr_ref[...] = u[:, C_XR:C_XR + D_RNN]
    yg_ref[...] = u[:, C_YG:C_YG + D_RNN]


def _even_proj(x, w1, gq, wuq, gkv, tabs):
    n = x.shape[0]
    tm = _row_tile(n, 512)
    row = lambda w: pl.BlockSpec((tm, w), lambda i: (i, 0))
    c, sa, sb = tabs
    return pl.pallas_call(
        _even_proj_kernel,
        grid=(n // tm,),
        in_specs=[row(D_MODEL), _full(w1.shape), _full(gq.shape), _full(wuq.shape), _full(gkv.shape),
                  row(LANES), row(LANES), row(LANES)],
        out_specs=[row(D_ATT), row(D_C), row(LANES), row(D_RNN), row(D_RNN)],
        out_shape=[jax.ShapeDtypeStruct((n, D_ATT), _act_dtype(w1)), jax.ShapeDtypeStruct((n, D_C), F32),
                   jax.ShapeDtypeStruct((n, LANES), F32), jax.ShapeDtypeStruct((n, D_RNN), F32),
                   jax.ShapeDtypeStruct((n, D_RNN), F32)],
        compiler_params=_cparams(("arbitrary",)),
        name="even_proj",
    )(x, w1, gq, wuq, gkv, c, sa, sb)


def _kv_proj_kernel(ckv_ref, kr_ref, wukv_ref, p_ref, k_ref, v_ref):
    kv = _wdot(ckv_ref[...], wukv_ref[...])
    k_ref[...] = (kv[:, :D_ATT] + _wdot(kr_ref[...], p_ref[...])).astype(k_ref.dtype)
    v_ref[...] = kv[:, D_ATT:].astype(v_ref.dtype)


def _kv_proj(ckv, kr, wukv, place):
    n = ckv.shape[0]
    tm = _row_tile(n, 512)
    row = lambda w: pl.BlockSpec((tm, w), lambda i: (i, 0))
    return pl.pallas_call(
        _kv_proj_kernel,
        grid=(n // tm,),
        in_specs=[row(D_C), row(LANES), _full(wukv.shape), _full(place.shape)],
        out_specs=[row(D_ATT), row(D_ATT)],
        out_shape=[jax.ShapeDtypeStruct((n, D_ATT), _act_dtype(wukv))] * 2,
        compiler_params=_cparams(("arbitrary",)),
        name="kv_proj",
    )(ckv, kr, wukv, place)


def _flash_kernel(qi_ref, kj_ref, q_ref, k_ref, v_ref, o_ref, m_scr, l_scr, acc_scr, *, tq, tk):
    step = pl.program_id(0)
    i = qi_ref[step]
    j = kj_ref[step]

    @pl.when(j == 0)
    def _():
        m_scr[...] = jnp.full(m_scr.shape, NEG, F32)
        l_scr[...] = jnp.zeros(l_scr.shape, F32)
        acc_scr[...] = jnp.zeros(acc_scr.shape, F32)

    def accumulate(masked):
        if masked:
            qrow = i * tq + lax.broadcasted_iota(jnp.int32, (tq, tk), 0)
            krow = j * tk + lax.broadcasted_iota(jnp.int32, (tq, tk), 1)
            keep = ((((qrow - ROW0) >> CHUNK_SHIFT) >= ((krow - ROW0) >> CHUNK_SHIFT))
                    & (krow >= PAD_FRONT))
        for h in range(MLA_HEADS):
            sl = slice(h * HEAD_SLOT, (h + 1) * HEAD_SLOT)
            s = lax.dot_general(q_ref[:, sl], k_ref[:, sl], (((1,), (1,)), ((), ())),
                                preferred_element_type=F32)
            if masked:
                s = jnp.where(keep, s, NEG)
            m_prev = m_scr[h]
            m_new = jnp.maximum(m_prev, jnp.max(s, axis=-1, keepdims=True))
            alpha = jnp.exp(m_prev - m_new)
            p = jnp.exp(s - m_new)
            l_scr[h] = alpha * l_scr[h] + jnp.sum(p, axis=-1, keepdims=True)
            acc_scr[:, sl] = alpha * acc_scr[:, sl] + jnp.dot(
                p.astype(BF16), v_ref[:, sl], preferred_element_type=F32)
            m_scr[h] = m_new

    edge = (j == i) | (j == 0)

    @pl.when(edge)
    def _():
        accumulate(True)

    @pl.when(jnp.logical_not(edge))
    def _():
        accumulate(False)

    @pl.when(j == i)
    def _():
        for h in range(MLA_HEADS):
            sl = slice(h * HEAD_SLOT, (h + 1) * HEAD_SLOT)
            o_ref[:, sl] = (acc_scr[:, sl] / l_scr[h]).astype(BF16)


def _flash_attention(q, k, v, tq):
    n = q.shape[0]
    nq = n // tq
    qi = np.concatenate([np.full(i + 1, i, np.int32) for i in range(nq)])
    kj = np.concatenate([np.arange(i + 1, dtype=np.int32) for i in range(nq)])
    grid_spec = pltpu.PrefetchScalarGridSpec(
        num_scalar_prefetch=2,
        grid=(len(qi),),
        in_specs=[pl.BlockSpec((tq, D_ATT), lambda s, qi, kj: (qi[s], 0)),
                  pl.BlockSpec((tq, D_ATT), lambda s, qi, kj: (kj[s], 0)),
                  pl.BlockSpec((tq, D_ATT), lambda s, qi, kj: (kj[s], 0))],
        out_specs=pl.BlockSpec((tq, D_ATT), lambda s, qi, kj: (qi[s], 0)),
        scratch_shapes=[pltpu.VMEM((MLA_HEADS, tq, 1), F32), pltpu.VMEM((MLA_HEADS, tq, 1), F32),
                        pltpu.VMEM((tq, D_ATT), F32)],
    )
    return pl.pallas_call(
        functools.partial(_flash_kernel, tq=tq, tk=tq),
        grid_spec=grid_spec,
        out_shape=jax.ShapeDtypeStruct((n, D_ATT), BF16),
        compiler_params=_cparams(("arbitrary",)),
        name="flash_attention",
    )(jnp.asarray(qi), jnp.asarray(kj), q, k, v)


def _sample_attn_kernel(q_ref, k_ref, v_ref, o_ref, *, n_keys):
    nk = k_ref.shape[1]
    exact = k_ref.dtype == F32
    keep = lax.broadcasted_iota(jnp.int32, (q_ref.shape[1], nk), 1) < n_keys
    for h in range(MLA_HEADS):
        sl = slice(h * HEAD_SLOT, (h + 1) * HEAD_SLOT)
        s = _mm(q_ref[0, :, sl], k_ref[0, :, sl], _NT, exact)
        s = jnp.where(keep, s, NEG)
        p = jnp.exp(s - jnp.max(s, axis=-1, keepdims=True))
        p = p / jnp.sum(p, axis=-1, keepdims=True)
        o_ref[0, :, sl] = _mm(p, v_ref[0, :, sl], _NN, exact).astype(o_ref.dtype)


def _sample_attention(q, k, v, n_keys):
    b, l, _ = q.shape
    nk = k.shape[1]
    return pl.pallas_call(
        functools.partial(_sample_attn_kernel, n_keys=n_keys),
        grid=(b,),
        in_specs=[pl.BlockSpec((1, l, D_ATT), lambda i: (i, 0, 0)),
                  pl.BlockSpec((1, nk, D_ATT), lambda i: (i, 0, 0)),
                  pl.BlockSpec((1, nk, D_ATT), lambda i: (i, 0, 0))],
        out_specs=pl.BlockSpec((1, l, D_ATT), lambda i: (i, 0, 0)),
        out_shape=jax.ShapeDtypeStruct((b, l, D_ATT), q.dtype),
        compiler_params=_cparams(("arbitrary",)),
        name="sample_attention",
    )(q, k, v)


def _expm1(x):
    series = x * (1.0 + x * (0.5 + x * (1.0 / 6.0 + x * (1.0 / 24.0 + x * (1.0 / 120.0)))))
    return jnp.where(jnp.abs(x) < 0.05, series, jnp.exp(x) - 1.0)


def _gelu_tanh(x):
    return 0.5 * x * (1.0 + jnp.tanh(0.7978845608028654 * (x + 0.044715 * x * x * x)))


def _rglru_kernel(xr_ref, yg_ref, cw_ref, cb_ref, wrg_ref, brg_ref, wig_ref, big_ref, sp_ref,
                  buf0_ref, h0_ref, rnn_ref, tailx_ref, tailh_ref, prev_scr, h_scr,
                  *, tm, start, end):
    t = pl.program_id(1)

    @pl.when(t == 0)
    def _():
        prev_scr[...] = buf0_ref[0]
        h_scr[...] = jnp.broadcast_to(h0_ref[0], h_scr.shape)

    x = xr_ref[0]
    ext = jnp.concatenate([prev_scr[...], x], axis=0)
    cw = cw_ref[...]
    xc = cb_ref[...] + cw[CONV_W - 1:CONV_W] * x
    for d in range(1, CONV_W):
        xc = xc + cw[CONV_W - 1 - d:CONV_W - d] * pltpu.roll(ext, d, 0)[SUBLANES:]
    prev_scr[...] = x[tm - SUBLANES:]

    r = jax.nn.sigmoid(_wdot(xc, wrg_ref[...]) + brg_ref[...])
    ig = jax.nn.sigmoid(_wdot(xc, wig_ref[...]) + big_ref[...])
    log_a = -LRU_C * r * sp_ref[...]
    a = jnp.exp(log_a)
    b = jnp.sqrt(-_expm1(2.0 * log_a)) * (ig * xc)
    row = lax.broadcasted_iota(jnp.int32, (tm, D_RNN), 0)
    if start > 0:
        live = (t * tm + row) >= start
        a = jnp.where(live, a, 1.0)
        b = jnp.where(live, b, 0.0)
    d = 1
    while d < tm:
        b = a * jnp.where(row >= d, pltpu.roll(b, d, 0), 0.0) + b
        a = a * jnp.where(row >= d, pltpu.roll(a, d, 0), 1.0)
        d *= 2
    h = a * h_scr[0:1] + b
    h_scr[...] = jnp.broadcast_to(h[tm - 1:tm], h_scr.shape)
    rnn_ref[0] = (h * _gelu_tanh(yg_ref[0])).astype(rnn_ref.dtype)

    t_end = (end - 1) // tm
    el = end - t_end * tm

    @pl.when(t == t_end)
    def _():
        tailx_ref[0] = ext[el:el + SUBLANES]
        tailh_ref[0] = h[el - SUBLANES:el]


def _rglru(xr, yg, cw, cb, wrg, brg, wig, big, sp, buf0, h0, start, end):
    b, l, _ = xr.shape
    tm = _row_tile(l, 512)
    seq = pl.BlockSpec((1, tm, D_RNN), lambda i, t: (i, t, 0))
    per_b = lambda r: pl.BlockSpec((1, r, D_RNN), lambda i, t: (i, 0, 0))
    return pl.pallas_call(
        functools.partial(_rglru_kernel, tm=tm, start=start, end=end),
        grid=(b, l // tm),
        in_specs=[seq, seq, _full(cw.shape), _full(cb.shape), _full(wrg.shape), _full(brg.shape),
                  _full(wig.shape), _full(big.shape), _full(sp.shape), per_b(SUBLANES), per_b(1)],
        out_specs=[seq, per_b(SUBLANES), per_b(SUBLANES)],
        out_shape=[jax.ShapeDtypeStruct((b, l, D_RNN), _act_dtype(wrg)),
                   jax.ShapeDtypeStruct((b, SUBLANES, D_RNN), F32),
                   jax.ShapeDtypeStruct((b, SUBLANES, D_RNN), F32)],
        scratch_shapes=[pltpu.VMEM((SUBLANES, D_RNN), F32), pltpu.VMEM((SUBLANES, D_RNN), F32)],
        compiler_params=_cparams(("arbitrary", "arbitrary")),
        name="rglru",
    )(xr, yg, cw, cb, wrg, brg, wig, big, sp, buf0, h0)


def _mix_out_kernel(attn_ref, rnn_ref, x_ref, wa_ref, wr_ref, g_ref, b_ref, o_ref):
    mix = _wdot(attn_ref[...], wa_ref[...]) + _wdot(rnn_ref[...], wr_ref[...])
    o_ref[...] = _ln(ALPHA * x_ref[...] + mix, g_ref[...], b_ref[...])


def _mix_out(attn, rnn, x, wa, wr, g, b):
    n = x.shape[0]
    tm = _row_tile(n, 512)
    row = lambda w: pl.BlockSpec((tm, w), lambda i: (i, 0))
    return pl.pallas_call(
        _mix_out_kernel,
        grid=(n // tm,),
        in_specs=[row(D_ATT), row(D_RNN), row(D_MODEL), _full(wa.shape), _full(wr.shape),
                  _full(g.shape), _full(b.shape)],
        out_specs=row(D_MODEL),
        out_shape=jax.ShapeDtypeStruct((n, D_MODEL), F32),
        compiler_params=_cparams(("arbitrary",)),
        name="mix_out",
    )(attn, rnn, x, wa, wr, g, b)


def _first_argmax(vals, lane):
    m = jnp.max(vals, axis=-1, keepdims=True)
    idx = jnp.min(jnp.where(vals == m, lane, N_EXPERTS), axis=-1, keepdims=True)
    return m, idx


def _router_gate(x, rw, rb):
    logits = jnp.dot(x, rw, preferred_element_type=F32, precision=lax.Precision.HIGHEST)
    s = jax.nn.sigmoid(logits)
    sel = s + rb
    lane = lax.broadcasted_iota(jnp.int32, sel.shape, 1)
    grp = lane >> 2
    best = None
    g_best = None
    for g in range(N_GROUPS):
        vals = jnp.where(grp == g, sel, NEG)
        m1, i1 = _first_argmax(vals, lane)
        m2, _ = _first_argmax(jnp.where(lane == i1, NEG, vals), lane)
        score = m1 + m2
        if g == 0:
            best, g_best = score, jnp.zeros_like(i1)
        else:
            upd = score > best
            g_best = jnp.where(upd, g, g_best)
            best = jnp.where(upd, score, best)
    vals = jnp.where(grp == g_best, sel, NEG)
    _, i1 = _first_argmax(vals, lane)
    _, i2 = _first_argmax(jnp.where(lane == i1, NEG, vals), lane)
    w1 = jnp.sum(jnp.where(lane == i1, s, 0.0), axis=-1, keepdims=True)
    w2 = jnp.sum(jnp.where(lane == i2, s, 0.0), axis=-1, keepdims=True)
    den = w1 + w2
    return jnp.where(lane == i1, w1 / den, 0.0) + jnp.where(lane == i2, w2 / den, 0.0)


def _moe_kernel(x_ref, rw_ref, rb_ref, wg_ref, wu_ref, wd_ref, g_ref, b_ref, o_ref,
                gate_scr, xb_scr, acc_scr):
    e = pl.program_id(1)

    @pl.when(e == 0)
    def _():
        x = x_ref[...]
        gate = _router_gate(x, rw_ref[...], rb_ref[...])
        for k in range(N_EXPERTS):
            gate_scr[k] = jnp.broadcast_to(gate[:, k:k + 1], gate_scr.shape[1:])
        xb_scr[...] = x.astype(BF16)
        acc_scr[...] = jnp.zeros(acc_scr.shape, F32)

    xb = xb_scr[...]
    hg = jnp.dot(xb, wg_ref[0], preferred_element_type=F32)
    hu = jnp.dot(xb, wu_ref[0], preferred_element_type=F32)
    gate_e = gate_scr[e]
    h = jax.nn.silu(hg) * hu * jnp.concatenate([gate_e] * (D_EXPERT // LANES), axis=1)
    acc_scr[...] += jnp.dot(h.astype(BF16), wd_ref[0], preferred_element_type=F32)

    @pl.when(e == N_EXPERTS - 1)
    def _():
        o_ref[...] = _ln(ALPHA * x_ref[...] + acc_scr[...], g_ref[...], b_ref[...])


def _moe(x, rw, rb, wg, wu, wd, g, b):
    n = x.shape[0]
    tm = _row_tile(n, 512)
    row = pl.BlockSpec((tm, D_MODEL), lambda i, e: (i, 0))
    return pl.pallas_call(
        _moe_kernel,
        grid=(n // tm, N_EXPERTS),
        in_specs=[row, _full(rw.shape), _full(rb.shape),
                  pl.BlockSpec((1, D_MODEL, D_EXPERT), lambda i, e: (e, 0, 0)),
                  pl.BlockSpec((1, D_MODEL, D_EXPERT), lambda i, e: (e, 0, 0)),
                  pl.BlockSpec((1, D_EXPERT, D_MODEL), lambda i, e: (e, 0, 0)),
                  _full(g.shape), _full(b.shape)],
        out_specs=row,
        out_shape=jax.ShapeDtypeStruct((n, D_MODEL), F32),
        scratch_shapes=[pltpu.VMEM((N_EXPERTS, tm, LANES), F32), pltpu.VMEM((tm, D_MODEL), BF16),
                        pltpu.VMEM((tm, D_MODEL), F32)],
        compiler_params=_cparams(("arbitrary", "arbitrary")),
        name="moe",
    )(x, rw, rb, wg, wu, wd, g, b)


def _head_sum(z, ones):
    hi, lo = _split2(z)
    parts = []
    for g in range(D_MODEL // LANES):
        sl = slice(g * LANES, (g + 1) * LANES)
        parts.append(jnp.dot(hi[:, sl], ones, preferred_element_type=F32)
                     + jnp.dot(lo[:, sl], ones, preferred_element_type=F32))
    return jnp.concatenate(parts, axis=1)


def _rwkv_proj_kernel(x_ref, sh0_ref, mu_ref, wr_ref, wk_ref, wv_ref, w0_ref, w1_ref, w2_ref,
                      a0_ref, a1_ref, a2_ref, g1_ref, g2_ref, kkw_ref, kaw_ref, ones_ref,
                      r_ref, lw_ref, k_ref, v_ref, kk_ref, a_ref, g_ref, prev_scr,
                      *, tm, start, end):
    t = pl.program_id(1)

    @pl.when(t == 0)
    def _():
        prev_scr[...] = jnp.zeros(prev_scr.shape, F32)

    x = x_ref[0]
    ext = jnp.concatenate([prev_scr[...], x], axis=0)
    x_prev = pltpu.roll(ext, 1, 0)[SUBLANES:]
    grow = t * tm + lax.broadcasted_iota(jnp.int32, (tm, D_MODEL), 0)
    x_prev = jnp.where(grow == start, sh0_ref[0], x_prev)
    prev_scr[...] = x[tm - SUBLANES:]
    xx = x_prev - x
    mu = mu_ref[...]
    xr, xw, xk, xv, xa, xg = (x + xx * mu[n:n + 1] for n in range(6))
    r = _bdot(xr, wr_ref[...])
    k = _bdot(xk, wk_ref[...])
    v = _bdot(xv, wv_ref[...])
    log_w = -DECAY_SCALE * jax.nn.sigmoid(w0_ref[...] + _bdot(jnp.tanh(_bdot(xw, w1_ref[...])), w2_ref[...]))
    a = jax.nn.sigmoid(a0_ref[...] + _bdot(_bdot(xa, a1_ref[...]), a2_ref[...]))
    g = _bdot(jax.nn.sigmoid(_bdot(xg, g1_ref[...])), g2_ref[...])
    kk = k * kkw_ref[...]
    norm = jnp.sqrt(_head_sum(kk * kk, ones_ref[...]))
    kk = kk / jnp.maximum(norm, 1e-12)
    k = k * (1.0 + (a - 1.0) * kaw_ref[...])
    live = (grow >= start) & (grow < end)
    r_ref[0] = r
    lw_ref[0] = jnp.where(live, log_w, 0.0)
    k_ref[0] = jnp.where(live, k, 0.0)
    v_ref[0] = v
    kk_ref[0] = jnp.where(live, kk, 0.0)
    a_ref[0] = a
    g_ref[0] = g


def _rwkv_proj(x, sh0, od, start, end):
    b, l, _ = x.shape
    tm = _row_tile(l, 256)
    seq = pl.BlockSpec((1, tm, D_MODEL), lambda i, t: (i, t, 0))
    ws = [od[n] for n in ("mu", "w_r", "w_k", "w_v", "w0", "w1", "w2", "a0", "a1", "a2", "g1", "g2",
                          "k_k", "k_a", "ones")]
    return pl.pallas_call(
        functools.partial(_rwkv_proj_kernel, tm=tm, start=start, end=end),
        grid=(b, l // tm),
        in_specs=[seq, pl.BlockSpec((1, 1, D_MODEL), lambda i, t: (i, 0, 0))] + [_full(w.shape) for w in ws],
        out_specs=[seq] * 7,
        out_shape=[jax.ShapeDtypeStruct((b, l, D_MODEL), F32)] * 7,
        scratch_shapes=[pltpu.VMEM((SUBLANES, D_MODEL), F32)],
        compiler_params=_cparams(("arbitrary", "arbitrary")),
        name="rwkv_proj",
    )(x, sh0, *ws)


def _wkv_kernel(r_ref, lw_ref, k_ref, v_ref, kk_ref, a_ref, s0_ref, o_ref, sout_ref, s_scr,
                *, c, exact):
    t = pl.program_id(1)

    @pl.when(t == 0)
    def _():
        s_scr[...] = s0_ref[0]

    head0 = lax.broadcasted_iota(jnp.int32, (c, LANES), 1) < RWKV_HEAD
    c2 = 2 * c
    row = lax.broadcasted_iota(jnp.int32, (c2, c2), 0)
    col = lax.broadcasted_iota(jnp.int32, (c2, c2), 1)
    row_hi = jnp.where(row >= c, c, 0)
    col_hi = jnp.where(col >= c, c, 0)
    same = row_hi == col_hi
    rr = row - row_hi
    cc = col - col_hi
    strict = same & (rr > cc)
    incl = same & (rr >= cc)
    eye = (row == col).astype(F32)
    tri = (lax.broadcasted_iota(jnp.int32, (c, c), 0) >= lax.broadcasted_iota(jnp.int32, (c, c), 1)).astype(BF16)

    def stack(x):
        return jnp.concatenate([jnp.where(head0, x, 0.0), jnp.where(head0, 0.0, x)], axis=0)

    mm = functools.partial(_mm, exact=exact)
    for p in range(RWKV_HEADS // 2):
        sl = slice(p * LANES, (p + 1) * LANES)
        r, lw, k, v, kk, a = (ref[0, :, sl] for ref in (r_ref, lw_ref, k_ref, v_ref, kk_ref, a_ref))
        lc = sum(jnp.dot(tri, part, preferred_element_type=F32) for part in _split3(lw))
        lc_end = lc[c - 1:c]
        g_in = jnp.exp(lc)
        g_ex = jnp.exp(lc - lw)
        g_inv = jnp.exp(-lc)
        g_rem = jnp.exp(lc_end - lc)
        b = kk * a
        lhs = jnp.concatenate([stack(-kk * g_ex), stack(r * g_in)], axis=0)
        rhs = jnp.concatenate([stack(b * g_inv), stack(k * g_inv)], axis=0)
        pm = mm(lhs, rhs, _NT)
        l_ab = jnp.where(strict, pm[:c2, :c2], 0.0)
        l_ak = jnp.where(strict, pm[:c2, c2:], 0.0)
        m_rb = jnp.where(incl, pm[c2:, :c2], 0.0)
        m_rk = jnp.where(incl, pm[c2:, c2:], 0.0)
        tm_ = eye + l_ab
        lp = l_ab
        n = 2
        while n < c:
            lp = mm(lp, lp, _NN)
            tm_ = tm_ + mm(tm_, lp, _NN)
            n *= 2
        s = s_scr[p]
        xs = mm(lhs, s, _NT)
        vs = stack(v)
        u = mm(tm_, xs[:c2] + mm(l_ak, vs, _NN), _NN)
        os_ = xs[c2:] + mm(m_rb, u, _NN) + mm(m_rk, vs, _NN)
        o_ref[0, :, sl] = os_[:c] + os_[c:]
        uv = jnp.concatenate([u, vs], axis=0)
        bk = jnp.concatenate([stack(b * g_rem), stack(k * g_rem)], axis=0)
        s_scr[p] = s * jnp.exp(lc_end) + mm(uv, bk, _TN)

    @pl.when(t == pl.num_programs(1) - 1)
    def _():
        sout_ref[0] = s_scr[...]


def _wkv(r, lw, k, v, kk, a, s0, c, exact):
    b, l, _ = r.shape
    seq = pl.BlockSpec((1, c, D_MODEL), lambda i, t: (i, t, 0))
    st = pl.BlockSpec((1, RWKV_HEADS // 2, LANES, LANES), lambda i, t: (i, 0, 0, 0))
    return pl.pallas_call(
        functools.partial(_wkv_kernel, c=c, exact=exact),
        grid=(b, l // c),
        in_specs=[seq] * 6 + [st],
        out_specs=[seq, st],
        out_shape=[jax.ShapeDtypeStruct((b, l, D_MODEL), F32),
                   jax.ShapeDtypeStruct((b, RWKV_HEADS // 2, LANES, LANES), F32)],
        scratch_shapes=[pltpu.VMEM((RWKV_HEADS // 2, LANES, LANES), F32)],
        compiler_params=_cparams(("arbitrary", "arbitrary")),
        name="wkv",
    )(r, lw, k, v, kk, a, s0)


def _rwkv_out_kernel(o_ref, r_ref, k_ref, v_ref, g_ref, x_ref, rk_ref, gng_ref, gnb_ref, wo_ref,
                     ones_ref, lg_ref, lb_ref, y_ref):
    ones = ones_ref[...]
    o = o_ref[...]
    inv = 1.0 / RWKV_HEAD
    mu = _head_sum(o, ones) * inv
    oc = o - mu
    var = _head_sum(oc * oc, ones) * inv
    on = oc * lax.rsqrt(var + GN_EPS) * gng_ref[...] + gnb_ref[...]
    on = on + _head_sum(r_ref[...] * k_ref[...] * rk_ref[...], ones) * v_ref[...]
    out = _bdot(on * g_ref[...], wo_ref[...])
    y_ref[...] = _ln(ALPHA * x_ref[...] + out, lg_ref[...], lb_ref[...])


def _rwkv_out(o, r, k, v, g, x, od, lg, lb):
    n = x.shape[0]
    tm = _row_tile(n, 256)
    row = pl.BlockSpec((tm, D_MODEL), lambda i: (i, 0))
    ws = [od["r_k"], od["ln_g"], od["ln_b"], od["w_o"], od["ones"], lg, lb]
    return pl.pallas_call(
        _rwkv_out_kernel,
        grid=(n // tm,),
        in_specs=[row] * 6 + [_full(w.shape) for w in ws],
        out_specs=row,
        out_shape=jax.ShapeDtypeStruct((n, D_MODEL), F32),
        compiler_params=_cparams(("arbitrary",)),
        name="rwkv_out",
    )(o, r, k, v, g, x, *ws)


def _rope_tables(pos):
    half = D_ROPE // 2
    freq = ROPE_BASE ** (-jnp.arange(half, dtype=F32) / half)
    ang = pos.astype(F32)[:, None] * freq[None, :]
    cos, sin = jnp.cos(ang), jnp.sin(ang)
    n = pos.shape[0]
    ones = jnp.ones((n, D_NOPE), F32)
    zeros = jnp.zeros((n, D_NOPE), F32)
    z16 = jnp.zeros((n, half), F32)
    tail1 = jnp.ones((n, LANES - D_NOPE - D_ROPE), F32)
    tail0 = jnp.zeros((n, LANES - D_NOPE - D_ROPE), F32)
    c = jnp.concatenate([ones, cos, cos, tail1], axis=1)
    sa = jnp.concatenate([zeros, -sin, z16, tail0], axis=1)
    sb = jnp.concatenate([zeros, z16, sin, tail0], axis=1)
    return c, sa, sb


def _slot_cols(w, width):
    k, h, _ = w.shape
    return jnp.pad(w, ((0, 0), (0, 0), (0, HEAD_SLOT - width))).reshape(k, h * HEAD_SLOT)


def _block_diag(w):
    n, c, d = w.shape
    eye = jnp.eye(n, dtype=w.dtype)
    return (eye[:, None, :, None] * w[:, :, None, :]).reshape(n * c, n * d)


def _row2(v):
    return v.reshape(1, -1).astype(F32)


def _prep_even(w_in, g_q, w_uq, g_kv, w_uk, w_uv, conv_w, conv_b, w_rg, b_rg, w_ig, b_ig, lam, w_out):
    off_ckv, off_kr = D_CQ, D_CQ + D_C
    off_xr = off_kr + D_ROPE
    off_y = off_xr + D_RNN
    kr_cols = jnp.pad(w_in[:, off_kr:off_xr], ((0, 0), (D_NOPE, LANES - D_NOPE - D_ROPE)))
    w_in, w_uq, w_uk, w_uv, w_rg, w_ig, w_out = (
        w.astype(F32) for w in (w_in, w_uq, w_uk, w_uv, w_rg, w_ig, w_out))
    w1 = jnp.concatenate([w_in[:, :off_ckv], w_in[:, off_ckv:off_kr], w_in[:, off_xr:off_y],
                          w_in[:, off_y:], kr_cols], axis=1)
    wuq = _slot_cols(w_uq, D_NOPE + D_ROPE)
    wukv = jnp.concatenate([_slot_cols(w_uk, D_NOPE), _slot_cols(w_uv, D_V)], axis=1)
    place = np.zeros((LANES, D_ATT), np.float32)
    for h in range(MLA_HEADS):
        for cidx in range(D_ROPE):
            place[D_NOPE + cidx, h * HEAD_SLOT + D_NOPE + cidx] = 1.0
    wa = jnp.pad(w_out[:MLA_HEADS * D_V].reshape(MLA_HEADS, D_V, D_MODEL),
                 ((0, 0), (0, HEAD_SLOT - D_V), (0, 0))).reshape(D_ATT, D_MODEL)
    wr = w_out[MLA_HEADS * D_V:]
    return dict(
        w1=w1, gq=_row2(g_q), wuq=wuq, gkv=_row2(g_kv), wukv=wukv, place=jnp.asarray(place, F32),
        cw=conv_w.astype(F32), cb=_row2(conv_b), wrg=_block_diag(w_rg), brg=_row2(b_rg),
        wig=_block_diag(w_ig), big=_row2(b_ig), sp=_row2(jax.nn.softplus(-lam.astype(F32))),
        wa=wa, wr=wr)


_EVEN_MATMUL_WEIGHTS = ("w1", "wuq", "wukv", "place", "wrg", "wig", "wa", "wr")


def _single_pass(ev):
    return {n: (w.astype(BF16) if n in _EVEN_MATMUL_WEIGHTS else w) for n, w in ev.items()}


def _prep_odd(mu, w_r, w_k, w_v, w0, w1, w2, a0, a1, a2, g1, g2, k_k, k_a, r_k, ln_g, ln_b, w_o):
    ones = np.zeros((LANES, LANES), np.float32)
    ones[:RWKV_HEAD, :RWKV_HEAD] = 1.0
    ones[RWKV_HEAD:, RWKV_HEAD:] = 1.0
    return dict(
        mu=jnp.pad(mu.astype(F32), ((0, SUBLANES - mu.shape[0]), (0, 0))),
        w_r=w_r.astype(BF16), w_k=w_k.astype(BF16), w_v=w_v.astype(BF16), w0=_row2(w0),
        w1=w1.astype(BF16), w2=w2.astype(BF16), a0=_row2(a0), a1=a1.astype(BF16), a2=a2.astype(BF16),
        g1=g1.astype(BF16), g2=g2.astype(BF16), k_k=_row2(k_k), k_a=_row2(k_a), r_k=_row2(r_k),
        ln_g=_row2(ln_g), ln_b=_row2(ln_b), w_o=w_o.astype(BF16), ones=jnp.asarray(ones, BF16))


def _pair_states(s):
    b = s.shape[0]
    s = s.reshape(b, RWKV_HEADS // 2, 2, RWKV_HEAD, RWKV_HEAD).astype(F32)
    eye = jnp.eye(2, dtype=F32)
    out = s[:, :, :, :, None, :] * eye[None, None, :, None, :, None]
    return out.reshape(b, RWKV_HEADS // 2, LANES, LANES)


def _unpair_states(s):
    b = s.shape[0]
    s = s.reshape(b, RWKV_HEADS // 2, 2, RWKV_HEAD, 2, RWKV_HEAD)
    return jnp.stack([s[:, :, 0, :, 0, :], s[:, :, 1, :, 1, :]], axis=2).reshape(
        b, RWKV_HEADS, RWKV_HEAD, RWKV_HEAD)


def _round_up(n, m):
    return -(-n // m) * m


def kernel(x_prompt, x_sample, cache_ckv, cache_krope, state_conv, state_lru, state_shift, state_wkv,
           meta_tokens, ev_w_in, ev_g_q, ev_w_uq, ev_g_kv, ev_w_uk, ev_w_uv, ev_conv_w, ev_conv_b,
           ev_w_rg, ev_b_rg, ev_w_ig, ev_b_ig, ev_lru_lambda, ev_w_out, od_mu, od_w_r, od_w_k, od_w_v,
           od_w0, od_w1, od_w2, od_a0, od_a1, od_a2, od_g1, od_g2, od_k_k, od_k_a, od_r_k, od_ln_g,
           od_ln_b, od_w_o, ln_g, ln_b, router_w, router_b, exp_w_gate, exp_w_up, exp_w_down):
    assert x_prompt.shape[0] == 1 and x_prompt.shape[2] == D_MODEL
    seq = x_prompt.shape[1]
    assert seq % CHUNK == 0
    bs, ls, _ = x_sample.shape
    past = cache_ckv.shape[2]
    ns = bs * ls
    end = ROW0 + seq
    tp = _round_up(end, 512)

    ev = _prep_even(ev_w_in[0], ev_g_q[0], ev_w_uq[0], ev_g_kv[0], ev_w_uk[0], ev_w_uv[0], ev_conv_w[0],
                    ev_conv_b[0], ev_w_rg[0], ev_b_rg[0], ev_w_ig[0], ev_b_ig[0], ev_lru_lambda[0],
                    ev_w_out[0])
    od = _prep_odd(od_mu[0], od_w_r[0], od_w_k[0], od_w_v[0], od_w0[0], od_w1[0], od_w2[0], od_a0[0],
                   od_a1[0], od_a2[0], od_g1[0], od_g2[0], od_k_k[0], od_k_a[0], od_r_k[0], od_ln_g[0],
                   od_ln_b[0], od_w_o[0])
    rw = router_w.astype(F32)
    rb = _row2(router_b)
    wg, wu, wd = exp_w_gate.astype(BF16), exp_w_up.astype(BF16), exp_w_down.astype(BF16)
    lng = ln_g.astype(F32)[:, :, None, :]
    lnb = ln_b.astype(F32)[:, :, None, :]

    def moe(x, layer):
        return _moe(x, rw, rb, wg[layer], wu[layer], wd[layer], lng[layer, 1], lnb[layer, 1])

    xp = jnp.concatenate([jnp.zeros((PAD_FRONT, D_MODEL), F32), meta_tokens.astype(F32),
                          x_prompt[0].astype(F32), jnp.zeros((tp - end, D_MODEL), F32)], axis=0)
    tabs_p = _rope_tables(jnp.maximum(jnp.arange(tp) - PAD_FRONT, 0))
    evb = _single_pass(ev)
    q_p, ckv_p, kr_p, xr_p, yg_p = _even_proj(xp, evb["w1"], ev["gq"], evb["wuq"], ev["gkv"], tabs_p)
    k_p, v_p = _kv_proj(ckv_p, kr_p, evb["wukv"], evb["place"])
    attn_p = _flash_attention(q_p, k_p, v_p, 512)
    rnn_p, tailx_p, tailh_p = _rglru(
        xr_p[None], yg_p[None], ev["cw"], ev["cb"], evb["wrg"], ev["brg"], evb["wig"], ev["big"], ev["sp"],
        jnp.zeros((1, SUBLANES, D_RNN), F32), jnp.zeros((1, 1, D_RNN), F32), PAD_FRONT, end)
    x1_p = _mix_out(attn_p, rnn_p[0], xp, evb["wa"], evb["wr"], lng[0, 0], lnb[0, 0])
    x2_p = moe(x1_p, 0)

    xs = x_sample.reshape(ns, D_MODEL).astype(F32)
    pos_s = jnp.tile(N_META + past + jnp.arange(ls), bs)
    q_s, ckv_s, kr_s, xr_s, yg_s = _even_proj(xs, ev["w1"], ev["gq"], ev["wuq"], ev["gkv"], _rope_tables(pos_s))
    _, ckv_m, kr_m, _, _ = _even_proj(meta_tokens.astype(F32), ev["w1"], ev["gq"], ev["wuq"], ev["gkv"],
                                      _rope_tables(jnp.arange(N_META)))
    n_keys = N_META + past + ls
    nk_pad = _round_up(n_keys, LANES)
    meta_ckv = jnp.broadcast_to(ckv_m[None], (bs, N_META, D_C))
    meta_kr = jnp.broadcast_to(kr_m[None], (bs, N_META, LANES))
    cache_kr = jnp.pad(cache_krope[0].astype(F32), ((0, 0), (0, 0), (D_NOPE, LANES - D_NOPE - D_ROPE)))
    all_ckv = jnp.concatenate([meta_ckv, cache_ckv[0].astype(F32), ckv_s.reshape(bs, ls, D_C),
                               jnp.zeros((bs, nk_pad - n_keys, D_C), F32)], axis=1)
    all_kr = jnp.concatenate([meta_kr, cache_kr, kr_s.reshape(bs, ls, LANES),
                              jnp.zeros((bs, nk_pad - n_keys, LANES), F32)], axis=1)
    k_s, v_s = _kv_proj(all_ckv.reshape(bs * nk_pad, D_C), all_kr.reshape(bs * nk_pad, LANES),
                        ev["wukv"], ev["place"])
    attn_s = _sample_attention(q_s.reshape(bs, ls, D_ATT), k_s.reshape(bs, nk_pad, D_ATT),
                               v_s.reshape(bs, nk_pad, D_ATT), n_keys)
    buf0_s = jnp.pad(state_conv[0].astype(F32), ((0, 0), (SUBLANES - (CONV_W - 1), 0), (0, 0)))
    rnn_s, tailx_s, tailh_s = _rglru(
        xr_s.reshape(bs, ls, D_RNN), yg_s.reshape(bs, ls, D_RNN), ev["cw"], ev["cb"], ev["wrg"], ev["brg"],
        ev["wig"], ev["big"], ev["sp"], buf0_s, state_lru[0].astype(F32)[:, None, :], 0, ls)
    x1_s = _mix_out(attn_s.reshape(ns, D_ATT), rnn_s.reshape(ns, D_RNN), xs, ev["wa"], ev["wr"],
                    lng[0, 0], lnb[0, 0])
    x2_s = moe(x1_s, 0)

    r_p, lw_p, kk_in_p, v1_p, kkn_p, a_p, g_p = _rwkv_proj(
        x2_p[None], jnp.zeros((1, 1, D_MODEL), F32), od, PAD_FRONT, end)
    o_p, s_p = _wkv(r_p, lw_p, kk_in_p, v1_p, kkn_p, a_p,
                    jnp.zeros((1, RWKV_HEADS // 2, LANES, LANES), F32), CHUNK, True)
    x3_p = _rwkv_out(o_p[0], r_p[0], kk_in_p[0], v1_p[0], g_p[0], x2_p, od, lng[1, 0], lnb[1, 0])
    x4_p = moe(x3_p, 1)

    x2_s3 = x2_s.reshape(bs, ls, D_MODEL)
    r_s, lw_s, kk_in_s, v1_s, kkn_s, a_s, g_s = _rwkv_proj(
        x2_s3, state_shift[0].astype(F32)[:, None, :], od, 0, ls)
    to_chunk = lambda z: jnp.pad(z, ((0, 0), (0, _round_up(ls, CHUNK) - ls), (0, 0)))
    o_s, s_s = _wkv(*(to_chunk(z) for z in (r_s, lw_s, kk_in_s, v1_s, kkn_s, a_s)),
                    _pair_states(state_wkv[0]), CHUNK, True)
    o_s = o_s[:, :ls]
    flat = lambda z: z.reshape(ns, D_MODEL)
    x3_s = _rwkv_out(flat(o_s), flat(r_s), flat(kk_in_s), flat(v1_s), flat(g_s), x2_s, od,
                     lng[1, 0], lnb[1, 0])
    x4_s = moe(x3_s, 1)

    dt = x_prompt.dtype
    nb = CONV_W - 1
    return (
        x4_p[ROW0:end][None].astype(dt),
        x4_s.reshape(bs, ls, D_MODEL).astype(dt),
        ckv_p[PAD_FRONT:end][None, None].astype(dt),
        kr_p[PAD_FRONT:end, D_NOPE:D_NOPE + D_ROPE][None, None].astype(dt),
        tailx_p[:, SUBLANES - nb:][None].astype(dt),
        tailh_p[:, SUBLANES - 1][None].astype(dt),
        x2_p[end - 1][None, None].astype(dt),
        _unpair_states(s_p)[None].astype(dt),
        ckv_s.reshape(bs, ls, D_C)[None].astype(dt),
        kr_s.reshape(bs, ls, LANES)[:, :, D_NOPE:D_NOPE + D_ROPE][None].astype(dt),
        tailx_s[:, SUBLANES - nb:][None].astype(dt),
        tailh_s[:, SUBLANES - 1][None].astype(dt),
        x2_s3[:, ls - 1][None].astype(dt),
        _unpair_states(s_s)[None].astype(dt),
    )
```

```python
import functools

import numpy as np
import jax
import jax.numpy as jnp
from jax import lax
from jax.experimental import pallas as pl
from jax.experimental.pallas import tpu as pltpu

F32 = jnp.float32
BF16 = jnp.bfloat16

D_MODEL = 1024
N_META = 16
CHUNK = 64
CHUNK_SHIFT = 6
LN_EPS = 1e-5
RMS_EPS = 1e-6
DEPTH = 2
ALPHA = (2 * DEPTH) ** 0.25
MLA_HEADS = 8
D_NOPE = 64
D_ROPE = 32
D_V = 64
D_C = 256
D_CQ = 384
ROPE_BASE = 10000.0
MLA_SCALE = (D_NOPE + D_ROPE) ** -0.5
D_RNN = 512
LRU_BLOCKS = 8
LRU_BLOCK_W = D_RNN // LRU_BLOCKS
CONV_W = 4
LRU_C = 8.0
RWKV_HEAD = 64
RWKV_HEADS = D_MODEL // RWKV_HEAD
DECAY_SCALE = float(np.exp(-0.5))
GN_EPS = 64e-5
N_EXPERTS = 16
N_GROUPS = 4
EXPERTS_PER_GROUP = N_EXPERTS // N_GROUPS
D_EXPERT = 512

LANES = 128
SUBLANES = 8
HEAD_SLOT = LANES
D_ATT = MLA_HEADS * HEAD_SLOT
PAD_FRONT = CHUNK - N_META
ROW0 = PAD_FRONT + N_META
NEG = -1e30
LOG2E = 1.4426950408889634
VMEM_LIMIT = 56 * 1024 * 1024

C_CQ = 0
C_CKV = D_CQ
C_XR = C_CKV + D_C
C_YG = C_XR + D_RNN
C_KR = C_YG + D_RNN
N_COL = C_KR + LANES


def _cparams(sem):
    return pltpu.CompilerParams(dimension_semantics=sem, vmem_limit_bytes=VMEM_LIMIT)


def _row_tile(n, cap):
    for t in (1024, 512, 256, 128, 64, 32, 16, 8):
        if t <= cap and n % t == 0:
            return t
    return n


def _full(shape):
    zeros = (0,) * len(shape)
    return pl.BlockSpec(shape, lambda *_: zeros)


def _ln(x, g, b):
    mu = jnp.mean(x, axis=-1, keepdims=True)
    xc = x - mu
    var = jnp.mean(xc * xc, axis=-1, keepdims=True)
    return xc * lax.rsqrt(var + LN_EPS) * g + b


def _bdot(a, b):
    return jnp.dot(a.astype(BF16), b.astype(BF16), preferred_element_type=F32)


def _split2(x):
    hi = x.astype(BF16)
    return hi, (x - hi.astype(F32)).astype(BF16)


def _split3(x):
    hi = x.astype(BF16)
    r1 = x - hi.astype(F32)
    mid = r1.astype(BF16)
    return hi, mid, (r1 - mid.astype(F32)).astype(BF16)


_NN = ((1,), (0,))
_NT = ((1,), (1,))
_TN = ((0,), (0,))


def _mm(a, b, dims, exact):
    dn = (dims, ((), ()))
    if not exact:
        return lax.dot_general(a.astype(BF16), b.astype(BF16), dn, preferred_element_type=F32)
    ah, al = _split2(a)
    bh, bl = _split2(b)
    return (lax.dot_general(ah, bh, dn, preferred_element_type=F32)
            + lax.dot_general(al, bh, dn, preferred_element_type=F32)
            + lax.dot_general(ah, bl, dn, preferred_element_type=F32))


def _wdot(a, w):
    return _mm(a, w, _NN, exact=(w.dtype == F32))


def _act_dtype(w):
    return F32 if w.dtype == F32 else BF16


def _rope_slot(x, c, sa, sb):
    return x * c + pltpu.roll(x, LANES - D_ROPE // 2, 1) * sa + pltpu.roll(x, D_ROPE // 2, 1) * sb


def _even_proj_kernel(x_ref, w1_ref, gq_ref, wuq_ref, gkv_ref, c_ref, sa_ref, sb_ref,
                      q_ref, ckv_ref, kr_ref, xr_ref, yg_ref, *, q_scale):
    u = _wdot(x_ref[...], w1_ref[...])
    cq = u[:, C_CQ:C_CQ + D_CQ]
    cq = cq * lax.rsqrt(jnp.mean(cq * cq, axis=-1, keepdims=True) + RMS_EPS) * gq_ref[...]
    q = _wdot(cq, wuq_ref[...])
    c, sa, sb = c_ref[...], sa_ref[...], sb_ref[...]
    for h in range(MLA_HEADS):
        sl = slice(h * HEAD_SLOT, (h + 1) * HEAD_SLOT)
        q_ref[:, sl] = (_rope_slot(q[:, sl], c, sa, sb) * q_scale).astype(q_ref.dtype)
    ckv = u[:, C_CKV:C_CKV + D_C]
    ckv_ref[...] = ckv * lax.rsqrt(jnp.mean(ckv * ckv, axis=-1, keepdims=True) + RMS_EPS) * gkv_ref[...]
    kr_ref[...] = _rope_slot(u[:, C_KR:C_KR + LANES], c, sa, sb)
    xr_ref[...] = u[:, C_XR:C_XR + D_RNN]
    yg_ref[...] = u[:, C_YG:C_YG + D_RNN]


def _even_proj(x, w1, gq, wuq, gkv, tabs, q_scale):
    n = x.shape[0]
    tm = _row_tile(n, 512)
    row = lambda w: pl.BlockSpec((tm, w), lambda i: (i, 0))
    c, sa, sb = tabs
    return pl.pallas_call(
        functools.partial(_even_proj_kernel, q_scale=q_scale),
        grid=(n // tm,),
        in_specs=[row(D_MODEL), _full(w1.shape), _full(gq.shape), _full(wuq.shape), _full(gkv.shape),
                  row(LANES), row(LANES), row(LANES)],
        out_specs=[row(D_ATT), row(D_C), row(LANES), row(D_RNN), row(D_RNN)],
        out_shape=[jax.ShapeDtypeStruct((n, D_ATT), _act_dtype(w1)), jax.ShapeDtypeStruct((n, D_C), F32),
                   jax.ShapeDtypeStruct((n, LANES), F32), jax.ShapeDtypeStruct((n, D_RNN), F32),
                   jax.ShapeDtypeStruct((n, D_RNN), F32)],
        compiler_params=_cparams(("arbitrary",)),
        name="even_proj",
    )(x, w1, gq, wuq, gkv, c, sa, sb)


def _kv_proj_kernel(ckv_ref, kr_ref, wukv_ref, p_ref, k_ref, v_ref):
    kv = _wdot(ckv_ref[...], wukv_ref[...])
    k_ref[...] = (kv[:, :D_ATT] + _wdot(kr_ref[...], p_ref[...])).astype(k_ref.dtype)
    v_ref[...] = kv[:, D_ATT:].astype(v_ref.dtype)


def _kv_proj(ckv, kr, wukv, place):
    n = ckv.shape[0]
    tm = _row_tile(n, 512)
    row = lambda w: pl.BlockSpec((tm, w), lambda i: (i, 0))
    return pl.pallas_call(
        _kv_proj_kernel,
        grid=(n // tm,),
        in_specs=[row(D_C), row(LANES), _full(wukv.shape), _full(place.shape)],
        out_specs=[row(D_ATT), row(D_ATT)],
        out_shape=[jax.ShapeDtypeStruct((n, D_ATT), _act_dtype(wukv))] * 2,
        compiler_params=_cparams(("arbitrary",)),
        name="kv_proj",
    )(ckv, kr, wukv, place)


def _kv_proj_t_kernel(ckv_ref, kr_ref, wuk_ref, p_ref, wuvt_ref, ones_ref, k_ref, vt_ref):
    ckv = ckv_ref[...].astype(BF16)
    k = jnp.dot(ckv, wuk_ref[...], preferred_element_type=F32) + _bdot(kr_ref[...], p_ref[...])
    k_ref[...] = k.astype(BF16)
    vt = lax.dot_general(wuvt_ref[...], ckv, (_NT, ((), ())), preferred_element_type=F32)
    vt_ref[...] = (vt + ones_ref[...]).astype(BF16)


def _kv_proj_t(ckv, kr, wuk, place, wuvt, ones_col):
    n = ckv.shape[0]
    tm = _row_tile(n, 512)
    row = lambda w: pl.BlockSpec((tm, w), lambda i: (i, 0))
    return pl.pallas_call(
        _kv_proj_t_kernel,
        grid=(n // tm,),
        in_specs=[row(D_C), row(LANES), _full(wuk.shape), _full(place.shape), _full(wuvt.shape),
                  _full(ones_col.shape)],
        out_specs=[row(D_ATT), pl.BlockSpec((D_ATT, tm), lambda i: (0, i))],
        out_shape=[jax.ShapeDtypeStruct((n, D_ATT), BF16), jax.ShapeDtypeStruct((D_ATT, n), BF16)],
        compiler_params=_cparams(("arbitrary",)),
        name="kv_proj_t",
    )(ckv, kr, wuk, place, wuvt, ones_col)


def _flash_kernel(qi_ref, kj_ref, q_ref, k_ref, vt_ref, o_ref, m_scr, acc_scr, *, tq, tk):
    step = pl.program_id(0)
    i = qi_ref[step]
    j = kj_ref[step]

    @pl.when(j == 0)
    def _():
        m_scr[...] = jnp.full(m_scr.shape, NEG, F32)
        acc_scr[...] = jnp.zeros(acc_scr.shape, F32)

    heads = range(MLA_HEADS)
    slots = [slice(h * HEAD_SLOT, (h + 1) * HEAD_SLOT) for h in heads]

    def accumulate(masked):
        st = [lax.dot_general(k_ref[:, sl], q_ref[:, sl], (_NT, ((), ())), preferred_element_type=F32)
              for sl in slots]
        if masked:
            krow = j * tk + lax.broadcasted_iota(jnp.int32, (tk, tq), 0)
            qrow = i * tq + lax.broadcasted_iota(jnp.int32, (tk, tq), 1)
            keep = ((((qrow - ROW0) >> CHUNK_SHIFT) >= ((krow - ROW0) >> CHUNK_SHIFT))
                    & (krow >= PAD_FRONT))
            st = [jnp.where(keep, x, NEG) for x in st]
        m_prev = [m_scr[h:h + 1, :] for h in heads]
        m_new = [jnp.maximum(mp, jnp.max(x, axis=0, keepdims=True)) for mp, x in zip(m_prev, st)]
        alpha = [jnp.exp2(mp - mn) for mp, mn in zip(m_prev, m_new)]
        pt = [jnp.exp2(x - mn).astype(BF16) for x, mn in zip(st, m_new)]
        pv = [jnp.dot(vt_ref[sl, :], x, preferred_element_type=F32) for x, sl in zip(pt, slots)]
        for h in heads:
            acc_scr[slots[h], :] = alpha[h] * acc_scr[slots[h], :] + pv[h]
            m_scr[h:h + 1, :] = m_new[h]

    edge = (j == i) | (j == 0)

    @pl.when(edge)
    def _():
        accumulate(True)

    @pl.when(jnp.logical_not(edge))
    def _():
        accumulate(False)

    @pl.when(j == i)
    def _():
        for sl in slots:
            acc = acc_scr[sl, :]
            o_ref[:, sl] = (acc / acc[D_V:D_V + 1, :]).T.astype(BF16)


def _flash_attention(q, k, vt, tq):
    n = q.shape[0]
    nq = n // tq
    qi = np.concatenate([np.full(i + 1, i, np.int32) for i in range(nq)])
    kj = np.concatenate([np.arange(i + 1, dtype=np.int32) for i in range(nq)])
    grid_spec = pltpu.PrefetchScalarGridSpec(
        num_scalar_prefetch=2,
        grid=(len(qi),),
        in_specs=[pl.BlockSpec((tq, D_ATT), lambda s, qi, kj: (qi[s], 0)),
                  pl.BlockSpec((tq, D_ATT), lambda s, qi, kj: (kj[s], 0)),
                  pl.BlockSpec((D_ATT, tq), lambda s, qi, kj: (0, kj[s]))],
        out_specs=pl.BlockSpec((tq, D_ATT), lambda s, qi, kj: (qi[s], 0)),
        scratch_shapes=[pltpu.VMEM((MLA_HEADS, tq), F32), pltpu.VMEM((D_ATT, tq), F32)],
    )
    return pl.pallas_call(
        functools.partial(_flash_kernel, tq=tq, tk=tq),
        grid_spec=grid_spec,
        out_shape=jax.ShapeDtypeStruct((n, D_ATT), BF16),
        compiler_params=_cparams(("arbitrary",)),
        name="flash_attention",
    )(jnp.asarray(qi), jnp.asarray(kj), q, k, vt)


def _sample_attn_kernel(q_ref, k_ref, v_ref, o_ref, *, n_keys):
    nk = k_ref.shape[1]
    exact = k_ref.dtype == F32
    keep = lax.broadcasted_iota(jnp.int32, (q_ref.shape[1], nk), 1) < n_keys
    for h in range(MLA_HEADS):
        sl = slice(h * HEAD_SLOT, (h + 1) * HEAD_SLOT)
        s = _mm(q_ref[0, :, sl], k_ref[0, :, sl], _NT, exact)
        s = jnp.where(keep, s, NEG)
        p = jnp.exp(s - jnp.max(s, axis=-1, keepdims=True))
        p = p / jnp.sum(p, axis=-1, keepdims=True)
        o_ref[0, :, sl] = _mm(p, v_ref[0, :, sl], _NN, exact).astype(o_ref.dtype)


def _sample_attention(q, k, v, n_keys):
    b, l, _ = q.shape
    nk = k.shape[1]
    return pl.pallas_call(
        functools.partial(_sample_attn_kernel, n_keys=n_keys),
        grid=(b,),
        in_specs=[pl.BlockSpec((1, l, D_ATT), lambda i: (i, 0, 0)),
                  pl.BlockSpec((1, nk, D_ATT), lambda i: (i, 0, 0)),
                  pl.BlockSpec((1, nk, D_ATT), lambda i: (i, 0, 0))],
        out_specs=pl.BlockSpec((1, l, D_ATT), lambda i: (i, 0, 0)),
        out_shape=jax.ShapeDtypeStruct((b, l, D_ATT), q.dtype),
        compiler_params=_cparams(("arbitrary",)),
        name="sample_attention",
    )(q, k, v)


def _expm1(x):
    series = x * (1.0 + x * (0.5 + x * (1.0 / 6.0 + x * (1.0 / 24.0 + x * (1.0 / 120.0)))))
    return jnp.where(jnp.abs(x) < 0.05, series, jnp.exp(x) - 1.0)


def _gelu_tanh(x):
    return 0.5 * x * (1.0 + jnp.tanh(0.7978845608028654 * (x + 0.044715 * x * x * x)))


def _rglru_kernel(xr_ref, yg_ref, cw_ref, cb_ref, wrg_ref, brg_ref, wig_ref, big_ref, sp_ref,
                  buf0_ref, h0_ref, rnn_ref, tailx_ref, tailh_ref, prev_scr, h_scr,
                  *, tm, start, end):
    t = pl.program_id(1)

    @pl.when(t == 0)
    def _():
        prev_scr[...] = buf0_ref[0]
        h_scr[...] = jnp.broadcast_to(h0_ref[0], h_scr.shape)

    x = xr_ref[0]
    ext = jnp.concatenate([prev_scr[...], x], axis=0)
    cw = cw_ref[...]
    xc = cb_ref[...] + cw[CONV_W - 1:CONV_W] * x
    for d in range(1, CONV_W):
        xc = xc + cw[CONV_W - 1 - d:CONV_W - d] * pltpu.roll(ext, d, 0)[SUBLANES:]
    prev_scr[...] = x[tm - SUBLANES:]

    r = jax.nn.sigmoid(_wdot(xc, wrg_ref[...]) + brg_ref[...])
    ig = jax.nn.sigmoid(_wdot(xc, wig_ref[...]) + big_ref[...])
    log_a = -LRU_C * r * sp_ref[...]
    a = jnp.exp(log_a)
    b = jnp.sqrt(-_expm1(2.0 * log_a)) * (ig * xc)
    row = lax.broadcasted_iota(jnp.int32, (tm, D_RNN), 0)
    if start > 0:
        live = (t * tm + row) >= start
        a = jnp.where(live, a, 1.0)
        b = jnp.where(live, b, 0.0)
    d = 1
    while d < tm:
        b = a * jnp.where(row >= d, pltpu.roll(b, d, 0), 0.0) + b
        a = a * jnp.where(row >= d, pltpu.roll(a, d, 0), 1.0)
        d *= 2
    h = a * h_scr[0:1] + b
    h_scr[...] = jnp.broadcast_to(h[tm - 1:tm], h_scr.shape)
    rnn_ref[0] = (h * _gelu_tanh(yg_ref[0])).astype(rnn_ref.dtype)

    t_end = (end - 1) // tm
    el = end - t_end * tm

    @pl.when(t == t_end)
    def _():
        tailx_ref[0] = ext[el:el + SUBLANES]
        tailh_ref[0] = h[el - SUBLANES:el]


def _rglru(xr, yg, cw, cb, wrg, brg, wig, big, sp, buf0, h0, start, end):
    b, l, _ = xr.shape
    tm = _row_tile(l, 512)
    seq = pl.BlockSpec((1, tm, D_RNN), lambda i, t: (i, t, 0))
    per_b = lambda r: pl.BlockSpec((1, r, D_RNN), lambda i, t: (i, 0, 0))
    return pl.pallas_call(
        functools.partial(_rglru_kernel, tm=tm, start=start, end=end),
        grid=(b, l // tm),
        in_specs=[seq, seq, _full(cw.shape), _full(cb.shape), _full(wrg.shape), _full(brg.shape),
                  _full(wig.shape), _full(big.shape), _full(sp.shape), per_b(SUBLANES), per_b(1)],
        out_specs=[seq, per_b(SUBLANES), per_b(SUBLANES)],
        out_shape=[jax.ShapeDtypeStruct((b, l, D_RNN), _act_dtype(wrg)),
                   jax.ShapeDtypeStruct((b, SUBLANES, D_RNN), F32),
                   jax.ShapeDtypeStruct((b, SUBLANES, D_RNN), F32)],
        scratch_shapes=[pltpu.VMEM((SUBLANES, D_RNN), F32), pltpu.VMEM((SUBLANES, D_RNN), F32)],
        compiler_params=_cparams(("arbitrary", "arbitrary")),
        name="rglru",
    )(xr, yg, cw, cb, wrg, brg, wig, big, sp, buf0, h0)


def _mix_out_kernel(attn_ref, rnn_ref, x_ref, wa_ref, wr_ref, g_ref, b_ref, o_ref):
    mix = _wdot(attn_ref[...], wa_ref[...]) + _wdot(rnn_ref[...], wr_ref[...])
    o_ref[...] = _ln(ALPHA * x_ref[...] + mix, g_ref[...], b_ref[...])


def _mix_out(attn, rnn, x, wa, wr, g, b):
    n = x.shape[0]
    tm = _row_tile(n, 512)
    row = lambda w: pl.BlockSpec((tm, w), lambda i: (i, 0))
    return pl.pallas_call(
        _mix_out_kernel,
        grid=(n // tm,),
        in_specs=[row(D_ATT), row(D_RNN), row(D_MODEL), _full(wa.shape), _full(wr.shape),
                  _full(g.shape), _full(b.shape)],
        out_specs=row(D_MODEL),
        out_shape=jax.ShapeDtypeStruct((n, D_MODEL), F32),
        compiler_params=_cparams(("arbitrary",)),
        name="mix_out",
    )(attn, rnn, x, wa, wr, g, b)


def _first_argmax(vals, lane):
    m = jnp.max(vals, axis=-1, keepdims=True)
    idx = jnp.min(jnp.where(vals == m, lane, N_EXPERTS), axis=-1, keepdims=True)
    return m, idx


def _router_gate(x, rw, rb):
    logits = jnp.dot(x, rw, preferred_element_type=F32, precision=lax.Precision.HIGHEST)
    s = jax.nn.sigmoid(logits)
    sel = s + rb
    lane = lax.broadcasted_iota(jnp.int32, sel.shape, 1)
    grp = lane >> 2
    best = None
    g_best = None
    for g in range(N_GROUPS):
        vals = jnp.where(grp == g, sel, NEG)
        m1, i1 = _first_argmax(vals, lane)
        m2, _ = _first_argmax(jnp.where(lane == i1, NEG, vals), lane)
        score = m1 + m2
        if g == 0:
            best, g_best = score, jnp.zeros_like(i1)
        else:
            upd = score > best
            g_best = jnp.where(upd, g, g_best)
            best = jnp.where(upd, score, best)
    vals = jnp.where(grp == g_best, sel, NEG)
    _, i1 = _first_argmax(vals, lane)
    _, i2 = _first_argmax(jnp.where(lane == i1, NEG, vals), lane)
    w1 = jnp.sum(jnp.where(lane == i1, s, 0.0), axis=-1, keepdims=True)
    w2 = jnp.sum(jnp.where(lane == i2, s, 0.0), axis=-1, keepdims=True)
    den = w1 + w2
    return jnp.where(lane == i1, w1 / den, 0.0) + jnp.where(lane == i2, w2 / den, 0.0)


def _moe_kernel(x_ref, rw_ref, rb_ref, wg_ref, wu_ref, wd_ref, g_ref, b_ref, o_ref,
                gate_scr, xb_scr, acc_scr):
    e = pl.program_id(1)

    @pl.when(e == 0)
    def _():
        x = x_ref[...]
        gate = _router_gate(x, rw_ref[...], rb_ref[...])
        for k in range(N_EXPERTS):
            gate_scr[k] = jnp.broadcast_to(gate[:, k:k + 1], gate_scr.shape[1:])
        xb_scr[...] = x.astype(BF16)
        acc_scr[...] = jnp.zeros(acc_scr.shape, F32)

    xb = xb_scr[...]
    hg = jnp.dot(xb, wg_ref[0], preferred_element_type=F32)
    hu = jnp.dot(xb, wu_ref[0], preferred_element_type=F32)
    gate_e = gate_scr[e]
    h = jax.nn.silu(hg) * hu * jnp.concatenate([gate_e] * (D_EXPERT // LANES), axis=1)
    acc_scr[...] += jnp.dot(h.astype(BF16), wd_ref[0], preferred_element_type=F32)

    @pl.when(e == N_EXPERTS - 1)
    def _():
        o_ref[...] = _ln(ALPHA * x_ref[...] + acc_scr[...], g_ref[...], b_ref[...])


def _moe(x, rw, rb, wg, wu, wd, g, b):
    n = x.shape[0]
    tm = _row_tile(n, 512)
    row = pl.BlockSpec((tm, D_MODEL), lambda i, e: (i, 0))
    return pl.pallas_call(
        _moe_kernel,
        grid=(n // tm, N_EXPERTS),
        in_specs=[row, _full(rw.shape), _full(rb.shape),
                  pl.BlockSpec((1, D_MODEL, D_EXPERT), lambda i, e: (e, 0, 0)),
                  pl.BlockSpec((1, D_MODEL, D_EXPERT), lambda i, e: (e, 0, 0)),
                  pl.BlockSpec((1, D_EXPERT, D_MODEL), lambda i, e: (e, 0, 0)),
                  _full(g.shape), _full(b.shape)],
        out_specs=row,
        out_shape=jax.ShapeDtypeStruct((n, D_MODEL), F32),
        scratch_shapes=[pltpu.VMEM((N_EXPERTS, tm, LANES), F32), pltpu.VMEM((tm, D_MODEL), BF16),
                        pltpu.VMEM((tm, D_MODEL), F32)],
        compiler_params=_cparams(("arbitrary", "arbitrary")),
        name="moe",
    )(x, rw, rb, wg, wu, wd, g, b)


def _head_sum(z, ones):
    hi, lo = _split2(z)
    parts = []
    for g in range(D_MODEL // LANES):
        sl = slice(g * LANES, (g + 1) * LANES)
        parts.append(jnp.dot(hi[:, sl], ones, preferred_element_type=F32)
                     + jnp.dot(lo[:, sl], ones, preferred_element_type=F32))
    return jnp.concatenate(parts, axis=1)


def _rwkv_proj_kernel(x_ref, sh0_ref, mu_ref, wr_ref, wk_ref, wv_ref, w0_ref, w1_ref, w2_ref,
                      a0_ref, a1_ref, a2_ref, g1_ref, g2_ref, kkw_ref, kaw_ref, ones_ref,
                      r_ref, lw_ref, k_ref, v_ref, kk_ref, a_ref, g_ref, prev_scr,
                      *, tm, start, end):
    t = pl.program_id(1)

    @pl.when(t == 0)
    def _():
        prev_scr[...] = jnp.zeros(prev_scr.shape, F32)

    x = x_ref[0]
    ext = jnp.concatenate([prev_scr[...], x], axis=0)
    x_prev = pltpu.roll(ext, 1, 0)[SUBLANES:]
    grow = t * tm + lax.broadcasted_iota(jnp.int32, (tm, D_MODEL), 0)
    x_prev = jnp.where(grow == start, sh0_ref[0], x_prev)
    prev_scr[...] = x[tm - SUBLANES:]
    xx = x_prev - x
    mu = mu_ref[...]
    xr, xw, xk, xv, xa, xg = (x + xx * mu[n:n + 1] for n in range(6))
    r = _bdot(xr, wr_ref[...])
    k = _bdot(xk, wk_ref[...])
    v = _bdot(xv, wv_ref[...])
    log_w = -DECAY_SCALE * jax.nn.sigmoid(w0_ref[...] + _bdot(jnp.tanh(_bdot(xw, w1_ref[...])), w2_ref[...]))
    a = jax.nn.sigmoid(a0_ref[...] + _bdot(_bdot(xa, a1_ref[...]), a2_ref[...]))
    g = _bdot(jax.nn.sigmoid(_bdot(xg, g1_ref[...])), g2_ref[...])
    kk = k * kkw_ref[...]
    norm = jnp.sqrt(_head_sum(kk * kk, ones_ref[...]))
    kk = kk / jnp.maximum(norm, 1e-12)
    k = k * (1.0 + (a - 1.0) * kaw_ref[...])
    live = (grow >= start) & (grow < end)
    r_ref[0] = r
    lw_ref[0] = jnp.where(live, log_w, 0.0)
    k_ref[0] = jnp.where(live, k, 0.0)
    v_ref[0] = v
    kk_ref[0] = jnp.where(live, kk, 0.0)
    a_ref[0] = a
    g_ref[0] = g


def _rwkv_proj(x, sh0, od, start, end):
    b, l, _ = x.shape
    tm = _row_tile(l, 256)
    seq = pl.BlockSpec((1, tm, D_MODEL), lambda i, t: (i, t, 0))
    ws = [od[n] for n in ("mu", "w_r", "w_k", "w_v", "w0", "w1", "w2", "a0", "a1", "a2", "g1", "g2",
                          "k_k", "k_a", "ones")]
    return pl.pallas_call(
        functools.partial(_rwkv_proj_kernel, tm=tm, start=start, end=end),
        grid=(b, l // tm),
        in_specs=[seq, pl.BlockSpec((1, 1, D_MODEL), lambda i, t: (i, 0, 0))] + [_full(w.shape) for w in ws],
        out_specs=[seq] * 7,
        out_shape=[jax.ShapeDtypeStruct((b, l, D_MODEL), F32)] * 7,
        scratch_shapes=[pltpu.VMEM((SUBLANES, D_MODEL), F32)],
        compiler_params=_cparams(("arbitrary", "arbitrary")),
        name="rwkv_proj",
    )(x, sh0, *ws)


def _wkv_kernel(r_ref, lw_ref, k_ref, v_ref, kk_ref, a_ref, s0_ref, o_ref, sout_ref, s_scr,
                *, c, exact):
    t = pl.program_id(1)

    @pl.when(t == 0)
    def _():
        s_scr[...] = s0_ref[0]

    head0 = lax.broadcasted_iota(jnp.int32, (c, LANES), 1) < RWKV_HEAD
    c2 = 2 * c
    row = lax.broadcasted_iota(jnp.int32, (c2, c2), 0)
    col = lax.broadcasted_iota(jnp.int32, (c2, c2), 1)
    row_hi = jnp.where(row >= c, c, 0)
    col_hi = jnp.where(col >= c, c, 0)
    same = row_hi == col_hi
    rr = row - row_hi
    cc = col - col_hi
    strict = same & (rr > cc)
    incl = same & (rr >= cc)
    eye = (row == col).astype(F32)
    tri = (lax.broadcasted_iota(jnp.int32, (c, c), 0) >= lax.broadcasted_iota(jnp.int32, (c, c), 1)).astype(BF16)

    def stack(x):
        return jnp.concatenate([jnp.where(head0, x, 0.0), jnp.where(head0, 0.0, x)], axis=0)

    mm = functools.partial(_mm, exact=exact)
    pairs = range(RWKV_HEADS // 2)
    load = lambda ref: [ref[0, :, p * LANES:(p + 1) * LANES] for p in pairs]
    r, lw, k, v, kk, a = (load(ref) for ref in (r_ref, lw_ref, k_ref, v_ref, kk_ref, a_ref))
    lc = [sum(jnp.dot(tri, part, preferred_element_type=F32) for part in _split3(x)) for x in lw]
    lc_end = [x[c - 1:c] for x in lc]
    b = [x * y for x, y in zip(kk, a)]
    lhs = [jnp.concatenate([stack(-kk[p] * jnp.exp(lc[p] - lw[p])), stack(r[p] * jnp.exp(lc[p]))], axis=0)
           for p in pairs]
    g_inv = [jnp.exp(-x) for x in lc]
    rhs = [jnp.concatenate([stack(b[p] * g_inv[p]), stack(k[p] * g_inv[p])], axis=0) for p in pairs]
    pm = [mm(x, y, _NT) for x, y in zip(lhs, rhs)]
    l_ab = [jnp.where(strict, x[:c2, :c2], 0.0) for x in pm]
    l_ak = [jnp.where(strict, x[:c2, c2:], 0.0) for x in pm]
    m_rb = [jnp.where(incl, x[c2:, :c2], 0.0) for x in pm]
    m_rk = [jnp.where(incl, x[c2:, c2:], 0.0) for x in pm]
    vs = [stack(x) for x in v]
    lakv = [mm(x, y, _NN) for x, y in zip(l_ak, vs)]
    mrkv = [mm(x, y, _NN) for x, y in zip(m_rk, vs)]
    tinv = [eye + x for x in l_ab]
    lp = l_ab
    n = 2
    while n < c:
        lp = [mm(x, x, _NN) for x in lp]
        tinv = [x + mm(x, y, _NN) for x, y in zip(tinv, lp)]
        n *= 2
    s = [s_scr[p] for p in pairs]
    xs = [mm(x, y, _NT) for x, y in zip(lhs, s)]
    u = [mm(tinv[p], xs[p][:c2] + lakv[p], _NN) for p in pairs]
    for p in pairs:
        os_ = xs[p][c2:] + mm(m_rb[p], u[p], _NN) + mrkv[p]
        o_ref[0, :, p * LANES:(p + 1) * LANES] = os_[:c] + os_[c:]
    for p in pairs:
        g_rem = jnp.exp(lc_end[p] - lc[p])
        uv = jnp.concatenate([u[p], vs[p]], axis=0)
        bk = jnp.concatenate([stack(b[p] * g_rem), stack(k[p] * g_rem)], axis=0)
        s_scr[p] = s[p] * jnp.exp(lc_end[p]) + mm(uv, bk, _TN)

    @pl.when(t == pl.num_programs(1) - 1)
    def _():
        sout_ref[0] = s_scr[...]


def _wkv(r, lw, k, v, kk, a, s0, c, exact):
    b, l, _ = r.shape
    seq = pl.BlockSpec((1, c, D_MODEL), lambda i, t: (i, t, 0))
    st = pl.BlockSpec((1, RWKV_HEADS // 2, LANES, LANES), lambda i, t: (i, 0, 0, 0))
    return pl.pallas_call(
        functools.partial(_wkv_kernel, c=c, exact=exact),
        grid=(b, l // c),
        in_specs=[seq] * 6 + [st],
        out_specs=[seq, st],
        out_shape=[jax.ShapeDtypeStruct((b, l, D_MODEL), F32),
                   jax.ShapeDtypeStruct((b, RWKV_HEADS // 2, LANES, LANES), F32)],
        scratch_shapes=[pltpu.VMEM((RWKV_HEADS // 2, LANES, LANES), F32)],
        compiler_params=_cparams(("arbitrary", "arbitrary")),
        name="wkv",
    )(r, lw, k, v, kk, a, s0)


def _rwkv_out_kernel(o_ref, r_ref, k_ref, v_ref, g_ref, x_ref, rk_ref, gng_ref, gnb_ref, wo_ref,
                     ones_ref, lg_ref, lb_ref, y_ref):
    ones = ones_ref[...]
    o = o_ref[...]
    inv = 1.0 / RWKV_HEAD
    mu = _head_sum(o, ones) * inv
    oc = o - mu
    var = _head_sum(oc * oc, ones) * inv
    on = oc * lax.rsqrt(var + GN_EPS) * gng_ref[...] + gnb_ref[...]
    on = on + _head_sum(r_ref[...] * k_ref[...] * rk_ref[...], ones) * v_ref[...]
    out = _bdot(on * g_ref[...], wo_ref[...])
    y_ref[...] = _ln(ALPHA * x_ref[...] + out, lg_ref[...], lb_ref[...])


def _rwkv_out(o, r, k, v, g, x, od, lg, lb):
    n = x.shape[0]
    tm = _row_tile(n, 256)
    row = pl.BlockSpec((tm, D_MODEL), lambda i: (i, 0))
    ws = [od["r_k"], od["ln_g"], od["ln_b"], od["w_o"], od["ones"], lg, lb]
    return pl.pallas_call(
        _rwkv_out_kernel,
        grid=(n // tm,),
        in_specs=[row] * 6 + [_full(w.shape) for w in ws],
        out_specs=row,
        out_shape=jax.ShapeDtypeStruct((n, D_MODEL), F32),
        compiler_params=_cparams(("arbitrary",)),
        name="rwkv_out",
    )(o, r, k, v, g, x, *ws)


def _rope_tables(pos):
    half = D_ROPE // 2
    freq = ROPE_BASE ** (-jnp.arange(half, dtype=F32) / half)
    ang = pos.astype(F32)[:, None] * freq[None, :]
    cos, sin = jnp.cos(ang), jnp.sin(ang)
    n = pos.shape[0]
    ones = jnp.ones((n, D_NOPE), F32)
    zeros = jnp.zeros((n, D_NOPE), F32)
    z16 = jnp.zeros((n, half), F32)
    tail1 = jnp.ones((n, LANES - D_NOPE - D_ROPE), F32)
    tail0 = jnp.zeros((n, LANES - D_NOPE - D_ROPE), F32)
    c = jnp.concatenate([ones, cos, cos, tail1], axis=1)
    sa = jnp.concatenate([zeros, -sin, z16, tail0], axis=1)
    sb = jnp.concatenate([zeros, z16, sin, tail0], axis=1)
    return c, sa, sb


def _slot_cols(w, width):
    k, h, _ = w.shape
    return jnp.pad(w, ((0, 0), (0, 0), (0, HEAD_SLOT - width))).reshape(k, h * HEAD_SLOT)


def _block_diag(w):
    n, c, d = w.shape
    eye = jnp.eye(n, dtype=w.dtype)
    return (eye[:, None, :, None] * w[:, :, None, :]).reshape(n * c, n * d)


def _row2(v):
    return v.reshape(1, -1).astype(F32)


def _prep_even(w_in, g_q, w_uq, g_kv, w_uk, w_uv, conv_w, conv_b, w_rg, b_rg, w_ig, b_ig, lam, w_out):
    off_ckv, off_kr = D_CQ, D_CQ + D_C
    off_xr = off_kr + D_ROPE
    off_y = off_xr + D_RNN
    kr_cols = jnp.pad(w_in[:, off_kr:off_xr], ((0, 0), (D_NOPE, LANES - D_NOPE - D_ROPE)))
    w_in, w_uq, w_uk, w_uv, w_rg, w_ig, w_out = (
        w.astype(F32) for w in (w_in, w_uq, w_uk, w_uv, w_rg, w_ig, w_out))
    w1 = jnp.concatenate([w_in[:, :off_ckv], w_in[:, off_ckv:off_kr], w_in[:, off_xr:off_y],
                          w_in[:, off_y:], kr_cols], axis=1)
    wuq = _slot_cols(w_uq, D_NOPE + D_ROPE)
    wukv = jnp.concatenate([_slot_cols(w_uk, D_NOPE), _slot_cols(w_uv, D_V)], axis=1)
    place = np.zeros((LANES, D_ATT), np.float32)
    for h in range(MLA_HEADS):
        for cidx in range(D_ROPE):
            place[D_NOPE + cidx, h * HEAD_SLOT + D_NOPE + cidx] = 1.0
    wa = jnp.pad(w_out[:MLA_HEADS * D_V].reshape(MLA_HEADS, D_V, D_MODEL),
                 ((0, 0), (0, HEAD_SLOT - D_V), (0, 0))).reshape(D_ATT, D_MODEL)
    wr = w_out[MLA_HEADS * D_V:]
    return dict(
        w1=w1, gq=_row2(g_q), wuq=wuq, gkv=_row2(g_kv), wukv=wukv, place=jnp.asarray(place, F32),
        cw=conv_w.astype(F32), cb=_row2(conv_b), wrg=_block_diag(w_rg), brg=_row2(b_rg),
        wig=_block_diag(w_ig), big=_row2(b_ig), sp=_row2(jax.nn.softplus(-lam.astype(F32))),
        wa=wa, wr=wr)


_EVEN_MATMUL_WEIGHTS = ("w1", "wuq", "wukv", "place", "wrg", "wig", "wa", "wr")


def _single_pass(ev):
    return {n: (w.astype(BF16) if n in _EVEN_MATMUL_WEIGHTS else w) for n, w in ev.items()}


def _prep_odd(mu, w_r, w_k, w_v, w0, w1, w2, a0, a1, a2, g1, g2, k_k, k_a, r_k, ln_g, ln_b, w_o):
    ones = np.zeros((LANES, LANES), np.float32)
    ones[:RWKV_HEAD, :RWKV_HEAD] = 1.0
    ones[RWKV_HEAD:, RWKV_HEAD:] = 1.0
    return dict(
        mu=jnp.pad(mu.astype(F32), ((0, SUBLANES - mu.shape[0]), (0, 0))),
        w_r=w_r.astype(BF16), w_k=w_k.astype(BF16), w_v=w_v.astype(BF16), w0=_row2(w0),
        w1=w1.astype(BF16), w2=w2.astype(BF16), a0=_row2(a0), a1=a1.astype(BF16), a2=a2.astype(BF16),
        g1=g1.astype(BF16), g2=g2.astype(BF16), k_k=_row2(k_k), k_a=_row2(k_a), r_k=_row2(r_k),
        ln_g=_row2(ln_g), ln_b=_row2(ln_b), w_o=w_o.astype(BF16), ones=jnp.asarray(ones, BF16))


def _pair_states(s):
    b = s.shape[0]
    s = s.reshape(b, RWKV_HEADS // 2, 2, RWKV_HEAD, RWKV_HEAD).astype(F32)
    eye = jnp.eye(2, dtype=F32)
    out = s[:, :, :, :, None, :] * eye[None, None, :, None, :, None]
    return out.reshape(b, RWKV_HEADS // 2, LANES, LANES)


def _unpair_states(s):
    b = s.shape[0]
    s = s.reshape(b, RWKV_HEADS // 2, 2, RWKV_HEAD, 2, RWKV_HEAD)
    return jnp.stack([s[:, :, 0, :, 0, :], s[:, :, 1, :, 1, :]], axis=2).reshape(
        b, RWKV_HEADS, RWKV_HEAD, RWKV_HEAD)


def _round_up(n, m):
    return -(-n // m) * m


def kernel(x_prompt, x_sample, cache_ckv, cache_krope, state_conv, state_lru, state_shift, state_wkv,
           meta_tokens, ev_w_in, ev_g_q, ev_w_uq, ev_g_kv, ev_w_uk, ev_w_uv, ev_conv_w, ev_conv_b,
           ev_w_rg, ev_b_rg, ev_w_ig, ev_b_ig, ev_lru_lambda, ev_w_out, od_mu, od_w_r, od_w_k, od_w_v,
           od_w0, od_w1, od_w2, od_a0, od_a1, od_a2, od_g1, od_g2, od_k_k, od_k_a, od_r_k, od_ln_g,
           od_ln_b, od_w_o, ln_g, ln_b, router_w, router_b, exp_w_gate, exp_w_up, exp_w_down):
    assert x_prompt.shape[0] == 1 and x_prompt.shape[2] == D_MODEL
    seq = x_prompt.shape[1]
    assert seq % CHUNK == 0
    bs, ls, _ = x_sample.shape
    past = cache_ckv.shape[2]
    ns = bs * ls
    end = ROW0 + seq
    tp = _round_up(end, 512)

    ev = _prep_even(ev_w_in[0], ev_g_q[0], ev_w_uq[0], ev_g_kv[0], ev_w_uk[0], ev_w_uv[0], ev_conv_w[0],
                    ev_conv_b[0], ev_w_rg[0], ev_b_rg[0], ev_w_ig[0], ev_b_ig[0], ev_lru_lambda[0],
                    ev_w_out[0])
    od = _prep_odd(od_mu[0], od_w_r[0], od_w_k[0], od_w_v[0], od_w0[0], od_w1[0], od_w2[0], od_a0[0],
                   od_a1[0], od_a2[0], od_g1[0], od_g2[0], od_k_k[0], od_k_a[0], od_r_k[0], od_ln_g[0],
                   od_ln_b[0], od_w_o[0])
    rw = router_w.astype(F32)
    rb = _row2(router_b)
    wg, wu, wd = exp_w_gate.astype(BF16), exp_w_up.astype(BF16), exp_w_down.astype(BF16)
    lng = ln_g.astype(F32)[:, :, None, :]
    lnb = ln_b.astype(F32)[:, :, None, :]

    def moe(x, layer):
        return _moe(x, rw, rb, wg[layer], wu[layer], wd[layer], lng[layer, 1], lnb[layer, 1])

    xp = jnp.concatenate([jnp.zeros((PAD_FRONT, D_MODEL), F32), meta_tokens.astype(F32),
                          x_prompt[0].astype(F32), jnp.zeros((tp - end, D_MODEL), F32)], axis=0)
    tabs_p = _rope_tables(jnp.maximum(jnp.arange(tp) - PAD_FRONT, 0))
    evb = _single_pass(ev)
    q_p, ckv_p, kr_p, xr_p, yg_p = _even_proj(xp, evb["w1"], ev["gq"], evb["wuq"], ev["gkv"], tabs_p,
                                              MLA_SCALE * LOG2E)
    ones_col = np.zeros((D_ATT, 1), np.float32)
    ones_col[D_V::HEAD_SLOT] = 1.0
    k_p, vt_p = _kv_proj_t(ckv_p, kr_p, evb["wukv"][:, :D_ATT], evb["place"], evb["wukv"][:, D_ATT:].T,
                           jnp.asarray(ones_col))
    attn_p = _flash_attention(q_p, k_p, vt_p, 512)
    rnn_p, tailx_p, tailh_p = _rglru(
        xr_p[None], yg_p[None], ev["cw"], ev["cb"], evb["wrg"], ev["brg"], evb["wig"], ev["big"], ev["sp"],
        jnp.zeros((1, SUBLANES, D_RNN), F32), jnp.zeros((1, 1, D_RNN), F32), PAD_FRONT, end)
    x1_p = _mix_out(attn_p, rnn_p[0], xp, evb["wa"], evb["wr"], lng[0, 0], lnb[0, 0])
    x2_p = moe(x1_p, 0)

    xs = x_sample.reshape(ns, D_MODEL).astype(F32)
    pos_s = jnp.tile(N_META + past + jnp.arange(ls), bs)
    q_s, ckv_s, kr_s, xr_s, yg_s = _even_proj(xs, ev["w1"], ev["gq"], ev["wuq"], ev["gkv"], _rope_tables(pos_s),
                                              MLA_SCALE)
    _, ckv_m, kr_m, _, _ = _even_proj(meta_tokens.astype(F32), ev["w1"], ev["gq"], ev["wuq"], ev["gkv"],
                                      _rope_tables(jnp.arange(N_META)), MLA_SCALE)
    n_keys = N_META + past + ls
    nk_pad = _round_up(n_keys, LANES)
    meta_ckv = jnp.broadcast_to(ckv_m[None], (bs, N_META, D_C))
    meta_kr = jnp.broadcast_to(kr_m[None], (bs, N_META, LANES))
    cache_kr = jnp.pad(cache_krope[0].astype(F32), ((0, 0), (0, 0), (D_NOPE, LANES - D_NOPE - D_ROPE)))
    all_ckv = jnp.concatenate([meta_ckv, cache_ckv[0].astype(F32), ckv_s.reshape(bs, ls, D_C),
                               jnp.zeros((bs, nk_pad - n_keys, D_C), F32)], axis=1)
    all_kr = jnp.concatenate([meta_kr, cache_kr, kr_s.reshape(bs, ls, LANES),
                              jnp.zeros((bs, nk_pad - n_keys, LANES), F32)], axis=1)
    k_s, v_s = _kv_proj(all_ckv.reshape(bs * nk_pad, D_C), all_kr.reshape(bs * nk_pad, LANES),
                        ev["wukv"], ev["place"])
    attn_s = _sample_attention(q_s.reshape(bs, ls, D_ATT), k_s.reshape(bs, nk_pad, D_ATT),
                               v_s.reshape(bs, nk_pad, D_ATT), n_keys)
    buf0_s = jnp.pad(state_conv[0].astype(F32), ((0, 0), (SUBLANES - (CONV_W - 1), 0), (0, 0)))
    rnn_s, tailx_s, tailh_s = _rglru(
        xr_s.reshape(bs, ls, D_RNN), yg_s.reshape(bs, ls, D_RNN), ev["cw"], ev["cb"], ev["wrg"], ev["brg"],
        ev["wig"], ev["big"], ev["sp"], buf0_s, state_lru[0].astype(F32)[:, None, :], 0, ls)
    x1_s = _mix_out(attn_s.reshape(ns, D_ATT), rnn_s.reshape(ns, D_RNN), xs, ev["wa"], ev["wr"],
                    lng[0, 0], lnb[0, 0])
    x2_s = moe(x1_s, 0)

    r_p, lw_p, kk_in_p, v1_p, kkn_p, a_p, g_p = _rwkv_proj(
        x2_p[None], jnp.zeros((1, 1, D_MODEL), F32), od, PAD_FRONT, end)
    o_p, s_p = _wkv(r_p, lw_p, kk_in_p, v1_p, kkn_p, a_p,
                    jnp.zeros((1, RWKV_HEADS // 2, LANES, LANES), F32), CHUNK, False)
    x3_p = _rwkv_out(o_p[0], r_p[0], kk_in_p[0], v1_p[0], g_p[0], x2_p, od, lng[1, 0], lnb[1, 0])
    x4_p = moe(x3_p, 1)

    x2_s3 = x2_s.reshape(bs, ls, D_MODEL)
    r_s, lw_s, kk_in_s, v1_s, kkn_s, a_s, g_s = _rwkv_proj(
        x2_s3, state_shift[0].astype(F32)[:, None, :], od, 0, ls)
    to_chunk = lambda z: jnp.pad(z, ((0, 0), (0, _round_up(ls, CHUNK) - ls), (0, 0)))
    o_s, s_s = _wkv(*(to_chunk(z) for z in (r_s, lw_s, kk_in_s, v1_s, kkn_s, a_s)),
                    _pair_states(state_wkv[0]), CHUNK, True)
    o_s = o_s[:, :ls]
    flat = lambda z: z.reshape(ns, D_MODEL)
    x3_s = _rwkv_out(flat(o_s), flat(r_s), flat(kk_in_s), flat(v1_s), flat(g_s), x2_s, od,
                     lng[1, 0], lnb[1, 0])
    x4_s = moe(x3_s, 1)

    dt = x_prompt.dtype
    nb = CONV_W - 1
    return (
        x4_p[ROW0:end][None].astype(dt),
        x4_s.reshape(bs, ls, D_MODEL).astype(dt),
        ckv_p[PAD_FRONT:end][None, None].astype(dt),
        kr_p[PAD_FRONT:end, D_NOPE:D_NOPE + D_ROPE][None, None].astype(dt),
        tailx_p[:, SUBLANES - nb:][None].astype(dt),
        tailh_p[:, SUBLANES - 1][None].astype(dt),
        x2_p[end - 1][None, None].astype(dt),
        _unpair_states(s_p)[None].astype(dt),
        ckv_s.reshape(bs, ls, D_C)[None].astype(dt),
        kr_s.reshape(bs, ls, LANES)[:, :, D_NOPE:D_NOPE + D_ROPE][None].astype(dt),
        tailx_s[:, SUBLANES - nb:][None].astype(dt),
        tailh_s[:, SUBLANES - 1][None].astype(dt),
        x2_s3[:, ls - 1][None].astype(dt),
        _unpair_states(s_s)[None].astype(dt),
    )
```

```python
import functools

import numpy as np
import jax
import jax.numpy as jnp
from jax import lax
from jax.experimental import pallas as pl
from jax.experimental.pallas import tpu as pltpu
from jax.experimental.pallas import tpu_sc as plsc

F32 = jnp.float32
BF16 = jnp.bfloat16

D_MODEL = 1024
N_META = 16
CHUNK = 64
CHUNK_SHIFT = 6
LN_EPS = 1e-5
RMS_EPS = 1e-6
DEPTH = 2
ALPHA = (2 * DEPTH) ** 0.25
MLA_HEADS = 8
D_NOPE = 64
D_ROPE = 32
D_V = 64
D_C = 256
D_CQ = 384
ROPE_BASE = 10000.0
MLA_SCALE = (D_NOPE + D_ROPE) ** -0.5
D_RNN = 512
LRU_BLOCKS = 8
LRU_BLOCK_W = D_RNN // LRU_BLOCKS
CONV_W = 4
LRU_C = 8.0
RWKV_HEAD = 64
RWKV_HEADS = D_MODEL // RWKV_HEAD
DECAY_SCALE = float(np.exp(-0.5))
GN_EPS = 64e-5
N_EXPERTS = 16
N_GROUPS = 4
EXPERTS_PER_GROUP = N_EXPERTS // N_GROUPS
D_EXPERT = 512

LANES = 128
SUBLANES = 8
HEAD_SLOT = LANES
D_ATT = MLA_HEADS * HEAD_SLOT
PAD_FRONT = CHUNK - N_META
ROW0 = PAD_FRONT + N_META
NEG = -1e30
LOG2E = 1.4426950408889634
SC_CORES = 2
SC_SUBCORES = 16
SC_WORKERS = SC_CORES * SC_SUBCORES
MOE_TILE = 256
SPARSE_MIN_ROWS = 1024
VMEM_LIMIT = 56 * 1024 * 1024

C_CQ = 0
C_CKV = D_CQ
C_XR = C_CKV + D_C
C_YG = C_XR + D_RNN
C_KR = C_YG + D_RNN
N_COL = C_KR + LANES


def _cparams(sem):
    return pltpu.CompilerParams(dimension_semantics=sem, vmem_limit_bytes=VMEM_LIMIT)


def _row_tile(n, cap):
    for t in (1024, 512, 256, 128, 64, 32, 16, 8):
        if t <= cap and n % t == 0:
            return t
    return n


def _full(shape):
    zeros = (0,) * len(shape)
    return pl.BlockSpec(shape, lambda *_: zeros)


def _ln(x, g, b):
    mu = jnp.mean(x, axis=-1, keepdims=True)
    xc = x - mu
    var = jnp.mean(xc * xc, axis=-1, keepdims=True)
    return xc * lax.rsqrt(var + LN_EPS) * g + b


def _bdot(a, b):
    return jnp.dot(a.astype(BF16), b.astype(BF16), preferred_element_type=F32)


def _split2(x):
    hi = x.astype(BF16)
    return hi, (x - hi.astype(F32)).astype(BF16)


def _split3(x):
    hi = x.astype(BF16)
    r1 = x - hi.astype(F32)
    mid = r1.astype(BF16)
    return hi, mid, (r1 - mid.astype(F32)).astype(BF16)


_NN = ((1,), (0,))
_NT = ((1,), (1,))
_TN = ((0,), (0,))


def _mm(a, b, dims, exact):
    dn = (dims, ((), ()))
    if not exact:
        return lax.dot_general(a.astype(BF16), b.astype(BF16), dn, preferred_element_type=F32)
    ah, al = _split2(a)
    bh, bl = _split2(b)
    return (lax.dot_general(ah, bh, dn, preferred_element_type=F32)
            + lax.dot_general(al, bh, dn, preferred_element_type=F32)
            + lax.dot_general(ah, bl, dn, preferred_element_type=F32))


def _wdot(a, w):
    return _mm(a, w, _NN, exact=(w.dtype == F32))


def _act_dtype(w):
    return F32 if w.dtype == F32 else BF16


def _rope_slot(x, c, sa, sb):
    return x * c + pltpu.roll(x, LANES - D_ROPE // 2, 1) * sa + pltpu.roll(x, D_ROPE // 2, 1) * sb


def _even_proj_kernel(x_ref, w1_ref, gq_ref, wuq_ref, gkv_ref, c_ref, sa_ref, sb_ref,
                      q_ref, ckv_ref, kr_ref, xr_ref, yg_ref, *, q_scale):
    u = _wdot(x_ref[...], w1_ref[...])
    cq = u[:, C_CQ:C_CQ + D_CQ]
    cq = cq * lax.rsqrt(jnp.mean(cq * cq, axis=-1, keepdims=True) + RMS_EPS) * gq_ref[...]
    q = _wdot(cq, wuq_ref[...])
    c, sa, sb = c_ref[...], sa_ref[...], sb_ref[...]
    for h in range(MLA_HEADS):
        sl = slice(h * HEAD_SLOT, (h + 1) * HEAD_SLOT)
        q_ref[:, sl] = (_rope_slot(q[:, sl], c, sa, sb) * q_scale).astype(q_ref.dtype)
    ckv = u[:, C_CKV:C_CKV + D_C]
    ckv_ref[...] = ckv * lax.rsqrt(jnp.mean(ckv * ckv, axis=-1, keepdims=True) + RMS_EPS) * gkv_ref[...]
    kr_ref[...] = _rope_slot(u[:, C_KR:C_KR + LANES], c, sa, sb)
    xr_ref[...] = u[:, C_XR:C_XR + D_RNN]
    yg_ref[...] = u[:, C_YG:C_YG + D_RNN]


def _even_proj(x, w1, gq, wuq, gkv, tabs, q_scale):
    n = x.shape[0]
    tm = _row_tile(n, 512)
    row = lambda w: pl.BlockSpec((tm, w), lambda i: (i, 0))
    c, sa, sb = tabs
    return pl.pallas_call(
        functools.partial(_even_proj_kernel, q_scale=q_scale),
        grid=(n // tm,),
        in_specs=[row(D_MODEL), _full(w1.shape), _full(gq.shape), _full(wuq.shape), _full(gkv.shape),
                  row(LANES), row(LANES), row(LANES)],
        out_specs=[row(D_ATT), row(D_C), row(LANES), row(D_RNN), row(D_RNN)],
        out_shape=[jax.ShapeDtypeStruct((n, D_ATT), _act_dtype(w1)), jax.ShapeDtypeStruct((n, D_C), F32),
                   jax.ShapeDtypeStruct((n, LANES), F32), jax.ShapeDtypeStruct((n, D_RNN), F32),
                   jax.ShapeDtypeStruct((n, D_RNN), F32)],
        compiler_params=_cparams(("arbitrary",)),
        name="even_proj",
    )(x, w1, gq, wuq, gkv, c, sa, sb)


def _kv_proj_kernel(ckv_ref, kr_ref, wukv_ref, p_ref, k_ref, v_ref):
    kv = _wdot(ckv_ref[...], wukv_ref[...])
    k_ref[...] = (kv[:, :D_ATT] + _wdot(kr_ref[...], p_ref[...])).astype(k_ref.dtype)
    v_ref[...] = kv[:, D_ATT:].astype(v_ref.dtype)


def _kv_proj(ckv, kr, wukv, place):
    n = ckv.shape[0]
    tm = _row_tile(n, 512)
    row = lambda w: pl.BlockSpec((tm, w), lambda i: (i, 0))
    return pl.pallas_call(
        _kv_proj_kernel,
        grid=(n // tm,),
        in_specs=[row(D_C), row(LANES), _full(wukv.shape), _full(place.shape)],
        out_specs=[row(D_ATT), row(D_ATT)],
        out_shape=[jax.ShapeDtypeStruct((n, D_ATT), _act_dtype(wukv))] * 2,
        compiler_params=_cparams(("arbitrary",)),
        name="kv_proj",
    )(ckv, kr, wukv, place)


def _kv_proj_t_kernel(ckv_ref, kr_ref, wuk_ref, p_ref, wuvt_ref, ones_ref, k_ref, vt_ref):
    ckv = ckv_ref[...].astype(BF16)
    k = jnp.dot(ckv, wuk_ref[...], preferred_element_type=F32) + _bdot(kr_ref[...], p_ref[...])
    k_ref[...] = k.astype(BF16)
    vt = lax.dot_general(wuvt_ref[...], ckv, (_NT, ((), ())), preferred_element_type=F32)
    vt_ref[...] = (vt + ones_ref[...]).astype(BF16)


def _kv_proj_t(ckv, kr, wuk, place, wuvt, ones_col):
    n = ckv.shape[0]
    tm = _row_tile(n, 512)
    row = lambda w: pl.BlockSpec((tm, w), lambda i: (i, 0))
    return pl.pallas_call(
        _kv_proj_t_kernel,
        grid=(n // tm,),
        in_specs=[row(D_C), row(LANES), _full(wuk.shape), _full(place.shape), _full(wuvt.shape),
                  _full(ones_col.shape)],
        out_specs=[row(D_ATT), pl.BlockSpec((D_ATT, tm), lambda i: (0, i))],
        out_shape=[jax.ShapeDtypeStruct((n, D_ATT), BF16), jax.ShapeDtypeStruct((D_ATT, n), BF16)],
        compiler_params=_cparams(("arbitrary",)),
        name="kv_proj_t",
    )(ckv, kr, wuk, place, wuvt, ones_col)


def _flash_kernel(qi_ref, kj_ref, q_ref, k_ref, vt_ref, o_ref, m_scr, acc_scr, *, tq, tk):
    step = pl.program_id(0)
    i = qi_ref[step]
    j = kj_ref[step]

    @pl.when(j == 0)
    def _():
        m_scr[...] = jnp.full(m_scr.shape, NEG, F32)
        acc_scr[...] = jnp.zeros(acc_scr.shape, F32)

    heads = range(MLA_HEADS)
    slots = [slice(h * HEAD_SLOT, (h + 1) * HEAD_SLOT) for h in heads]

    def accumulate(masked):
        st = [lax.dot_general(k_ref[:, sl], q_ref[:, sl], (_NT, ((), ())), preferred_element_type=F32)
              for sl in slots]
        if masked:
            krow = j * tk + lax.broadcasted_iota(jnp.int32, (tk, tq), 0)
            qrow = i * tq + lax.broadcasted_iota(jnp.int32, (tk, tq), 1)
            keep = ((((qrow - ROW0) >> CHUNK_SHIFT) >= ((krow - ROW0) >> CHUNK_SHIFT))
                    & (krow >= PAD_FRONT))
            st = [jnp.where(keep, x, NEG) for x in st]
        m_prev = [m_scr[h:h + 1, :] for h in heads]
        m_new = [jnp.maximum(mp, jnp.max(x, axis=0, keepdims=True)) for mp, x in zip(m_prev, st)]
        alpha = [jnp.exp2(mp - mn) for mp, mn in zip(m_prev, m_new)]
        pt = [jnp.exp2(x - mn).astype(BF16) for x, mn in zip(st, m_new)]
        pv = [jnp.dot(vt_ref[sl, :], x, preferred_element_type=F32) for x, sl in zip(pt, slots)]
        for h in heads:
            acc_scr[slots[h], :] = alpha[h] * acc_scr[slots[h], :] + pv[h]
            m_scr[h:h + 1, :] = m_new[h]

    edge = (j == i) | (j == 0)

    @pl.when(edge)
    def _():
        accumulate(True)

    @pl.when(jnp.logical_not(edge))
    def _():
        accumulate(False)

    @pl.when(j == i)
    def _():
        for sl in slots:
            acc = acc_scr[sl, :]
            o_ref[:, sl] = (acc / acc[D_V:D_V + 1, :]).T.astype(BF16)


def _flash_attention(q, k, vt, tq):
    n = q.shape[0]
    nq = n // tq
    qi = np.concatenate([np.full(i + 1, i, np.int32) for i in range(nq)])
    kj = np.concatenate([np.arange(i + 1, dtype=np.int32) for i in range(nq)])
    grid_spec = pltpu.PrefetchScalarGridSpec(
        num_scalar_prefetch=2,
        grid=(len(qi),),
        in_specs=[pl.BlockSpec((tq, D_ATT), lambda s, qi, kj: (qi[s], 0)),
                  pl.BlockSpec((tq, D_ATT), lambda s, qi, kj: (kj[s], 0)),
                  pl.BlockSpec((D_ATT, tq), lambda s, qi, kj: (0, kj[s]))],
        out_specs=pl.BlockSpec((tq, D_ATT), lambda s, qi, kj: (qi[s], 0)),
        scratch_shapes=[pltpu.VMEM((MLA_HEADS, tq), F32), pltpu.VMEM((D_ATT, tq), F32)],
    )
    return pl.pallas_call(
        functools.partial(_flash_kernel, tq=tq, tk=tq),
        grid_spec=grid_spec,
        out_shape=jax.ShapeDtypeStruct((n, D_ATT), BF16),
        compiler_params=_cparams(("arbitrary",)),
        name="flash_attention",
    )(jnp.asarray(qi), jnp.asarray(kj), q, k, vt)


def _sample_attn_kernel(q_ref, k_ref, v_ref, o_ref, *, n_keys):
    nk = k_ref.shape[1]
    exact = k_ref.dtype == F32
    keep = lax.broadcasted_iota(jnp.int32, (q_ref.shape[1], nk), 1) < n_keys
    for h in range(MLA_HEADS):
        sl = slice(h * HEAD_SLOT, (h + 1) * HEAD_SLOT)
        s = _mm(q_ref[0, :, sl], k_ref[0, :, sl], _NT, exact)
        s = jnp.where(keep, s, NEG)
        p = jnp.exp(s - jnp.max(s, axis=-1, keepdims=True))
        p = p / jnp.sum(p, axis=-1, keepdims=True)
        o_ref[0, :, sl] = _mm(p, v_ref[0, :, sl], _NN, exact).astype(o_ref.dtype)


def _sample_attention(q, k, v, n_keys):
    b, l, _ = q.shape
    nk = k.shape[1]
    return pl.pallas_call(
        functools.partial(_sample_attn_kernel, n_keys=n_keys),
        grid=(b,),
        in_specs=[pl.BlockSpec((1, l, D_ATT), lambda i: (i, 0, 0)),
                  pl.BlockSpec((1, nk, D_ATT), lambda i: (i, 0, 0)),
                  pl.BlockSpec((1, nk, D_ATT), lambda i: (i, 0, 0))],
        out_specs=pl.BlockSpec((1, l, D_ATT), lambda i: (i, 0, 0)),
        out_shape=jax.ShapeDtypeStruct((b, l, D_ATT), q.dtype),
        compiler_params=_cparams(("arbitrary",)),
        name="sample_attention",
    )(q, k, v)


def _expm1(x):
    series = x * (1.0 + x * (0.5 + x * (1.0 / 6.0 + x * (1.0 / 24.0 + x * (1.0 / 120.0)))))
    return jnp.where(jnp.abs(x) < 0.05, series, jnp.exp(x) - 1.0)


def _gelu_tanh(x):
    return 0.5 * x * (1.0 + jnp.tanh(0.7978845608028654 * (x + 0.044715 * x * x * x)))


def _rglru_kernel(xr_ref, yg_ref, cw_ref, cb_ref, wrg_ref, brg_ref, wig_ref, big_ref, sp_ref,
                  buf0_ref, h0_ref, rnn_ref, tailx_ref, tailh_ref, prev_scr, h_scr,
                  *, tm, start, end):
    t = pl.program_id(1)

    @pl.when(t == 0)
    def _():
        prev_scr[...] = buf0_ref[0]
        h_scr[...] = jnp.broadcast_to(h0_ref[0], h_scr.shape)

    x = xr_ref[0]
    ext = jnp.concatenate([prev_scr[...], x], axis=0)
    cw = cw_ref[...]
    xc = cb_ref[...] + cw[CONV_W - 1:CONV_W] * x
    for d in range(1, CONV_W):
        xc = xc + cw[CONV_W - 1 - d:CONV_W - d] * pltpu.roll(ext, d, 0)[SUBLANES:]
    prev_scr[...] = x[tm - SUBLANES:]

    r = jax.nn.sigmoid(_wdot(xc, wrg_ref[...]) + brg_ref[...])
    ig = jax.nn.sigmoid(_wdot(xc, wig_ref[...]) + big_ref[...])
    log_a = -LRU_C * r * sp_ref[...]
    a = jnp.exp(log_a)
    b = jnp.sqrt(-_expm1(2.0 * log_a)) * (ig * xc)
    row = lax.broadcasted_iota(jnp.int32, (tm, D_RNN), 0)
    if start > 0:
        live = (t * tm + row) >= start
        a = jnp.where(live, a, 1.0)
        b = jnp.where(live, b, 0.0)
    d = 1
    while d < tm:
        b = a * jnp.where(row >= d, pltpu.roll(b, d, 0), 0.0) + b
        a = a * jnp.where(row >= d, pltpu.roll(a, d, 0), 1.0)
        d *= 2
    h = a * h_scr[0:1] + b
    h_scr[...] = jnp.broadcast_to(h[tm - 1:tm], h_scr.shape)
    rnn_ref[0] = (h * _gelu_tanh(yg_ref[0])).astype(rnn_ref.dtype)

    t_end = (end - 1) // tm
    el = end - t_end * tm

    @pl.when(t == t_end)
    def _():
        tailx_ref[0] = ext[el:el + SUBLANES]
        tailh_ref[0] = h[el - SUBLANES:el]


def _rglru(xr, yg, cw, cb, wrg, brg, wig, big, sp, buf0, h0, start, end):
    b, l, _ = xr.shape
    tm = _row_tile(l, 512)
    seq = pl.BlockSpec((1, tm, D_RNN), lambda i, t: (i, t, 0))
    per_b = lambda r: pl.BlockSpec((1, r, D_RNN), lambda i, t: (i, 0, 0))
    return pl.pallas_call(
        functools.partial(_rglru_kernel, tm=tm, start=start, end=end),
        grid=(b, l // tm),
        in_specs=[seq, seq, _full(cw.shape), _full(cb.shape), _full(wrg.shape), _full(brg.shape),
                  _full(wig.shape), _full(big.shape), _full(sp.shape), per_b(SUBLANES), per_b(1)],
        out_specs=[seq, per_b(SUBLANES), per_b(SUBLANES)],
        out_shape=[jax.ShapeDtypeStruct((b, l, D_RNN), _act_dtype(wrg)),
                   jax.ShapeDtypeStruct((b, SUBLANES, D_RNN), F32),
                   jax.ShapeDtypeStruct((b, SUBLANES, D_RNN), F32)],
        scratch_shapes=[pltpu.VMEM((SUBLANES, D_RNN), F32), pltpu.VMEM((SUBLANES, D_RNN), F32)],
        compiler_params=_cparams(("arbitrary", "arbitrary")),
        name="rglru",
    )(xr, yg, cw, cb, wrg, brg, wig, big, sp, buf0, h0)


def _mix_out_kernel(attn_ref, rnn_ref, x_ref, wa_ref, wr_ref, g_ref, b_ref, o_ref):
    mix = _wdot(attn_ref[...], wa_ref[...]) + _wdot(rnn_ref[...], wr_ref[...])
    o_ref[...] = _ln(ALPHA * x_ref[...] + mix, g_ref[...], b_ref[...])


def _mix_out(attn, rnn, x, wa, wr, g, b):
    n = x.shape[0]
    tm = _row_tile(n, 512)
    row = lambda w: pl.BlockSpec((tm, w), lambda i: (i, 0))
    return pl.pallas_call(
        _mix_out_kernel,
        grid=(n // tm,),
        in_specs=[row(D_ATT), row(D_RNN), row(D_MODEL), _full(wa.shape), _full(wr.shape),
                  _full(g.shape), _full(b.shape)],
        out_specs=row(D_MODEL),
        out_shape=jax.ShapeDtypeStruct((n, D_MODEL), F32),
        compiler_params=_cparams(("arbitrary",)),
        name="mix_out",
    )(attn, rnn, x, wa, wr, g, b)


def _first_argmax(vals, lane):
    m = jnp.max(vals, axis=-1, keepdims=True)
    idx = jnp.min(jnp.where(vals == m, lane, N_EXPERTS), axis=-1, keepdims=True)
    return m, idx


def _router_top2(x, rw, rb):
    logits = jnp.dot(x, rw, preferred_element_type=F32, precision=lax.Precision.HIGHEST)
    s = jax.nn.sigmoid(logits)
    sel = s + rb
    lane = lax.broadcasted_iota(jnp.int32, sel.shape, 1)
    grp = lane >> 2
    best = None
    g_best = None
    for g in range(N_GROUPS):
        vals = jnp.where(grp == g, sel, NEG)
        m1, i1 = _first_argmax(vals, lane)
        m2, _ = _first_argmax(jnp.where(lane == i1, NEG, vals), lane)
        score = m1 + m2
        if g == 0:
            best, g_best = score, jnp.zeros_like(i1)
        else:
            upd = score > best
            g_best = jnp.where(upd, g, g_best)
            best = jnp.where(upd, score, best)
    vals = jnp.where(grp == g_best, sel, NEG)
    _, i1 = _first_argmax(vals, lane)
    _, i2 = _first_argmax(jnp.where(lane == i1, NEG, vals), lane)
    w1 = jnp.sum(jnp.where(lane == i1, s, 0.0), axis=-1, keepdims=True)
    w2 = jnp.sum(jnp.where(lane == i2, s, 0.0), axis=-1, keepdims=True)
    den = w1 + w2
    return lane, i1, i2, w1 / den, w2 / den


def _router_gate(x, rw, rb):
    lane, i1, i2, g1, g2 = _router_top2(x, rw, rb)
    return jnp.where(lane == i1, g1, 0.0) + jnp.where(lane == i2, g2, 0.0)


def _moe_kernel(x_ref, rw_ref, rb_ref, wg_ref, wu_ref, wd_ref, g_ref, b_ref, o_ref,
                gate_scr, xb_scr, acc_scr):
    e = pl.program_id(1)

    @pl.when(e == 0)
    def _():
        x = x_ref[...]
        gate = _router_gate(x, rw_ref[...], rb_ref[...])
        for k in range(N_EXPERTS):
            gate_scr[k] = jnp.broadcast_to(gate[:, k:k + 1], gate_scr.shape[1:])
        xb_scr[...] = x.astype(BF16)
        acc_scr[...] = jnp.zeros(acc_scr.shape, F32)

    xb = xb_scr[...]
    hg = jnp.dot(xb, wg_ref[0], preferred_element_type=F32)
    hu = jnp.dot(xb, wu_ref[0], preferred_element_type=F32)
    gate_e = gate_scr[e]
    h = jax.nn.silu(hg) * hu * jnp.concatenate([gate_e] * (D_EXPERT // LANES), axis=1)
    acc_scr[...] += jnp.dot(h.astype(BF16), wd_ref[0], preferred_element_type=F32)

    @pl.when(e == N_EXPERTS - 1)
    def _():
        o_ref[...] = _ln(ALPHA * x_ref[...] + acc_scr[...], g_ref[...], b_ref[...])


def _moe(x, rw, rb, wg, wu, wd, g, b):
    n = x.shape[0]
    tm = _row_tile(n, 512)
    row = pl.BlockSpec((tm, D_MODEL), lambda i, e: (i, 0))
    return pl.pallas_call(
        _moe_kernel,
        grid=(n // tm, N_EXPERTS),
        in_specs=[row, _full(rw.shape), _full(rb.shape),
                  pl.BlockSpec((1, D_MODEL, D_EXPERT), lambda i, e: (e, 0, 0)),
                  pl.BlockSpec((1, D_MODEL, D_EXPERT), lambda i, e: (e, 0, 0)),
                  pl.BlockSpec((1, D_EXPERT, D_MODEL), lambda i, e: (e, 0, 0)),
                  _full(g.shape), _full(b.shape)],
        out_specs=row,
        out_shape=jax.ShapeDtypeStruct((n, D_MODEL), F32),
        scratch_shapes=[pltpu.VMEM((N_EXPERTS, tm, LANES), F32), pltpu.VMEM((tm, D_MODEL), BF16),
                        pltpu.VMEM((tm, D_MODEL), F32)],
        compiler_params=_cparams(("arbitrary", "arbitrary")),
        name="moe",
    )(x, rw, rb, wg, wu, wd, g, b)


M_I1, M_I2, M_R1, M_R2, M_G1, M_G2, M_COLS = 0, 1, 2, 3, 4, 5, 8


def _route_kernel(x_ref, rw_ref, rb_ref, meta_ref, cnt_ref, carry_scr, *, tm):
    @pl.when(pl.program_id(0) == 0)
    def _():
        carry_scr[...] = jnp.zeros(carry_scr.shape, F32)

    lane, i1, i2, g1, g2 = _router_top2(x_ref[...], rw_ref[...], rb_ref[...])
    chosen = jnp.where((lane == i1) | (lane == i2), 1.0, 0.0)
    before = (lax.broadcasted_iota(jnp.int32, (tm, tm), 0)
              > lax.broadcasted_iota(jnp.int32, (tm, tm), 1)).astype(BF16)
    seen = jnp.dot(before, chosen.astype(BF16), preferred_element_type=F32) + carry_scr[0:1]
    r1 = jnp.sum(jnp.where(lane == i1, seen, 0.0), axis=-1, keepdims=True)
    r2 = jnp.sum(jnp.where(lane == i2, seen, 0.0), axis=-1, keepdims=True)
    carry_scr[...] = carry_scr[...] + jnp.sum(chosen, axis=0, keepdims=True)
    col = lax.broadcasted_iota(jnp.int32, (tm, M_COLS), 1)
    meta = jnp.zeros((tm, M_COLS), F32)
    for c, val in ((M_I1, i1.astype(F32)), (M_I2, i2.astype(F32)), (M_R1, r1), (M_R2, r2),
                   (M_G1, g1), (M_G2, g2)):
        meta = jnp.where(col == c, val, meta)
    meta_ref[...] = meta
    cnt_ref[...] = carry_scr[...]


def _route(x, rw, rb):
    n = x.shape[0]
    tm = _row_tile(n, 512)
    return pl.pallas_call(
        functools.partial(_route_kernel, tm=tm),
        grid=(n // tm,),
        in_specs=[pl.BlockSpec((tm, D_MODEL), lambda i: (i, 0)), _full(rw.shape), _full(rb.shape)],
        out_specs=[pl.BlockSpec((tm, M_COLS), lambda i: (i, 0)), _full((SUBLANES, N_EXPERTS))],
        out_shape=[jax.ShapeDtypeStruct((n, M_COLS), F32), jax.ShapeDtypeStruct((SUBLANES, N_EXPERTS), F32)],
        scratch_shapes=[pltpu.VMEM((SUBLANES, N_EXPERTS), F32)],
        compiler_params=_cparams(("arbitrary",)),
        name="moe_route",
    )(x, rw, rb)


def _sc_chunk(per_worker):
    for c in (64, 48, 32, 16, 8):
        if per_worker % c == 0:
            return c
    raise ValueError(per_worker)


def _sc_mesh():
    return plsc.VectorSubcoreMesh(core_axis_name="c", subcore_axis_name="s")


def _sc_scatter2(x, idx1, idx2, n_out):
    n, d = x.shape
    per_w = n // SC_WORKERS
    assert per_w * SC_WORKERS == n
    chunk = _sc_chunk(per_w)

    @functools.partial(
        pl.kernel, mesh=_sc_mesh(), out_type=jax.ShapeDtypeStruct((n_out, d), x.dtype),
        scratch_types=[pltpu.VMEM((chunk,), jnp.int32), pltpu.VMEM((chunk,), jnp.int32),
                       pltpu.VMEM((chunk, d), x.dtype), pltpu.SemaphoreType.DMA])
    def scatter(x_hbm, i1_hbm, i2_hbm, out_hbm, i1_v, i2_v, rows_v, sem):
        base = (lax.axis_index("s") * SC_CORES + lax.axis_index("c")) * per_w

        @pl.loop(0, per_w // chunk)
        def _(c):
            off = pl.multiple_of(base + c * chunk, SUBLANES)
            pltpu.sync_copy(i1_hbm.at[pl.ds(off, chunk)], i1_v)
            pltpu.sync_copy(i2_hbm.at[pl.ds(off, chunk)], i2_v)
            pltpu.sync_copy(x_hbm.at[pl.ds(off, chunk)], rows_v)
            pltpu.async_copy(rows_v, out_hbm.at[i1_v], sem).wait()
            pltpu.async_copy(rows_v, out_hbm.at[i2_v], sem).wait()

    return scatter(x, idx1, idx2)


def _sc_gather(y, idx):
    n = idx.shape[0]
    d = y.shape[1]
    per_w = n // SC_WORKERS
    assert per_w * SC_WORKERS == n
    chunk = _sc_chunk(per_w)

    @functools.partial(
        pl.kernel, mesh=_sc_mesh(), out_type=jax.ShapeDtypeStruct((n, d), y.dtype),
        scratch_types=[pltpu.VMEM((chunk,), jnp.int32), pltpu.VMEM((chunk, d), y.dtype),
                       pltpu.SemaphoreType.DMA])
    def gather(y_hbm, idx_hbm, out_hbm, idx_v, rows_v, sem):
        base = (lax.axis_index("s") * SC_CORES + lax.axis_index("c")) * per_w

        @pl.loop(0, per_w // chunk)
        def _(c):
            off = pl.multiple_of(base + c * chunk, SUBLANES)
            pltpu.sync_copy(idx_hbm.at[pl.ds(off, chunk)], idx_v)
            pltpu.async_copy(y_hbm.at[idx_v], rows_v, sem).wait()
            pltpu.sync_copy(rows_v, out_hbm.at[pl.ds(off, chunk)])

    return gather(y, idx)


def _experts_kernel(te_ref, used_ref, x_ref, wg_ref, wu_ref, wd_ref, o_ref):
    @pl.when(pl.program_id(0) < used_ref[0])
    def _():
        xb = x_ref[...].astype(BF16)
        hg = jnp.dot(xb, wg_ref[0], preferred_element_type=F32)
        hu = jnp.dot(xb, wu_ref[0], preferred_element_type=F32)
        h = jax.nn.silu(hg) * hu
        o_ref[...] = jnp.dot(h.astype(BF16), wd_ref[0], preferred_element_type=F32)


def _experts(xg, tile_expert, n_used, wg, wu, wd):
    n_tiles = xg.shape[0] // MOE_TILE
    row = pl.BlockSpec((MOE_TILE, D_MODEL), lambda i, te, used: (i, 0))
    grid_spec = pltpu.PrefetchScalarGridSpec(
        num_scalar_prefetch=2,
        grid=(n_tiles,),
        in_specs=[row,
                  pl.BlockSpec((1, D_MODEL, D_EXPERT), lambda i, te, used: (te[i], 0, 0)),
                  pl.BlockSpec((1, D_MODEL, D_EXPERT), lambda i, te, used: (te[i], 0, 0)),
                  pl.BlockSpec((1, D_EXPERT, D_MODEL), lambda i, te, used: (te[i], 0, 0))],
        out_specs=row,
    )
    return pl.pallas_call(
        _experts_kernel,
        grid_spec=grid_spec,
        out_shape=jax.ShapeDtypeStruct(xg.shape, F32),
        compiler_params=_cparams(("arbitrary",)),
        name="moe_experts",
    )(tile_expert, n_used, xg, wg, wu, wd)


def _combine_kernel(x_ref, y1_ref, y2_ref, meta_ref, g_ref, b_ref, o_ref):
    meta = meta_ref[...]
    moe = meta[:, M_G1:M_G1 + 1] * y1_ref[...] + meta[:, M_G2:M_G2 + 1] * y2_ref[...]
    o_ref[...] = _ln(ALPHA * x_ref[...] + moe, g_ref[...], b_ref[...])


def _combine(x, y1, y2, meta, g, b):
    n = x.shape[0]
    tm = _row_tile(n, 512)
    row = pl.BlockSpec((tm, D_MODEL), lambda i: (i, 0))
    return pl.pallas_call(
        _combine_kernel,
        grid=(n // tm,),
        in_specs=[row, row, row, pl.BlockSpec((tm, M_COLS), lambda i: (i, 0)), _full(g.shape), _full(b.shape)],
        out_specs=row,
        out_shape=jax.ShapeDtypeStruct((n, D_MODEL), F32),
        compiler_params=_cparams(("arbitrary",)),
        name="moe_combine",
    )(x, y1, y2, meta, g, b)


def _moe_sparse(x, rw, rb, wg, wu, wd, g, b):
    n = x.shape[0]
    meta, counts = _route(x, rw, rb)
    cnt = counts[0].astype(jnp.int32)
    padded = (cnt + MOE_TILE - 1) // MOE_TILE * MOE_TILE
    seg_end = jnp.cumsum(padded)
    seg_start = seg_end - padded
    e1, e2 = meta[:, M_I1].astype(jnp.int32), meta[:, M_I2].astype(jnp.int32)
    pos1 = jnp.take(seg_start, e1) + meta[:, M_R1].astype(jnp.int32)
    pos2 = jnp.take(seg_start, e2) + meta[:, M_R2].astype(jnp.int32)
    n_tiles = -(-2 * n // MOE_TILE) + N_EXPERTS
    tile_expert = jnp.minimum(
        jnp.searchsorted(seg_end, jnp.arange(n_tiles, dtype=jnp.int32) * MOE_TILE, side="right"),
        N_EXPERTS - 1).astype(jnp.int32)
    n_used = (seg_end[-1:] // MOE_TILE).astype(jnp.int32)
    xg = _sc_scatter2(x, pos1, pos2, n_tiles * MOE_TILE)
    yg = _experts(xg, tile_expert, n_used, wg, wu, wd)
    return _combine(x, _sc_gather(yg, pos1), _sc_gather(yg, pos2), meta, g, b)


def _head_sum(z, ones):
    hi, lo = _split2(z)
    parts = []
    for g in range(D_MODEL // LANES):
        sl = slice(g * LANES, (g + 1) * LANES)
        parts.append(jnp.dot(hi[:, sl], ones, preferred_element_type=F32)
                     + jnp.dot(lo[:, sl], ones, preferred_element_type=F32))
    return jnp.concatenate(parts, axis=1)


def _rwkv_proj_kernel(x_ref, sh0_ref, mu_ref, wr_ref, wk_ref, wv_ref, w0_ref, w1_ref, w2_ref,
                      a0_ref, a1_ref, a2_ref, g1_ref, g2_ref, kkw_ref, kaw_ref, ones_ref,
                      r_ref, lw_ref, k_ref, v_ref, kk_ref, a_ref, g_ref, prev_scr,
                      *, tm, start, end):
    t = pl.program_id(1)

    @pl.when(t == 0)
    def _():
        prev_scr[...] = jnp.zeros(prev_scr.shape, F32)

    x = x_ref[0]
    ext = jnp.concatenate([prev_scr[...], x], axis=0)
    x_prev = pltpu.roll(ext, 1, 0)[SUBLANES:]
    grow = t * tm + lax.broadcasted_iota(jnp.int32, (tm, D_MODEL), 0)
    x_prev = jnp.where(grow == start, sh0_ref[0], x_prev)
    prev_scr[...] = x[tm - SUBLANES:]
    xx = x_prev - x
    mu = mu_ref[...]
    xr, xw, xk, xv, xa, xg = (x + xx * mu[n:n + 1] for n in range(6))
    r = _bdot(xr, wr_ref[...])
    k = _bdot(xk, wk_ref[...])
    v = _bdot(xv, wv_ref[...])
    log_w = -DECAY_SCALE * jax.nn.sigmoid(w0_ref[...] + _bdot(jnp.tanh(_bdot(xw, w1_ref[...])), w2_ref[...]))
    a = jax.nn.sigmoid(a0_ref[...] + _bdot(_bdot(xa, a1_ref[...]), a2_ref[...]))
    g = _bdot(jax.nn.sigmoid(_bdot(xg, g1_ref[...])), g2_ref[...])
    kk = k * kkw_ref[...]
    norm = jnp.sqrt(_head_sum(kk * kk, ones_ref[...]))
    kk = kk / jnp.maximum(norm, 1e-12)
    k = k * (1.0 + (a - 1.0) * kaw_ref[...])
    live = (grow >= start) & (grow < end)
    r_ref[0] = r
    lw_ref[0] = jnp.where(live, log_w, 0.0)
    k_ref[0] = jnp.where(live, k, 0.0)
    v_ref[0] = v
    kk_ref[0] = jnp.where(live, kk, 0.0)
    a_ref[0] = a
    g_ref[0] = g


def _rwkv_proj(x, sh0, od, start, end):
    b, l, _ = x.shape
    tm = _row_tile(l, 256)
    seq = pl.BlockSpec((1, tm, D_MODEL), lambda i, t: (i, t, 0))
    ws = [od[n] for n in ("mu", "w_r", "w_k", "w_v", "w0", "w1", "w2", "a0", "a1", "a2", "g1", "g2",
                          "k_k", "k_a", "ones")]
    return pl.pallas_call(
        functools.partial(_rwkv_proj_kernel, tm=tm, start=start, end=end),
        grid=(b, l // tm),
        in_specs=[seq, pl.BlockSpec((1, 1, D_MODEL), lambda i, t: (i, 0, 0))] + [_full(w.shape) for w in ws],
        out_specs=[seq] * 7,
        out_shape=[jax.ShapeDtypeStruct((b, l, D_MODEL), F32)] * 7,
        scratch_shapes=[pltpu.VMEM((SUBLANES, D_MODEL), F32)],
        compiler_params=_cparams(("arbitrary", "arbitrary")),
        name="rwkv_proj",
    )(x, sh0, *ws)


def _wkv_kernel(r_ref, lw_ref, k_ref, v_ref, kk_ref, a_ref, s0_ref, o_ref, sout_ref, s_scr,
                *, c, exact):
    t = pl.program_id(1)

    @pl.when(t == 0)
    def _():
        s_scr[...] = s0_ref[0]

    head0 = lax.broadcasted_iota(jnp.int32, (c, LANES), 1) < RWKV_HEAD
    c2 = 2 * c
    row = lax.broadcasted_iota(jnp.int32, (c2, c2), 0)
    col = lax.broadcasted_iota(jnp.int32, (c2, c2), 1)
    row_hi = jnp.where(row >= c, c, 0)
    col_hi = jnp.where(col >= c, c, 0)
    same = row_hi == col_hi
    rr = row - row_hi
    cc = col - col_hi
    strict = same & (rr > cc)
    incl = same & (rr >= cc)
    eye = (row == col).astype(F32)
    tri = (lax.broadcasted_iota(jnp.int32, (c, c), 0) >= lax.broadcasted_iota(jnp.int32, (c, c), 1)).astype(BF16)

    def stack(x):
        return jnp.concatenate([jnp.where(head0, x, 0.0), jnp.where(head0, 0.0, x)], axis=0)

    mm = functools.partial(_mm, exact=exact)
    pairs = range(RWKV_HEADS // 2)
    load = lambda ref: [ref[0, :, p * LANES:(p + 1) * LANES] for p in pairs]
    r, lw, k, v, kk, a = (load(ref) for ref in (r_ref, lw_ref, k_ref, v_ref, kk_ref, a_ref))
    lc = [sum(jnp.dot(tri, part, preferred_element_type=F32) for part in _split3(x)) for x in lw]
    lc_end = [x[c - 1:c] for x in lc]
    b = [x * y for x, y in zip(kk, a)]
    lhs = [jnp.concatenate([stack(-kk[p] * jnp.exp(lc[p] - lw[p])), stack(r[p] * jnp.exp(lc[p]))], axis=0)
           for p in pairs]
    g_inv = [jnp.exp(-x) for x in lc]
    rhs = [jnp.concatenate([stack(b[p] * g_inv[p]), stack(k[p] * g_inv[p])], axis=0) for p in pairs]
    pm = [mm(x, y, _NT) for x, y in zip(lhs, rhs)]
    l_ab = [jnp.where(strict, x[:c2, :c2], 0.0) for x in pm]
    l_ak = [jnp.where(strict, x[:c2, c2:], 0.0) for x in pm]
    m_rb = [jnp.where(incl, x[c2:, :c2], 0.0) for x in pm]
    m_rk = [jnp.where(incl, x[c2:, c2:], 0.0) for x in pm]
    vs = [stack(x) for x in v]
    lakv = [mm(x, y, _NN) for x, y in zip(l_ak, vs)]
    mrkv = [mm(x, y, _NN) for x, y in zip(m_rk, vs)]
    tinv = [eye + x for x in l_ab]
    lp = l_ab
    n = 2
    while n < c:
        lp = [mm(x, x, _NN) for x in lp]
        tinv = [x + mm(x, y, _NN) for x, y in zip(tinv, lp)]
        n *= 2
    s = [s_scr[p] for p in pairs]
    xs = [mm(x, y, _NT) for x, y in zip(lhs, s)]
    u = [mm(tinv[p], xs[p][:c2] + lakv[p], _NN) for p in pairs]
    for p in pairs:
        os_ = xs[p][c2:] + mm(m_rb[p], u[p], _NN) + mrkv[p]
        o_ref[0, :, p * LANES:(p + 1) * LANES] = os_[:c] + os_[c:]
    for p in pairs:
        g_rem = jnp.exp(lc_end[p] - lc[p])
        uv = jnp.concatenate([u[p], vs[p]], axis=0)
        bk = jnp.concatenate([stack(b[p] * g_rem), stack(k[p] * g_rem)], axis=0)
        s_scr[p] = s[p] * jnp.exp(lc_end[p]) + mm(uv, bk, _TN)

    @pl.when(t == pl.num_programs(1) - 1)
    def _():
        sout_ref[0] = s_scr[...]


def _wkv(r, lw, k, v, kk, a, s0, c, exact):
    b, l, _ = r.shape
    seq = pl.BlockSpec((1, c, D_MODEL), lambda i, t: (i, t, 0))
    st = pl.BlockSpec((1, RWKV_HEADS // 2, LANES, LANES), lambda i, t: (i, 0, 0, 0))
    return pl.pallas_call(
        functools.partial(_wkv_kernel, c=c, exact=exact),
        grid=(b, l // c),
        in_specs=[seq] * 6 + [st],
        out_specs=[seq, st],
        out_shape=[jax.ShapeDtypeStruct((b, l, D_MODEL), F32),
                   jax.ShapeDtypeStruct((b, RWKV_HEADS // 2, LANES, LANES), F32)],
        scratch_shapes=[pltpu.VMEM((RWKV_HEADS // 2, LANES, LANES), F32)],
        compiler_params=_cparams(("arbitrary", "arbitrary")),
        name="wkv",
    )(r, lw, k, v, kk, a, s0)


def _rwkv_out_kernel(o_ref, r_ref, k_ref, v_ref, g_ref, x_ref, rk_ref, gng_ref, gnb_ref, wo_ref,
                     ones_ref, lg_ref, lb_ref, y_ref):
    ones = ones_ref[...]
    o = o_ref[...]
    inv = 1.0 / RWKV_HEAD
    mu = _head_sum(o, ones) * inv
    oc = o - mu
    var = _head_sum(oc * oc, ones) * inv
    on = oc * lax.rsqrt(var + GN_EPS) * gng_ref[...] + gnb_ref[...]
    on = on + _head_sum(r_ref[...] * k_ref[...] * rk_ref[...], ones) * v_ref[...]
    out = _bdot(on * g_ref[...], wo_ref[...])
    y_ref[...] = _ln(ALPHA * x_ref[...] + out, lg_ref[...], lb_ref[...])


def _rwkv_out(o, r, k, v, g, x, od, lg, lb):
    n = x.shape[0]
    tm = _row_tile(n, 256)
    row = pl.BlockSpec((tm, D_MODEL), lambda i: (i, 0))
    ws = [od["r_k"], od["ln_g"], od["ln_b"], od["w_o"], od["ones"], lg, lb]
    return pl.pallas_call(
        _rwkv_out_kernel,
        grid=(n // tm,),
        in_specs=[row] * 6 + [_full(w.shape) for w in ws],
        out_specs=row,
        out_shape=jax.ShapeDtypeStruct((n, D_MODEL), F32),
        compiler_params=_cparams(("arbitrary",)),
        name="rwkv_out",
    )(o, r, k, v, g, x, *ws)


def _rope_tables(pos):
    half = D_ROPE // 2
    freq = ROPE_BASE ** (-jnp.arange(half, dtype=F32) / half)
    ang = pos.astype(F32)[:, None] * freq[None, :]
    cos, sin = jnp.cos(ang), jnp.sin(ang)
    n = pos.shape[0]
    ones = jnp.ones((n, D_NOPE), F32)
    zeros = jnp.zeros((n, D_NOPE), F32)
    z16 = jnp.zeros((n, half), F32)
    tail1 = jnp.ones((n, LANES - D_NOPE - D_ROPE), F32)
    tail0 = jnp.zeros((n, LANES - D_NOPE - D_ROPE), F32)
    c = jnp.concatenate([ones, cos, cos, tail1], axis=1)
    sa = jnp.concatenate([zeros, -sin, z16, tail0], axis=1)
    sb = jnp.concatenate([zeros, z16, sin, tail0], axis=1)
    return c, sa, sb


def _slot_cols(w, width):
    k, h, _ = w.shape
    return jnp.pad(w, ((0, 0), (0, 0), (0, HEAD_SLOT - width))).reshape(k, h * HEAD_SLOT)


def _block_diag(w):
    n, c, d = w.shape
    eye = jnp.eye(n, dtype=w.dtype)
    return (eye[:, None, :, None] * w[:, :, None, :]).reshape(n * c, n * d)


def _row2(v):
    return v.reshape(1, -1).astype(F32)


def _prep_even(w_in, g_q, w_uq, g_kv, w_uk, w_uv, conv_w, conv_b, w_rg, b_rg, w_ig, b_ig, lam, w_out):
    off_ckv, off_kr = D_CQ, D_CQ + D_C
    off_xr = off_kr + D_ROPE
    off_y = off_xr + D_RNN
    kr_cols = jnp.pad(w_in[:, off_kr:off_xr], ((0, 0), (D_NOPE, LANES - D_NOPE - D_ROPE)))
    w_in, w_uq, w_uk, w_uv, w_rg, w_ig, w_out = (
        w.astype(F32) for w in (w_in, w_uq, w_uk, w_uv, w_rg, w_ig, w_out))
    w1 = jnp.concatenate([w_in[:, :off_ckv], w_in[:, off_ckv:off_kr], w_in[:, off_xr:off_y],
                          w_in[:, off_y:], kr_cols], axis=1)
    wuq = _slot_cols(w_uq, D_NOPE + D_ROPE)
    wukv = jnp.concatenate([_slot_cols(w_uk, D_NOPE), _slot_cols(w_uv, D_V)], axis=1)
    place = np.zeros((LANES, D_ATT), np.float32)
    for h in range(MLA_HEADS):
        for cidx in range(D_ROPE):
            place[D_NOPE + cidx, h * HEAD_SLOT + D_NOPE + cidx] = 1.0
    wa = jnp.pad(w_out[:MLA_HEADS * D_V].reshape(MLA_HEADS, D_V, D_MODEL),
                 ((0, 0), (0, HEAD_SLOT - D_V), (0, 0))).reshape(D_ATT, D_MODEL)
    wr = w_out[MLA_HEADS * D_V:]
    return dict(
        w1=w1, gq=_row2(g_q), wuq=wuq, gkv=_row2(g_kv), wukv=wukv, place=jnp.asarray(place, F32),
        cw=conv_w.astype(F32), cb=_row2(conv_b), wrg=_block_diag(w_rg), brg=_row2(b_rg),
        wig=_block_diag(w_ig), big=_row2(b_ig), sp=_row2(jax.nn.softplus(-lam.astype(F32))),
        wa=wa, wr=wr)


_EVEN_MATMUL_WEIGHTS = ("w1", "wuq", "wukv", "place", "wrg", "wig", "wa", "wr")


def _single_pass(ev):
    return {n: (w.astype(BF16) if n in _EVEN_MATMUL_WEIGHTS else w) for n, w in ev.items()}


def _prep_odd(mu, w_r, w_k, w_v, w0, w1, w2, a0, a1, a2, g1, g2, k_k, k_a, r_k, ln_g, ln_b, w_o):
    ones = np.zeros((LANES, LANES), np.float32)
    ones[:RWKV_HEAD, :RWKV_HEAD] = 1.0
    ones[RWKV_HEAD:, RWKV_HEAD:] = 1.0
    return dict(
        mu=jnp.pad(mu.astype(F32), ((0, SUBLANES - mu.shape[0]), (0, 0))),
        w_r=w_r.astype(BF16), w_k=w_k.astype(BF16), w_v=w_v.astype(BF16), w0=_row2(w0),
        w1=w1.astype(BF16), w2=w2.astype(BF16), a0=_row2(a0), a1=a1.astype(BF16), a2=a2.astype(BF16),
        g1=g1.astype(BF16), g2=g2.astype(BF16), k_k=_row2(k_k), k_a=_row2(k_a), r_k=_row2(r_k),
        ln_g=_row2(ln_g), ln_b=_row2(ln_b), w_o=w_o.astype(BF16), ones=jnp.asarray(ones, BF16))


def _pair_states(s):
    b = s.shape[0]
    s = s.reshape(b, RWKV_HEADS // 2, 2, RWKV_HEAD, RWKV_HEAD).astype(F32)
    eye = jnp.eye(2, dtype=F32)
    out = s[:, :, :, :, None, :] * eye[None, None, :, None, :, None]
    return out.reshape(b, RWKV_HEADS // 2, LANES, LANES)


def _unpair_states(s):
    b = s.shape[0]
    s = s.reshape(b, RWKV_HEADS // 2, 2, RWKV_HEAD, 2, RWKV_HEAD)
    return jnp.stack([s[:, :, 0, :, 0, :], s[:, :, 1, :, 1, :]], axis=2).reshape(
        b, RWKV_HEADS, RWKV_HEAD, RWKV_HEAD)


def _round_up(n, m):
    return -(-n // m) * m


def kernel(x_prompt, x_sample, cache_ckv, cache_krope, state_conv, state_lru, state_shift, state_wkv,
           meta_tokens, ev_w_in, ev_g_q, ev_w_uq, ev_g_kv, ev_w_uk, ev_w_uv, ev_conv_w, ev_conv_b,
           ev_w_rg, ev_b_rg, ev_w_ig, ev_b_ig, ev_lru_lambda, ev_w_out, od_mu, od_w_r, od_w_k, od_w_v,
           od_w0, od_w1, od_w2, od_a0, od_a1, od_a2, od_g1, od_g2, od_k_k, od_k_a, od_r_k, od_ln_g,
           od_ln_b, od_w_o, ln_g, ln_b, router_w, router_b, exp_w_gate, exp_w_up, exp_w_down):
    assert x_prompt.shape[0] == 1 and x_prompt.shape[2] == D_MODEL
    seq = x_prompt.shape[1]
    assert seq % CHUNK == 0
    bs, ls, _ = x_sample.shape
    past = cache_ckv.shape[2]
    ns = bs * ls
    end = ROW0 + seq
    tp = _round_up(end, 512)

    ev = _prep_even(ev_w_in[0], ev_g_q[0], ev_w_uq[0], ev_g_kv[0], ev_w_uk[0], ev_w_uv[0], ev_conv_w[0],
                    ev_conv_b[0], ev_w_rg[0], ev_b_rg[0], ev_w_ig[0], ev_b_ig[0], ev_lru_lambda[0],
                    ev_w_out[0])
    od = _prep_odd(od_mu[0], od_w_r[0], od_w_k[0], od_w_v[0], od_w0[0], od_w1[0], od_w2[0], od_a0[0],
                   od_a1[0], od_a2[0], od_g1[0], od_g2[0], od_k_k[0], od_k_a[0], od_r_k[0], od_ln_g[0],
                   od_ln_b[0], od_w_o[0])
    rw = router_w.astype(F32)
    rb = _row2(router_b)
    wg, wu, wd = exp_w_gate.astype(BF16), exp_w_up.astype(BF16), exp_w_down.astype(BF16)
    lng = ln_g.astype(F32)[:, :, None, :]
    lnb = ln_b.astype(F32)[:, :, None, :]

    def moe(x, layer):
        sparse = x.shape[0] % (SC_WORKERS * SUBLANES) == 0 and x.shape[0] >= SPARSE_MIN_ROWS
        fn = _moe_sparse if sparse else _moe
        return fn(x, rw, rb, wg[layer], wu[layer], wd[layer], lng[layer, 1], lnb[layer, 1])

    xp = jnp.concatenate([jnp.zeros((PAD_FRONT, D_MODEL), F32), meta_tokens.astype(F32),
                          x_prompt[0].astype(F32), jnp.zeros((tp - end, D_MODEL), F32)], axis=0)
    tabs_p = _rope_tables(jnp.maximum(jnp.arange(tp) - PAD_FRONT, 0))
    evb = _single_pass(ev)
    q_p, ckv_p, kr_p, xr_p, yg_p = _even_proj(xp, evb["w1"], ev["gq"], evb["wuq"], ev["gkv"], tabs_p,
                                              MLA_SCALE * LOG2E)
    ones_col = np.zeros((D_ATT, 1), np.float32)
    ones_col[D_V::HEAD_SLOT] = 1.0
    k_p, vt_p = _kv_proj_t(ckv_p, kr_p, evb["wukv"][:, :D_ATT], evb["place"], evb["wukv"][:, D_ATT:].T,
                           jnp.asarray(ones_col))
    attn_p = _flash_attention(q_p, k_p, vt_p, 512)
    rnn_p, tailx_p, tailh_p = _rglru(
        xr_p[None], yg_p[None], ev["cw"], ev["cb"], evb["wrg"], ev["brg"], evb["wig"], ev["big"], ev["sp"],
        jnp.zeros((1, SUBLANES, D_RNN), F32), jnp.zeros((1, 1, D_RNN), F32), PAD_FRONT, end)
    x1_p = _mix_out(attn_p, rnn_p[0], xp, evb["wa"], evb["wr"], lng[0, 0], lnb[0, 0])
    x2_p = moe(x1_p, 0)

    xs = x_sample.reshape(ns, D_MODEL).astype(F32)
    pos_s = jnp.tile(N_META + past + jnp.arange(ls), bs)
    q_s, ckv_s, kr_s, xr_s, yg_s = _even_proj(xs, ev["w1"], ev["gq"], ev["wuq"], ev["gkv"], _rope_tables(pos_s),
                                              MLA_SCALE)
    _, ckv_m, kr_m, _, _ = _even_proj(meta_tokens.astype(F32), ev["w1"], ev["gq"], ev["wuq"], ev["gkv"],
                                      _rope_tables(jnp.arange(N_META)), MLA_SCALE)
    n_keys = N_META + past + ls
    nk_pad = _round_up(n_keys, LANES)
    meta_ckv = jnp.broadcast_to(ckv_m[None], (bs, N_META, D_C))
    meta_kr = jnp.broadcast_to(kr_m[None], (bs, N_META, LANES))
    cache_kr = jnp.pad(cache_krope[0].astype(F32), ((0, 0), (0, 0), (D_NOPE, LANES - D_NOPE - D_ROPE)))
    all_ckv = jnp.concatenate([meta_ckv, cache_ckv[0].astype(F32), ckv_s.reshape(bs, ls, D_C),
                               jnp.zeros((bs, nk_pad - n_keys, D_C), F32)], axis=1)
    all_kr = jnp.concatenate([meta_kr, cache_kr, kr_s.reshape(bs, ls, LANES),
                              jnp.zeros((bs, nk_pad - n_keys, LANES), F32)], axis=1)
    k_s, v_s = _kv_proj(all_ckv.reshape(bs * nk_pad, D_C), all_kr.reshape(bs * nk_pad, LANES),
                        ev["wukv"], ev["place"])
    attn_s = _sample_attention(q_s.reshape(bs, ls, D_ATT), k_s.reshape(bs, nk_pad, D_ATT),
                               v_s.reshape(bs, nk_pad, D_ATT), n_keys)
    buf0_s = jnp.pad(state_conv[0].astype(F32), ((0, 0), (SUBLANES - (CONV_W - 1), 0), (0, 0)))
    rnn_s, tailx_s, tailh_s = _rglru(
        xr_s.reshape(bs, ls, D_RNN), yg_s.reshape(bs, ls, D_RNN), ev["cw"], ev["cb"], ev["wrg"], ev["brg"],
        ev["wig"], ev["big"], ev["sp"], buf0_s, state_lru[0].astype(F32)[:, None, :], 0, ls)
    x1_s = _mix_out(attn_s.reshape(ns, D_ATT), rnn_s.reshape(ns, D_RNN), xs, ev["wa"], ev["wr"],
                    lng[0, 0], lnb[0, 0])
    x2_s = moe(x1_s, 0)

    r_p, lw_p, kk_in_p, v1_p, kkn_p, a_p, g_p = _rwkv_proj(
        x2_p[None], jnp.zeros((1, 1, D_MODEL), F32), od, PAD_FRONT, end)
    o_p, s_p = _wkv(r_p, lw_p, kk_in_p, v1_p, kkn_p, a_p,
                    jnp.zeros((1, RWKV_HEADS // 2, LANES, LANES), F32), CHUNK, False)
    x3_p = _rwkv_out(o_p[0], r_p[0], kk_in_p[0], v1_p[0], g_p[0], x2_p, od, lng[1, 0], lnb[1, 0])
    x4_p = moe(x3_p, 1)

    x2_s3 = x2_s.reshape(bs, ls, D_MODEL)
    r_s, lw_s, kk_in_s, v1_s, kkn_s, a_s, g_s = _rwkv_proj(
        x2_s3, state_shift[0].astype(F32)[:, None, :], od, 0, ls)
    to_chunk = lambda z: jnp.pad(z, ((0, 0), (0, _round_up(ls, CHUNK) - ls), (0, 0)))
    o_s, s_s = _wkv(*(to_chunk(z) for z in (r_s, lw_s, kk_in_s, v1_s, kkn_s, a_s)),
                    _pair_states(state_wkv[0]), CHUNK, True)
    o_s = o_s[:, :ls]
    flat = lambda z: z.reshape(ns, D_MODEL)
    x3_s = _rwkv_out(flat(o_s), flat(r_s), flat(kk_in_s), flat(v1_s), flat(g_s), x2_s, od,
                     lng[1, 0], lnb[1, 0])
    x4_s = moe(x3_s, 1)

    dt = x_prompt.dtype
    nb = CONV_W - 1
    return (
        x4_p[ROW0:end][None].astype(dt),
        x4_s.reshape(bs, ls, D_MODEL).astype(dt),
        ckv_p[PAD_FRONT:end][None, None].astype(dt),
        kr_p[PAD_FRONT:end, D_NOPE:D_NOPE + D_ROPE][None, None].astype(dt),
        tailx_p[:, SUBLANES - nb:][None].astype(dt),
        tailh_p[:, SUBLANES - 1][None].astype(dt),
        x2_p[end - 1][None, None].astype(dt),
        _unpair_states(s_p)[None].astype(dt),
        ckv_s.reshape(bs, ls, D_C)[None].astype(dt),
        kr_s.reshape(bs, ls, LANES)[:, :, D_NOPE:D_NOPE + D_ROPE][None].astype(dt),
        tailx_s[:, SUBLANES - nb:][None].astype(dt),
        tailh_s[:, SUBLANES - 1][None].astype(dt),
        x2_s3[:, ls - 1][None].astype(dt),
        _unpair_states(s_s)[None].astype(dt),
    )
```

```python
import functools

import numpy as np
import jax
import jax.numpy as jnp
from jax import lax
from jax.experimental import pallas as pl
from jax.experimental.pallas import tpu as pltpu
from jax.experimental.pallas import tpu_sc as plsc

F32 = jnp.float32
BF16 = jnp.bfloat16

D_MODEL = 1024
N_META = 16
CHUNK = 64
CHUNK_SHIFT = 6
LN_EPS = 1e-5
RMS_EPS = 1e-6
DEPTH = 2
ALPHA = (2 * DEPTH) ** 0.25
MLA_HEADS = 8
D_NOPE = 64
D_ROPE = 32
D_V = 64
D_C = 256
D_CQ = 384
ROPE_BASE = 10000.0
MLA_SCALE = (D_NOPE + D_ROPE) ** -0.5
D_RNN = 512
LRU_BLOCKS = 8
LRU_BLOCK_W = D_RNN // LRU_BLOCKS
CONV_W = 4
LRU_C = 8.0
RWKV_HEAD = 64
RWKV_HEADS = D_MODEL // RWKV_HEAD
DECAY_SCALE = float(np.exp(-0.5))
GN_EPS = 64e-5
N_EXPERTS = 16
N_GROUPS = 4
EXPERTS_PER_GROUP = N_EXPERTS // N_GROUPS
D_EXPERT = 512

LANES = 128
SUBLANES = 8
HEAD_SLOT = LANES
D_ATT = MLA_HEADS * HEAD_SLOT
PAD_FRONT = CHUNK - N_META
ROW0 = PAD_FRONT + N_META
NEG = -1e30
LOG2E = 1.4426950408889634
SC_CORES = 2
SC_SUBCORES = 16
SC_WORKERS = SC_CORES * SC_SUBCORES
MOE_TILE = 256
SPARSE_MIN_ROWS = 1024
VMEM_LIMIT = 56 * 1024 * 1024

C_CQ = 0
C_CKV = D_CQ
C_XR = C_CKV + D_C
C_YG = C_XR + D_RNN
C_KR = C_YG + D_RNN
N_COL = C_KR + LANES


def _cparams(sem):
    return pltpu.CompilerParams(dimension_semantics=sem, vmem_limit_bytes=VMEM_LIMIT)


def _row_tile(n, cap):
    for t in (1024, 512, 256, 128, 64, 32, 16, 8):
        if t <= cap and n % t == 0:
            return t
    return n


def _full(shape):
    zeros = (0,) * len(shape)
    return pl.BlockSpec(shape, lambda *_: zeros)


def _ln(x, g, b):
    mu = jnp.mean(x, axis=-1, keepdims=True)
    xc = x - mu
    var = jnp.mean(xc * xc, axis=-1, keepdims=True)
    return xc * lax.rsqrt(var + LN_EPS) * g + b


def _bdot(a, b):
    return jnp.dot(a.astype(BF16), b.astype(BF16), preferred_element_type=F32)


def _split2(x):
    hi = x.astype(BF16)
    return hi, (x - hi.astype(F32)).astype(BF16)


def _split3(x):
    hi = x.astype(BF16)
    r1 = x - hi.astype(F32)
    mid = r1.astype(BF16)
    return hi, mid, (r1 - mid.astype(F32)).astype(BF16)


_NN = ((1,), (0,))
_NT = ((1,), (1,))
_TN = ((0,), (0,))


def _mm(a, b, dims, exact):
    dn = (dims, ((), ()))
    if not exact:
        return lax.dot_general(a.astype(BF16), b.astype(BF16), dn, preferred_element_type=F32)
    ah, al = _split2(a)
    bh, bl = _split2(b)
    return (lax.dot_general(ah, bh, dn, preferred_element_type=F32)
            + lax.dot_general(al, bh, dn, preferred_element_type=F32)
            + lax.dot_general(ah, bl, dn, preferred_element_type=F32))


def _wdot(a, w):
    return _mm(a, w, _NN, exact=(w.dtype == F32))


def _act_dtype(w):
    return F32 if w.dtype == F32 else BF16


def _rope_slot(x, c, sa, sb):
    return x * c + pltpu.roll(x, LANES - D_ROPE // 2, 1) * sa + pltpu.roll(x, D_ROPE // 2, 1) * sb


def _even_proj_kernel(x_ref, w1_ref, gq_ref, wuq_ref, gkv_ref, c_ref, sa_ref, sb_ref,
                      q_ref, ckv_ref, kr_ref, xr_ref, yg_ref, *, q_scale):
    u = _wdot(x_ref[...], w1_ref[...])
    cq = u[:, C_CQ:C_CQ + D_CQ]
    cq = cq * lax.rsqrt(jnp.mean(cq * cq, axis=-1, keepdims=True) + RMS_EPS) * gq_ref[...]
    q = _wdot(cq, wuq_ref[...])
    c, sa, sb = c_ref[...], sa_ref[...], sb_ref[...]
    for h in range(MLA_HEADS):
        sl = slice(h * HEAD_SLOT, (h + 1) * HEAD_SLOT)
        q_ref[:, sl] = (_rope_slot(q[:, sl], c, sa, sb) * q_scale).astype(q_ref.dtype)
    ckv = u[:, C_CKV:C_CKV + D_C]
    ckv_ref[...] = ckv * lax.rsqrt(jnp.mean(ckv * ckv, axis=-1, keepdims=True) + RMS_EPS) * gkv_ref[...]
    kr_ref[...] = _rope_slot(u[:, C_KR:C_KR + LANES], c, sa, sb)
    xr_ref[...] = u[:, C_XR:C_XR + D_RNN]
    yg_ref[...] = u[:, C_YG:C_YG + D_RNN]


def _even_proj(x, w1, gq, wuq, gkv, tabs, q_scale):
    n = x.shape[0]
    tm = _row_tile(n, 512)
    row = lambda w: pl.BlockSpec((tm, w), lambda i: (i, 0))
    c, sa, sb = tabs
    return pl.pallas_call(
        functools.partial(_even_proj_kernel, q_scale=q_scale),
        grid=(n // tm,),
        in_specs=[row(D_MODEL), _full(w1.shape), _full(gq.shape), _full(wuq.shape), _full(gkv.shape),
                  row(LANES), row(LANES), row(LANES)],
        out_specs=[row(D_ATT), row(D_C), row(LANES), row(D_RNN), row(D_RNN)],
        out_shape=[jax.ShapeDtypeStruct((n, D_ATT), _act_dtype(w1)), jax.ShapeDtypeStruct((n, D_C), F32),
                   jax.ShapeDtypeStruct((n, LANES), F32), jax.ShapeDtypeStruct((n, D_RNN), F32),
                   jax.ShapeDtypeStruct((n, D_RNN), F32)],
        compiler_params=_cparams(("arbitrary",)),
        name="even_proj",
    )(x, w1, gq, wuq, gkv, c, sa, sb)


def _kv_proj_kernel(ckv_ref, kr_ref, wukv_ref, p_ref, k_ref, v_ref):
    kv = _wdot(ckv_ref[...], wukv_ref[...])
    k_ref[...] = (kv[:, :D_ATT] + _wdot(kr_ref[...], p_ref[...])).astype(k_ref.dtype)
    v_ref[...] = kv[:, D_ATT:].astype(v_ref.dtype)


def _kv_proj(ckv, kr, wukv, place):
    n = ckv.shape[0]
    tm = _row_tile(n, 512)
    row = lambda w: pl.BlockSpec((tm, w), lambda i: (i, 0))
    return pl.pallas_call(
        _kv_proj_kernel,
        grid=(n // tm,),
        in_specs=[row(D_C), row(LANES), _full(wukv.shape), _full(place.shape)],
        out_specs=[row(D_ATT), row(D_ATT)],
        out_shape=[jax.ShapeDtypeStruct((n, D_ATT), _act_dtype(wukv))] * 2,
        compiler_params=_cparams(("arbitrary",)),
        name="kv_proj",
    )(ckv, kr, wukv, place)


def _kv_proj_t_kernel(ckv_ref, kr_ref, wuk_ref, p_ref, wuvt_ref, ones_ref, k_ref, vt_ref):
    ckv = ckv_ref[...].astype(BF16)
    k = jnp.dot(ckv, wuk_ref[...], preferred_element_type=F32) + _bdot(kr_ref[...], p_ref[...])
    k_ref[...] = k.astype(BF16)
    vt = lax.dot_general(wuvt_ref[...], ckv, (_NT, ((), ())), preferred_element_type=F32)
    vt_ref[...] = (vt + ones_ref[...]).astype(BF16)


def _kv_proj_t(ckv, kr, wuk, place, wuvt, ones_col):
    n = ckv.shape[0]
    tm = _row_tile(n, 512)
    row = lambda w: pl.BlockSpec((tm, w), lambda i: (i, 0))
    return pl.pallas_call(
        _kv_proj_t_kernel,
        grid=(n // tm,),
        in_specs=[row(D_C), row(LANES), _full(wuk.shape), _full(place.shape), _full(wuvt.shape),
                  _full(ones_col.shape)],
        out_specs=[row(D_ATT), pl.BlockSpec((D_ATT, tm), lambda i: (0, i))],
        out_shape=[jax.ShapeDtypeStruct((n, D_ATT), BF16), jax.ShapeDtypeStruct((D_ATT, n), BF16)],
        compiler_params=_cparams(("arbitrary",)),
        name="kv_proj_t",
    )(ckv, kr, wuk, place, wuvt, ones_col)


def _flash_kernel(qi_ref, kj_ref, q_ref, k_ref, vt_ref, o_ref, m_scr, acc_scr, *, tq, tk):
    step = pl.program_id(0)
    i = qi_ref[step]
    j = kj_ref[step]

    @pl.when(j == 0)
    def _():
        m_scr[...] = jnp.full(m_scr.shape, NEG, F32)
        acc_scr[...] = jnp.zeros(acc_scr.shape, F32)

    heads = range(MLA_HEADS)
    slots = [slice(h * HEAD_SLOT, (h + 1) * HEAD_SLOT) for h in heads]

    def accumulate(masked):
        st = [lax.dot_general(k_ref[:, sl], q_ref[:, sl], (_NT, ((), ())), preferred_element_type=F32)
              for sl in slots]
        if masked:
            krow = j * tk + lax.broadcasted_iota(jnp.int32, (tk, tq), 0)
            qrow = i * tq + lax.broadcasted_iota(jnp.int32, (tk, tq), 1)
            keep = ((((qrow - ROW0) >> CHUNK_SHIFT) >= ((krow - ROW0) >> CHUNK_SHIFT))
                    & (krow >= PAD_FRONT))
            st = [jnp.where(keep, x, NEG) for x in st]
        m_prev = [m_scr[h:h + 1, :] for h in heads]
        m_new = [jnp.maximum(mp, jnp.max(x, axis=0, keepdims=True)) for mp, x in zip(m_prev, st)]
        alpha = [jnp.exp2(mp - mn) for mp, mn in zip(m_prev, m_new)]
        pt = [jnp.exp2(x - mn).astype(BF16) for x, mn in zip(st, m_new)]
        pv = [jnp.dot(vt_ref[sl, :], x, preferred_element_type=F32) for x, sl in zip(pt, slots)]
        for h in heads:
            acc_scr[slots[h], :] = alpha[h] * acc_scr[slots[h], :] + pv[h]
            m_scr[h:h + 1, :] = m_new[h]

    edge = (j == i) | (j == 0)

    @pl.when(edge)
    def _():
        accumulate(True)

    @pl.when(jnp.logical_not(edge))
    def _():
        accumulate(False)

    @pl.when(j == i)
    def _():
        for sl in slots:
            acc = acc_scr[sl, :]
            o_ref[:, sl] = (acc / acc[D_V:D_V + 1, :]).T.astype(BF16)


def _flash_attention(q, k, vt, tq):
    n = q.shape[0]
    nq = n // tq
    qi = np.concatenate([np.full(i + 1, i, np.int32) for i in range(nq)])
    kj = np.concatenate([np.arange(i + 1, dtype=np.int32) for i in range(nq)])
    grid_spec = pltpu.PrefetchScalarGridSpec(
        num_scalar_prefetch=2,
        grid=(len(qi),),
        in_specs=[pl.BlockSpec((tq, D_ATT), lambda s, qi, kj: (qi[s], 0)),
                  pl.BlockSpec((tq, D_ATT), lambda s, qi, kj: (kj[s], 0)),
                  pl.BlockSpec((D_ATT, tq), lambda s, qi, kj: (0, kj[s]))],
        out_specs=pl.BlockSpec((tq, D_ATT), lambda s, qi, kj: (qi[s], 0)),
        scratch_shapes=[pltpu.VMEM((MLA_HEADS, tq), F32), pltpu.VMEM((D_ATT, tq), F32)],
    )
    return pl.pallas_call(
        functools.partial(_flash_kernel, tq=tq, tk=tq),
        grid_spec=grid_spec,
        out_shape=jax.ShapeDtypeStruct((n, D_ATT), BF16),
        compiler_params=_cparams(("arbitrary",)),
        name="flash_attention",
    )(jnp.asarray(qi), jnp.asarray(kj), q, k, vt)


def _sample_attn_kernel(q_ref, k_ref, v_ref, o_ref, *, n_keys):
    nk = k_ref.shape[1]
    exact = k_ref.dtype == F32
    keep = lax.broadcasted_iota(jnp.int32, (q_ref.shape[1], nk), 1) < n_keys
    for h in range(MLA_HEADS):
        sl = slice(h * HEAD_SLOT, (h + 1) * HEAD_SLOT)
        s = _mm(q_ref[0, :, sl], k_ref[0, :, sl], _NT, exact)
        s = jnp.where(keep, s, NEG)
        p = jnp.exp(s - jnp.max(s, axis=-1, keepdims=True))
        p = p / jnp.sum(p, axis=-1, keepdims=True)
        o_ref[0, :, sl] = _mm(p, v_ref[0, :, sl], _NN, exact).astype(o_ref.dtype)


def _sample_attention(q, k, v, n_keys):
    b, l, _ = q.shape
    nk = k.shape[1]
    return pl.pallas_call(
        functools.partial(_sample_attn_kernel, n_keys=n_keys),
        grid=(b,),
        in_specs=[pl.BlockSpec((1, l, D_ATT), lambda i: (i, 0, 0)),
                  pl.BlockSpec((1, nk, D_ATT), lambda i: (i, 0, 0)),
                  pl.BlockSpec((1, nk, D_ATT), lambda i: (i, 0, 0))],
        out_specs=pl.BlockSpec((1, l, D_ATT), lambda i: (i, 0, 0)),
        out_shape=jax.ShapeDtypeStruct((b, l, D_ATT), q.dtype),
        compiler_params=_cparams(("arbitrary",)),
        name="sample_attention",
    )(q, k, v)


def _expm1(x):
    series = x * (1.0 + x * (0.5 + x * (1.0 / 6.0 + x * (1.0 / 24.0 + x * (1.0 / 120.0)))))
    return jnp.where(jnp.abs(x) < 0.05, series, jnp.exp(x) - 1.0)


def _gelu_tanh(x):
    return 0.5 * x * (1.0 + jnp.tanh(0.7978845608028654 * (x + 0.044715 * x * x * x)))


def _rglru_kernel(xr_ref, yg_ref, cw_ref, cb_ref, wrg_ref, brg_ref, wig_ref, big_ref, sp_ref,
                  buf0_ref, h0_ref, rnn_ref, tailx_ref, tailh_ref, prev_scr, h_scr,
                  *, tm, start, end):
    t = pl.program_id(1)

    @pl.when(t == 0)
    def _():
        prev_scr[...] = buf0_ref[0]
        h_scr[...] = jnp.broadcast_to(h0_ref[0], h_scr.shape)

    x = xr_ref[0]
    ext = jnp.concatenate([prev_scr[...], x], axis=0)
    cw = cw_ref[...]
    xc = cb_ref[...] + cw[CONV_W - 1:CONV_W] * x
    for d in range(1, CONV_W):
        xc = xc + cw[CONV_W - 1 - d:CONV_W - d] * pltpu.roll(ext, d, 0)[SUBLANES:]
    prev_scr[...] = x[tm - SUBLANES:]

    r = jax.nn.sigmoid(_wdot(xc, wrg_ref[...]) + brg_ref[...])
    ig = jax.nn.sigmoid(_wdot(xc, wig_ref[...]) + big_ref[...])
    log_a = -LRU_C * r * sp_ref[...]
    a = jnp.exp(log_a)
    b = jnp.sqrt(-_expm1(2.0 * log_a)) * (ig * xc)
    row = lax.broadcasted_iota(jnp.int32, (tm, D_RNN), 0)
    if start > 0:
        live = (t * tm + row) >= start
        a = jnp.where(live, a, 1.0)
        b = jnp.where(live, b, 0.0)
    d = 1
    while d < tm:
        b = a * jnp.where(row >= d, pltpu.roll(b, d, 0), 0.0) + b
        a = a * jnp.where(row >= d, pltpu.roll(a, d, 0), 1.0)
        d *= 2
    h = a * h_scr[0:1] + b
    h_scr[...] = jnp.broadcast_to(h[tm - 1:tm], h_scr.shape)
    rnn_ref[0] = (h * _gelu_tanh(yg_ref[0])).astype(rnn_ref.dtype)

    t_end = (end - 1) // tm
    el = end - t_end * tm

    @pl.when(t == t_end)
    def _():
        tailx_ref[0] = ext[el:el + SUBLANES]
        tailh_ref[0] = h[el - SUBLANES:el]


def _rglru(xr, yg, cw, cb, wrg, brg, wig, big, sp, buf0, h0, start, end):
    b, l, _ = xr.shape
    tm = _row_tile(l, 512)
    seq = pl.BlockSpec((1, tm, D_RNN), lambda i, t: (i, t, 0))
    per_b = lambda r: pl.BlockSpec((1, r, D_RNN), lambda i, t: (i, 0, 0))
    return pl.pallas_call(
        functools.partial(_rglru_kernel, tm=tm, start=start, end=end),
        grid=(b, l // tm),
        in_specs=[seq, seq, _full(cw.shape), _full(cb.shape), _full(wrg.shape), _full(brg.shape),
                  _full(wig.shape), _full(big.shape), _full(sp.shape), per_b(SUBLANES), per_b(1)],
        out_specs=[seq, per_b(SUBLANES), per_b(SUBLANES)],
        out_shape=[jax.ShapeDtypeStruct((b, l, D_RNN), _act_dtype(wrg)),
                   jax.ShapeDtypeStruct((b, SUBLANES, D_RNN), F32),
                   jax.ShapeDtypeStruct((b, SUBLANES, D_RNN), F32)],
        scratch_shapes=[pltpu.VMEM((SUBLANES, D_RNN), F32), pltpu.VMEM((SUBLANES, D_RNN), F32)],
        compiler_params=_cparams(("arbitrary", "arbitrary")),
        name="rglru",
    )(xr, yg, cw, cb, wrg, brg, wig, big, sp, buf0, h0)


def _mix_out_kernel(attn_ref, rnn_ref, x_ref, wa_ref, wr_ref, g_ref, b_ref, o_ref):
    mix = _wdot(attn_ref[...], wa_ref[...]) + _wdot(rnn_ref[...], wr_ref[...])
    o_ref[...] = _ln(ALPHA * x_ref[...] + mix, g_ref[...], b_ref[...])


def _mix_out(attn, rnn, x, wa, wr, g, b):
    n = x.shape[0]
    tm = _row_tile(n, 512)
    row = lambda w: pl.BlockSpec((tm, w), lambda i: (i, 0))
    return pl.pallas_call(
        _mix_out_kernel,
        grid=(n // tm,),
        in_specs=[row(D_ATT), row(D_RNN), row(D_MODEL), _full(wa.shape), _full(wr.shape),
                  _full(g.shape), _full(b.shape)],
        out_specs=row(D_MODEL),
        out_shape=jax.ShapeDtypeStruct((n, D_MODEL), F32),
        compiler_params=_cparams(("arbitrary",)),
        name="mix_out",
    )(attn, rnn, x, wa, wr, g, b)


def _first_argmax(vals, lane):
    m = jnp.max(vals, axis=-1, keepdims=True)
    idx = jnp.min(jnp.where(vals == m, lane, N_EXPERTS), axis=-1, keepdims=True)
    return m, idx


def _router_top2(x, rw, rb):
    logits = jnp.dot(x, rw, preferred_element_type=F32, precision=lax.Precision.HIGHEST)
    s = jax.nn.sigmoid(logits)
    sel = s + rb
    lane = lax.broadcasted_iota(jnp.int32, sel.shape, 1)
    grp = lane >> 2
    best = None
    g_best = None
    for g in range(N_GROUPS):
        vals = jnp.where(grp == g, sel, NEG)
        m1, i1 = _first_argmax(vals, lane)
        m2, _ = _first_argmax(jnp.where(lane == i1, NEG, vals), lane)
        score = m1 + m2
        if g == 0:
            best, g_best = score, jnp.zeros_like(i1)
        else:
            upd = score > best
            g_best = jnp.where(upd, g, g_best)
            best = jnp.where(upd, score, best)
    vals = jnp.where(grp == g_best, sel, NEG)
    _, i1 = _first_argmax(vals, lane)
    _, i2 = _first_argmax(jnp.where(lane == i1, NEG, vals), lane)
    w1 = jnp.sum(jnp.where(lane == i1, s, 0.0), axis=-1, keepdims=True)
    w2 = jnp.sum(jnp.where(lane == i2, s, 0.0), axis=-1, keepdims=True)
    den = w1 + w2
    return lane, i1, i2, w1 / den, w2 / den


def _router_gate(x, rw, rb):
    lane, i1, i2, g1, g2 = _router_top2(x, rw, rb)
    return jnp.where(lane == i1, g1, 0.0) + jnp.where(lane == i2, g2, 0.0)


def _moe_kernel(x_ref, rw_ref, rb_ref, wg_ref, wu_ref, wd_ref, g_ref, b_ref, o_ref,
                gate_scr, xb_scr, acc_scr):
    e = pl.program_id(1)

    @pl.when(e == 0)
    def _():
        x = x_ref[...]
        gate = _router_gate(x, rw_ref[...], rb_ref[...])
        for k in range(N_EXPERTS):
            gate_scr[k] = jnp.broadcast_to(gate[:, k:k + 1], gate_scr.shape[1:])
        xb_scr[...] = x.astype(BF16)
        acc_scr[...] = jnp.zeros(acc_scr.shape, F32)

    xb = xb_scr[...]
    hg = jnp.dot(xb, wg_ref[0].astype(BF16), preferred_element_type=F32)
    hu = jnp.dot(xb, wu_ref[0].astype(BF16), preferred_element_type=F32)
    gate_e = gate_scr[e]
    h = jax.nn.silu(hg) * hu * jnp.concatenate([gate_e] * (D_EXPERT // LANES), axis=1)
    acc_scr[...] += jnp.dot(h.astype(BF16), wd_ref[0].astype(BF16), preferred_element_type=F32)

    @pl.when(e == N_EXPERTS - 1)
    def _():
        o_ref[...] = _ln(ALPHA * x_ref[...] + acc_scr[...], g_ref[...], b_ref[...])


def _moe(x, rw, rb, wg, wu, wd, g, b):
    n = x.shape[0]
    tm = _row_tile(n, 512)
    row = pl.BlockSpec((tm, D_MODEL), lambda i, e: (i, 0))
    return pl.pallas_call(
        _moe_kernel,
        grid=(n // tm, N_EXPERTS),
        in_specs=[row, _full(rw.shape), _full(rb.shape),
                  pl.BlockSpec((1, D_MODEL, D_EXPERT), lambda i, e: (e, 0, 0)),
                  pl.BlockSpec((1, D_MODEL, D_EXPERT), lambda i, e: (e, 0, 0)),
                  pl.BlockSpec((1, D_EXPERT, D_MODEL), lambda i, e: (e, 0, 0)),
                  _full(g.shape), _full(b.shape)],
        out_specs=row,
        out_shape=jax.ShapeDtypeStruct((n, D_MODEL), F32),
        scratch_shapes=[pltpu.VMEM((N_EXPERTS, tm, LANES), F32), pltpu.VMEM((tm, D_MODEL), BF16),
                        pltpu.VMEM((tm, D_MODEL), F32)],
        compiler_params=_cparams(("arbitrary", "arbitrary")),
        name="moe",
    )(x, rw, rb, wg, wu, wd, g, b)


M_I1, M_I2, M_R1, M_R2, M_G1, M_G2, M_COLS = 0, 1, 2, 3, 4, 5, 8


def _first_argmax_rows(vals, row):
    m = jnp.max(vals, axis=0, keepdims=True)
    idx = jnp.min(jnp.where(vals == m, row, N_EXPERTS), axis=0, keepdims=True)
    return m, idx


def _route_kernel(x_ref, rwt_ref, rbc_ref, meta_ref, cnt_ref, carry_scr, *, tm):
    @pl.when(pl.program_id(0) == 0)
    def _():
        carry_scr[...] = jnp.zeros(carry_scr.shape, F32)

    logits = lax.dot_general(rwt_ref[...], x_ref[...], (_NT, ((), ())), preferred_element_type=F32,
                             precision=lax.Precision.HIGHEST)
    s = jax.nn.sigmoid(logits)
    sel = s + rbc_ref[...]
    row = lax.broadcasted_iota(jnp.int32, sel.shape, 0)
    grp = row >> 2
    best = None
    g_best = None
    for g in range(N_GROUPS):
        vals = jnp.where(grp == g, sel, NEG)
        m1, i1 = _first_argmax_rows(vals, row)
        m2, _ = _first_argmax_rows(jnp.where(row == i1, NEG, vals), row)
        score = m1 + m2
        if g == 0:
            best, g_best = score, jnp.zeros_like(i1)
        else:
            upd = score > best
            g_best = jnp.where(upd, g, g_best)
            best = jnp.where(upd, score, best)
    vals = jnp.where(grp == g_best, sel, NEG)
    _, i1 = _first_argmax_rows(vals, row)
    _, i2 = _first_argmax_rows(jnp.where(row == i1, NEG, vals), row)
    w1 = jnp.sum(jnp.where(row == i1, s, 0.0), axis=0, keepdims=True)
    w2 = jnp.sum(jnp.where(row == i2, s, 0.0), axis=0, keepdims=True)
    den = w1 + w2

    chosen = jnp.where((row == i1) | (row == i2), 1.0, 0.0)
    earlier = (lax.broadcasted_iota(jnp.int32, (tm, tm), 0)
               < lax.broadcasted_iota(jnp.int32, (tm, tm), 1)).astype(BF16)
    seen = jnp.dot(chosen.astype(BF16), earlier, preferred_element_type=F32) + carry_scr[:, 0:1]
    r1 = jnp.sum(jnp.where(row == i1, seen, 0.0), axis=0, keepdims=True)
    r2 = jnp.sum(jnp.where(row == i2, seen, 0.0), axis=0, keepdims=True)
    carry_scr[...] = carry_scr[...] + jnp.sum(chosen, axis=1, keepdims=True)
    mrow = lax.broadcasted_iota(jnp.int32, (M_COLS, tm), 0)
    meta = jnp.zeros((M_COLS, tm), F32)
    for c, val in ((M_I1, i1.astype(F32)), (M_I2, i2.astype(F32)), (M_R1, r1), (M_R2, r2),
                   (M_G1, w1 / den), (M_G2, w2 / den)):
        meta = jnp.where(mrow == c, val, meta)
    meta_ref[...] = meta
    cnt_ref[...] = carry_scr[...]


def _route(x, rwt, rbc):
    n = x.shape[0]
    tm = _row_tile(n, 512)
    return pl.pallas_call(
        functools.partial(_route_kernel, tm=tm),
        grid=(n // tm,),
        in_specs=[pl.BlockSpec((tm, D_MODEL), lambda i: (i, 0)), _full(rwt.shape), _full(rbc.shape)],
        out_specs=[pl.BlockSpec((M_COLS, tm), lambda i: (0, i)), _full((N_EXPERTS, LANES))],
        out_shape=[jax.ShapeDtypeStruct((M_COLS, n), F32), jax.ShapeDtypeStruct((N_EXPERTS, LANES), F32)],
        scratch_shapes=[pltpu.VMEM((N_EXPERTS, LANES), F32)],
        compiler_params=_cparams(("arbitrary",)),
        name="moe_route",
    )(x, rwt, rbc)


def _sc_chunk(per_worker):
    for c in (64, 48, 32, 16, 8):
        if per_worker % c == 0:
            return c
    raise ValueError(per_worker)


def _sc_mesh():
    return plsc.VectorSubcoreMesh(core_axis_name="c", subcore_axis_name="s")


def _sc_scatter2(x, idx1, idx2, n_out):
    n, d = x.shape
    per_w = n // SC_WORKERS
    assert per_w * SC_WORKERS == n
    chunk = _sc_chunk(per_w)

    @functools.partial(
        pl.kernel, mesh=_sc_mesh(), out_type=jax.ShapeDtypeStruct((n_out, d), x.dtype),
        scratch_types=[pltpu.VMEM((chunk,), jnp.int32), pltpu.VMEM((chunk,), jnp.int32),
                       pltpu.VMEM((chunk, d), x.dtype), pltpu.SemaphoreType.DMA])
    def scatter(x_hbm, i1_hbm, i2_hbm, out_hbm, i1_v, i2_v, rows_v, sem):
        base = (lax.axis_index("s") * SC_CORES + lax.axis_index("c")) * per_w

        @pl.loop(0, per_w // chunk)
        def _(c):
            off = pl.multiple_of(base + c * chunk, SUBLANES)
            pltpu.sync_copy(i1_hbm.at[pl.ds(off, chunk)], i1_v)
            pltpu.sync_copy(i2_hbm.at[pl.ds(off, chunk)], i2_v)
            pltpu.sync_copy(x_hbm.at[pl.ds(off, chunk)], rows_v)
            pltpu.async_copy(rows_v, out_hbm.at[i1_v], sem).wait()
            pltpu.async_copy(rows_v, out_hbm.at[i2_v], sem).wait()

    return scatter(x, idx1, idx2)


def _sc_gather(y, idx):
    n = idx.shape[0]
    d = y.shape[1]
    per_w = n // SC_WORKERS
    assert per_w * SC_WORKERS == n
    chunk = _sc_chunk(per_w)

    @functools.partial(
        pl.kernel, mesh=_sc_mesh(), out_type=jax.ShapeDtypeStruct((n, d), y.dtype),
        scratch_types=[pltpu.VMEM((chunk,), jnp.int32), pltpu.VMEM((chunk, d), y.dtype),
                       pltpu.SemaphoreType.DMA])
    def gather(y_hbm, idx_hbm, out_hbm, idx_v, rows_v, sem):
        base = (lax.axis_index("s") * SC_CORES + lax.axis_index("c")) * per_w

        @pl.loop(0, per_w // chunk)
        def _(c):
            off = pl.multiple_of(base + c * chunk, SUBLANES)
            pltpu.sync_copy(idx_hbm.at[pl.ds(off, chunk)], idx_v)
            pltpu.async_copy(y_hbm.at[idx_v], rows_v, sem).wait()
            pltpu.sync_copy(rows_v, out_hbm.at[pl.ds(off, chunk)])

    return gather(y, idx)


def _experts_kernel(te_ref, used_ref, x_ref, wg_ref, wu_ref, wd_ref, o_ref):
    @pl.when(pl.program_id(0) < used_ref[0])
    def _():
        xb = x_ref[...].astype(BF16)
        hg = jnp.dot(xb, wg_ref[0].astype(BF16), preferred_element_type=F32)
        hu = jnp.dot(xb, wu_ref[0].astype(BF16), preferred_element_type=F32)
        h = jax.nn.silu(hg) * hu
        o_ref[...] = jnp.dot(h.astype(BF16), wd_ref[0].astype(BF16), preferred_element_type=F32)


def _experts(xg, tile_expert, n_used, wg, wu, wd):
    n_tiles = xg.shape[0] // MOE_TILE
    row = pl.BlockSpec((MOE_TILE, D_MODEL), lambda i, te, used: (i, 0))
    grid_spec = pltpu.PrefetchScalarGridSpec(
        num_scalar_prefetch=2,
        grid=(n_tiles,),
        in_specs=[row,
                  pl.BlockSpec((1, D_MODEL, D_EXPERT), lambda i, te, used: (te[i], 0, 0)),
                  pl.BlockSpec((1, D_MODEL, D_EXPERT), lambda i, te, used: (te[i], 0, 0)),
                  pl.BlockSpec((1, D_EXPERT, D_MODEL), lambda i, te, used: (te[i], 0, 0))],
        out_specs=row,
    )
    return pl.pallas_call(
        _experts_kernel,
        grid_spec=grid_spec,
        out_shape=jax.ShapeDtypeStruct(xg.shape, F32),
        compiler_params=_cparams(("arbitrary",)),
        name="moe_experts",
    )(tile_expert, n_used, xg, wg, wu, wd)


def _combine_kernel(x_ref, y1_ref, y2_ref, meta_ref, g_ref, b_ref, o_ref):
    meta = meta_ref[...]
    moe = meta[:, M_G1:M_G1 + 1] * y1_ref[...] + meta[:, M_G2:M_G2 + 1] * y2_ref[...]
    o_ref[...] = _ln(ALPHA * x_ref[...] + moe, g_ref[...], b_ref[...])


def _combine(x, y1, y2, meta, g, b):
    n = x.shape[0]
    tm = _row_tile(n, 512)
    row = pl.BlockSpec((tm, D_MODEL), lambda i: (i, 0))
    return pl.pallas_call(
        _combine_kernel,
        grid=(n // tm,),
        in_specs=[row, row, row, pl.BlockSpec((tm, M_COLS), lambda i: (i, 0)), _full(g.shape), _full(b.shape)],
        out_specs=row,
        out_shape=jax.ShapeDtypeStruct((n, D_MODEL), F32),
        compiler_params=_cparams(("arbitrary",)),
        name="moe_combine",
    )(x, y1, y2, meta, g, b)


def _moe_sparse(x, rw, rb, wg, wu, wd, g, b):
    n = x.shape[0]
    meta_t, counts = _route(x, rw.T, rb.reshape(N_EXPERTS, 1))
    cnt = counts[:, 0].astype(jnp.int32)
    padded = (cnt + MOE_TILE - 1) // MOE_TILE * MOE_TILE
    seg_end = jnp.cumsum(padded)
    seg_start = seg_end - padded
    experts = jnp.arange(N_EXPERTS, dtype=jnp.int32)[:, None]
    start_of = lambda e: jnp.sum(jnp.where(experts == e[None, :], seg_start[:, None], 0), axis=0)
    e1, e2 = meta_t[M_I1].astype(jnp.int32), meta_t[M_I2].astype(jnp.int32)
    pos1 = start_of(e1) + meta_t[M_R1].astype(jnp.int32)
    pos2 = start_of(e2) + meta_t[M_R2].astype(jnp.int32)
    meta = meta_t.T
    n_tiles = -(-2 * n // MOE_TILE) + N_EXPERTS
    tile_start = jnp.arange(n_tiles, dtype=jnp.int32) * MOE_TILE
    tile_expert = jnp.minimum(jnp.sum(tile_start[:, None] >= seg_end[None, :], axis=1),
                              N_EXPERTS - 1).astype(jnp.int32)
    n_used = (seg_end[-1:] // MOE_TILE).astype(jnp.int32)
    xg = _sc_scatter2(x, pos1, pos2, n_tiles * MOE_TILE)
    yg = _experts(xg, tile_expert, n_used, wg, wu, wd)
    return _combine(x, _sc_gather(yg, pos1), _sc_gather(yg, pos2), meta, g, b)


def _head_sum(z, ones):
    hi, lo = _split2(z)
    parts = []
    for g in range(D_MODEL // LANES):
        sl = slice(g * LANES, (g + 1) * LANES)
        parts.append(jnp.dot(hi[:, sl], ones, preferred_element_type=F32)
                     + jnp.dot(lo[:, sl], ones, preferred_element_type=F32))
    return jnp.concatenate(parts, axis=1)


def _rwkv_proj_kernel(x_ref, sh0_ref, mu_ref, wr_ref, wk_ref, wv_ref, w0_ref, w1_ref, w2_ref,
                      a0_ref, a1_ref, a2_ref, g1_ref, g2_ref, kkw_ref, kaw_ref, ones_ref,
                      r_ref, lw_ref, k_ref, v_ref, kk_ref, a_ref, g_ref, prev_scr,
                      *, tm, start, end):
    t = pl.program_id(1)

    @pl.when(t == 0)
    def _():
        prev_scr[...] = jnp.zeros(prev_scr.shape, F32)

    x = x_ref[0]
    ext = jnp.concatenate([prev_scr[...], x], axis=0)
    x_prev = pltpu.roll(ext, 1, 0)[SUBLANES:]
    grow = t * tm + lax.broadcasted_iota(jnp.int32, (tm, D_MODEL), 0)
    x_prev = jnp.where(grow == start, sh0_ref[0], x_prev)
    prev_scr[...] = x[tm - SUBLANES:]
    xx = x_prev - x
    mu = mu_ref[...]
    xr, xw, xk, xv, xa, xg = (x + xx * mu[n:n + 1] for n in range(6))
    r = _bdot(xr, wr_ref[...])
    k = _bdot(xk, wk_ref[...])
    v = _bdot(xv, wv_ref[...])
    log_w = -DECAY_SCALE * jax.nn.sigmoid(w0_ref[...] + _bdot(jnp.tanh(_bdot(xw, w1_ref[...])), w2_ref[...]))
    a = jax.nn.sigmoid(a0_ref[...] + _bdot(_bdot(xa, a1_ref[...]), a2_ref[...]))
    g = _bdot(jax.nn.sigmoid(_bdot(xg, g1_ref[...])), g2_ref[...])
    kk = k * kkw_ref[...]
    norm = jnp.sqrt(_head_sum(kk * kk, ones_ref[...]))
    kk = kk / jnp.maximum(norm, 1e-12)
    k = k * (1.0 + (a - 1.0) * kaw_ref[...])
    live = (grow >= start) & (grow < end)
    r_ref[0] = r
    lw_ref[0] = jnp.where(live, log_w, 0.0)
    k_ref[0] = jnp.where(live, k, 0.0)
    v_ref[0] = v
    kk_ref[0] = jnp.where(live, kk, 0.0)
    a_ref[0] = a
    g_ref[0] = g


def _rwkv_proj(x, sh0, od, start, end):
    b, l, _ = x.shape
    tm = _row_tile(l, 256)
    seq = pl.BlockSpec((1, tm, D_MODEL), lambda i, t: (i, t, 0))
    ws = [od[n] for n in ("mu", "w_r", "w_k", "w_v", "w0", "w1", "w2", "a0", "a1", "a2", "g1", "g2",
                          "k_k", "k_a", "ones")]
    return pl.pallas_call(
        functools.partial(_rwkv_proj_kernel, tm=tm, start=start, end=end),
        grid=(b, l // tm),
        in_specs=[seq, pl.BlockSpec((1, 1, D_MODEL), lambda i, t: (i, 0, 0))] + [_full(w.shape) for w in ws],
        out_specs=[seq] * 7,
        out_shape=[jax.ShapeDtypeStruct((b, l, D_MODEL), F32)] * 7,
        scratch_shapes=[pltpu.VMEM((SUBLANES, D_MODEL), F32)],
        compiler_params=_cparams(("arbitrary", "arbitrary")),
        name="rwkv_proj",
    )(x, sh0, *ws)


def _wkv_kernel(r_ref, lw_ref, k_ref, v_ref, kk_ref, a_ref, s0_ref, o_ref, sout_ref, s_scr,
                *, c, exact):
    t = pl.program_id(1)

    @pl.when(t == 0)
    def _():
        s_scr[...] = s0_ref[0]

    head0 = lax.broadcasted_iota(jnp.int32, (c, LANES), 1) < RWKV_HEAD
    c2 = 2 * c
    row = lax.broadcasted_iota(jnp.int32, (c2, c2), 0)
    col = lax.broadcasted_iota(jnp.int32, (c2, c2), 1)
    row_hi = jnp.where(row >= c, c, 0)
    col_hi = jnp.where(col >= c, c, 0)
    same = row_hi == col_hi
    rr = row - row_hi
    cc = col - col_hi
    strict = same & (rr > cc)
    incl = same & (rr >= cc)
    eye = (row == col).astype(F32)
    tri = (lax.broadcasted_iota(jnp.int32, (c, c), 0) >= lax.broadcasted_iota(jnp.int32, (c, c), 1)).astype(BF16)

    def stack(x):
        return jnp.concatenate([jnp.where(head0, x, 0.0), jnp.where(head0, 0.0, x)], axis=0)

    mm = functools.partial(_mm, exact=exact)
    pairs = range(RWKV_HEADS // 2)
    load = lambda ref: [ref[0, :, p * LANES:(p + 1) * LANES] for p in pairs]
    r, lw, k, v, kk, a = (load(ref) for ref in (r_ref, lw_ref, k_ref, v_ref, kk_ref, a_ref))
    lc = [sum(jnp.dot(tri, part, preferred_element_type=F32) for part in _split3(x)) for x in lw]
    lc_end = [x[c - 1:c] for x in lc]
    b = [x * y for x, y in zip(kk, a)]
    lhs = [jnp.concatenate([stack(-kk[p] * jnp.exp(lc[p] - lw[p])), stack(r[p] * jnp.exp(lc[p]))], axis=0)
           for p in pairs]
    g_inv = [jnp.exp(-x) for x in lc]
    rhs = [jnp.concatenate([stack(b[p] * g_inv[p]), stack(k[p] * g_inv[p])], axis=0) for p in pairs]
    pm = [mm(x, y, _NT) for x, y in zip(lhs, rhs)]
    l_ab = [jnp.where(strict, x[:c2, :c2], 0.0) for x in pm]
    l_ak = [jnp.where(strict, x[:c2, c2:], 0.0) for x in pm]
    m_rb = [jnp.where(incl, x[c2:, :c2], 0.0) for x in pm]
    m_rk = [jnp.where(incl, x[c2:, c2:], 0.0) for x in pm]
    vs = [stack(x) for x in v]
    lakv = [mm(x, y, _NN) for x, y in zip(l_ak, vs)]
    mrkv = [mm(x, y, _NN) for x, y in zip(m_rk, vs)]
    tinv = [eye + x for x in l_ab]
    lp = l_ab
    n = 2
    while n < c:
        lp = [mm(x, x, _NN) for x in lp]
        tinv = [x + mm(x, y, _NN) for x, y in zip(tinv, lp)]
        n *= 2
    s = [s_scr[p] for p in pairs]
    xs = [mm(x, y, _NT) for x, y in zip(lhs, s)]
    u = [mm(tinv[p], xs[p][:c2] + lakv[p], _NN) for p in pairs]
    for p in pairs:
        os_ = xs[p][c2:] + mm(m_rb[p], u[p], _NN) + mrkv[p]
        o_ref[0, :, p * LANES:(p + 1) * LANES] = os_[:c] + os_[c:]
    for p in pairs:
        g_rem = jnp.exp(lc_end[p] - lc[p])
        uv = jnp.concatenate([u[p], vs[p]], axis=0)
        bk = jnp.concatenate([stack(b[p] * g_rem), stack(k[p] * g_rem)], axis=0)
        s_scr[p] = s[p] * jnp.exp(lc_end[p]) + mm(uv, bk, _TN)

    @pl.when(t == pl.num_programs(1) - 1)
    def _():
        sout_ref[0] = s_scr[...]


def _wkv(r, lw, k, v, kk, a, s0, c, exact):
    b, l, _ = r.shape
    seq = pl.BlockSpec((1, c, D_MODEL), lambda i, t: (i, t, 0))
    st = pl.BlockSpec((1, RWKV_HEADS // 2, LANES, LANES), lambda i, t: (i, 0, 0, 0))
    return pl.pallas_call(
        functools.partial(_wkv_kernel, c=c, exact=exact),
        grid=(b, l // c),
        in_specs=[seq] * 6 + [st],
        out_specs=[seq, st],
        out_shape=[jax.ShapeDtypeStruct((b, l, D_MODEL), F32),
                   jax.ShapeDtypeStruct((b, RWKV_HEADS // 2, LANES, LANES), F32)],
        scratch_shapes=[pltpu.VMEM((RWKV_HEADS // 2, LANES, LANES), F32)],
        compiler_params=_cparams(("arbitrary", "arbitrary")),
        name="wkv",
    )(r, lw, k, v, kk, a, s0)


def _rwkv_out_kernel(o_ref, r_ref, k_ref, v_ref, g_ref, x_ref, rk_ref, gng_ref, gnb_ref, wo_ref,
                     ones_ref, lg_ref, lb_ref, y_ref):
    ones = ones_ref[...]
    o = o_ref[...]
    inv = 1.0 / RWKV_HEAD
    mu = _head_sum(o, ones) * inv
    oc = o - mu
    var = _head_sum(oc * oc, ones) * inv
    on = oc * lax.rsqrt(var + GN_EPS) * gng_ref[...] + gnb_ref[...]
    on = on + _head_sum(r_ref[...] * k_ref[...] * rk_ref[...], ones) * v_ref[...]
    out = _bdot(on * g_ref[...], wo_ref[...])
    y_ref[...] = _ln(ALPHA * x_ref[...] + out, lg_ref[...], lb_ref[...])


def _rwkv_out(o, r, k, v, g, x, od, lg, lb):
    n = x.shape[0]
    tm = _row_tile(n, 256)
    row = pl.BlockSpec((tm, D_MODEL), lambda i: (i, 0))
    ws = [od["r_k"], od["ln_g"], od["ln_b"], od["w_o"], od["ones"], lg, lb]
    return pl.pallas_call(
        _rwkv_out_kernel,
        grid=(n // tm,),
        in_specs=[row] * 6 + [_full(w.shape) for w in ws],
        out_specs=row,
        out_shape=jax.ShapeDtypeStruct((n, D_MODEL), F32),
        compiler_params=_cparams(("arbitrary",)),
        name="rwkv_out",
    )(o, r, k, v, g, x, *ws)


def _rope_tables(pos):
    half = D_ROPE // 2
    freq = ROPE_BASE ** (-jnp.arange(half, dtype=F32) / half)
    ang = pos.astype(F32)[:, None] * freq[None, :]
    cos, sin = jnp.cos(ang), jnp.sin(ang)
    n = pos.shape[0]
    ones = jnp.ones((n, D_NOPE), F32)
    zeros = jnp.zeros((n, D_NOPE), F32)
    z16 = jnp.zeros((n, half), F32)
    tail1 = jnp.ones((n, LANES - D_NOPE - D_ROPE), F32)
    tail0 = jnp.zeros((n, LANES - D_NOPE - D_ROPE), F32)
    c = jnp.concatenate([ones, cos, cos, tail1], axis=1)
    sa = jnp.concatenate([zeros, -sin, z16, tail0], axis=1)
    sb = jnp.concatenate([zeros, z16, sin, tail0], axis=1)
    return c, sa, sb


def _slot_cols(w, width):
    k, h, _ = w.shape
    return jnp.pad(w, ((0, 0), (0, 0), (0, HEAD_SLOT - width))).reshape(k, h * HEAD_SLOT)


def _block_diag(w):
    n, c, d = w.shape
    eye = jnp.eye(n, dtype=w.dtype)
    return (eye[:, None, :, None] * w[:, :, None, :]).reshape(n * c, n * d)


def _row2(v):
    return v.reshape(1, -1).astype(F32)


def _prep_even(w_in, g_q, w_uq, g_kv, w_uk, w_uv, conv_w, conv_b, w_rg, b_rg, w_ig, b_ig, lam, w_out):
    off_ckv, off_kr = D_CQ, D_CQ + D_C
    off_xr = off_kr + D_ROPE
    off_y = off_xr + D_RNN
    kr_cols = jnp.pad(w_in[:, off_kr:off_xr], ((0, 0), (D_NOPE, LANES - D_NOPE - D_ROPE)))
    w_in, w_uq, w_uk, w_uv, w_rg, w_ig, w_out = (
        w.astype(F32) for w in (w_in, w_uq, w_uk, w_uv, w_rg, w_ig, w_out))
    w1 = jnp.concatenate([w_in[:, :off_ckv], w_in[:, off_ckv:off_kr], w_in[:, off_xr:off_y],
                          w_in[:, off_y:], kr_cols], axis=1)
    wuq = _slot_cols(w_uq, D_NOPE + D_ROPE)
    wukv = jnp.concatenate([_slot_cols(w_uk, D_NOPE), _slot_cols(w_uv, D_V)], axis=1)
    place = np.zeros((LANES, D_ATT), np.float32)
    for h in range(MLA_HEADS):
        for cidx in range(D_ROPE):
            place[D_NOPE + cidx, h * HEAD_SLOT + D_NOPE + cidx] = 1.0
    wa = jnp.pad(w_out[:MLA_HEADS * D_V].reshape(MLA_HEADS, D_V, D_MODEL),
                 ((0, 0), (0, HEAD_SLOT - D_V), (0, 0))).reshape(D_ATT, D_MODEL)
    wr = w_out[MLA_HEADS * D_V:]
    return dict(
        w1=w1, gq=_row2(g_q), wuq=wuq, gkv=_row2(g_kv), wukv=wukv, place=jnp.asarray(place, F32),
        cw=conv_w.astype(F32), cb=_row2(conv_b), wrg=_block_diag(w_rg), brg=_row2(b_rg),
        wig=_block_diag(w_ig), big=_row2(b_ig), sp=_row2(jax.nn.softplus(-lam.astype(F32))),
        wa=wa, wr=wr)


_EVEN_MATMUL_WEIGHTS = ("w1", "wuq", "wukv", "place", "wrg", "wig", "wa", "wr")


def _single_pass(ev):
    return {n: (w.astype(BF16) if n in _EVEN_MATMUL_WEIGHTS else w) for n, w in ev.items()}


def _prep_odd(mu, w_r, w_k, w_v, w0, w1, w2, a0, a1, a2, g1, g2, k_k, k_a, r_k, ln_g, ln_b, w_o):
    ones = np.zeros((LANES, LANES), np.float32)
    ones[:RWKV_HEAD, :RWKV_HEAD] = 1.0
    ones[RWKV_HEAD:, RWKV_HEAD:] = 1.0
    return dict(
        mu=jnp.pad(mu.astype(F32), ((0, SUBLANES - mu.shape[0]), (0, 0))),
        w_r=w_r.astype(BF16), w_k=w_k.astype(BF16), w_v=w_v.astype(BF16), w0=_row2(w0),
        w1=w1.astype(BF16), w2=w2.astype(BF16), a0=_row2(a0), a1=a1.astype(BF16), a2=a2.astype(BF16),
        g1=g1.astype(BF16), g2=g2.astype(BF16), k_k=_row2(k_k), k_a=_row2(k_a), r_k=_row2(r_k),
        ln_g=_row2(ln_g), ln_b=_row2(ln_b), w_o=w_o.astype(BF16), ones=jnp.asarray(ones, BF16))


def _pair_states(s):
    b = s.shape[0]
    s = s.reshape(b, RWKV_HEADS // 2, 2, RWKV_HEAD, RWKV_HEAD).astype(F32)
    eye = jnp.eye(2, dtype=F32)
    out = s[:, :, :, :, None, :] * eye[None, None, :, None, :, None]
    return out.reshape(b, RWKV_HEADS // 2, LANES, LANES)


def _unpair_states(s):
    b = s.shape[0]
    s = s.reshape(b, RWKV_HEADS // 2, 2, RWKV_HEAD, 2, RWKV_HEAD)
    return jnp.stack([s[:, :, 0, :, 0, :], s[:, :, 1, :, 1, :]], axis=2).reshape(
        b, RWKV_HEADS, RWKV_HEAD, RWKV_HEAD)


def _round_up(n, m):
    return -(-n // m) * m


def kernel(x_prompt, x_sample, cache_ckv, cache_krope, state_conv, state_lru, state_shift, state_wkv,
           meta_tokens, ev_w_in, ev_g_q, ev_w_uq, ev_g_kv, ev_w_uk, ev_w_uv, ev_conv_w, ev_conv_b,
           ev_w_rg, ev_b_rg, ev_w_ig, ev_b_ig, ev_lru_lambda, ev_w_out, od_mu, od_w_r, od_w_k, od_w_v,
           od_w0, od_w1, od_w2, od_a0, od_a1, od_a2, od_g1, od_g2, od_k_k, od_k_a, od_r_k, od_ln_g,
           od_ln_b, od_w_o, ln_g, ln_b, router_w, router_b, exp_w_gate, exp_w_up, exp_w_down):
    assert x_prompt.shape[0] == 1 and x_prompt.shape[2] == D_MODEL
    seq = x_prompt.shape[1]
    assert seq % CHUNK == 0
    bs, ls, _ = x_sample.shape
    past = cache_ckv.shape[2]
    ns = bs * ls
    end = ROW0 + seq
    tp = _round_up(end, 512)

    ev = _prep_even(ev_w_in[0], ev_g_q[0], ev_w_uq[0], ev_g_kv[0], ev_w_uk[0], ev_w_uv[0], ev_conv_w[0],
                    ev_conv_b[0], ev_w_rg[0], ev_b_rg[0], ev_w_ig[0], ev_b_ig[0], ev_lru_lambda[0],
                    ev_w_out[0])
    od = _prep_odd(od_mu[0], od_w_r[0], od_w_k[0], od_w_v[0], od_w0[0], od_w1[0], od_w2[0], od_a0[0],
                   od_a1[0], od_a2[0], od_g1[0], od_g2[0], od_k_k[0], od_k_a[0], od_r_k[0], od_ln_g[0],
                   od_ln_b[0], od_w_o[0])
    rw = router_w.astype(F32)
    rb = _row2(router_b)
    wg, wu, wd = exp_w_gate, exp_w_up, exp_w_down
    lng = ln_g.astype(F32)[:, :, None, :]
    lnb = ln_b.astype(F32)[:, :, None, :]

    def moe(x, layer):
        sparse = x.shape[0] % (SC_WORKERS * SUBLANES) == 0 and x.shape[0] >= SPARSE_MIN_ROWS
        fn = _moe_sparse if sparse else _moe
        return fn(x, rw, rb, wg[layer], wu[layer], wd[layer], lng[layer, 1], lnb[layer, 1])

    xp = jnp.concatenate([jnp.zeros((PAD_FRONT, D_MODEL), F32), meta_tokens.astype(F32),
                          x_prompt[0].astype(F32), jnp.zeros((tp - end, D_MODEL), F32)], axis=0)
    tabs_p = _rope_tables(jnp.maximum(jnp.arange(tp) - PAD_FRONT, 0))
    evb = _single_pass(ev)
    q_p, ckv_p, kr_p, xr_p, yg_p = _even_proj(xp, evb["w1"], ev["gq"], evb["wuq"], ev["gkv"], tabs_p,
                                              MLA_SCALE * LOG2E)
    ones_col = np.zeros((D_ATT, 1), np.float32)
    ones_col[D_V::HEAD_SLOT] = 1.0
    k_p, vt_p = _kv_proj_t(ckv_p, kr_p, evb["wukv"][:, :D_ATT], evb["place"], evb["wukv"][:, D_ATT:].T,
                           jnp.asarray(ones_col))
    attn_p = _flash_attention(q_p, k_p, vt_p, 512)
    rnn_p, tailx_p, tailh_p = _rglru(
        xr_p[None], yg_p[None], ev["cw"], ev["cb"], evb["wrg"], ev["brg"], evb["wig"], ev["big"], ev["sp"],
        jnp.zeros((1, SUBLANES, D_RNN), F32), jnp.zeros((1, 1, D_RNN), F32), PAD_FRONT, end)
    x1_p = _mix_out(attn_p, rnn_p[0], xp, evb["wa"], evb["wr"], lng[0, 0], lnb[0, 0])
    x2_p = moe(x1_p, 0)

    xs = x_sample.reshape(ns, D_MODEL).astype(F32)
    pos_s = jnp.tile(N_META + past + jnp.arange(ls), bs)
    q_s, ckv_s, kr_s, xr_s, yg_s = _even_proj(xs, ev["w1"], ev["gq"], ev["wuq"], ev["gkv"], _rope_tables(pos_s),
                                              MLA_SCALE)
    _, ckv_m, kr_m, _, _ = _even_proj(meta_tokens.astype(F32), ev["w1"], ev["gq"], ev["wuq"], ev["gkv"],
                                      _rope_tables(jnp.arange(N_META)), MLA_SCALE)
    n_keys = N_META + past + ls
    nk_pad = _round_up(n_keys, LANES)
    meta_ckv = jnp.broadcast_to(ckv_m[None], (bs, N_META, D_C))
    meta_kr = jnp.broadcast_to(kr_m[None], (bs, N_META, LANES))
    cache_kr = jnp.pad(cache_krope[0].astype(F32), ((0, 0), (0, 0), (D_NOPE, LANES - D_NOPE - D_ROPE)))
    all_ckv = jnp.concatenate([meta_ckv, cache_ckv[0].astype(F32), ckv_s.reshape(bs, ls, D_C),
                               jnp.zeros((bs, nk_pad - n_keys, D_C), F32)], axis=1)
    all_kr = jnp.concatenate([meta_kr, cache_kr, kr_s.reshape(bs, ls, LANES),
                              jnp.zeros((bs, nk_pad - n_keys, LANES), F32)], axis=1)
    k_s, v_s = _kv_proj(all_ckv.reshape(bs * nk_pad, D_C), all_kr.reshape(bs * nk_pad, LANES),
                        ev["wukv"], ev["place"])
    attn_s = _sample_attention(q_s.reshape(bs, ls, D_ATT), k_s.reshape(bs, nk_pad, D_ATT),
                               v_s.reshape(bs, nk_pad, D_ATT), n_keys)
    buf0_s = jnp.pad(state_conv[0].astype(F32), ((0, 0), (SUBLANES - (CONV_W - 1), 0), (0, 0)))
    rnn_s, tailx_s, tailh_s = _rglru(
        xr_s.reshape(bs, ls, D_RNN), yg_s.reshape(bs, ls, D_RNN), ev["cw"], ev["cb"], ev["wrg"], ev["brg"],
        ev["wig"], ev["big"], ev["sp"], buf0_s, state_lru[0].astype(F32)[:, None, :], 0, ls)
    x1_s = _mix_out(attn_s.reshape(ns, D_ATT), rnn_s.reshape(ns, D_RNN), xs, ev["wa"], ev["wr"],
                    lng[0, 0], lnb[0, 0])
    x2_s = moe(x1_s, 0)

    r_p, lw_p, kk_in_p, v1_p, kkn_p, a_p, g_p = _rwkv_proj(
        x2_p[None], jnp.zeros((1, 1, D_MODEL), F32), od, PAD_FRONT, end)
    o_p, s_p = _wkv(r_p, lw_p, kk_in_p, v1_p, kkn_p, a_p,
                    jnp.zeros((1, RWKV_HEADS // 2, LANES, LANES), F32), CHUNK, False)
    x3_p = _rwkv_out(o_p[0], r_p[0], kk_in_p[0], v1_p[0], g_p[0], x2_p, od, lng[1, 0], lnb[1, 0])
    x4_p = moe(x3_p, 1)

    x2_s3 = x2_s.reshape(bs, ls, D_MODEL)
    r_s, lw_s, kk_in_s, v1_s, kkn_s, a_s, g_s = _rwkv_proj(
        x2_s3, state_shift[0].astype(F32)[:, None, :], od, 0, ls)
    to_chunk = lambda z: jnp.pad(z, ((0, 0), (0, _round_up(ls, CHUNK) - ls), (0, 0)))
    o_s, s_s = _wkv(*(to_chunk(z) for z in (r_s, lw_s, kk_in_s, v1_s, kkn_s, a_s)),
                    _pair_states(state_wkv[0]), CHUNK, True)
    o_s = o_s[:, :ls]
    flat = lambda z: z.reshape(ns, D_MODEL)
    x3_s = _rwkv_out(flat(o_s), flat(r_s), flat(kk_in_s), flat(v1_s), flat(g_s), x2_s, od,
                     lng[1, 0], lnb[1, 0])
    x4_s = moe(x3_s, 1)

    dt = x_prompt.dtype
    nb = CONV_W - 1
    return (
        x4_p[ROW0:end][None].astype(dt),
        x4_s.reshape(bs, ls, D_MODEL).astype(dt),
        ckv_p[PAD_FRONT:end][None, None].astype(dt),
        kr_p[PAD_FRONT:end, D_NOPE:D_NOPE + D_ROPE][None, None].astype(dt),
        tailx_p[:, SUBLANES - nb:][None].astype(dt),
        tailh_p[:, SUBLANES - 1][None].astype(dt),
        x2_p[end - 1][None, None].astype(dt),
        _unpair_states(s_p)[None].astype(dt),
        ckv_s.reshape(bs, ls, D_C)[None].astype(dt),
        kr_s.reshape(bs, ls, LANES)[:, :, D_NOPE:D_NOPE + D_ROPE][None].astype(dt),
        tailx_s[:, SUBLANES - nb:][None].astype(dt),
        tailh_s[:, SUBLANES - 1][None].astype(dt),
        x2_s3[:, ls - 1][None].astype(dt),
        _unpair_states(s_s)[None].astype(dt),
    )
```

```python
import functools

import numpy as np
import jax
import jax.numpy as jnp
from jax import lax
from jax.experimental import pallas as pl
from jax.experimental.pallas import tpu as pltpu
from jax.experimental.pallas import tpu_sc as plsc

F32 = jnp.float32
BF16 = jnp.bfloat16

D_MODEL = 1024
N_META = 16
CHUNK = 64
CHUNK_SHIFT = 6
LN_EPS = 1e-5
RMS_EPS = 1e-6
DEPTH = 2
ALPHA = (2 * DEPTH) ** 0.25
MLA_HEADS = 8
D_NOPE = 64
D_ROPE = 32
D_V = 64
D_C = 256
D_CQ = 384
ROPE_BASE = 10000.0
MLA_SCALE = (D_NOPE + D_ROPE) ** -0.5
D_RNN = 512
LRU_BLOCKS = 8
LRU_BLOCK_W = D_RNN // LRU_BLOCKS
CONV_W = 4
LRU_C = 8.0
RWKV_HEAD = 64
RWKV_HEADS = D_MODEL // RWKV_HEAD
DECAY_SCALE = float(np.exp(-0.5))
GN_EPS = 64e-5
N_EXPERTS = 16
N_GROUPS = 4
EXPERTS_PER_GROUP = N_EXPERTS // N_GROUPS
D_EXPERT = 512

LANES = 128
SUBLANES = 8
HEAD_SLOT = LANES
D_ATT = MLA_HEADS * HEAD_SLOT
PAD_FRONT = CHUNK - N_META
ROW0 = PAD_FRONT + N_META
NEG = -1e30
LOG2E = 1.4426950408889634
SC_CORES = 2
SC_SUBCORES = 16
SC_WORKERS = SC_CORES * SC_SUBCORES
MOE_TILE = 256
SPARSE_MIN_ROWS = 1024
VMEM_LIMIT = 56 * 1024 * 1024

C_CQ = 0
C_CKV = D_CQ
C_XR = C_CKV + D_C
C_YG = C_XR + D_RNN
C_KR = C_YG + D_RNN
N_COL = C_KR + LANES


def _cparams(sem):
    return pltpu.CompilerParams(dimension_semantics=sem, vmem_limit_bytes=VMEM_LIMIT)


def _row_tile(n, cap):
    for t in (1024, 512, 256, 128, 64, 32, 16, 8):
        if t <= cap and n % t == 0:
            return t
    return n


def _full(shape):
    zeros = (0,) * len(shape)
    return pl.BlockSpec(shape, lambda *_: zeros)


def _ln(x, g, b):
    mu = jnp.mean(x, axis=-1, keepdims=True)
    xc = x - mu
    var = jnp.mean(xc * xc, axis=-1, keepdims=True)
    return xc * lax.rsqrt(var + LN_EPS) * g + b


def _bdot(a, b):
    return jnp.dot(a.astype(BF16), b.astype(BF16), preferred_element_type=F32)


def _split2(x):
    hi = x.astype(BF16)
    return hi, (x - hi.astype(F32)).astype(BF16)


def _split3(x):
    hi = x.astype(BF16)
    r1 = x - hi.astype(F32)
    mid = r1.astype(BF16)
    return hi, mid, (r1 - mid.astype(F32)).astype(BF16)


_NN = ((1,), (0,))
_NT = ((1,), (1,))
_TN = ((0,), (0,))


def _mm(a, b, dims, exact):
    dn = (dims, ((), ()))
    if not exact:
        return lax.dot_general(a.astype(BF16), b.astype(BF16), dn, preferred_element_type=F32)
    ah, al = _split2(a)
    bh, bl = _split2(b)
    return (lax.dot_general(ah, bh, dn, preferred_element_type=F32)
            + lax.dot_general(al, bh, dn, preferred_element_type=F32)
            + lax.dot_general(ah, bl, dn, preferred_element_type=F32))


def _wdot(a, w):
    return _mm(a, w, _NN, exact=(w.dtype == F32))


def _act_dtype(w):
    return F32 if w.dtype == F32 else BF16


def _rope_slot(x, c, sa, sb):
    return x * c + pltpu.roll(x, LANES - D_ROPE // 2, 1) * sa + pltpu.roll(x, D_ROPE // 2, 1) * sb


def _even_proj_kernel(x_ref, w1_ref, gq_ref, wuq_ref, gkv_ref, c_ref, sa_ref, sb_ref,
                      q_ref, ckv_ref, kr_ref, xr_ref, yg_ref, *, q_scale):
    u = _wdot(x_ref[...], w1_ref[...])
    cq = u[:, C_CQ:C_CQ + D_CQ]
    cq = cq * lax.rsqrt(jnp.mean(cq * cq, axis=-1, keepdims=True) + RMS_EPS) * gq_ref[...]
    q = _wdot(cq, wuq_ref[...])
    c, sa, sb = c_ref[...], sa_ref[...], sb_ref[...]
    for h in range(MLA_HEADS):
        sl = slice(h * HEAD_SLOT, (h + 1) * HEAD_SLOT)
        q_ref[:, sl] = (_rope_slot(q[:, sl], c, sa, sb) * q_scale).astype(q_ref.dtype)
    ckv = u[:, C_CKV:C_CKV + D_C]
    ckv_ref[...] = ckv * lax.rsqrt(jnp.mean(ckv * ckv, axis=-1, keepdims=True) + RMS_EPS) * gkv_ref[...]
    kr_ref[...] = _rope_slot(u[:, C_KR:C_KR + LANES], c, sa, sb)
    xr_ref[...] = u[:, C_XR:C_XR + D_RNN]
    yg_ref[...] = u[:, C_YG:C_YG + D_RNN]


def _even_proj(x, w1, gq, wuq, gkv, tabs, q_scale):
    n = x.shape[0]
    tm = _row_tile(n, 512)
    row = lambda w: pl.BlockSpec((tm, w), lambda i: (i, 0))
    c, sa, sb = tabs
    return pl.pallas_call(
        functools.partial(_even_proj_kernel, q_scale=q_scale),
        grid=(n // tm,),
        in_specs=[row(D_MODEL), _full(w1.shape), _full(gq.shape), _full(wuq.shape), _full(gkv.shape),
                  row(LANES), row(LANES), row(LANES)],
        out_specs=[row(D_ATT), row(D_C), row(LANES), row(D_RNN), row(D_RNN)],
        out_shape=[jax.ShapeDtypeStruct((n, D_ATT), _act_dtype(w1)), jax.ShapeDtypeStruct((n, D_C), F32),
                   jax.ShapeDtypeStruct((n, LANES), F32), jax.ShapeDtypeStruct((n, D_RNN), F32),
                   jax.ShapeDtypeStruct((n, D_RNN), F32)],
        compiler_params=_cparams(("arbitrary",)),
        name="even_proj",
    )(x, w1, gq, wuq, gkv, c, sa, sb)


def _kv_proj_kernel(ckv_ref, kr_ref, wukv_ref, p_ref, k_ref, v_ref):
    kv = _wdot(ckv_ref[...], wukv_ref[...])
    k_ref[...] = (kv[:, :D_ATT] + _wdot(kr_ref[...], p_ref[...])).astype(k_ref.dtype)
    v_ref[...] = kv[:, D_ATT:].astype(v_ref.dtype)


def _kv_proj(ckv, kr, wukv, place):
    n = ckv.shape[0]
    tm = _row_tile(n, 512)
    row = lambda w: pl.BlockSpec((tm, w), lambda i: (i, 0))
    return pl.pallas_call(
        _kv_proj_kernel,
        grid=(n // tm,),
        in_specs=[row(D_C), row(LANES), _full(wukv.shape), _full(place.shape)],
        out_specs=[row(D_ATT), row(D_ATT)],
        out_shape=[jax.ShapeDtypeStruct((n, D_ATT), _act_dtype(wukv))] * 2,
        compiler_params=_cparams(("arbitrary",)),
        name="kv_proj",
    )(ckv, kr, wukv, place)


def _kv_proj_t_kernel(ckv_ref, kr_ref, wuk_ref, p_ref, wuvt_ref, ones_ref, k_ref, vt_ref):
    ckv = ckv_ref[...].astype(BF16)
    k = jnp.dot(ckv, wuk_ref[...], preferred_element_type=F32) + _bdot(kr_ref[...], p_ref[...])
    k_ref[...] = k.astype(BF16)
    vt = lax.dot_general(wuvt_ref[...], ckv, (_NT, ((), ())), preferred_element_type=F32)
    vt_ref[...] = (vt + ones_ref[...]).astype(BF16)


def _kv_proj_t(ckv, kr, wuk, place, wuvt, ones_col):
    n = ckv.shape[0]
    tm = _row_tile(n, 512)
    row = lambda w: pl.BlockSpec((tm, w), lambda i: (i, 0))
    return pl.pallas_call(
        _kv_proj_t_kernel,
        grid=(n // tm,),
        in_specs=[row(D_C), row(LANES), _full(wuk.shape), _full(place.shape), _full(wuvt.shape),
                  _full(ones_col.shape)],
        out_specs=[row(D_ATT), pl.BlockSpec((D_ATT, tm), lambda i: (0, i))],
        out_shape=[jax.ShapeDtypeStruct((n, D_ATT), BF16), jax.ShapeDtypeStruct((D_ATT, n), BF16)],
        compiler_params=_cparams(("arbitrary",)),
        name="kv_proj_t",
    )(ckv, kr, wuk, place, wuvt, ones_col)


def _flash_kernel(qi_ref, kj_ref, q_ref, k_ref, vt_ref, o_ref, m_scr, acc_scr, *, tq, tk):
    step = pl.program_id(0)
    i = qi_ref[step]
    j = kj_ref[step]

    @pl.when(j == 0)
    def _():
        m_scr[...] = jnp.full(m_scr.shape, NEG, F32)
        acc_scr[...] = jnp.zeros(acc_scr.shape, F32)

    heads = range(MLA_HEADS)
    slots = [slice(h * HEAD_SLOT, (h + 1) * HEAD_SLOT) for h in heads]

    def accumulate(masked):
        st = [lax.dot_general(k_ref[:, sl], q_ref[:, sl], (_NT, ((), ())), preferred_element_type=F32)
              for sl in slots]
        if masked:
            krow = j * tk + lax.broadcasted_iota(jnp.int32, (tk, tq), 0)
            qrow = i * tq + lax.broadcasted_iota(jnp.int32, (tk, tq), 1)
            keep = ((((qrow - ROW0) >> CHUNK_SHIFT) >= ((krow - ROW0) >> CHUNK_SHIFT))
                    & (krow >= PAD_FRONT))
            st = [jnp.where(keep, x, NEG) for x in st]
        m_prev = [m_scr[h:h + 1, :] for h in heads]
        m_new = [jnp.maximum(mp, jnp.max(x, axis=0, keepdims=True)) for mp, x in zip(m_prev, st)]
        alpha = [jnp.exp2(mp - mn) for mp, mn in zip(m_prev, m_new)]
        pt = [jnp.exp2(x - mn).astype(BF16) for x, mn in zip(st, m_new)]
        pv = [jnp.dot(vt_ref[sl, :], x, preferred_element_type=F32) for x, sl in zip(pt, slots)]
        for h in heads:
            acc_scr[slots[h], :] = alpha[h] * acc_scr[slots[h], :] + pv[h]
            m_scr[h:h + 1, :] = m_new[h]

    edge = (j == i) | (j == 0)

    @pl.when(edge)
    def _():
        accumulate(True)

    @pl.when(jnp.logical_not(edge))
    def _():
        accumulate(False)

    @pl.when(j == i)
    def _():
        for sl in slots:
            acc = acc_scr[sl, :]
            o_ref[:, sl] = (acc / acc[D_V:D_V + 1, :]).T.astype(BF16)


def _flash_attention(q, k, vt, tq):
    n = q.shape[0]
    nq = n // tq
    qi = np.concatenate([np.full(i + 1, i, np.int32) for i in range(nq)])
    kj = np.concatenate([np.arange(i + 1, dtype=np.int32) for i in range(nq)])
    grid_spec = pltpu.PrefetchScalarGridSpec(
        num_scalar_prefetch=2,
        grid=(len(qi),),
        in_specs=[pl.BlockSpec((tq, D_ATT), lambda s, qi, kj: (qi[s], 0)),
                  pl.BlockSpec((tq, D_ATT), lambda s, qi, kj: (kj[s], 0)),
                  pl.BlockSpec((D_ATT, tq), lambda s, qi, kj: (0, kj[s]))],
        out_specs=pl.BlockSpec((tq, D_ATT), lambda s, qi, kj: (qi[s], 0)),
        scratch_shapes=[pltpu.VMEM((MLA_HEADS, tq), F32), pltpu.VMEM((D_ATT, tq), F32)],
    )
    return pl.pallas_call(
        functools.partial(_flash_kernel, tq=tq, tk=tq),
        grid_spec=grid_spec,
        out_shape=jax.ShapeDtypeStruct((n, D_ATT), BF16),
        compiler_params=_cparams(("arbitrary",)),
        name="flash_attention",
    )(jnp.asarray(qi), jnp.asarray(kj), q, k, vt)


def _sample_attn_kernel(q_ref, k_ref, v_ref, o_ref, *, n_keys):
    nk = k_ref.shape[1]
    exact = k_ref.dtype == F32
    keep = lax.broadcasted_iota(jnp.int32, (q_ref.shape[1], nk), 1) < n_keys
    for h in range(MLA_HEADS):
        sl = slice(h * HEAD_SLOT, (h + 1) * HEAD_SLOT)
        s = _mm(q_ref[0, :, sl], k_ref[0, :, sl], _NT, exact)
        s = jnp.where(keep, s, NEG)
        p = jnp.exp(s - jnp.max(s, axis=-1, keepdims=True))
        p = p / jnp.sum(p, axis=-1, keepdims=True)
        o_ref[0, :, sl] = _mm(p, v_ref[0, :, sl], _NN, exact).astype(o_ref.dtype)


def _sample_attention(q, k, v, n_keys):
    b, l, _ = q.shape
    nk = k.shape[1]
    return pl.pallas_call(
        functools.partial(_sample_attn_kernel, n_keys=n_keys),
        grid=(b,),
        in_specs=[pl.BlockSpec((1, l, D_ATT), lambda i: (i, 0, 0)),
                  pl.BlockSpec((1, nk, D_ATT), lambda i: (i, 0, 0)),
                  pl.BlockSpec((1, nk, D_ATT), lambda i: (i, 0, 0))],
        out_specs=pl.BlockSpec((1, l, D_ATT), lambda i: (i, 0, 0)),
        out_shape=jax.ShapeDtypeStruct((b, l, D_ATT), q.dtype),
        compiler_params=_cparams(("arbitrary",)),
        name="sample_attention",
    )(q, k, v)


def _expm1(x):
    series = x * (1.0 + x * (0.5 + x * (1.0 / 6.0 + x * (1.0 / 24.0 + x * (1.0 / 120.0)))))
    return jnp.where(jnp.abs(x) < 0.05, series, jnp.exp(x) - 1.0)


def _gelu_tanh(x):
    return 0.5 * x * (1.0 + jnp.tanh(0.7978845608028654 * (x + 0.044715 * x * x * x)))


def _rglru_kernel(xr_ref, yg_ref, cw_ref, cb_ref, wrg_ref, brg_ref, wig_ref, big_ref, sp_ref,
                  buf0_ref, h0_ref, rnn_ref, tailx_ref, tailh_ref, prev_scr, h_scr,
                  *, tm, start, end):
    t = pl.program_id(1)

    @pl.when(t == 0)
    def _():
        prev_scr[...] = buf0_ref[0]
        h_scr[...] = jnp.broadcast_to(h0_ref[0], h_scr.shape)

    x = xr_ref[0]
    ext = jnp.concatenate([prev_scr[...], x], axis=0)
    cw = cw_ref[...]
    xc = cb_ref[...] + cw[CONV_W - 1:CONV_W] * x
    for d in range(1, CONV_W):
        xc = xc + cw[CONV_W - 1 - d:CONV_W - d] * pltpu.roll(ext, d, 0)[SUBLANES:]
    prev_scr[...] = x[tm - SUBLANES:]

    r = jax.nn.sigmoid(_wdot(xc, wrg_ref[...]) + brg_ref[...])
    ig = jax.nn.sigmoid(_wdot(xc, wig_ref[...]) + big_ref[...])
    log_a = -LRU_C * r * sp_ref[...]
    a = jnp.exp(log_a)
    b = jnp.sqrt(-_expm1(2.0 * log_a)) * (ig * xc)
    row = lax.broadcasted_iota(jnp.int32, (tm, D_RNN), 0)
    if start > 0:
        live = (t * tm + row) >= start
        a = jnp.where(live, a, 1.0)
        b = jnp.where(live, b, 0.0)
    d = 1
    while d < tm:
        b = a * jnp.where(row >= d, pltpu.roll(b, d, 0), 0.0) + b
        a = a * jnp.where(row >= d, pltpu.roll(a, d, 0), 1.0)
        d *= 2
    h = a * h_scr[0:1] + b
    h_scr[...] = jnp.broadcast_to(h[tm - 1:tm], h_scr.shape)
    rnn_ref[0] = (h * _gelu_tanh(yg_ref[0])).astype(rnn_ref.dtype)

    t_end = (end - 1) // tm
    el = end - t_end * tm

    @pl.when(t == t_end)
    def _():
        tailx_ref[0] = ext[el:el + SUBLANES]
        tailh_ref[0] = h[el - SUBLANES:el]


def _rglru(xr, yg, cw, cb, wrg, brg, wig, big, sp, buf0, h0, start, end):
    b, l, _ = xr.shape
    tm = _row_tile(l, 512)
    seq = pl.BlockSpec((1, tm, D_RNN), lambda i, t: (i, t, 0))
    per_b = lambda r: pl.BlockSpec((1, r, D_RNN), lambda i, t: (i, 0, 0))
    return pl.pallas_call(
        functools.partial(_rglru_kernel, tm=tm, start=start, end=end),
        grid=(b, l // tm),
        in_specs=[seq, seq, _full(cw.shape), _full(cb.shape), _full(wrg.shape), _full(brg.shape),
                  _full(wig.shape), _full(big.shape), _full(sp.shape), per_b(SUBLANES), per_b(1)],
        out_specs=[seq, per_b(SUBLANES), per_b(SUBLANES)],
        out_shape=[jax.ShapeDtypeStruct((b, l, D_RNN), _act_dtype(wrg)),
                   jax.ShapeDtypeStruct((b, SUBLANES, D_RNN), F32),
                   jax.ShapeDtypeStruct((b, SUBLANES, D_RNN), F32)],
        scratch_shapes=[pltpu.VMEM((SUBLANES, D_RNN), F32), pltpu.VMEM((SUBLANES, D_RNN), F32)],
        compiler_params=_cparams(("arbitrary", "arbitrary")),
        name="rglru",
    )(xr, yg, cw, cb, wrg, brg, wig, big, sp, buf0, h0)


def _mix_out_kernel(attn_ref, rnn_ref, x_ref, wa_ref, wr_ref, g_ref, b_ref, o_ref):
    mix = _wdot(attn_ref[...], wa_ref[...]) + _wdot(rnn_ref[...], wr_ref[...])
    o_ref[...] = _ln(ALPHA * x_ref[...] + mix, g_ref[...], b_ref[...])


def _mix_out(attn, rnn, x, wa, wr, g, b):
    n = x.shape[0]
    tm = _row_tile(n, 512)
    row = lambda w: pl.BlockSpec((tm, w), lambda i: (i, 0))
    return pl.pallas_call(
        _mix_out_kernel,
        grid=(n // tm,),
        in_specs=[row(D_ATT), row(D_RNN), row(D_MODEL), _full(wa.shape), _full(wr.shape),
                  _full(g.shape), _full(b.shape)],
        out_specs=row(D_MODEL),
        out_shape=jax.ShapeDtypeStruct((n, D_MODEL), F32),
        compiler_params=_cparams(("arbitrary",)),
        name="mix_out",
    )(attn, rnn, x, wa, wr, g, b)


def _first_argmax(vals, lane):
    m = jnp.max(vals, axis=-1, keepdims=True)
    idx = jnp.min(jnp.where(vals == m, lane, N_EXPERTS), axis=-1, keepdims=True)
    return m, idx


def _router_top2(x, rw, rb):
    logits = jnp.dot(x, rw, preferred_element_type=F32, precision=lax.Precision.HIGHEST)
    s = jax.nn.sigmoid(logits)
    sel = s + rb
    lane = lax.broadcasted_iota(jnp.int32, sel.shape, 1)
    grp = lane >> 2
    best = None
    g_best = None
    for g in range(N_GROUPS):
        vals = jnp.where(grp == g, sel, NEG)
        m1, i1 = _first_argmax(vals, lane)
        m2, _ = _first_argmax(jnp.where(lane == i1, NEG, vals), lane)
        score = m1 + m2
        if g == 0:
            best, g_best = score, jnp.zeros_like(i1)
        else:
            upd = score > best
            g_best = jnp.where(upd, g, g_best)
            best = jnp.where(upd, score, best)
    vals = jnp.where(grp == g_best, sel, NEG)
    _, i1 = _first_argmax(vals, lane)
    _, i2 = _first_argmax(jnp.where(lane == i1, NEG, vals), lane)
    w1 = jnp.sum(jnp.where(lane == i1, s, 0.0), axis=-1, keepdims=True)
    w2 = jnp.sum(jnp.where(lane == i2, s, 0.0), axis=-1, keepdims=True)
    den = w1 + w2
    return lane, i1, i2, w1 / den, w2 / den


def _router_gate(x, rw, rb):
    lane, i1, i2, g1, g2 = _router_top2(x, rw, rb)
    return jnp.where(lane == i1, g1, 0.0) + jnp.where(lane == i2, g2, 0.0)


def _moe_kernel(x_ref, rw_ref, rb_ref, wg_ref, wu_ref, wd_ref, g_ref, b_ref, o_ref,
                gate_scr, xb_scr, acc_scr):
    e = pl.program_id(1)

    @pl.when(e == 0)
    def _():
        x = x_ref[...]
        gate = _router_gate(x, rw_ref[...], rb_ref[...])
        for k in range(N_EXPERTS):
            gate_scr[k] = jnp.broadcast_to(gate[:, k:k + 1], gate_scr.shape[1:])
        xb_scr[...] = x.astype(BF16)
        acc_scr[...] = jnp.zeros(acc_scr.shape, F32)

    xb = xb_scr[...]
    hg = jnp.dot(xb, wg_ref[0].astype(BF16), preferred_element_type=F32)
    hu = jnp.dot(xb, wu_ref[0].astype(BF16), preferred_element_type=F32)
    gate_e = gate_scr[e]
    h = jax.nn.silu(hg) * hu * jnp.concatenate([gate_e] * (D_EXPERT // LANES), axis=1)
    acc_scr[...] += jnp.dot(h.astype(BF16), wd_ref[0].astype(BF16), preferred_element_type=F32)

    @pl.when(e == N_EXPERTS - 1)
    def _():
        o_ref[...] = _ln(ALPHA * x_ref[...] + acc_scr[...], g_ref[...], b_ref[...])


def _moe(x, rw, rb, wg, wu, wd, e0, g, b):
    n = x.shape[0]
    tm = _row_tile(n, 512)
    row = pl.BlockSpec((tm, D_MODEL), lambda i, e: (i, 0))
    return pl.pallas_call(
        _moe_kernel,
        grid=(n // tm, N_EXPERTS),
        in_specs=[row, _full(rw.shape), _full(rb.shape),
                  pl.BlockSpec((1, D_MODEL, D_EXPERT), lambda i, e: (e0 + e, 0, 0)),
                  pl.BlockSpec((1, D_MODEL, D_EXPERT), lambda i, e: (e0 + e, 0, 0)),
                  pl.BlockSpec((1, D_EXPERT, D_MODEL), lambda i, e: (e0 + e, 0, 0)),
                  _full(g.shape), _full(b.shape)],
        out_specs=row,
        out_shape=jax.ShapeDtypeStruct((n, D_MODEL), F32),
        scratch_shapes=[pltpu.VMEM((N_EXPERTS, tm, LANES), F32), pltpu.VMEM((tm, D_MODEL), BF16),
                        pltpu.VMEM((tm, D_MODEL), F32)],
        compiler_params=_cparams(("arbitrary", "arbitrary")),
        name="moe",
    )(x, rw, rb, wg, wu, wd, g, b)


M_I1, M_I2, M_R1, M_R2, M_G1, M_G2, M_COLS = 0, 1, 2, 3, 4, 5, 8


def _first_argmax_rows(vals, row):
    m = jnp.max(vals, axis=0, keepdims=True)
    idx = jnp.min(jnp.where(vals == m, row, N_EXPERTS), axis=0, keepdims=True)
    return m, idx


def _route_kernel(x_ref, rwt_ref, rbc_ref, meta_ref, cnt_ref, carry_scr, *, tm):
    @pl.when(pl.program_id(0) == 0)
    def _():
        carry_scr[...] = jnp.zeros(carry_scr.shape, F32)

    logits = lax.dot_general(rwt_ref[...], x_ref[...], (_NT, ((), ())), preferred_element_type=F32,
                             precision=lax.Precision.HIGHEST)
    s = jax.nn.sigmoid(logits)
    sel = s + rbc_ref[...]
    row = lax.broadcasted_iota(jnp.int32, sel.shape, 0)
    grp = row >> 2
    best = None
    g_best = None
    for g in range(N_GROUPS):
        vals = jnp.where(grp == g, sel, NEG)
        m1, i1 = _first_argmax_rows(vals, row)
        m2, _ = _first_argmax_rows(jnp.where(row == i1, NEG, vals), row)
        score = m1 + m2
        if g == 0:
            best, g_best = score, jnp.zeros_like(i1)
        else:
            upd = score > best
            g_best = jnp.where(upd, g, g_best)
            best = jnp.where(upd, score, best)
    vals = jnp.where(grp == g_best, sel, NEG)
    _, i1 = _first_argmax_rows(vals, row)
    _, i2 = _first_argmax_rows(jnp.where(row == i1, NEG, vals), row)
    w1 = jnp.sum(jnp.where(row == i1, s, 0.0), axis=0, keepdims=True)
    w2 = jnp.sum(jnp.where(row == i2, s, 0.0), axis=0, keepdims=True)
    den = w1 + w2

    chosen = jnp.where((row == i1) | (row == i2), 1.0, 0.0)
    earlier = (lax.broadcasted_iota(jnp.int32, (tm, tm), 0)
               < lax.broadcasted_iota(jnp.int32, (tm, tm), 1)).astype(BF16)
    seen = jnp.dot(chosen.astype(BF16), earlier, preferred_element_type=F32) + carry_scr[:, 0:1]
    r1 = jnp.sum(jnp.where(row == i1, seen, 0.0), axis=0, keepdims=True)
    r2 = jnp.sum(jnp.where(row == i2, seen, 0.0), axis=0, keepdims=True)
    carry_scr[...] = carry_scr[...] + jnp.sum(chosen, axis=1, keepdims=True)
    mrow = lax.broadcasted_iota(jnp.int32, (M_COLS, tm), 0)
    meta = jnp.zeros((M_COLS, tm), F32)
    for c, val in ((M_I1, i1.astype(F32)), (M_I2, i2.astype(F32)), (M_R1, r1), (M_R2, r2),
                   (M_G1, w1 / den), (M_G2, w2 / den)):
        meta = jnp.where(mrow == c, val, meta)
    meta_ref[...] = meta
    cnt_ref[...] = carry_scr[...]


def _route(x, rwt, rbc):
    n = x.shape[0]
    tm = _row_tile(n, 512)
    return pl.pallas_call(
        functools.partial(_route_kernel, tm=tm),
        grid=(n // tm,),
        in_specs=[pl.BlockSpec((tm, D_MODEL), lambda i: (i, 0)), _full(rwt.shape), _full(rbc.shape)],
        out_specs=[pl.BlockSpec((M_COLS, tm), lambda i: (0, i)), _full((N_EXPERTS, LANES))],
        out_shape=[jax.ShapeDtypeStruct((M_COLS, n), F32), jax.ShapeDtypeStruct((N_EXPERTS, LANES), F32)],
        scratch_shapes=[pltpu.VMEM((N_EXPERTS, LANES), F32)],
        compiler_params=_cparams(("arbitrary",)),
        name="moe_route",
    )(x, rwt, rbc)


def _sc_chunk(per_worker):
    for c in (64, 48, 32, 16, 8):
        if per_worker % c == 0:
            return c
    raise ValueError(per_worker)


def _sc_mesh():
    return plsc.VectorSubcoreMesh(core_axis_name="c", subcore_axis_name="s")


def _sc_scatter2(x, idx1, idx2, n_out):
    n, d = x.shape
    per_w = n // SC_WORKERS
    assert per_w * SC_WORKERS == n
    chunk = _sc_chunk(per_w)

    @functools.partial(
        pl.kernel, mesh=_sc_mesh(), out_type=jax.ShapeDtypeStruct((n_out, d), x.dtype),
        scratch_types=[pltpu.VMEM((chunk,), jnp.int32), pltpu.VMEM((chunk,), jnp.int32),
                       pltpu.VMEM((chunk, d), x.dtype), pltpu.SemaphoreType.DMA])
    def scatter(x_hbm, i1_hbm, i2_hbm, out_hbm, i1_v, i2_v, rows_v, sem):
        base = (lax.axis_index("s") * SC_CORES + lax.axis_index("c")) * per_w

        @pl.loop(0, per_w // chunk)
        def _(c):
            off = pl.multiple_of(base + c * chunk, SUBLANES)
            pltpu.sync_copy(i1_hbm.at[pl.ds(off, chunk)], i1_v)
            pltpu.sync_copy(i2_hbm.at[pl.ds(off, chunk)], i2_v)
            pltpu.sync_copy(x_hbm.at[pl.ds(off, chunk)], rows_v)
            pltpu.async_copy(rows_v, out_hbm.at[i1_v], sem).wait()
            pltpu.async_copy(rows_v, out_hbm.at[i2_v], sem).wait()

    return scatter(x, idx1, idx2)


def _sc_gather(y, idx):
    n = idx.shape[0]
    d = y.shape[1]
    per_w = n // SC_WORKERS
    assert per_w * SC_WORKERS == n
    chunk = _sc_chunk(per_w)

    @functools.partial(
        pl.kernel, mesh=_sc_mesh(), out_type=jax.ShapeDtypeStruct((n, d), y.dtype),
        scratch_types=[pltpu.VMEM((chunk,), jnp.int32), pltpu.VMEM((chunk, d), y.dtype),
                       pltpu.SemaphoreType.DMA])
    def gather(y_hbm, idx_hbm, out_hbm, idx_v, rows_v, sem):
        base = (lax.axis_index("s") * SC_CORES + lax.axis_index("c")) * per_w

        @pl.loop(0, per_w // chunk)
        def _(c):
            off = pl.multiple_of(base + c * chunk, SUBLANES)
            pltpu.sync_copy(idx_hbm.at[pl.ds(off, chunk)], idx_v)
            pltpu.async_copy(y_hbm.at[idx_v], rows_v, sem).wait()
            pltpu.sync_copy(rows_v, out_hbm.at[pl.ds(off, chunk)])

    return gather(y, idx)


def _experts_kernel(te_ref, used_ref, x_ref, wg_ref, wu_ref, wd_ref, o_ref):
    @pl.when(pl.program_id(0) < used_ref[0])
    def _():
        xb = x_ref[...].astype(BF16)
        hg = jnp.dot(xb, wg_ref[0].astype(BF16), preferred_element_type=F32)
        hu = jnp.dot(xb, wu_ref[0].astype(BF16), preferred_element_type=F32)
        h = jax.nn.silu(hg) * hu
        o_ref[...] = jnp.dot(h.astype(BF16), wd_ref[0].astype(BF16), preferred_element_type=F32)


def _experts(xg, tile_expert, n_used, wg, wu, wd):
    n_tiles = xg.shape[0] // MOE_TILE
    row = pl.BlockSpec((MOE_TILE, D_MODEL), lambda i, te, used: (i, 0))
    grid_spec = pltpu.PrefetchScalarGridSpec(
        num_scalar_prefetch=2,
        grid=(n_tiles,),
        in_specs=[row,
                  pl.BlockSpec((1, D_MODEL, D_EXPERT), lambda i, te, used: (te[i], 0, 0)),
                  pl.BlockSpec((1, D_MODEL, D_EXPERT), lambda i, te, used: (te[i], 0, 0)),
                  pl.BlockSpec((1, D_EXPERT, D_MODEL), lambda i, te, used: (te[i], 0, 0))],
        out_specs=row,
    )
    return pl.pallas_call(
        _experts_kernel,
        grid_spec=grid_spec,
        out_shape=jax.ShapeDtypeStruct(xg.shape, F32),
        compiler_params=_cparams(("arbitrary",)),
        name="moe_experts",
    )(tile_expert, n_used, xg, wg, wu, wd)


def _combine_kernel(x_ref, y1_ref, y2_ref, meta_ref, g_ref, b_ref, o_ref):
    meta = meta_ref[...]
    moe = meta[:, M_G1:M_G1 + 1] * y1_ref[...] + meta[:, M_G2:M_G2 + 1] * y2_ref[...]
    o_ref[...] = _ln(ALPHA * x_ref[...] + moe, g_ref[...], b_ref[...])


def _combine(x, y1, y2, meta, g, b):
    n = x.shape[0]
    tm = _row_tile(n, 512)
    row = pl.BlockSpec((tm, D_MODEL), lambda i: (i, 0))
    return pl.pallas_call(
        _combine_kernel,
        grid=(n // tm,),
        in_specs=[row, row, row, pl.BlockSpec((tm, M_COLS), lambda i: (i, 0)), _full(g.shape), _full(b.shape)],
        out_specs=row,
        out_shape=jax.ShapeDtypeStruct((n, D_MODEL), F32),
        compiler_params=_cparams(("arbitrary",)),
        name="moe_combine",
    )(x, y1, y2, meta, g, b)


def _moe_sparse(x, rw, rb, wg, wu, wd, e0, g, b):
    n = x.shape[0]
    meta_t, counts = _route(x, rw.T, rb.reshape(N_EXPERTS, 1))
    cnt = counts[:, 0].astype(jnp.int32)
    padded = (cnt + MOE_TILE - 1) // MOE_TILE * MOE_TILE
    seg_end = jnp.cumsum(padded)
    seg_start = seg_end - padded
    experts = jnp.arange(N_EXPERTS, dtype=jnp.int32)[:, None]
    start_of = lambda e: jnp.sum(jnp.where(experts == e[None, :], seg_start[:, None], 0), axis=0)
    e1, e2 = meta_t[M_I1].astype(jnp.int32), meta_t[M_I2].astype(jnp.int32)
    pos1 = start_of(e1) + meta_t[M_R1].astype(jnp.int32)
    pos2 = start_of(e2) + meta_t[M_R2].astype(jnp.int32)
    meta = meta_t.T
    n_tiles = -(-2 * n // MOE_TILE) + N_EXPERTS
    tile_start = jnp.arange(n_tiles, dtype=jnp.int32) * MOE_TILE
    tile_expert = e0 + jnp.minimum(jnp.sum(tile_start[:, None] >= seg_end[None, :], axis=1),
                                   N_EXPERTS - 1).astype(jnp.int32)
    n_used = (seg_end[-1:] // MOE_TILE).astype(jnp.int32)
    xg = _sc_scatter2(x, pos1, pos2, n_tiles * MOE_TILE)
    yg = _experts(xg, tile_expert, n_used, wg, wu, wd)
    return _combine(x, _sc_gather(yg, pos1), _sc_gather(yg, pos2), meta, g, b)


def _head_sum(z, ones):
    hi, lo = _split2(z)
    parts = []
    for g in range(D_MODEL // LANES):
        sl = slice(g * LANES, (g + 1) * LANES)
        parts.append(jnp.dot(hi[:, sl], ones, preferred_element_type=F32)
                     + jnp.dot(lo[:, sl], ones, preferred_element_type=F32))
    return jnp.concatenate(parts, axis=1)


def _rwkv_proj_kernel(x_ref, sh0_ref, mu_ref, wr_ref, wk_ref, wv_ref, w0_ref, w1_ref, w2_ref,
                      a0_ref, a1_ref, a2_ref, g1_ref, g2_ref, kkw_ref, kaw_ref, ones_ref,
                      r_ref, lw_ref, k_ref, v_ref, kk_ref, a_ref, g_ref, prev_scr,
                      *, tm, start, end):
    t = pl.program_id(1)

    @pl.when(t == 0)
    def _():
        prev_scr[...] = jnp.zeros(prev_scr.shape, F32)

    x = x_ref[0]
    ext = jnp.concatenate([prev_scr[...], x], axis=0)
    x_prev = pltpu.roll(ext, 1, 0)[SUBLANES:]
    grow = t * tm + lax.broadcasted_iota(jnp.int32, (tm, D_MODEL), 0)
    x_prev = jnp.where(grow == start, sh0_ref[0], x_prev)
    prev_scr[...] = x[tm - SUBLANES:]
    xx = x_prev - x
    mu = mu_ref[...]
    xr, xw, xk, xv, xa, xg = (x + xx * mu[n:n + 1] for n in range(6))
    r = _bdot(xr, wr_ref[...])
    k = _bdot(xk, wk_ref[...])
    v = _bdot(xv, wv_ref[...])
    log_w = -DECAY_SCALE * jax.nn.sigmoid(w0_ref[...] + _bdot(jnp.tanh(_bdot(xw, w1_ref[...])), w2_ref[...]))
    a = jax.nn.sigmoid(a0_ref[...] + _bdot(_bdot(xa, a1_ref[...]), a2_ref[...]))
    g = _bdot(jax.nn.sigmoid(_bdot(xg, g1_ref[...])), g2_ref[...])
    kk = k * kkw_ref[...]
    norm = jnp.sqrt(_head_sum(kk * kk, ones_ref[...]))
    kk = kk / jnp.maximum(norm, 1e-12)
    k = k * (1.0 + (a - 1.0) * kaw_ref[...])
    live = (grow >= start) & (grow < end)
    r_ref[0] = r.astype(r_ref.dtype)
    lw_ref[0] = jnp.where(live, log_w, 0.0)
    k_ref[0] = jnp.where(live, k, 0.0).astype(k_ref.dtype)
    v_ref[0] = v.astype(v_ref.dtype)
    kk_ref[0] = jnp.where(live, kk, 0.0).astype(kk_ref.dtype)
    a_ref[0] = a.astype(a_ref.dtype)
    g_ref[0] = g.astype(g_ref.dtype)


def _rwkv_proj(x, sh0, od, start, end, act_dtype):
    b, l, _ = x.shape
    tm = _row_tile(l, 256)
    dtypes = [act_dtype, F32] + [act_dtype] * 5
    seq = pl.BlockSpec((1, tm, D_MODEL), lambda i, t: (i, t, 0))
    ws = [od[n] for n in ("mu", "w_r", "w_k", "w_v", "w0", "w1", "w2", "a0", "a1", "a2", "g1", "g2",
                          "k_k", "k_a", "ones")]
    return pl.pallas_call(
        functools.partial(_rwkv_proj_kernel, tm=tm, start=start, end=end),
        grid=(b, l // tm),
        in_specs=[seq, pl.BlockSpec((1, 1, D_MODEL), lambda i, t: (i, 0, 0))] + [_full(w.shape) for w in ws],
        out_specs=[seq] * 7,
        out_shape=[jax.ShapeDtypeStruct((b, l, D_MODEL), dt) for dt in dtypes],
        scratch_shapes=[pltpu.VMEM((SUBLANES, D_MODEL), F32)],
        compiler_params=_cparams(("arbitrary", "arbitrary")),
        name="rwkv_proj",
    )(x, sh0, *ws)


def _wkv_kernel(r_ref, lw_ref, k_ref, v_ref, kk_ref, a_ref, s0_ref, o_ref, sout_ref, s_scr,
                *, c, exact):
    t = pl.program_id(1)

    @pl.when(t == 0)
    def _():
        s_scr[...] = s0_ref[0]

    head0 = lax.broadcasted_iota(jnp.int32, (c, LANES), 1) < RWKV_HEAD
    c2 = 2 * c
    row = lax.broadcasted_iota(jnp.int32, (c2, c2), 0)
    col = lax.broadcasted_iota(jnp.int32, (c2, c2), 1)
    row_hi = jnp.where(row >= c, c, 0)
    col_hi = jnp.where(col >= c, c, 0)
    same = row_hi == col_hi
    rr = row - row_hi
    cc = col - col_hi
    strict = same & (rr > cc)
    incl = same & (rr >= cc)
    eye = (row == col).astype(F32)
    tri = (lax.broadcasted_iota(jnp.int32, (c, c), 0) >= lax.broadcasted_iota(jnp.int32, (c, c), 1)).astype(BF16)

    def stack(x):
        return jnp.concatenate([jnp.where(head0, x, 0.0), jnp.where(head0, 0.0, x)], axis=0)

    mm = functools.partial(_mm, exact=exact)
    pairs = range(RWKV_HEADS // 2)
    for sub in range(r_ref.shape[1] // c):
        rows = slice(sub * c, (sub + 1) * c)
        load = lambda ref: [ref[0, rows, p * LANES:(p + 1) * LANES].astype(F32) for p in pairs]
        r, lw, k, v, kk, a = (load(ref) for ref in (r_ref, lw_ref, k_ref, v_ref, kk_ref, a_ref))
        lc = [sum(jnp.dot(tri, part, preferred_element_type=F32) for part in _split3(x)) for x in lw]
        lc_end = [x[c - 1:c] for x in lc]
        b = [x * y for x, y in zip(kk, a)]
        lhs = [jnp.concatenate([stack(-kk[p] * jnp.exp(lc[p] - lw[p])), stack(r[p] * jnp.exp(lc[p]))], axis=0)
               for p in pairs]
        g_inv = [jnp.exp(-x) for x in lc]
        rhs = [jnp.concatenate([stack(b[p] * g_inv[p]), stack(k[p] * g_inv[p])], axis=0) for p in pairs]
        pm = [mm(x, y, _NT) for x, y in zip(lhs, rhs)]
        l_ab = [jnp.where(strict, x[:c2, :c2], 0.0) for x in pm]
        l_ak = [jnp.where(strict, x[:c2, c2:], 0.0) for x in pm]
        m_rb = [jnp.where(incl, x[c2:, :c2], 0.0) for x in pm]
        m_rk = [jnp.where(incl, x[c2:, c2:], 0.0) for x in pm]
        vs = [stack(x) for x in v]
        lakv = [mm(x, y, _NN) for x, y in zip(l_ak, vs)]
        mrkv = [mm(x, y, _NN) for x, y in zip(m_rk, vs)]
        tinv = [eye + x for x in l_ab]
        lp = l_ab
        n = 2
        while n < c:
            lp = [mm(x, x, _NN) for x in lp]
            tinv = [x + mm(x, y, _NN) for x, y in zip(tinv, lp)]
            n *= 2
        s = [s_scr[p] for p in pairs]
        xs = [mm(x, y, _NT) for x, y in zip(lhs, s)]
        u = [mm(tinv[p], xs[p][:c2] + lakv[p], _NN) for p in pairs]
        for p in pairs:
            os_ = xs[p][c2:] + mm(m_rb[p], u[p], _NN) + mrkv[p]
            o_ref[0, rows, p * LANES:(p + 1) * LANES] = os_[:c] + os_[c:]
        for p in pairs:
            g_rem = jnp.exp(lc_end[p] - lc[p])
            uv = jnp.concatenate([u[p], vs[p]], axis=0)
            bk = jnp.concatenate([stack(b[p] * g_rem), stack(k[p] * g_rem)], axis=0)
            s_scr[p] = s[p] * jnp.exp(lc_end[p]) + mm(uv, bk, _TN)

    @pl.when(t == pl.num_programs(1) - 1)
    def _():
        sout_ref[0] = s_scr[...]


def _wkv(r, lw, k, v, kk, a, s0, c, exact):
    b, l, _ = r.shape
    rows = 2 * c if l % (2 * c) == 0 else c
    seq = pl.BlockSpec((1, rows, D_MODEL), lambda i, t: (i, t, 0))
    st = pl.BlockSpec((1, RWKV_HEADS // 2, LANES, LANES), lambda i, t: (i, 0, 0, 0))
    return pl.pallas_call(
        functools.partial(_wkv_kernel, c=c, exact=exact),
        grid=(b, l // rows),
        in_specs=[seq] * 6 + [st],
        out_specs=[seq, st],
        out_shape=[jax.ShapeDtypeStruct((b, l, D_MODEL), F32),
                   jax.ShapeDtypeStruct((b, RWKV_HEADS // 2, LANES, LANES), F32)],
        scratch_shapes=[pltpu.VMEM((RWKV_HEADS // 2, LANES, LANES), F32)],
        compiler_params=_cparams(("arbitrary", "arbitrary")),
        name="wkv",
    )(r, lw, k, v, kk, a, s0)


def _rwkv_out_kernel(o_ref, r_ref, k_ref, v_ref, g_ref, x_ref, rk_ref, gng_ref, gnb_ref, wo_ref,
                     ones_ref, lg_ref, lb_ref, y_ref):
    ones = ones_ref[...]
    o = o_ref[...]
    inv = 1.0 / RWKV_HEAD
    mu = _head_sum(o, ones) * inv
    oc = o - mu
    var = _head_sum(oc * oc, ones) * inv
    on = oc * lax.rsqrt(var + GN_EPS) * gng_ref[...] + gnb_ref[...]
    r, k, v, g = (ref[...].astype(F32) for ref in (r_ref, k_ref, v_ref, g_ref))
    on = on + _head_sum(r * k * rk_ref[...], ones) * v
    out = _bdot(on * g, wo_ref[...])
    y_ref[...] = _ln(ALPHA * x_ref[...] + out, lg_ref[...], lb_ref[...])


def _rwkv_out(o, r, k, v, g, x, od, lg, lb):
    n = x.shape[0]
    tm = _row_tile(n, 256)
    row = pl.BlockSpec((tm, D_MODEL), lambda i: (i, 0))
    ws = [od["r_k"], od["ln_g"], od["ln_b"], od["w_o"], od["ones"], lg, lb]
    return pl.pallas_call(
        _rwkv_out_kernel,
        grid=(n // tm,),
        in_specs=[row] * 6 + [_full(w.shape) for w in ws],
        out_specs=row,
        out_shape=jax.ShapeDtypeStruct((n, D_MODEL), F32),
        compiler_params=_cparams(("arbitrary",)),
        name="rwkv_out",
    )(o, r, k, v, g, x, *ws)


def _rope_tables(pos):
    half = D_ROPE // 2
    freq = ROPE_BASE ** (-jnp.arange(half, dtype=F32) / half)
    ang = pos.astype(F32)[:, None] * freq[None, :]
    cos, sin = lax.optimization_barrier((jnp.cos(ang), jnp.sin(ang)))
    n = pos.shape[0]
    ones = jnp.ones((n, D_NOPE), F32)
    zeros = jnp.zeros((n, D_NOPE), F32)
    z16 = jnp.zeros((n, half), F32)
    tail1 = jnp.ones((n, LANES - D_NOPE - D_ROPE), F32)
    tail0 = jnp.zeros((n, LANES - D_NOPE - D_ROPE), F32)
    c = jnp.concatenate([ones, cos, cos, tail1], axis=1)
    sa = jnp.concatenate([zeros, -sin, z16, tail0], axis=1)
    sb = jnp.concatenate([zeros, z16, sin, tail0], axis=1)
    return c, sa, sb


def _slot_cols(w, width):
    k, h, _ = w.shape
    return jnp.pad(w, ((0, 0), (0, 0), (0, HEAD_SLOT - width))).reshape(k, h * HEAD_SLOT)


def _block_diag(w):
    n, c, d = w.shape
    eye = jnp.eye(n, dtype=w.dtype)
    return (eye[:, None, :, None] * w[:, :, None, :]).reshape(n * c, n * d)


def _row2(v):
    return v.reshape(1, -1).astype(F32)


def _prep_even(w_in, g_q, w_uq, g_kv, w_uk, w_uv, conv_w, conv_b, w_rg, b_rg, w_ig, b_ig, lam, w_out):
    off_ckv, off_kr = D_CQ, D_CQ + D_C
    off_xr = off_kr + D_ROPE
    off_y = off_xr + D_RNN
    kr_cols = jnp.pad(w_in[:, off_kr:off_xr], ((0, 0), (D_NOPE, LANES - D_NOPE - D_ROPE)))
    w_in, w_uq, w_uk, w_uv, w_rg, w_ig, w_out = (
        w.astype(F32) for w in (w_in, w_uq, w_uk, w_uv, w_rg, w_ig, w_out))
    w1 = jnp.concatenate([w_in[:, :off_ckv], w_in[:, off_ckv:off_kr], w_in[:, off_xr:off_y],
                          w_in[:, off_y:], kr_cols], axis=1)
    wuq = _slot_cols(w_uq, D_NOPE + D_ROPE)
    wukv = jnp.concatenate([_slot_cols(w_uk, D_NOPE), _slot_cols(w_uv, D_V)], axis=1)
    place = np.zeros((LANES, D_ATT), np.float32)
    for h in range(MLA_HEADS):
        for cidx in range(D_ROPE):
            place[D_NOPE + cidx, h * HEAD_SLOT + D_NOPE + cidx] = 1.0
    wa = jnp.pad(w_out[:MLA_HEADS * D_V].reshape(MLA_HEADS, D_V, D_MODEL),
                 ((0, 0), (0, HEAD_SLOT - D_V), (0, 0))).reshape(D_ATT, D_MODEL)
    wr = w_out[MLA_HEADS * D_V:]
    return dict(
        w1=w1, gq=_row2(g_q), wuq=wuq, gkv=_row2(g_kv), wukv=wukv, place=jnp.asarray(place, F32),
        cw=conv_w.astype(F32), cb=_row2(conv_b), wrg=_block_diag(w_rg), brg=_row2(b_rg),
        wig=_block_diag(w_ig), big=_row2(b_ig), sp=_row2(jax.nn.softplus(-lam.astype(F32))),
        wa=wa, wr=wr)


_EVEN_MATMUL_WEIGHTS = ("w1", "wuq", "wukv", "place", "wrg", "wig", "wa", "wr")


def _single_pass(ev):
    return {n: (w.astype(BF16) if n in _EVEN_MATMUL_WEIGHTS else w) for n, w in ev.items()}


def _prep_odd(mu, w_r, w_k, w_v, w0, w1, w2, a0, a1, a2, g1, g2, k_k, k_a, r_k, ln_g, ln_b, w_o):
    ones = np.zeros((LANES, LANES), np.float32)
    ones[:RWKV_HEAD, :RWKV_HEAD] = 1.0
    ones[RWKV_HEAD:, RWKV_HEAD:] = 1.0
    return dict(
        mu=jnp.pad(mu.astype(F32), ((0, SUBLANES - mu.shape[0]), (0, 0))),
        w_r=w_r.astype(BF16), w_k=w_k.astype(BF16), w_v=w_v.astype(BF16), w0=_row2(w0),
        w1=w1.astype(BF16), w2=w2.astype(BF16), a0=_row2(a0), a1=a1.astype(BF16), a2=a2.astype(BF16),
        g1=g1.astype(BF16), g2=g2.astype(BF16), k_k=_row2(k_k), k_a=_row2(k_a), r_k=_row2(r_k),
        ln_g=_row2(ln_g), ln_b=_row2(ln_b), w_o=w_o.astype(BF16), ones=jnp.asarray(ones, BF16))


def _pair_states(s):
    b = s.shape[0]
    s = s.reshape(b, RWKV_HEADS // 2, 2, RWKV_HEAD, RWKV_HEAD).astype(F32)
    eye = jnp.eye(2, dtype=F32)
    out = s[:, :, :, :, None, :] * eye[None, None, :, None, :, None]
    return out.reshape(b, RWKV_HEADS // 2, LANES, LANES)


def _unpair_states(s):
    b = s.shape[0]
    s = s.reshape(b, RWKV_HEADS // 2, 2, RWKV_HEAD, 2, RWKV_HEAD)
    return jnp.stack([s[:, :, 0, :, 0, :], s[:, :, 1, :, 1, :]], axis=2).reshape(
        b, RWKV_HEADS, RWKV_HEAD, RWKV_HEAD)


def _round_up(n, m):
    return -(-n // m) * m


def kernel(x_prompt, x_sample, cache_ckv, cache_krope, state_conv, state_lru, state_shift, state_wkv,
           meta_tokens, ev_w_in, ev_g_q, ev_w_uq, ev_g_kv, ev_w_uk, ev_w_uv, ev_conv_w, ev_conv_b,
           ev_w_rg, ev_b_rg, ev_w_ig, ev_b_ig, ev_lru_lambda, ev_w_out, od_mu, od_w_r, od_w_k, od_w_v,
           od_w0, od_w1, od_w2, od_a0, od_a1, od_a2, od_g1, od_g2, od_k_k, od_k_a, od_r_k, od_ln_g,
           od_ln_b, od_w_o, ln_g, ln_b, router_w, router_b, exp_w_gate, exp_w_up, exp_w_down):
    assert x_prompt.shape[0] == 1 and x_prompt.shape[2] == D_MODEL
    seq = x_prompt.shape[1]
    assert seq % CHUNK == 0
    bs, ls, _ = x_sample.shape
    past = cache_ckv.shape[2]
    ns = bs * ls
    end = ROW0 + seq
    tp = _round_up(end, 512)

    ev = _prep_even(ev_w_in[0], ev_g_q[0], ev_w_uq[0], ev_g_kv[0], ev_w_uk[0], ev_w_uv[0], ev_conv_w[0],
                    ev_conv_b[0], ev_w_rg[0], ev_b_rg[0], ev_w_ig[0], ev_b_ig[0], ev_lru_lambda[0],
                    ev_w_out[0])
    od = _prep_odd(od_mu[0], od_w_r[0], od_w_k[0], od_w_v[0], od_w0[0], od_w1[0], od_w2[0], od_a0[0],
                   od_a1[0], od_a2[0], od_g1[0], od_g2[0], od_k_k[0], od_k_a[0], od_r_k[0], od_ln_g[0],
                   od_ln_b[0], od_w_o[0])
    rw = router_w.astype(F32)
    rb = _row2(router_b)
    wg, wu, wd = (w.reshape((DEPTH * N_EXPERTS,) + w.shape[2:]) for w in (exp_w_gate, exp_w_up, exp_w_down))
    lng = ln_g.astype(F32)[:, :, None, :]
    lnb = ln_b.astype(F32)[:, :, None, :]

    def moe(x, layer):
        sparse = x.shape[0] % (SC_WORKERS * SUBLANES) == 0 and x.shape[0] >= SPARSE_MIN_ROWS
        fn = _moe_sparse if sparse else _moe
        return fn(x, rw, rb, wg, wu, wd, layer * N_EXPERTS, lng[layer, 1], lnb[layer, 1])

    xp = jnp.concatenate([jnp.zeros((PAD_FRONT, D_MODEL), F32), meta_tokens.astype(F32),
                          x_prompt[0].astype(F32), jnp.zeros((tp - end, D_MODEL), F32)], axis=0)
    tabs_p = _rope_tables(jnp.maximum(jnp.arange(tp) - PAD_FRONT, 0))
    evb = _single_pass(ev)
    q_p, ckv_p, kr_p, xr_p, yg_p = _even_proj(xp, evb["w1"], ev["gq"], evb["wuq"], ev["gkv"], tabs_p,
                                              MLA_SCALE * LOG2E)
    ones_col = np.zeros((D_ATT, 1), np.float32)
    ones_col[D_V::HEAD_SLOT] = 1.0
    k_p, vt_p = _kv_proj_t(ckv_p, kr_p, evb["wukv"][:, :D_ATT], evb["place"], evb["wukv"][:, D_ATT:].T,
                           jnp.asarray(ones_col))
    attn_p = _flash_attention(q_p, k_p, vt_p, 512)
    rnn_p, tailx_p, tailh_p = _rglru(
        xr_p[None], yg_p[None], ev["cw"], ev["cb"], evb["wrg"], ev["brg"], evb["wig"], ev["big"], ev["sp"],
        jnp.zeros((1, SUBLANES, D_RNN), F32), jnp.zeros((1, 1, D_RNN), F32), PAD_FRONT, end)
    x1_p = _mix_out(attn_p, rnn_p[0], xp, evb["wa"], evb["wr"], lng[0, 0], lnb[0, 0])
    x2_p = moe(x1_p, 0)

    xs = x_sample.reshape(ns, D_MODEL).astype(F32)
    pos_s = jnp.tile(N_META + past + jnp.arange(ls), bs)
    q_s, ckv_s, kr_s, xr_s, yg_s = _even_proj(xs, ev["w1"], ev["gq"], ev["wuq"], ev["gkv"], _rope_tables(pos_s),
                                              MLA_SCALE)
    _, ckv_m, kr_m, _, _ = _even_proj(meta_tokens.astype(F32), ev["w1"], ev["gq"], ev["wuq"], ev["gkv"],
                                      _rope_tables(jnp.arange(N_META)), MLA_SCALE)
    n_keys = N_META + past + ls
    nk_pad = _round_up(n_keys, LANES)
    meta_ckv = jnp.broadcast_to(ckv_m[None], (bs, N_META, D_C))
    meta_kr = jnp.broadcast_to(kr_m[None], (bs, N_META, LANES))
    cache_kr = jnp.pad(cache_krope[0].astype(F32), ((0, 0), (0, 0), (D_NOPE, LANES - D_NOPE - D_ROPE)))
    all_ckv = jnp.concatenate([meta_ckv, cache_ckv[0].astype(F32), ckv_s.reshape(bs, ls, D_C),
                               jnp.zeros((bs, nk_pad - n_keys, D_C), F32)], axis=1)
    all_kr = jnp.concatenate([meta_kr, cache_kr, kr_s.reshape(bs, ls, LANES),
                              jnp.zeros((bs, nk_pad - n_keys, LANES), F32)], axis=1)
    k_s, v_s = _kv_proj(all_ckv.reshape(bs * nk_pad, D_C), all_kr.reshape(bs * nk_pad, LANES),
                        ev["wukv"], ev["place"])
    attn_s = _sample_attention(q_s.reshape(bs, ls, D_ATT), k_s.reshape(bs, nk_pad, D_ATT),
                               v_s.reshape(bs, nk_pad, D_ATT), n_keys)
    buf0_s = jnp.pad(state_conv[0].astype(F32), ((0, 0), (SUBLANES - (CONV_W - 1), 0), (0, 0)))
    rnn_s, tailx_s, tailh_s = _rglru(
        xr_s.reshape(bs, ls, D_RNN), yg_s.reshape(bs, ls, D_RNN), ev["cw"], ev["cb"], ev["wrg"], ev["brg"],
        ev["wig"], ev["big"], ev["sp"], buf0_s, state_lru[0].astype(F32)[:, None, :], 0, ls)
    x1_s = _mix_out(attn_s.reshape(ns, D_ATT), rnn_s.reshape(ns, D_RNN), xs, ev["wa"], ev["wr"],
                    lng[0, 0], lnb[0, 0])
    x2_s = moe(x1_s, 0)

    r_p, lw_p, kk_in_p, v1_p, kkn_p, a_p, g_p = _rwkv_proj(
        x2_p[None], jnp.zeros((1, 1, D_MODEL), F32), od, PAD_FRONT, end, BF16)
    o_p, s_p = _wkv(r_p, lw_p, kk_in_p, v1_p, kkn_p, a_p,
                    jnp.zeros((1, RWKV_HEADS // 2, LANES, LANES), F32), CHUNK, False)
    x3_p = _rwkv_out(o_p[0], r_p[0], kk_in_p[0], v1_p[0], g_p[0], x2_p, od, lng[1, 0], lnb[1, 0])
    x4_p = moe(x3_p, 1)

    x2_s3 = x2_s.reshape(bs, ls, D_MODEL)
    r_s, lw_s, kk_in_s, v1_s, kkn_s, a_s, g_s = _rwkv_proj(
        x2_s3, state_shift[0].astype(F32)[:, None, :], od, 0, ls, F32)
    to_chunk = lambda z: jnp.pad(z, ((0, 0), (0, _round_up(ls, CHUNK) - ls), (0, 0)))
    o_s, s_s = _wkv(*(to_chunk(z) for z in (r_s, lw_s, kk_in_s, v1_s, kkn_s, a_s)),
                    _pair_states(state_wkv[0]), CHUNK, True)
    o_s = o_s[:, :ls]
    flat = lambda z: z.reshape(ns, D_MODEL)
    x3_s = _rwkv_out(flat(o_s), flat(r_s), flat(kk_in_s), flat(v1_s), flat(g_s), x2_s, od,
                     lng[1, 0], lnb[1, 0])
    x4_s = moe(x3_s, 1)

    dt = x_prompt.dtype
    nb = CONV_W - 1
    return (
        x4_p[ROW0:end][None].astype(dt),
        x4_s.reshape(bs, ls, D_MODEL).astype(dt),
        ckv_p[PAD_FRONT:end][None, None].astype(dt),
        kr_p[PAD_FRONT:end, D_NOPE:D_NOPE + D_ROPE][None, None].astype(dt),
        tailx_p[:, SUBLANES - nb:][None].astype(dt),
        tailh_p[:, SUBLANES - 1][None].astype(dt),
        x2_p[end - 1][None, None].astype(dt),
        _unpair_states(s_p)[None].astype(dt),
        ckv_s.reshape(bs, ls, D_C)[None].astype(dt),
        kr_s.reshape(bs, ls, LANES)[:, :, D_NOPE:D_NOPE + D_ROPE][None].astype(dt),
        tailx_s[:, SUBLANES - nb:][None].astype(dt),
        tailh_s[:, SUBLANES - 1][None].astype(dt),
        x2_s3[:, ls - 1][None].astype(dt),
        _unpair_states(s_s)[None].astype(dt),
    )
```

```python
import functools

import numpy as np
import jax
import jax.numpy as jnp
from jax import lax
from jax.experimental import pallas as pl
from jax.experimental.pallas import tpu as pltpu
from jax.experimental.pallas import tpu_sc as plsc

F32 = jnp.float32
BF16 = jnp.bfloat16

D_MODEL = 1024
N_META = 16
CHUNK = 64
CHUNK_SHIFT = 6
LN_EPS = 1e-5
RMS_EPS = 1e-6
DEPTH = 2
ALPHA = (2 * DEPTH) ** 0.25
MLA_HEADS = 8
D_NOPE = 64
D_ROPE = 32
D_V = 64
D_C = 256
D_CQ = 384
ROPE_BASE = 10000.0
MLA_SCALE = (D_NOPE + D_ROPE) ** -0.5
D_RNN = 512
LRU_BLOCKS = 8
LRU_BLOCK_W = D_RNN // LRU_BLOCKS
CONV_W = 4
LRU_C = 8.0
RWKV_HEAD = 64
RWKV_HEADS = D_MODEL // RWKV_HEAD
DECAY_SCALE = float(np.exp(-0.5))
GN_EPS = 64e-5
N_EXPERTS = 16
N_GROUPS = 4
EXPERTS_PER_GROUP = N_EXPERTS // N_GROUPS
D_EXPERT = 512

LANES = 128
SUBLANES = 8
HEAD_SLOT = LANES
D_ATT = MLA_HEADS * HEAD_SLOT
PAD_FRONT = CHUNK - N_META
ROW0 = PAD_FRONT + N_META
NEG = -1e30
LOG2E = 1.4426950408889634
SC_CORES = 2
SC_SUBCORES = 16
SC_WORKERS = SC_CORES * SC_SUBCORES
MOE_TILE = 512
FLASH_TQ = 512
FLASH_TK = 1024
SPARSE_MIN_ROWS = 1024
VMEM_LIMIT = 56 * 1024 * 1024

C_CQ = 0
C_CKV = D_CQ
C_XR = C_CKV + D_C
C_YG = C_XR + D_RNN
C_KR = C_YG + D_RNN
N_COL = C_KR + LANES


def _cparams(sem):
    return pltpu.CompilerParams(dimension_semantics=sem, vmem_limit_bytes=VMEM_LIMIT)


def _row_tile(n, cap):
    for t in (1024, 512, 256, 128, 64, 32, 16, 8):
        if t <= cap and n % t == 0:
            return t
    return n


def _full(shape):
    zeros = (0,) * len(shape)
    return pl.BlockSpec(shape, lambda *_: zeros)


def _ln(x, g, b):
    mu = jnp.mean(x, axis=-1, keepdims=True)
    xc = x - mu
    var = jnp.mean(xc * xc, axis=-1, keepdims=True)
    return xc * lax.rsqrt(var + LN_EPS) * g + b


def _bdot(a, b):
    return jnp.dot(a.astype(BF16), b.astype(BF16), preferred_element_type=F32)


def _split2(x):
    hi = x.astype(BF16)
    return hi, (x - hi.astype(F32)).astype(BF16)


def _split3(x):
    hi = x.astype(BF16)
    r1 = x - hi.astype(F32)
    mid = r1.astype(BF16)
    return hi, mid, (r1 - mid.astype(F32)).astype(BF16)


_NN = ((1,), (0,))
_NT = ((1,), (1,))
_TN = ((0,), (0,))


def _mm(a, b, dims, exact):
    dn = (dims, ((), ()))
    if not exact:
        return lax.dot_general(a.astype(BF16), b.astype(BF16), dn, preferred_element_type=F32)
    ah, al = _split2(a)
    bh, bl = _split2(b)
    return (lax.dot_general(ah, bh, dn, preferred_element_type=F32)
            + lax.dot_general(al, bh, dn, preferred_element_type=F32)
            + lax.dot_general(ah, bl, dn, preferred_element_type=F32))


def _wdot(a, w):
    return _mm(a, w, _NN, exact=(w.dtype == F32))


def _act_dtype(w):
    return F32 if w.dtype == F32 else BF16


def _rope_slot(x, c, sa, sb):
    return x * c + pltpu.roll(x, LANES - D_ROPE // 2, 1) * sa + pltpu.roll(x, D_ROPE // 2, 1) * sb


def _even_proj_kernel(x_ref, w1_ref, gq_ref, wuq_ref, gkv_ref, c_ref, sa_ref, sb_ref,
                      q_ref, ckv_ref, kr_ref, xr_ref, yg_ref, *, q_scale):
    u = _wdot(x_ref[...], w1_ref[...])
    cq = u[:, C_CQ:C_CQ + D_CQ]
    cq = cq * lax.rsqrt(jnp.mean(cq * cq, axis=-1, keepdims=True) + RMS_EPS) * gq_ref[...]
    q = _wdot(cq, wuq_ref[...])
    c, sa, sb = c_ref[...], sa_ref[...], sb_ref[...]
    for h in range(MLA_HEADS):
        sl = slice(h * HEAD_SLOT, (h + 1) * HEAD_SLOT)
        q_ref[:, sl] = (_rope_slot(q[:, sl], c, sa, sb) * q_scale).astype(q_ref.dtype)
    ckv = u[:, C_CKV:C_CKV + D_C]
    ckv_ref[...] = ckv * lax.rsqrt(jnp.mean(ckv * ckv, axis=-1, keepdims=True) + RMS_EPS) * gkv_ref[...]
    kr_ref[...] = _rope_slot(u[:, C_KR:C_KR + LANES], c, sa, sb)
    xr_ref[...] = u[:, C_XR:C_XR + D_RNN]
    yg_ref[...] = u[:, C_YG:C_YG + D_RNN]


def _even_proj(x, w1, gq, wuq, gkv, tabs, q_scale):
    n = x.shape[0]
    tm = _row_tile(n, 512)
    row = lambda w: pl.BlockSpec((tm, w), lambda i: (i, 0))
    c, sa, sb = tabs
    return pl.pallas_call(
        functools.partial(_even_proj_kernel, q_scale=q_scale),
        grid=(n // tm,),
        in_specs=[row(D_MODEL), _full(w1.shape), _full(gq.shape), _full(wuq.shape), _full(gkv.shape),
                  row(LANES), row(LANES), row(LANES)],
        out_specs=[row(D_ATT), row(D_C), row(LANES), row(D_RNN), row(D_RNN)],
        out_shape=[jax.ShapeDtypeStruct((n, D_ATT), _act_dtype(w1)), jax.ShapeDtypeStruct((n, D_C), F32),
                   jax.ShapeDtypeStruct((n, LANES), F32), jax.ShapeDtypeStruct((n, D_RNN), F32),
                   jax.ShapeDtypeStruct((n, D_RNN), F32)],
        compiler_params=_cparams(("arbitrary",)),
        name="even_proj",
    )(x, w1, gq, wuq, gkv, c, sa, sb)


def _kv_proj_kernel(ckv_ref, kr_ref, wukv_ref, p_ref, k_ref, v_ref):
    kv = _wdot(ckv_ref[...], wukv_ref[...])
    k_ref[...] = (kv[:, :D_ATT] + _wdot(kr_ref[...], p_ref[...])).astype(k_ref.dtype)
    v_ref[...] = kv[:, D_ATT:].astype(v_ref.dtype)


def _kv_proj(ckv, kr, wukv, place):
    n = ckv.shape[0]
    tm = _row_tile(n, 512)
    row = lambda w: pl.BlockSpec((tm, w), lambda i: (i, 0))
    return pl.pallas_call(
        _kv_proj_kernel,
        grid=(n // tm,),
        in_specs=[row(D_C), row(LANES), _full(wukv.shape), _full(place.shape)],
        out_specs=[row(D_ATT), row(D_ATT)],
        out_shape=[jax.ShapeDtypeStruct((n, D_ATT), _act_dtype(wukv))] * 2,
        compiler_params=_cparams(("arbitrary",)),
        name="kv_proj",
    )(ckv, kr, wukv, place)


def _kv_proj_t_kernel(ckv_ref, kr_ref, wuk_ref, p_ref, wuvt_ref, ones_ref, k_ref, vt_ref):
    ckv = ckv_ref[...].astype(BF16)
    k = jnp.dot(ckv, wuk_ref[...], preferred_element_type=F32) + _bdot(kr_ref[...], p_ref[...])
    k_ref[...] = k.astype(BF16)
    vt = lax.dot_general(wuvt_ref[...], ckv, (_NT, ((), ())), preferred_element_type=F32)
    vt_ref[...] = (vt + ones_ref[...]).astype(BF16)


def _kv_proj_t(ckv, kr, wuk, place, wuvt, ones_col, n_out):
    n = ckv.shape[0]
    tm = _row_tile(n, 512)
    assert n_out % tm == 0
    last = n // tm - 1
    row_in = lambda w: pl.BlockSpec((tm, w), lambda i: (jnp.minimum(i, last), 0))
    return pl.pallas_call(
        _kv_proj_t_kernel,
        grid=(n_out // tm,),
        in_specs=[row_in(D_C), row_in(LANES), _full(wuk.shape), _full(place.shape), _full(wuvt.shape),
                  _full(ones_col.shape)],
        out_specs=[pl.BlockSpec((tm, D_ATT), lambda i: (i, 0)), pl.BlockSpec((D_ATT, tm), lambda i: (0, i))],
        out_shape=[jax.ShapeDtypeStruct((n_out, D_ATT), BF16), jax.ShapeDtypeStruct((D_ATT, n_out), BF16)],
        compiler_params=_cparams(("arbitrary",)),
        name="kv_proj_t",
    )(ckv, kr, wuk, place, wuvt, ones_col)


def _flash_kernel(qi_ref, kj_ref, last_ref, q_ref, k_ref, vt_ref, o_ref, m_scr, acc_scr, *, tq, tk):
    step = pl.program_id(0)
    i = qi_ref[step]
    j = kj_ref[step]

    @pl.when(j == 0)
    def _():
        m_scr[...] = jnp.full(m_scr.shape, NEG, F32)
        acc_scr[...] = jnp.zeros(acc_scr.shape, F32)

    heads = range(MLA_HEADS)
    slots = [slice(h * HEAD_SLOT, (h + 1) * HEAD_SLOT) for h in heads]

    def accumulate(masked):
        st = [lax.dot_general(k_ref[:, sl], q_ref[:, sl], (_NT, ((), ())), preferred_element_type=F32)
              for sl in slots]
        if masked:
            krow = j * tk + lax.broadcasted_iota(jnp.int32, (tk, tq), 0)
            qrow = i * tq + lax.broadcasted_iota(jnp.int32, (tk, tq), 1)
            keep = ((((qrow - ROW0) >> CHUNK_SHIFT) >= ((krow - ROW0) >> CHUNK_SHIFT))
                    & (krow >= PAD_FRONT))
            st = [jnp.where(keep, x, NEG) for x in st]
        m_prev = [m_scr[h:h + 1, :] for h in heads]
        m_new = [jnp.maximum(mp, jnp.max(x, axis=0, keepdims=True)) for mp, x in zip(m_prev, st)]
        alpha = [jnp.exp2(mp - mn) for mp, mn in zip(m_prev, m_new)]
        pt = [jnp.exp2(x - mn).astype(BF16) for x, mn in zip(st, m_new)]
        pv = [jnp.dot(vt_ref[sl, :], x, preferred_element_type=F32) for x, sl in zip(pt, slots)]
        for h in heads:
            acc_scr[slots[h], :] = alpha[h] * acc_scr[slots[h], :] + pv[h]
            m_scr[h:h + 1, :] = m_new[h]

    last = last_ref[step] == 1
    edge = last | (j == 0)

    @pl.when(edge)
    def _():
        accumulate(True)

    @pl.when(jnp.logical_not(edge))
    def _():
        accumulate(False)

    @pl.when(last)
    def _():
        for sl in slots:
            acc = acc_scr[sl, :]
            o_ref[:, sl] = (acc / acc[D_V:D_V + 1, :]).T.astype(BF16)


def _flash_attention(q, k, vt, tq, tk):
    n = q.shape[0]
    nq = n // tq
    n_kblocks = [-(-(i + 1) * tq // tk) for i in range(nq)]
    assert k.shape[0] >= n_kblocks[-1] * tk
    qi = np.concatenate([np.full(c, i, np.int32) for i, c in enumerate(n_kblocks)])
    kj = np.concatenate([np.arange(c, dtype=np.int32) for c in n_kblocks])
    last = np.concatenate([np.arange(c, dtype=np.int32) == c - 1 for c in n_kblocks]).astype(np.int32)
    grid_spec = pltpu.PrefetchScalarGridSpec(
        num_scalar_prefetch=3,
        grid=(len(qi),),
        in_specs=[pl.BlockSpec((tq, D_ATT), lambda s, qi, kj, last: (qi[s], 0)),
                  pl.BlockSpec((tk, D_ATT), lambda s, qi, kj, last: (kj[s], 0)),
                  pl.BlockSpec((D_ATT, tk), lambda s, qi, kj, last: (0, kj[s]))],
        out_specs=pl.BlockSpec((tq, D_ATT), lambda s, qi, kj, last: (qi[s], 0)),
        scratch_shapes=[pltpu.VMEM((MLA_HEADS, tq), F32), pltpu.VMEM((D_ATT, tq), F32)],
    )
    return pl.pallas_call(
        functools.partial(_flash_kernel, tq=tq, tk=tk),
        grid_spec=grid_spec,
        out_shape=jax.ShapeDtypeStruct((n, D_ATT), BF16),
        compiler_params=_cparams(("arbitrary",)),
        name="flash_attention",
    )(jnp.asarray(qi), jnp.asarray(kj), jnp.asarray(last), q, k, vt)


def _sample_attn_kernel(q_ref, k_ref, v_ref, o_ref, *, n_keys):
    nk = k_ref.shape[1]
    exact = k_ref.dtype == F32
    keep = lax.broadcasted_iota(jnp.int32, (q_ref.shape[1], nk), 1) < n_keys
    for h in range(MLA_HEADS):
        sl = slice(h * HEAD_SLOT, (h + 1) * HEAD_SLOT)
        s = _mm(q_ref[0, :, sl], k_ref[0, :, sl], _NT, exact)
        s = jnp.where(keep, s, NEG)
        p = jnp.exp(s - jnp.max(s, axis=-1, keepdims=True))
        p = p / jnp.sum(p, axis=-1, keepdims=True)
        o_ref[0, :, sl] = _mm(p, v_ref[0, :, sl], _NN, exact).astype(o_ref.dtype)


def _sample_attention(q, k, v, n_keys):
    b, l, _ = q.shape
    nk = k.shape[1]
    return pl.pallas_call(
        functools.partial(_sample_attn_kernel, n_keys=n_keys),
        grid=(b,),
        in_specs=[pl.BlockSpec((1, l, D_ATT), lambda i: (i, 0, 0)),
                  pl.BlockSpec((1, nk, D_ATT), lambda i: (i, 0, 0)),
                  pl.BlockSpec((1, nk, D_ATT), lambda i: (i, 0, 0))],
        out_specs=pl.BlockSpec((1, l, D_ATT), lambda i: (i, 0, 0)),
        out_shape=jax.ShapeDtypeStruct((b, l, D_ATT), q.dtype),
        compiler_params=_cparams(("arbitrary",)),
        name="sample_attention",
    )(q, k, v)


def _expm1(x):
    series = x * (1.0 + x * (0.5 + x * (1.0 / 6.0 + x * (1.0 / 24.0 + x * (1.0 / 120.0)))))
    return jnp.where(jnp.abs(x) < 0.05, series, jnp.exp(x) - 1.0)


def _gelu_tanh(x):
    return 0.5 * x * (1.0 + jnp.tanh(0.7978845608028654 * (x + 0.044715 * x * x * x)))


def _rglru_kernel(xr_ref, yg_ref, cw_ref, cb_ref, wrg_ref, brg_ref, wig_ref, big_ref, sp_ref,
                  buf0_ref, h0_ref, rnn_ref, tailx_ref, tailh_ref, prev_scr, h_scr,
                  *, tm, start, end):
    t = pl.program_id(1)

    @pl.when(t == 0)
    def _():
        prev_scr[...] = buf0_ref[0]
        h_scr[...] = jnp.broadcast_to(h0_ref[0], h_scr.shape)

    x = xr_ref[0]
    ext = jnp.concatenate([prev_scr[...], x], axis=0)
    cw = cw_ref[...]
    xc = cb_ref[...] + cw[CONV_W - 1:CONV_W] * x
    for d in range(1, CONV_W):
        xc = xc + cw[CONV_W - 1 - d:CONV_W - d] * pltpu.roll(ext, d, 0)[SUBLANES:]
    prev_scr[...] = x[tm - SUBLANES:]

    r = jax.nn.sigmoid(_wdot(xc, wrg_ref[...]) + brg_ref[...])
    ig = jax.nn.sigmoid(_wdot(xc, wig_ref[...]) + big_ref[...])
    log_a = -LRU_C * r * sp_ref[...]
    a = jnp.exp(log_a)
    b = jnp.sqrt(-_expm1(2.0 * log_a)) * (ig * xc)
    row = lax.broadcasted_iota(jnp.int32, (tm, D_RNN), 0)
    if start > 0:
        live = (t * tm + row) >= start
        a = jnp.where(live, a, 1.0)
        b = jnp.where(live, b, 0.0)
    d = 1
    while d < tm:
        b = a * jnp.where(row >= d, pltpu.roll(b, d, 0), 0.0) + b
        a = a * jnp.where(row >= d, pltpu.roll(a, d, 0), 1.0)
        d *= 2
    h = a * h_scr[0:1] + b
    h_scr[...] = jnp.broadcast_to(h[tm - 1:tm], h_scr.shape)
    rnn_ref[0] = (h * _gelu_tanh(yg_ref[0])).astype(rnn_ref.dtype)

    t_end = (end - 1) // tm
    el = end - t_end * tm

    @pl.when(t == t_end)
    def _():
        tailx_ref[0] = ext[el:el + SUBLANES]
        tailh_ref[0] = h[el - SUBLANES:el]


def _rglru(xr, yg, cw, cb, wrg, brg, wig, big, sp, buf0, h0, start, end):
    b, l, _ = xr.shape
    tm = _row_tile(l, 512)
    seq = pl.BlockSpec((1, tm, D_RNN), lambda i, t: (i, t, 0))
    per_b = lambda r: pl.BlockSpec((1, r, D_RNN), lambda i, t: (i, 0, 0))
    return pl.pallas_call(
        functools.partial(_rglru_kernel, tm=tm, start=start, end=end),
        grid=(b, l // tm),
        in_specs=[seq, seq, _full(cw.shape), _full(cb.shape), _full(wrg.shape), _full(brg.shape),
                  _full(wig.shape), _full(big.shape), _full(sp.shape), per_b(SUBLANES), per_b(1)],
        out_specs=[seq, per_b(SUBLANES), per_b(SUBLANES)],
        out_shape=[jax.ShapeDtypeStruct((b, l, D_RNN), _act_dtype(wrg)),
                   jax.ShapeDtypeStruct((b, SUBLANES, D_RNN), F32),
                   jax.ShapeDtypeStruct((b, SUBLANES, D_RNN), F32)],
        scratch_shapes=[pltpu.VMEM((SUBLANES, D_RNN), F32), pltpu.VMEM((SUBLANES, D_RNN), F32)],
        compiler_params=_cparams(("arbitrary", "arbitrary")),
        name="rglru",
    )(xr, yg, cw, cb, wrg, brg, wig, big, sp, buf0, h0)


def _mix_out_kernel(attn_ref, rnn_ref, x_ref, wa_ref, wr_ref, g_ref, b_ref, o_ref):
    mix = _wdot(attn_ref[...], wa_ref[...]) + _wdot(rnn_ref[...], wr_ref[...])
    o_ref[...] = _ln(ALPHA * x_ref[...] + mix, g_ref[...], b_ref[...])


def _mix_out(attn, rnn, x, wa, wr, g, b):
    n = x.shape[0]
    tm = _row_tile(n, 512)
    row = lambda w: pl.BlockSpec((tm, w), lambda i: (i, 0))
    return pl.pallas_call(
        _mix_out_kernel,
        grid=(n // tm,),
        in_specs=[row(D_ATT), row(D_RNN), row(D_MODEL), _full(wa.shape), _full(wr.shape),
                  _full(g.shape), _full(b.shape)],
        out_specs=row(D_MODEL),
        out_shape=jax.ShapeDtypeStruct((n, D_MODEL), F32),
        compiler_params=_cparams(("arbitrary",)),
        name="mix_out",
    )(attn, rnn, x, wa, wr, g, b)


def _first_argmax(vals, lane):
    m = jnp.max(vals, axis=-1, keepdims=True)
    idx = jnp.min(jnp.where(vals == m, lane, N_EXPERTS), axis=-1, keepdims=True)
    return m, idx


def _router_top2(x, rw, rb):
    logits = jnp.dot(x, rw, preferred_element_type=F32, precision=lax.Precision.HIGHEST)
    s = jax.nn.sigmoid(logits)
    sel = s + rb
    lane = lax.broadcasted_iota(jnp.int32, sel.shape, 1)
    grp = lane >> 2
    best = None
    g_best = None
    for g in range(N_GROUPS):
        vals = jnp.where(grp == g, sel, NEG)
        m1, i1 = _first_argmax(vals, lane)
        m2, _ = _first_argmax(jnp.where(lane == i1, NEG, vals), lane)
        score = m1 + m2
        if g == 0:
            best, g_best = score, jnp.zeros_like(i1)
        else:
            upd = score > best
            g_best = jnp.where(upd, g, g_best)
            best = jnp.where(upd, score, best)
    vals = jnp.where(grp == g_best, sel, NEG)
    _, i1 = _first_argmax(vals, lane)
    _, i2 = _first_argmax(jnp.where(lane == i1, NEG, vals), lane)
    w1 = jnp.sum(jnp.where(lane == i1, s, 0.0), axis=-1, keepdims=True)
    w2 = jnp.sum(jnp.where(lane == i2, s, 0.0), axis=-1, keepdims=True)
    den = w1 + w2
    return lane, i1, i2, w1 / den, w2 / den


def _router_gate(x, rw, rb):
    lane, i1, i2, g1, g2 = _router_top2(x, rw, rb)
    return jnp.where(lane == i1, g1, 0.0) + jnp.where(lane == i2, g2, 0.0)


def _moe_kernel(x_ref, rw_ref, rb_ref, wg_ref, wu_ref, wd_ref, g_ref, b_ref, o_ref,
                gate_scr, xb_scr, acc_scr):
    e = pl.program_id(1)

    @pl.when(e == 0)
    def _():
        x = x_ref[...]
        gate = _router_gate(x, rw_ref[...], rb_ref[...])
        for k in range(N_EXPERTS):
            gate_scr[k] = jnp.broadcast_to(gate[:, k:k + 1], gate_scr.shape[1:])
        xb_scr[...] = x.astype(BF16)
        acc_scr[...] = jnp.zeros(acc_scr.shape, F32)

    xb = xb_scr[...]
    hg = jnp.dot(xb, wg_ref[0].astype(BF16), preferred_element_type=F32)
    hu = jnp.dot(xb, wu_ref[0].astype(BF16), preferred_element_type=F32)
    gate_e = gate_scr[e]
    h = jax.nn.silu(hg) * hu * jnp.concatenate([gate_e] * (D_EXPERT // LANES), axis=1)
    acc_scr[...] += jnp.dot(h.astype(BF16), wd_ref[0].astype(BF16), preferred_element_type=F32)

    @pl.when(e == N_EXPERTS - 1)
    def _():
        o_ref[...] = _ln(ALPHA * x_ref[...] + acc_scr[...], g_ref[...], b_ref[...])


def _moe(x, rw, rb, wg, wu, wd, e0, g, b):
    n = x.shape[0]
    tm = _row_tile(n, 512)
    row = pl.BlockSpec((tm, D_MODEL), lambda i, e: (i, 0))
    return pl.pallas_call(
        _moe_kernel,
        grid=(n // tm, N_EXPERTS),
        in_specs=[row, _full(rw.shape), _full(rb.shape),
                  pl.BlockSpec((1, D_MODEL, D_EXPERT), lambda i, e: (e0 + e, 0, 0)),
                  pl.BlockSpec((1, D_MODEL, D_EXPERT), lambda i, e: (e0 + e, 0, 0)),
                  pl.BlockSpec((1, D_EXPERT, D_MODEL), lambda i, e: (e0 + e, 0, 0)),
                  _full(g.shape), _full(b.shape)],
        out_specs=row,
        out_shape=jax.ShapeDtypeStruct((n, D_MODEL), F32),
        scratch_shapes=[pltpu.VMEM((N_EXPERTS, tm, LANES), F32), pltpu.VMEM((tm, D_MODEL), BF16),
                        pltpu.VMEM((tm, D_MODEL), F32)],
        compiler_params=_cparams(("arbitrary", "arbitrary")),
        name="moe",
    )(x, rw, rb, wg, wu, wd, g, b)


M_I1, M_I2, M_R1, M_R2, M_G1, M_G2, M_COLS = 0, 1, 2, 3, 4, 5, 8


def _first_argmax_rows(vals, row):
    m = jnp.max(vals, axis=0, keepdims=True)
    idx = jnp.min(jnp.where(vals == m, row, N_EXPERTS), axis=0, keepdims=True)
    return m, idx


def _route_kernel(x_ref, rwt_ref, rbc_ref, meta_ref, cnt_ref, carry_scr, *, tm):
    @pl.when(pl.program_id(0) == 0)
    def _():
        carry_scr[...] = jnp.zeros(carry_scr.shape, F32)

    logits = lax.dot_general(rwt_ref[...], x_ref[...], (_NT, ((), ())), preferred_element_type=F32,
                             precision=lax.Precision.HIGHEST)
    s = jax.nn.sigmoid(logits)
    sel = s + rbc_ref[...]
    row = lax.broadcasted_iota(jnp.int32, sel.shape, 0)
    grp = row >> 2
    best = None
    g_best = None
    for g in range(N_GROUPS):
        vals = jnp.where(grp == g, sel, NEG)
        m1, i1 = _first_argmax_rows(vals, row)
        m2, _ = _first_argmax_rows(jnp.where(row == i1, NEG, vals), row)
        score = m1 + m2
        if g == 0:
            best, g_best = score, jnp.zeros_like(i1)
        else:
            upd = score > best
            g_best = jnp.where(upd, g, g_best)
            best = jnp.where(upd, score, best)
    vals = jnp.where(grp == g_best, sel, NEG)
    _, i1 = _first_argmax_rows(vals, row)
    _, i2 = _first_argmax_rows(jnp.where(row == i1, NEG, vals), row)
    w1 = jnp.sum(jnp.where(row == i1, s, 0.0), axis=0, keepdims=True)
    w2 = jnp.sum(jnp.where(row == i2, s, 0.0), axis=0, keepdims=True)
    den = w1 + w2

    chosen = jnp.where((row == i1) | (row == i2), 1.0, 0.0)
    earlier = (lax.broadcasted_iota(jnp.int32, (tm, tm), 0)
               < lax.broadcasted_iota(jnp.int32, (tm, tm), 1)).astype(BF16)
    seen = jnp.dot(chosen.astype(BF16), earlier, preferred_element_type=F32) + carry_scr[:, 0:1]
    r1 = jnp.sum(jnp.where(row == i1, seen, 0.0), axis=0, keepdims=True)
    r2 = jnp.sum(jnp.where(row == i2, seen, 0.0), axis=0, keepdims=True)
    carry_scr[...] = carry_scr[...] + jnp.sum(chosen, axis=1, keepdims=True)
    mrow = lax.broadcasted_iota(jnp.int32, (M_COLS, tm), 0)
    meta = jnp.zeros((M_COLS, tm), F32)
    for c, val in ((M_I1, i1.astype(F32)), (M_I2, i2.astype(F32)), (M_R1, r1), (M_R2, r2),
                   (M_G1, w1 / den), (M_G2, w2 / den)):
        meta = jnp.where(mrow == c, val, meta)
    meta_ref[...] = meta
    cnt_ref[...] = carry_scr[...]


def _route(x, rwt, rbc):
    n = x.shape[0]
    tm = _row_tile(n, 512)
    return pl.pallas_call(
        functools.partial(_route_kernel, tm=tm),
        grid=(n // tm,),
        in_specs=[pl.BlockSpec((tm, D_MODEL), lambda i: (i, 0)), _full(rwt.shape), _full(rbc.shape)],
        out_specs=[pl.BlockSpec((M_COLS, tm), lambda i: (0, i)), _full((N_EXPERTS, LANES))],
        out_shape=[jax.ShapeDtypeStruct((M_COLS, n), F32), jax.ShapeDtypeStruct((N_EXPERTS, LANES), F32)],
        scratch_shapes=[pltpu.VMEM((N_EXPERTS, LANES), F32)],
        compiler_params=_cparams(("arbitrary",)),
        name="moe_route",
    )(x, rwt, rbc)


def _sc_chunk(per_worker):
    for c in (64, 48, 32, 16, 8):
        if per_worker % c == 0:
            return c
    raise ValueError(per_worker)


def _sc_mesh():
    return plsc.VectorSubcoreMesh(core_axis_name="c", subcore_axis_name="s")


def _sc_scatter2(x, idx1, idx2, n_out):
    n, d = x.shape
    per_w = n // SC_WORKERS
    assert per_w * SC_WORKERS == n
    chunk = _sc_chunk(per_w)

    @functools.partial(
        pl.kernel, mesh=_sc_mesh(), out_type=jax.ShapeDtypeStruct((n_out, d), x.dtype),
        scratch_types=[pltpu.VMEM((chunk,), jnp.int32), pltpu.VMEM((chunk,), jnp.int32),
                       pltpu.VMEM((chunk, d), x.dtype), pltpu.SemaphoreType.DMA])
    def scatter(x_hbm, i1_hbm, i2_hbm, out_hbm, i1_v, i2_v, rows_v, sem):
        base = (lax.axis_index("s") * SC_CORES + lax.axis_index("c")) * per_w

        @pl.loop(0, per_w // chunk)
        def _(c):
            off = pl.multiple_of(base + c * chunk, SUBLANES)
            pltpu.sync_copy(i1_hbm.at[pl.ds(off, chunk)], i1_v)
            pltpu.sync_copy(i2_hbm.at[pl.ds(off, chunk)], i2_v)
            pltpu.sync_copy(x_hbm.at[pl.ds(off, chunk)], rows_v)
            pltpu.async_copy(rows_v, out_hbm.at[i1_v], sem).wait()
            pltpu.async_copy(rows_v, out_hbm.at[i2_v], sem).wait()

    return scatter(x, idx1, idx2)


def _sc_gather(y, idx):
    n = idx.shape[0]
    d = y.shape[1]
    per_w = n // SC_WORKERS
    assert per_w * SC_WORKERS == n
    chunk = _sc_chunk(per_w)

    @functools.partial(
        pl.kernel, mesh=_sc_mesh(), out_type=jax.ShapeDtypeStruct((n, d), y.dtype),
        scratch_types=[pltpu.VMEM((chunk,), jnp.int32), pltpu.VMEM((chunk, d), y.dtype),
                       pltpu.SemaphoreType.DMA])
    def gather(y_hbm, idx_hbm, out_hbm, idx_v, rows_v, sem):
        base = (lax.axis_index("s") * SC_CORES + lax.axis_index("c")) * per_w

        @pl.loop(0, per_w // chunk)
        def _(c):
            off = pl.multiple_of(base + c * chunk, SUBLANES)
            pltpu.sync_copy(idx_hbm.at[pl.ds(off, chunk)], idx_v)
            pltpu.async_copy(y_hbm.at[idx_v], rows_v, sem).wait()
            pltpu.sync_copy(rows_v, out_hbm.at[pl.ds(off, chunk)])

    return gather(y, idx)


def _experts_kernel(te_ref, used_ref, x_ref, wg_ref, wu_ref, wd_ref, o_ref):
    @pl.when(pl.program_id(0) < used_ref[0])
    def _():
        xb = x_ref[...].astype(BF16)
        hg = jnp.dot(xb, wg_ref[0].astype(BF16), preferred_element_type=F32)
        hu = jnp.dot(xb, wu_ref[0].astype(BF16), preferred_element_type=F32)
        h = jax.nn.silu(hg) * hu
        o_ref[...] = jnp.dot(h.astype(BF16), wd_ref[0].astype(BF16), preferred_element_type=F32)


def _experts(xg, tile_expert, n_used, wg, wu, wd):
    n_tiles = xg.shape[0] // MOE_TILE
    row = pl.BlockSpec((MOE_TILE, D_MODEL), lambda i, te, used: (i, 0))
    grid_spec = pltpu.PrefetchScalarGridSpec(
        num_scalar_prefetch=2,
        grid=(n_tiles,),
        in_specs=[row,
                  pl.BlockSpec((1, D_MODEL, D_EXPERT), lambda i, te, used: (te[i], 0, 0)),
                  pl.BlockSpec((1, D_MODEL, D_EXPERT), lambda i, te, used: (te[i], 0, 0)),
                  pl.BlockSpec((1, D_EXPERT, D_MODEL), lambda i, te, used: (te[i], 0, 0))],
        out_specs=row,
    )
    return pl.pallas_call(
        _experts_kernel,
        grid_spec=grid_spec,
        out_shape=jax.ShapeDtypeStruct(xg.shape, F32),
        compiler_params=_cparams(("arbitrary",)),
        name="moe_experts",
    )(tile_expert, n_used, xg, wg, wu, wd)


def _combine_kernel(x_ref, y1_ref, y2_ref, meta_ref, g_ref, b_ref, o_ref):
    meta = meta_ref[...]
    moe = meta[:, M_G1:M_G1 + 1] * y1_ref[...] + meta[:, M_G2:M_G2 + 1] * y2_ref[...]
    o_ref[...] = _ln(ALPHA * x_ref[...] + moe, g_ref[...], b_ref[...])


def _combine(x, y1, y2, meta, g, b):
    n = x.shape[0]
    tm = _row_tile(n, 512)
    row = pl.BlockSpec((tm, D_MODEL), lambda i: (i, 0))
    return pl.pallas_call(
        _combine_kernel,
        grid=(n // tm,),
        in_specs=[row, row, row, pl.BlockSpec((tm, M_COLS), lambda i: (i, 0)), _full(g.shape), _full(b.shape)],
        out_specs=row,
        out_shape=jax.ShapeDtypeStruct((n, D_MODEL), F32),
        compiler_params=_cparams(("arbitrary",)),
        name="moe_combine",
    )(x, y1, y2, meta, g, b)


def _moe_sparse(x, rw, rb, wg, wu, wd, e0, g, b):
    n = x.shape[0]
    meta_t, counts = _route(x, rw.T, rb.reshape(N_EXPERTS, 1))
    cnt = counts[:, 0].astype(jnp.int32)
    padded = (cnt + MOE_TILE - 1) // MOE_TILE * MOE_TILE
    seg_end = jnp.cumsum(padded)
    seg_start = seg_end - padded
    experts = jnp.arange(N_EXPERTS, dtype=jnp.int32)[:, None]
    start_of = lambda e: jnp.sum(jnp.where(experts == e[None, :], seg_start[:, None], 0), axis=0)
    e1, e2 = meta_t[M_I1].astype(jnp.int32), meta_t[M_I2].astype(jnp.int32)
    pos1 = start_of(e1) + meta_t[M_R1].astype(jnp.int32)
    pos2 = start_of(e2) + meta_t[M_R2].astype(jnp.int32)
    meta = meta_t.T
    n_tiles = -(-2 * n // MOE_TILE) + N_EXPERTS
    tile_start = jnp.arange(n_tiles, dtype=jnp.int32) * MOE_TILE
    tile_expert = e0 + jnp.minimum(jnp.sum(tile_start[:, None] >= seg_end[None, :], axis=1),
                                   N_EXPERTS - 1).astype(jnp.int32)
    n_used = (seg_end[-1:] // MOE_TILE).astype(jnp.int32)
    xg = _sc_scatter2(x, pos1, pos2, n_tiles * MOE_TILE)
    yg = _experts(xg, tile_expert, n_used, wg, wu, wd)
    return _combine(x, _sc_gather(yg, pos1), _sc_gather(yg, pos2), meta, g, b)


def _head_sum(z, ones):
    hi, lo = _split2(z)
    parts = []
    for g in range(D_MODEL // LANES):
        sl = slice(g * LANES, (g + 1) * LANES)
        parts.append(jnp.dot(hi[:, sl], ones, preferred_element_type=F32)
                     + jnp.dot(lo[:, sl], ones, preferred_element_type=F32))
    return jnp.concatenate(parts, axis=1)


def _rwkv_proj_kernel(x_ref, sh0_ref, mu_ref, wr_ref, wk_ref, wv_ref, w0_ref, w1_ref, w2_ref,
                      a0_ref, a1_ref, a2_ref, g1_ref, g2_ref, kkw_ref, kaw_ref, ones_ref,
                      r_ref, lw_ref, k_ref, v_ref, kk_ref, a_ref, g_ref, prev_scr,
                      *, tm, start, end):
    t = pl.program_id(1)

    @pl.when(t == 0)
    def _():
        prev_scr[...] = jnp.zeros(prev_scr.shape, F32)

    x = x_ref[0]
    ext = jnp.concatenate([prev_scr[...], x], axis=0)
    x_prev = pltpu.roll(ext, 1, 0)[SUBLANES:]
    grow = t * tm + lax.broadcasted_iota(jnp.int32, (tm, D_MODEL), 0)
    x_prev = jnp.where(grow == start, sh0_ref[0], x_prev)
    prev_scr[...] = x[tm - SUBLANES:]
    xx = x_prev - x
    mu = mu_ref[...]
    xr, xw, xk, xv, xa, xg = (x + xx * mu[n:n + 1] for n in range(6))
    r = _bdot(xr, wr_ref[...])
    k = _bdot(xk, wk_ref[...])
    v = _bdot(xv, wv_ref[...])
    log_w = -DECAY_SCALE * jax.nn.sigmoid(w0_ref[...] + _bdot(jnp.tanh(_bdot(xw, w1_ref[...])), w2_ref[...]))
    a = jax.nn.sigmoid(a0_ref[...] + _bdot(_bdot(xa, a1_ref[...]), a2_ref[...]))
    g = _bdot(jax.nn.sigmoid(_bdot(xg, g1_ref[...])), g2_ref[...])
    kk = k * kkw_ref[...]
    norm = jnp.sqrt(_head_sum(kk * kk, ones_ref[...]))
    kk = kk / jnp.maximum(norm, 1e-12)
    k = k * (1.0 + (a - 1.0) * kaw_ref[...])
    live = (grow >= start) & (grow < end)
    r_ref[0] = r.astype(r_ref.dtype)
    lw_ref[0] = jnp.where(live, log_w, 0.0)
    k_ref[0] = jnp.where(live, k, 0.0).astype(k_ref.dtype)
    v_ref[0] = v.astype(v_ref.dtype)
    kk_ref[0] = jnp.where(live, kk, 0.0).astype(kk_ref.dtype)
    a_ref[0] = a.astype(a_ref.dtype)
    g_ref[0] = g.astype(g_ref.dtype)


def _rwkv_proj(x, sh0, od, start, end, act_dtype):
    b, l, _ = x.shape
    tm = _row_tile(l, 256)
    dtypes = [act_dtype, F32] + [act_dtype] * 5
    seq = pl.BlockSpec((1, tm, D_MODEL), lambda i, t: (i, t, 0))
    ws = [od[n] for n in ("mu", "w_r", "w_k", "w_v", "w0", "w1", "w2", "a0", "a1", "a2", "g1", "g2",
                          "k_k", "k_a", "ones")]
    return pl.pallas_call(
        functools.partial(_rwkv_proj_kernel, tm=tm, start=start, end=end),
        grid=(b, l // tm),
        in_specs=[seq, pl.BlockSpec((1, 1, D_MODEL), lambda i, t: (i, 0, 0))] + [_full(w.shape) for w in ws],
        out_specs=[seq] * 7,
        out_shape=[jax.ShapeDtypeStruct((b, l, D_MODEL), dt) for dt in dtypes],
        scratch_shapes=[pltpu.VMEM((SUBLANES, D_MODEL), F32)],
        compiler_params=_cparams(("arbitrary", "arbitrary")),
        name="rwkv_proj",
    )(x, sh0, *ws)


def _wkv_kernel(r_ref, lw_ref, k_ref, v_ref, kk_ref, a_ref, s0_ref, o_ref, sout_ref, s_scr,
                *, c, exact):
    t = pl.program_id(1)

    @pl.when(t == 0)
    def _():
        s_scr[...] = s0_ref[0]

    head0 = lax.broadcasted_iota(jnp.int32, (c, LANES), 1) < RWKV_HEAD
    c2 = 2 * c
    row = lax.broadcasted_iota(jnp.int32, (c2, c2), 0)
    col = lax.broadcasted_iota(jnp.int32, (c2, c2), 1)
    row_hi = jnp.where(row >= c, c, 0)
    col_hi = jnp.where(col >= c, c, 0)
    same = row_hi == col_hi
    rr = row - row_hi
    cc = col - col_hi
    strict = same & (rr > cc)
    incl = same & (rr >= cc)
    eye = (row == col).astype(F32)
    tri = (lax.broadcasted_iota(jnp.int32, (c, c), 0) >= lax.broadcasted_iota(jnp.int32, (c, c), 1)).astype(BF16)

    def stack(x):
        return jnp.concatenate([jnp.where(head0, x, 0.0), jnp.where(head0, 0.0, x)], axis=0)

    mm = functools.partial(_mm, exact=exact)
    pairs = range(RWKV_HEADS // 2)
    for sub in range(r_ref.shape[1] // c):
        rows = slice(sub * c, (sub + 1) * c)
        load = lambda ref: [ref[0, rows, p * LANES:(p + 1) * LANES].astype(F32) for p in pairs]
        r, lw, k, v, kk, a = (load(ref) for ref in (r_ref, lw_ref, k_ref, v_ref, kk_ref, a_ref))
        lc = [sum(jnp.dot(tri, part, preferred_element_type=F32) for part in _split3(x)) for x in lw]
        lc_end = [x[c - 1:c] for x in lc]
        b = [x * y for x, y in zip(kk, a)]
        lhs = [jnp.concatenate([stack(-kk[p] * jnp.exp(lc[p] - lw[p])), stack(r[p] * jnp.exp(lc[p]))], axis=0)
               for p in pairs]
        g_inv = [jnp.exp(-x) for x in lc]
        rhs = [jnp.concatenate([stack(b[p] * g_inv[p]), stack(k[p] * g_inv[p])], axis=0) for p in pairs]
        pm = [mm(x, y, _NT) for x, y in zip(lhs, rhs)]
        l_ab = [jnp.where(strict, x[:c2, :c2], 0.0) for x in pm]
        l_ak = [jnp.where(strict, x[:c2, c2:], 0.0) for x in pm]
        m_rb = [jnp.where(incl, x[c2:, :c2], 0.0) for x in pm]
        m_rk = [jnp.where(incl, x[c2:, c2:], 0.0) for x in pm]
        vs = [stack(x) for x in v]
        lakv = [mm(x, y, _NN) for x, y in zip(l_ak, vs)]
        mrkv = [mm(x, y, _NN) for x, y in zip(m_rk, vs)]
        tinv = [eye + x for x in l_ab]
        lp = l_ab
        n = 2
        while n < c:
            lp = [mm(x, x, _NN) for x in lp]
            tinv = [x + mm(x, y, _NN) for x, y in zip(tinv, lp)]
            n *= 2
        s = [s_scr[p] for p in pairs]
        xs = [mm(x, y, _NT) for x, y in zip(lhs, s)]
        u = [mm(tinv[p], xs[p][:c2] + lakv[p], _NN) for p in pairs]
        for p in pairs:
            os_ = xs[p][c2:] + mm(m_rb[p], u[p], _NN) + mrkv[p]
            o_ref[0, rows, p * LANES:(p + 1) * LANES] = os_[:c] + os_[c:]
        for p in pairs:
            g_rem = jnp.exp(lc_end[p] - lc[p])
            uv = jnp.concatenate([u[p], vs[p]], axis=0)
            bk = jnp.concatenate([stack(b[p] * g_rem), stack(k[p] * g_rem)], axis=0)
            s_scr[p] = s[p] * jnp.exp(lc_end[p]) + mm(uv, bk, _TN)

    @pl.when(t == pl.num_programs(1) - 1)
    def _():
        sout_ref[0] = s_scr[...]


def _wkv(r, lw, k, v, kk, a, s0, c, exact):
    b, l, _ = r.shape
    rows = 2 * c if l % (2 * c) == 0 else c
    seq = pl.BlockSpec((1, rows, D_MODEL), lambda i, t: (i, t, 0))
    st = pl.BlockSpec((1, RWKV_HEADS // 2, LANES, LANES), lambda i, t: (i, 0, 0, 0))
    return pl.pallas_call(
        functools.partial(_wkv_kernel, c=c, exact=exact),
        grid=(b, l // rows),
        in_specs=[seq] * 6 + [st],
        out_specs=[seq, st],
        out_shape=[jax.ShapeDtypeStruct((b, l, D_MODEL), F32),
                   jax.ShapeDtypeStruct((b, RWKV_HEADS // 2, LANES, LANES), F32)],
        scratch_shapes=[pltpu.VMEM((RWKV_HEADS // 2, LANES, LANES), F32)],
        compiler_params=_cparams(("arbitrary", "arbitrary")),
        name="wkv",
    )(r, lw, k, v, kk, a, s0)


def _rwkv_out_kernel(o_ref, r_ref, k_ref, v_ref, g_ref, x_ref, rk_ref, gng_ref, gnb_ref, wo_ref,
                     ones_ref, lg_ref, lb_ref, y_ref):
    ones = ones_ref[...]
    o = o_ref[...]
    inv = 1.0 / RWKV_HEAD
    mu = _head_sum(o, ones) * inv
    oc = o - mu
    var = _head_sum(oc * oc, ones) * inv
    on = oc * lax.rsqrt(var + GN_EPS) * gng_ref[...] + gnb_ref[...]
    r, k, v, g = (ref[...].astype(F32) for ref in (r_ref, k_ref, v_ref, g_ref))
    on = on + _head_sum(r * k * rk_ref[...], ones) * v
    out = _bdot(on * g, wo_ref[...])
    y_ref[...] = _ln(ALPHA * x_ref[...] + out, lg_ref[...], lb_ref[...])


def _rwkv_out(o, r, k, v, g, x, od, lg, lb):
    n = x.shape[0]
    tm = _row_tile(n, 256)
    row = pl.BlockSpec((tm, D_MODEL), lambda i: (i, 0))
    ws = [od["r_k"], od["ln_g"], od["ln_b"], od["w_o"], od["ones"], lg, lb]
    return pl.pallas_call(
        _rwkv_out_kernel,
        grid=(n // tm,),
        in_specs=[row] * 6 + [_full(w.shape) for w in ws],
        out_specs=row,
        out_shape=jax.ShapeDtypeStruct((n, D_MODEL), F32),
        compiler_params=_cparams(("arbitrary",)),
        name="rwkv_out",
    )(o, r, k, v, g, x, *ws)


def _rope_tables(pos):
    half = D_ROPE // 2
    freq = ROPE_BASE ** (-jnp.arange(half, dtype=F32) / half)
    ang_t = freq[:, None] * pos.astype(F32)[None, :]
    cos_t, sin_t = lax.optimization_barrier((jnp.cos(ang_t), jnp.sin(ang_t)))
    cos, sin = cos_t.T, sin_t.T
    n = pos.shape[0]
    ones = jnp.ones((n, D_NOPE), F32)
    zeros = jnp.zeros((n, D_NOPE), F32)
    z16 = jnp.zeros((n, half), F32)
    tail1 = jnp.ones((n, LANES - D_NOPE - D_ROPE), F32)
    tail0 = jnp.zeros((n, LANES - D_NOPE - D_ROPE), F32)
    c = jnp.concatenate([ones, cos, cos, tail1], axis=1)
    sa = jnp.concatenate([zeros, -sin, z16, tail0], axis=1)
    sb = jnp.concatenate([zeros, z16, sin, tail0], axis=1)
    return c, sa, sb


def _slot_cols(w, width):
    k, h, _ = w.shape
    return jnp.pad(w, ((0, 0), (0, 0), (0, HEAD_SLOT - width))).reshape(k, h * HEAD_SLOT)


def _block_diag(w):
    n, c, d = w.shape
    eye = jnp.eye(n, dtype=w.dtype)
    return (eye[:, None, :, None] * w[:, :, None, :]).reshape(n * c, n * d)


def _row2(v):
    return v.reshape(1, -1).astype(F32)


def _prep_even(w_in, g_q, w_uq, g_kv, w_uk, w_uv, conv_w, conv_b, w_rg, b_rg, w_ig, b_ig, lam, w_out):
    off_ckv, off_kr = D_CQ, D_CQ + D_C
    off_xr = off_kr + D_ROPE
    off_y = off_xr + D_RNN
    kr_cols = jnp.pad(w_in[:, off_kr:off_xr], ((0, 0), (D_NOPE, LANES - D_NOPE - D_ROPE)))
    w_in, w_uq, w_uk, w_uv, w_rg, w_ig, w_out = (
        w.astype(F32) for w in (w_in, w_uq, w_uk, w_uv, w_rg, w_ig, w_out))
    w1 = jnp.concatenate([w_in[:, :off_ckv], w_in[:, off_ckv:off_kr], w_in[:, off_xr:off_y],
                          w_in[:, off_y:], kr_cols], axis=1)
    wuq = _slot_cols(w_uq, D_NOPE + D_ROPE)
    wukv = jnp.concatenate([_slot_cols(w_uk, D_NOPE), _slot_cols(w_uv, D_V)], axis=1)
    place = np.zeros((LANES, D_ATT), np.float32)
    for h in range(MLA_HEADS):
        for cidx in range(D_ROPE):
            place[D_NOPE + cidx, h * HEAD_SLOT + D_NOPE + cidx] = 1.0
    wa = jnp.pad(w_out[:MLA_HEADS * D_V].reshape(MLA_HEADS, D_V, D_MODEL),
                 ((0, 0), (0, HEAD_SLOT - D_V), (0, 0))).reshape(D_ATT, D_MODEL)
    wr = w_out[MLA_HEADS * D_V:]
    return dict(
        w1=w1, gq=_row2(g_q), wuq=wuq, gkv=_row2(g_kv), wukv=wukv, place=jnp.asarray(place, F32),
        cw=conv_w.astype(F32), cb=_row2(conv_b), wrg=_block_diag(w_rg), brg=_row2(b_rg),
        wig=_block_diag(w_ig), big=_row2(b_ig), sp=_row2(jax.nn.softplus(-lam.astype(F32))),
        wa=wa, wr=wr)


_EVEN_MATMUL_WEIGHTS = ("w1", "wuq", "wukv", "place", "wrg", "wig", "wa", "wr")


def _single_pass(ev):
    return {n: (w.astype(BF16) if n in _EVEN_MATMUL_WEIGHTS else w) for n, w in ev.items()}


def _prep_odd(mu, w_r, w_k, w_v, w0, w1, w2, a0, a1, a2, g1, g2, k_k, k_a, r_k, ln_g, ln_b, w_o):
    ones = np.zeros((LANES, LANES), np.float32)
    ones[:RWKV_HEAD, :RWKV_HEAD] = 1.0
    ones[RWKV_HEAD:, RWKV_HEAD:] = 1.0
    return dict(
        mu=jnp.pad(mu.astype(F32), ((0, SUBLANES - mu.shape[0]), (0, 0))),
        w_r=w_r.astype(BF16), w_k=w_k.astype(BF16), w_v=w_v.astype(BF16), w0=_row2(w0),
        w1=w1.astype(BF16), w2=w2.astype(BF16), a0=_row2(a0), a1=a1.astype(BF16), a2=a2.astype(BF16),
        g1=g1.astype(BF16), g2=g2.astype(BF16), k_k=_row2(k_k), k_a=_row2(k_a), r_k=_row2(r_k),
        ln_g=_row2(ln_g), ln_b=_row2(ln_b), w_o=w_o.astype(BF16), ones=jnp.asarray(ones, BF16))


def _pair_states(s):
    b = s.shape[0]
    s = s.reshape(b, RWKV_HEADS // 2, 2, RWKV_HEAD, RWKV_HEAD).astype(F32)
    eye = jnp.eye(2, dtype=F32)
    out = s[:, :, :, :, None, :] * eye[None, None, :, None, :, None]
    return out.reshape(b, RWKV_HEADS // 2, LANES, LANES)


def _unpair_states(s):
    b = s.shape[0]
    s = s.reshape(b, RWKV_HEADS // 2, 2, RWKV_HEAD, 2, RWKV_HEAD)
    return jnp.stack([s[:, :, 0, :, 0, :], s[:, :, 1, :, 1, :]], axis=2).reshape(
        b, RWKV_HEADS, RWKV_HEAD, RWKV_HEAD)


def _round_up(n, m):
    return -(-n // m) * m


def kernel(x_prompt, x_sample, cache_ckv, cache_krope, state_conv, state_lru, state_shift, state_wkv,
           meta_tokens, ev_w_in, ev_g_q, ev_w_uq, ev_g_kv, ev_w_uk, ev_w_uv, ev_conv_w, ev_conv_b,
           ev_w_rg, ev_b_rg, ev_w_ig, ev_b_ig, ev_lru_lambda, ev_w_out, od_mu, od_w_r, od_w_k, od_w_v,
           od_w0, od_w1, od_w2, od_a0, od_a1, od_a2, od_g1, od_g2, od_k_k, od_k_a, od_r_k, od_ln_g,
           od_ln_b, od_w_o, ln_g, ln_b, router_w, router_b, exp_w_gate, exp_w_up, exp_w_down):
    assert x_prompt.shape[0] == 1 and x_prompt.shape[2] == D_MODEL
    seq = x_prompt.shape[1]
    assert seq % CHUNK == 0
    bs, ls, _ = x_sample.shape
    past = cache_ckv.shape[2]
    ns = bs * ls
    end = ROW0 + seq
    tp = _round_up(end, 512)

    ev = _prep_even(ev_w_in[0], ev_g_q[0], ev_w_uq[0], ev_g_kv[0], ev_w_uk[0], ev_w_uv[0], ev_conv_w[0],
                    ev_conv_b[0], ev_w_rg[0], ev_b_rg[0], ev_w_ig[0], ev_b_ig[0], ev_lru_lambda[0],
                    ev_w_out[0])
    od = _prep_odd(od_mu[0], od_w_r[0], od_w_k[0], od_w_v[0], od_w0[0], od_w1[0], od_w2[0], od_a0[0],
                   od_a1[0], od_a2[0], od_g1[0], od_g2[0], od_k_k[0], od_k_a[0], od_r_k[0], od_ln_g[0],
                   od_ln_b[0], od_w_o[0])
    rw = router_w.astype(F32)
    rb = _row2(router_b)
    wg, wu, wd = (w.reshape((DEPTH * N_EXPERTS,) + w.shape[2:]) for w in (exp_w_gate, exp_w_up, exp_w_down))
    lng = ln_g.astype(F32)[:, :, None, :]
    lnb = ln_b.astype(F32)[:, :, None, :]

    def moe(x, layer):
        sparse = x.shape[0] % (SC_WORKERS * SUBLANES) == 0 and x.shape[0] >= SPARSE_MIN_ROWS
        fn = _moe_sparse if sparse else _moe
        return fn(x, rw, rb, wg, wu, wd, layer * N_EXPERTS, lng[layer, 1], lnb[layer, 1])

    xp = jnp.concatenate([jnp.zeros((PAD_FRONT, D_MODEL), F32), meta_tokens.astype(F32),
                          x_prompt[0].astype(F32), jnp.zeros((tp - end, D_MODEL), F32)], axis=0)
    tabs_p = _rope_tables(jnp.maximum(jnp.arange(tp) - PAD_FRONT, 0))
    evb = _single_pass(ev)
    q_p, ckv_p, kr_p, xr_p, yg_p = _even_proj(xp, evb["w1"], ev["gq"], evb["wuq"], ev["gkv"], tabs_p,
                                              MLA_SCALE * LOG2E)
    ones_col = np.zeros((D_ATT, 1), np.float32)
    ones_col[D_V::HEAD_SLOT] = 1.0
    k_p, vt_p = _kv_proj_t(ckv_p, kr_p, evb["wukv"][:, :D_ATT], evb["place"], evb["wukv"][:, D_ATT:].T,
                           jnp.asarray(ones_col), _round_up(tp, FLASH_TK))
    attn_p = _flash_attention(q_p, k_p, vt_p, FLASH_TQ, FLASH_TK)
    rnn_p, tailx_p, tailh_p = _rglru(
        xr_p[None], yg_p[None], ev["cw"], ev["cb"], evb["wrg"], ev["brg"], evb["wig"], ev["big"], ev["sp"],
        jnp.zeros((1, SUBLANES, D_RNN), F32), jnp.zeros((1, 1, D_RNN), F32), PAD_FRONT, end)
    x1_p = _mix_out(attn_p, rnn_p[0], xp, evb["wa"], evb["wr"], lng[0, 0], lnb[0, 0])
    x2_p = moe(x1_p, 0)

    xs = x_sample.reshape(ns, D_MODEL).astype(F32)
    pos_s = jnp.tile(N_META + past + jnp.arange(ls), bs)
    q_s, ckv_s, kr_s, xr_s, yg_s = _even_proj(xs, ev["w1"], ev["gq"], ev["wuq"], ev["gkv"], _rope_tables(pos_s),
                                              MLA_SCALE)
    _, ckv_m, kr_m, _, _ = _even_proj(meta_tokens.astype(F32), ev["w1"], ev["gq"], ev["wuq"], ev["gkv"],
                                      _rope_tables(jnp.arange(N_META)), MLA_SCALE)
    n_keys = N_META + past + ls
    nk_pad = _round_up(n_keys, LANES)
    meta_ckv = jnp.broadcast_to(ckv_m[None], (bs, N_META, D_C))
    meta_kr = jnp.broadcast_to(kr_m[None], (bs, N_META, LANES))
    cache_kr = jnp.pad(cache_krope[0].astype(F32), ((0, 0), (0, 0), (D_NOPE, LANES - D_NOPE - D_ROPE)))
    all_ckv = jnp.concatenate([meta_ckv, cache_ckv[0].astype(F32), ckv_s.reshape(bs, ls, D_C),
                               jnp.zeros((bs, nk_pad - n_keys, D_C), F32)], axis=1)
    all_kr = jnp.concatenate([meta_kr, cache_kr, kr_s.reshape(bs, ls, LANES),
                              jnp.zeros((bs, nk_pad - n_keys, LANES), F32)], axis=1)
    k_s, v_s = _kv_proj(all_ckv.reshape(bs * nk_pad, D_C), all_kr.reshape(bs * nk_pad, LANES),
                        ev["wukv"], ev["place"])
    attn_s = _sample_attention(q_s.reshape(bs, ls, D_ATT), k_s.reshape(bs, nk_pad, D_ATT),
                               v_s.reshape(bs, nk_pad, D_ATT), n_keys)
    buf0_s = jnp.pad(state_conv[0].astype(F32), ((0, 0), (SUBLANES - (CONV_W - 1), 0), (0, 0)))
    rnn_s, tailx_s, tailh_s = _rglru(
        xr_s.reshape(bs, ls, D_RNN), yg_s.reshape(bs, ls, D_RNN), ev["cw"], ev["cb"], ev["wrg"], ev["brg"],
        ev["wig"], ev["big"], ev["sp"], buf0_s, state_lru[0].astype(F32)[:, None, :], 0, ls)
    x1_s = _mix_out(attn_s.reshape(ns, D_ATT), rnn_s.reshape(ns, D_RNN), xs, ev["wa"], ev["wr"],
                    lng[0, 0], lnb[0, 0])
    x2_s = moe(x1_s, 0)

    r_p, lw_p, kk_in_p, v1_p, kkn_p, a_p, g_p = _rwkv_proj(
        x2_p[None], jnp.zeros((1, 1, D_MODEL), F32), od, PAD_FRONT, end, BF16)
    o_p, s_p = _wkv(r_p, lw_p, kk_in_p, v1_p, kkn_p, a_p,
                    jnp.zeros((1, RWKV_HEADS // 2, LANES, LANES), F32), CHUNK, False)
    x3_p = _rwkv_out(o_p[0], r_p[0], kk_in_p[0], v1_p[0], g_p[0], x2_p, od, lng[1, 0], lnb[1, 0])
    x4_p = moe(x3_p, 1)

    x2_s3 = x2_s.reshape(bs, ls, D_MODEL)
    r_s, lw_s, kk_in_s, v1_s, kkn_s, a_s, g_s = _rwkv_proj(
        x2_s3, state_shift[0].astype(F32)[:, None, :], od, 0, ls, F32)
    to_chunk = lambda z: jnp.pad(z, ((0, 0), (0, _round_up(ls, CHUNK) - ls), (0, 0)))
    o_s, s_s = _wkv(*(to_chunk(z) for z in (r_s, lw_s, kk_in_s, v1_s, kkn_s, a_s)),
                    _pair_states(state_wkv[0]), CHUNK, True)
    o_s = o_s[:, :ls]
    flat = lambda z: z.reshape(ns, D_MODEL)
    x3_s = _rwkv_out(flat(o_s), flat(r_s), flat(kk_in_s), flat(v1_s), flat(g_s), x2_s, od,
                     lng[1, 0], lnb[1, 0])
    x4_s = moe(x3_s, 1)

    dt = x_prompt.dtype
    nb = CONV_W - 1
    return (
        x4_p[ROW0:end][None].astype(dt),
        x4_s.reshape(bs, ls, D_MODEL).astype(dt),
        ckv_p[PAD_FRONT:end][None, None].astype(dt),
        kr_p[PAD_FRONT:end, D_NOPE:D_NOPE + D_ROPE][None, None].astype(dt),
        tailx_p[:, SUBLANES - nb:][None].astype(dt),
        tailh_p[:, SUBLANES - 1][None].astype(dt),
        x2_p[end - 1][None, None].astype(dt),
        _unpair_states(s_p)[None].astype(dt),
        ckv_s.reshape(bs, ls, D_C)[None].astype(dt),
        kr_s.reshape(bs, ls, LANES)[:, :, D_NOPE:D_NOPE + D_ROPE][None].astype(dt),
        tailx_s[:, SUBLANES - nb:][None].astype(dt),
        tailh_s[:, SUBLANES - 1][None].astype(dt),
        x2_s3[:, ls - 1][None].astype(dt),
        _unpair_states(s_s)[None].astype(dt),
    )
```

```python
import functools

import numpy as np
import jax
import jax.numpy as jnp
from jax import lax
from jax.experimental import pallas as pl
from jax.experimental.pallas import tpu as pltpu
from jax.experimental.pallas import tpu_sc as plsc

F32 = jnp.float32
BF16 = jnp.bfloat16

D_MODEL = 1024
N_META = 16
CHUNK = 64
CHUNK_SHIFT = 6
LN_EPS = 1e-5
RMS_EPS = 1e-6
DEPTH = 2
ALPHA = (2 * DEPTH) ** 0.25
MLA_HEADS = 8
D_NOPE = 64
D_ROPE = 32
D_V = 64
D_C = 256
D_CQ = 384
ROPE_BASE = 10000.0
MLA_SCALE = (D_NOPE + D_ROPE) ** -0.5
D_RNN = 512
LRU_BLOCKS = 8
LRU_BLOCK_W = D_RNN // LRU_BLOCKS
CONV_W = 4
LRU_C = 8.0
RWKV_HEAD = 64
RWKV_HEADS = D_MODEL // RWKV_HEAD
DECAY_SCALE = float(np.exp(-0.5))
GN_EPS = 64e-5
N_EXPERTS = 16
N_GROUPS = 4
EXPERTS_PER_GROUP = N_EXPERTS // N_GROUPS
D_EXPERT = 512

LANES = 128
SUBLANES = 8
HEAD_SLOT = LANES
D_ATT = MLA_HEADS * HEAD_SLOT
PAD_FRONT = CHUNK - N_META
ROW0 = PAD_FRONT + N_META
NEG = -1e30
LOG2E = 1.4426950408889634
SC_CORES = 2
SC_SUBCORES = 16
SC_WORKERS = SC_CORES * SC_SUBCORES
MOE_TILE = 512
FLASH_TQ = 512
FLASH_TK = 1024
SPARSE_MIN_ROWS = 1024
VMEM_LIMIT = 56 * 1024 * 1024

C_CQ = 0
C_CKV = D_CQ
C_XR = C_CKV + D_C
C_YG = C_XR + D_RNN
C_KR = C_YG + D_RNN
N_COL = C_KR + LANES


def _cparams(sem):
    return pltpu.CompilerParams(dimension_semantics=sem, vmem_limit_bytes=VMEM_LIMIT)


def _row_tile(n, cap):
    for t in (1024, 512, 256, 128, 64, 32, 16, 8):
        if t <= cap and n % t == 0:
            return t
    return n


def _full(shape):
    zeros = (0,) * len(shape)
    return pl.BlockSpec(shape, lambda *_: zeros)


def _ln(x, g, b):
    mu = jnp.mean(x, axis=-1, keepdims=True)
    xc = x - mu
    var = jnp.mean(xc * xc, axis=-1, keepdims=True)
    return xc * lax.rsqrt(var + LN_EPS) * g + b


def _bdot(a, b):
    return jnp.dot(a.astype(BF16), b.astype(BF16), preferred_element_type=F32)


def _split2(x):
    hi = x.astype(BF16)
    return hi, (x - hi.astype(F32)).astype(BF16)


def _split3(x):
    hi = x.astype(BF16)
    r1 = x - hi.astype(F32)
    mid = r1.astype(BF16)
    return hi, mid, (r1 - mid.astype(F32)).astype(BF16)


_NN = ((1,), (0,))
_NT = ((1,), (1,))
_TN = ((0,), (0,))


def _mm(a, b, dims, exact):
    dn = (dims, ((), ()))
    if not exact:
        return lax.dot_general(a.astype(BF16), b.astype(BF16), dn, preferred_element_type=F32)
    ah, al = _split2(a)
    bh, bl = _split2(b)
    return (lax.dot_general(ah, bh, dn, preferred_element_type=F32)
            + lax.dot_general(al, bh, dn, preferred_element_type=F32)
            + lax.dot_general(ah, bl, dn, preferred_element_type=F32))


def _wdot(a, w):
    return _mm(a, w, _NN, exact=(w.dtype == F32))


def _act_dtype(w):
    return F32 if w.dtype == F32 else BF16


def _rope_slot(x, c, sa, sb):
    return x * c + pltpu.roll(x, LANES - D_ROPE // 2, 1) * sa + pltpu.roll(x, D_ROPE // 2, 1) * sb


def _even_proj_kernel(x_ref, w1_ref, gq_ref, wuq_ref, gkv_ref, c_ref, sa_ref, sb_ref,
                      q_ref, ckv_ref, kr_ref, xr_ref, yg_ref, *, q_scale):
    u = _wdot(x_ref[...], w1_ref[...])
    cq = u[:, C_CQ:C_CQ + D_CQ]
    cq = cq * lax.rsqrt(jnp.mean(cq * cq, axis=-1, keepdims=True) + RMS_EPS) * gq_ref[...]
    q = _wdot(cq, wuq_ref[...])
    c, sa, sb = c_ref[...], sa_ref[...], sb_ref[...]
    for h in range(MLA_HEADS):
        sl = slice(h * HEAD_SLOT, (h + 1) * HEAD_SLOT)
        q_ref[:, sl] = (_rope_slot(q[:, sl], c, sa, sb) * q_scale).astype(q_ref.dtype)
    ckv = u[:, C_CKV:C_CKV + D_C]
    ckv_ref[...] = ckv * lax.rsqrt(jnp.mean(ckv * ckv, axis=-1, keepdims=True) + RMS_EPS) * gkv_ref[...]
    kr_ref[...] = _rope_slot(u[:, C_KR:C_KR + LANES], c, sa, sb)
    xr_ref[...] = u[:, C_XR:C_XR + D_RNN]
    yg_ref[...] = u[:, C_YG:C_YG + D_RNN]


def _even_proj(x, w1, gq, wuq, gkv, tabs, q_scale):
    n = x.shape[0]
    tm = _row_tile(n, 512)
    row = lambda w: pl.BlockSpec((tm, w), lambda i: (i, 0))
    c, sa, sb = tabs
    return pl.pallas_call(
        functools.partial(_even_proj_kernel, q_scale=q_scale),
        grid=(n // tm,),
        in_specs=[row(D_MODEL), _full(w1.shape), _full(gq.shape), _full(wuq.shape), _full(gkv.shape),
                  row(LANES), row(LANES), row(LANES)],
        out_specs=[row(D_ATT), row(D_C), row(LANES), row(D_RNN), row(D_RNN)],
        out_shape=[jax.ShapeDtypeStruct((n, D_ATT), _act_dtype(w1)), jax.ShapeDtypeStruct((n, D_C), F32),
                   jax.ShapeDtypeStruct((n, LANES), F32), jax.ShapeDtypeStruct((n, D_RNN), F32),
                   jax.ShapeDtypeStruct((n, D_RNN), F32)],
        compiler_params=_cparams(("arbitrary",)),
        name="even_proj",
    )(x, w1, gq, wuq, gkv, c, sa, sb)


def _kv_proj_t_kernel(ckv_ref, kr_ref, wuk_ref, p_ref, wuvt_ref, ones_ref, k_ref, vt_ref):
    ckv = ckv_ref[...].astype(BF16)
    k = jnp.dot(ckv, wuk_ref[...], preferred_element_type=F32) + _bdot(kr_ref[...], p_ref[...])
    k_ref[...] = k.astype(BF16)
    vt = lax.dot_general(wuvt_ref[...], ckv, (_NT, ((), ())), preferred_element_type=F32)
    vt_ref[...] = (vt + ones_ref[...]).astype(BF16)


def _kv_proj_t(ckv, kr, wuk, place, wuvt, ones_col, n_out):
    n = ckv.shape[0]
    tm = _row_tile(n, 512)
    assert n_out % tm == 0
    last = n // tm - 1
    row_in = lambda w: pl.BlockSpec((tm, w), lambda i: (jnp.minimum(i, last), 0))
    return pl.pallas_call(
        _kv_proj_t_kernel,
        grid=(n_out // tm,),
        in_specs=[row_in(D_C), row_in(LANES), _full(wuk.shape), _full(place.shape), _full(wuvt.shape),
                  _full(ones_col.shape)],
        out_specs=[pl.BlockSpec((tm, D_ATT), lambda i: (i, 0)), pl.BlockSpec((D_ATT, tm), lambda i: (0, i))],
        out_shape=[jax.ShapeDtypeStruct((n_out, D_ATT), BF16), jax.ShapeDtypeStruct((D_ATT, n_out), BF16)],
        compiler_params=_cparams(("arbitrary",)),
        name="kv_proj_t",
    )(ckv, kr, wuk, place, wuvt, ones_col)


def _flash_kernel(qi_ref, kj_ref, last_ref, q_ref, k_ref, vt_ref, o_ref, m_scr, acc_scr, *, tq, tk):
    step = pl.program_id(0)
    i = qi_ref[step]
    j = kj_ref[step]

    @pl.when(j == 0)
    def _():
        m_scr[...] = jnp.full(m_scr.shape, NEG, F32)
        acc_scr[...] = jnp.zeros(acc_scr.shape, F32)

    heads = range(MLA_HEADS)
    slots = [slice(h * HEAD_SLOT, (h + 1) * HEAD_SLOT) for h in heads]

    def accumulate(masked):
        st = [lax.dot_general(k_ref[:, sl], q_ref[:, sl], (_NT, ((), ())), preferred_element_type=F32)
              for sl in slots]
        if masked:
            krow = j * tk + lax.broadcasted_iota(jnp.int32, (tk, tq), 0)
            qrow = i * tq + lax.broadcasted_iota(jnp.int32, (tk, tq), 1)
            keep = ((((qrow - ROW0) >> CHUNK_SHIFT) >= ((krow - ROW0) >> CHUNK_SHIFT))
                    & (krow >= PAD_FRONT))
            st = [jnp.where(keep, x, NEG) for x in st]
        m_prev = [m_scr[h:h + 1, :] for h in heads]
        m_new = [jnp.maximum(mp, jnp.max(x, axis=0, keepdims=True)) for mp, x in zip(m_prev, st)]
        alpha = [jnp.exp2(mp - mn) for mp, mn in zip(m_prev, m_new)]
        pt = [jnp.exp2(x - mn).astype(BF16) for x, mn in zip(st, m_new)]
        pv = [jnp.dot(vt_ref[sl, :], x, preferred_element_type=F32) for x, sl in zip(pt, slots)]
        for h in heads:
            acc_scr[slots[h], :] = alpha[h] * acc_scr[slots[h], :] + pv[h]
            m_scr[h:h + 1, :] = m_new[h]

    last = last_ref[step] == 1
    edge = last | (j == 0)

    @pl.when(edge)
    def _():
        accumulate(True)

    @pl.when(jnp.logical_not(edge))
    def _():
        accumulate(False)

    @pl.when(last)
    def _():
        for sl in slots:
            acc = acc_scr[sl, :]
            o_ref[:, sl] = (acc / acc[D_V:D_V + 1, :]).T.astype(BF16)


def _flash_attention(q, k, vt, tq, tk):
    n = q.shape[0]
    nq = n // tq
    n_kblocks = [-(-(i + 1) * tq // tk) for i in range(nq)]
    assert k.shape[0] >= n_kblocks[-1] * tk
    qi = np.concatenate([np.full(c, i, np.int32) for i, c in enumerate(n_kblocks)])
    kj = np.concatenate([np.arange(c, dtype=np.int32) for c in n_kblocks])
    last = np.concatenate([np.arange(c, dtype=np.int32) == c - 1 for c in n_kblocks]).astype(np.int32)
    grid_spec = pltpu.PrefetchScalarGridSpec(
        num_scalar_prefetch=3,
        grid=(len(qi),),
        in_specs=[pl.BlockSpec((tq, D_ATT), lambda s, qi, kj, last: (qi[s], 0)),
                  pl.BlockSpec((tk, D_ATT), lambda s, qi, kj, last: (kj[s], 0)),
                  pl.BlockSpec((D_ATT, tk), lambda s, qi, kj, last: (0, kj[s]))],
        out_specs=pl.BlockSpec((tq, D_ATT), lambda s, qi, kj, last: (qi[s], 0)),
        scratch_shapes=[pltpu.VMEM((MLA_HEADS, tq), F32), pltpu.VMEM((D_ATT, tq), F32)],
    )
    return pl.pallas_call(
        functools.partial(_flash_kernel, tq=tq, tk=tk),
        grid_spec=grid_spec,
        out_shape=jax.ShapeDtypeStruct((n, D_ATT), BF16),
        compiler_params=_cparams(("arbitrary",)),
        name="flash_attention",
    )(jnp.asarray(qi), jnp.asarray(kj), jnp.asarray(last), q, k, vt)


def _sample_attn_kernel(q_ref, kc_ref, wabs_ref, wuv_ref, o_ref, *, n_keys):
    kc = kc_ref[0]
    latent = kc[:, :D_C]
    keep = lax.broadcasted_iota(jnp.int32, (q_ref.shape[1], kc.shape[0]), 1) < n_keys
    for h in range(MLA_HEADS):
        sl = slice(h * HEAD_SLOT, (h + 1) * HEAD_SLOT)
        qa = _mm(q_ref[0, :, sl], wabs_ref[h], _NN, True)
        s = jnp.where(keep, _mm(qa, kc, _NT, True), NEG)
        p = jnp.exp(s - jnp.max(s, axis=-1, keepdims=True))
        p = p / jnp.sum(p, axis=-1, keepdims=True)
        o_ref[0, :, sl] = _mm(_mm(p, latent, _NN, True), wuv_ref[h], _NN, True)


def _sample_attention(q, kc, wabs, wuv, n_keys):
    b, l, _ = q.shape
    nk, dk = kc.shape[1:]
    return pl.pallas_call(
        functools.partial(_sample_attn_kernel, n_keys=n_keys),
        grid=(b,),
        in_specs=[pl.BlockSpec((1, l, D_ATT), lambda i: (i, 0, 0)),
                  pl.BlockSpec((1, nk, dk), lambda i: (i, 0, 0)),
                  _full(wabs.shape), _full(wuv.shape)],
        out_specs=pl.BlockSpec((1, l, D_ATT), lambda i: (i, 0, 0)),
        out_shape=jax.ShapeDtypeStruct((b, l, D_ATT), F32),
        compiler_params=_cparams(("arbitrary",)),
        name="sample_attention",
    )(q, kc, wabs, wuv)


def _expm1(x):
    series = x * (1.0 + x * (0.5 + x * (1.0 / 6.0 + x * (1.0 / 24.0 + x * (1.0 / 120.0)))))
    return jnp.where(jnp.abs(x) < 0.05, series, jnp.exp(x) - 1.0)


def _gelu_tanh(x):
    return 0.5 * x * (1.0 + jnp.tanh(0.7978845608028654 * (x + 0.044715 * x * x * x)))


def _rglru_kernel(xr_ref, yg_ref, cw_ref, cb_ref, wrg_ref, brg_ref, wig_ref, big_ref, sp_ref,
                  buf0_ref, h0_ref, rnn_ref, tailx_ref, tailh_ref, prev_scr, h_scr,
                  *, tm, start, end):
    t = pl.program_id(1)

    @pl.when(t == 0)
    def _():
        prev_scr[...] = buf0_ref[0]
        h_scr[...] = jnp.broadcast_to(h0_ref[0], h_scr.shape)

    x = xr_ref[0]
    ext = jnp.concatenate([prev_scr[...], x], axis=0)
    cw = cw_ref[...]
    xc = cb_ref[...] + cw[CONV_W - 1:CONV_W] * x
    for d in range(1, CONV_W):
        xc = xc + cw[CONV_W - 1 - d:CONV_W - d] * pltpu.roll(ext, d, 0)[SUBLANES:]
    prev_scr[...] = x[tm - SUBLANES:]

    r = jax.nn.sigmoid(_wdot(xc, wrg_ref[...]) + brg_ref[...])
    ig = jax.nn.sigmoid(_wdot(xc, wig_ref[...]) + big_ref[...])
    log_a = -LRU_C * r * sp_ref[...]
    a = jnp.exp(log_a)
    b = jnp.sqrt(-_expm1(2.0 * log_a)) * (ig * xc)
    row = lax.broadcasted_iota(jnp.int32, (tm, D_RNN), 0)
    if start > 0:
        live = (t * tm + row) >= start
        a = jnp.where(live, a, 1.0)
        b = jnp.where(live, b, 0.0)
    d = 1
    while d < tm:
        b = a * jnp.where(row >= d, pltpu.roll(b, d, 0), 0.0) + b
        a = a * jnp.where(row >= d, pltpu.roll(a, d, 0), 1.0)
        d *= 2
    h = a * h_scr[0:1] + b
    h_scr[...] = jnp.broadcast_to(h[tm - 1:tm], h_scr.shape)
    rnn_ref[0] = (h * _gelu_tanh(yg_ref[0])).astype(rnn_ref.dtype)

    t_end = (end - 1) // tm
    el = end - t_end * tm

    @pl.when(t == t_end)
    def _():
        tailx_ref[0] = ext[el:el + SUBLANES]
        tailh_ref[0] = h[el - SUBLANES:el]


def _rglru(xr, yg, cw, cb, wrg, brg, wig, big, sp, buf0, h0, start, end):
    b, l, _ = xr.shape
    tm = _row_tile(l, 512)
    seq = pl.BlockSpec((1, tm, D_RNN), lambda i, t: (i, t, 0))
    per_b = lambda r: pl.BlockSpec((1, r, D_RNN), lambda i, t: (i, 0, 0))
    return pl.pallas_call(
        functools.partial(_rglru_kernel, tm=tm, start=start, end=end),
        grid=(b, l // tm),
        in_specs=[seq, seq, _full(cw.shape), _full(cb.shape), _full(wrg.shape), _full(brg.shape),
                  _full(wig.shape), _full(big.shape), _full(sp.shape), per_b(SUBLANES), per_b(1)],
        out_specs=[seq, per_b(SUBLANES), per_b(SUBLANES)],
        out_shape=[jax.ShapeDtypeStruct((b, l, D_RNN), _act_dtype(wrg)),
                   jax.ShapeDtypeStruct((b, SUBLANES, D_RNN), F32),
                   jax.ShapeDtypeStruct((b, SUBLANES, D_RNN), F32)],
        scratch_shapes=[pltpu.VMEM((SUBLANES, D_RNN), F32), pltpu.VMEM((SUBLANES, D_RNN), F32)],
        compiler_params=_cparams(("arbitrary", "arbitrary")),
        name="rglru",
    )(xr, yg, cw, cb, wrg, brg, wig, big, sp, buf0, h0)


def _mix_out_kernel(attn_ref, rnn_ref, x_ref, wa_ref, wr_ref, g_ref, b_ref, o_ref):
    mix = _wdot(attn_ref[...], wa_ref[...]) + _wdot(rnn_ref[...], wr_ref[...])
    o_ref[...] = _ln(ALPHA * x_ref[...] + mix, g_ref[...], b_ref[...])


def _mix_out(attn, rnn, x, wa, wr, g, b):
    n = x.shape[0]
    tm = _row_tile(n, 512)
    row = lambda w: pl.BlockSpec((tm, w), lambda i: (i, 0))
    return pl.pallas_call(
        _mix_out_kernel,
        grid=(n // tm,),
        in_specs=[row(D_ATT), row(D_RNN), row(D_MODEL), _full(wa.shape), _full(wr.shape),
                  _full(g.shape), _full(b.shape)],
        out_specs=row(D_MODEL),
        out_shape=jax.ShapeDtypeStruct((n, D_MODEL), F32),
        compiler_params=_cparams(("arbitrary",)),
        name="mix_out",
    )(attn, rnn, x, wa, wr, g, b)


def _first_argmax(vals, lane):
    m = jnp.max(vals, axis=-1, keepdims=True)
    idx = jnp.min(jnp.where(vals == m, lane, N_EXPERTS), axis=-1, keepdims=True)
    return m, idx


def _router_top2(x, rw, rb):
    logits = jnp.dot(x, rw, preferred_element_type=F32, precision=lax.Precision.HIGHEST)
    s = jax.nn.sigmoid(logits)
    sel = s + rb
    lane = lax.broadcasted_iota(jnp.int32, sel.shape, 1)
    grp = lane >> 2
    best = None
    g_best = None
    for g in range(N_GROUPS):
        vals = jnp.where(grp == g, sel, NEG)
        m1, i1 = _first_argmax(vals, lane)
        m2, _ = _first_argmax(jnp.where(lane == i1, NEG, vals), lane)
        score = m1 + m2
        if g == 0:
            best, g_best = score, jnp.zeros_like(i1)
        else:
            upd = score > best
            g_best = jnp.where(upd, g, g_best)
            best = jnp.where(upd, score, best)
    vals = jnp.where(grp == g_best, sel, NEG)
    _, i1 = _first_argmax(vals, lane)
    _, i2 = _first_argmax(jnp.where(lane == i1, NEG, vals), lane)
    w1 = jnp.sum(jnp.where(lane == i1, s, 0.0), axis=-1, keepdims=True)
    w2 = jnp.sum(jnp.where(lane == i2, s, 0.0), axis=-1, keepdims=True)
    den = w1 + w2
    return lane, i1, i2, w1 / den, w2 / den


def _router_gate(x, rw, rb):
    lane, i1, i2, g1, g2 = _router_top2(x, rw, rb)
    return jnp.where(lane == i1, g1, 0.0) + jnp.where(lane == i2, g2, 0.0)


def _moe_kernel(x_ref, rw_ref, rb_ref, wg_ref, wu_ref, wd_ref, g_ref, b_ref, o_ref,
                gate_scr, xb_scr, acc_scr):
    e = pl.program_id(1)

    @pl.when(e == 0)
    def _():
        x = x_ref[...]
        gate = _router_gate(x, rw_ref[...], rb_ref[...])
        for k in range(N_EXPERTS):
            gate_scr[k] = jnp.broadcast_to(gate[:, k:k + 1], gate_scr.shape[1:])
        xb_scr[...] = x.astype(BF16)
        acc_scr[...] = jnp.zeros(acc_scr.shape, F32)

    xb = xb_scr[...]
    hg = jnp.dot(xb, wg_ref[0].astype(BF16), preferred_element_type=F32)
    hu = jnp.dot(xb, wu_ref[0].astype(BF16), preferred_element_type=F32)
    gate_e = gate_scr[e]
    h = jax.nn.silu(hg) * hu * jnp.concatenate([gate_e] * (D_EXPERT // LANES), axis=1)
    acc_scr[...] += jnp.dot(h.astype(BF16), wd_ref[0].astype(BF16), preferred_element_type=F32)

    @pl.when(e == N_EXPERTS - 1)
    def _():
        o_ref[...] = _ln(ALPHA * x_ref[...] + acc_scr[...], g_ref[...], b_ref[...])


def _moe(x, rw, rb, wg, wu, wd, e0, g, b):
    n = x.shape[0]
    tm = _row_tile(n, 512)
    row = pl.BlockSpec((tm, D_MODEL), lambda i, e: (i, 0))
    return pl.pallas_call(
        _moe_kernel,
        grid=(n // tm, N_EXPERTS),
        in_specs=[row, _full(rw.shape), _full(rb.shape),
                  pl.BlockSpec((1, D_MODEL, D_EXPERT), lambda i, e: (e0 + e, 0, 0)),
                  pl.BlockSpec((1, D_MODEL, D_EXPERT), lambda i, e: (e0 + e, 0, 0)),
                  pl.BlockSpec((1, D_EXPERT, D_MODEL), lambda i, e: (e0 + e, 0, 0)),
                  _full(g.shape), _full(b.shape)],
        out_specs=row,
        out_shape=jax.ShapeDtypeStruct((n, D_MODEL), F32),
        scratch_shapes=[pltpu.VMEM((N_EXPERTS, tm, LANES), F32), pltpu.VMEM((tm, D_MODEL), BF16),
                        pltpu.VMEM((tm, D_MODEL), F32)],
        compiler_params=_cparams(("arbitrary", "arbitrary")),
        name="moe",
    )(x, rw, rb, wg, wu, wd, g, b)


M_I1, M_I2, M_R1, M_R2, M_G1, M_G2, M_COLS = 0, 1, 2, 3, 4, 5, 8


def _first_argmax_rows(vals, row):
    m = jnp.max(vals, axis=0, keepdims=True)
    idx = jnp.min(jnp.where(vals == m, row, N_EXPERTS), axis=0, keepdims=True)
    return m, idx


HI16 = -65536


def _pack_bf16_pairs(x):
    w = x.shape[1] // 2
    hi = lax.bitcast_convert_type(x[:, :w].astype(BF16).astype(F32), jnp.int32)
    lo = lax.bitcast_convert_type(x[:, w:].astype(BF16).astype(F32), jnp.int32)
    return (hi & HI16) | lax.shift_right_logical(lo, 16)


def _unpack_bf16_pairs(p):
    hi = lax.bitcast_convert_type(p & HI16, F32)
    lo = lax.bitcast_convert_type(lax.shift_left(p, 16), F32)
    return jnp.concatenate([hi, lo], axis=1)


def _route_kernel(x_ref, rwt_ref, rbc_ref, meta_ref, cnt_ref, xpk_ref, carry_scr, *, tm):
    @pl.when(pl.program_id(0) == 0)
    def _():
        carry_scr[...] = jnp.zeros(carry_scr.shape, F32)

    logits = lax.dot_general(rwt_ref[...], x_ref[...], (_NT, ((), ())), preferred_element_type=F32,
                             precision=lax.Precision.HIGHEST)
    s = jax.nn.sigmoid(logits)
    sel = s + rbc_ref[...]
    row = lax.broadcasted_iota(jnp.int32, sel.shape, 0)
    grp = row >> 2
    best = None
    g_best = None
    for g in range(N_GROUPS):
        vals = jnp.where(grp == g, sel, NEG)
        m1, i1 = _first_argmax_rows(vals, row)
        m2, _ = _first_argmax_rows(jnp.where(row == i1, NEG, vals), row)
        score = m1 + m2
        if g == 0:
            best, g_best = score, jnp.zeros_like(i1)
        else:
            upd = score > best
            g_best = jnp.where(upd, g, g_best)
            best = jnp.where(upd, score, best)
    vals = jnp.where(grp == g_best, sel, NEG)
    _, i1 = _first_argmax_rows(vals, row)
    _, i2 = _first_argmax_rows(jnp.where(row == i1, NEG, vals), row)
    w1 = jnp.sum(jnp.where(row == i1, s, 0.0), axis=0, keepdims=True)
    w2 = jnp.sum(jnp.where(row == i2, s, 0.0), axis=0, keepdims=True)
    den = w1 + w2

    chosen = jnp.where((row == i1) | (row == i2), 1.0, 0.0)
    earlier = (lax.broadcasted_iota(jnp.int32, (tm, tm), 0)
               < lax.broadcasted_iota(jnp.int32, (tm, tm), 1)).astype(BF16)
    seen = jnp.dot(chosen.astype(BF16), earlier, preferred_element_type=F32) + carry_scr[:, 0:1]
    r1 = jnp.sum(jnp.where(row == i1, seen, 0.0), axis=0, keepdims=True)
    r2 = jnp.sum(jnp.where(row == i2, seen, 0.0), axis=0, keepdims=True)
    carry_scr[...] = carry_scr[...] + jnp.sum(chosen, axis=1, keepdims=True)
    mrow = lax.broadcasted_iota(jnp.int32, (M_COLS, tm), 0)
    meta = jnp.zeros((M_COLS, tm), F32)
    for c, val in ((M_I1, i1.astype(F32)), (M_I2, i2.astype(F32)), (M_R1, r1), (M_R2, r2),
                   (M_G1, w1 / den), (M_G2, w2 / den)):
        meta = jnp.where(mrow == c, val, meta)
    meta_ref[...] = meta
    cnt_ref[...] = carry_scr[...]
    xpk_ref[...] = _pack_bf16_pairs(x_ref[...])


def _route(x, rwt, rbc):
    n = x.shape[0]
    tm = _row_tile(n, 512)
    return pl.pallas_call(
        functools.partial(_route_kernel, tm=tm),
        grid=(n // tm,),
        in_specs=[pl.BlockSpec((tm, D_MODEL), lambda i: (i, 0)), _full(rwt.shape), _full(rbc.shape)],
        out_specs=[pl.BlockSpec((M_COLS, tm), lambda i: (0, i)), _full((N_EXPERTS, LANES)),
                   pl.BlockSpec((tm, D_MODEL // 2), lambda i: (i, 0))],
        out_shape=[jax.ShapeDtypeStruct((M_COLS, n), F32), jax.ShapeDtypeStruct((N_EXPERTS, LANES), F32),
                   jax.ShapeDtypeStruct((n, D_MODEL // 2), jnp.int32)],
        scratch_shapes=[pltpu.VMEM((N_EXPERTS, LANES), F32)],
        compiler_params=_cparams(("arbitrary",)),
        name="moe_route",
    )(x, rwt, rbc)


def _sc_chunk(per_worker):
    for c in (64, 48, 32, 16, 8):
        if per_worker % c == 0:
            return c
    raise ValueError(per_worker)


def _sc_mesh():
    return plsc.VectorSubcoreMesh(core_axis_name="c", subcore_axis_name="s")


def _sc_scatter2(x, idx1, idx2, n_out):
    n, d = x.shape
    per_w = n // SC_WORKERS
    assert per_w * SC_WORKERS == n
    chunk = _sc_chunk(per_w)

    @functools.partial(
        pl.kernel, mesh=_sc_mesh(), out_type=jax.ShapeDtypeStruct((n_out, d), x.dtype),
        scratch_types=[pltpu.VMEM((chunk,), jnp.int32), pltpu.VMEM((chunk,), jnp.int32),
                       pltpu.VMEM((chunk, d), x.dtype), pltpu.SemaphoreType.DMA])
    def scatter(x_hbm, i1_hbm, i2_hbm, out_hbm, i1_v, i2_v, rows_v, sem):
        base = (lax.axis_index("s") * SC_CORES + lax.axis_index("c")) * per_w

        @pl.loop(0, per_w // chunk)
        def _(c):
            off = pl.multiple_of(base + c * chunk, SUBLANES)
            pltpu.sync_copy(i1_hbm.at[pl.ds(off, chunk)], i1_v)
            pltpu.sync_copy(i2_hbm.at[pl.ds(off, chunk)], i2_v)
            pltpu.sync_copy(x_hbm.at[pl.ds(off, chunk)], rows_v)
            pltpu.async_copy(rows_v, out_hbm.at[i1_v], sem).wait()
            pltpu.async_copy(rows_v, out_hbm.at[i2_v], sem).wait()

    return scatter(x, idx1, idx2)


def _sc_gather(y, idx):
    n = idx.shape[0]
    d = y.shape[1]
    per_w = n // SC_WORKERS
    assert per_w * SC_WORKERS == n
    chunk = _sc_chunk(per_w)

    @functools.partial(
        pl.kernel, mesh=_sc_mesh(), out_type=jax.ShapeDtypeStruct((n, d), y.dtype),
        scratch_types=[pltpu.VMEM((chunk,), jnp.int32), pltpu.VMEM((chunk, d), y.dtype),
                       pltpu.SemaphoreType.DMA])
    def gather(y_hbm, idx_hbm, out_hbm, idx_v, rows_v, sem):
        base = (lax.axis_index("s") * SC_CORES + lax.axis_index("c")) * per_w

        @pl.loop(0, per_w // chunk)
        def _(c):
            off = pl.multiple_of(base + c * chunk, SUBLANES)
            pltpu.sync_copy(idx_hbm.at[pl.ds(off, chunk)], idx_v)
            pltpu.async_copy(y_hbm.at[idx_v], rows_v, sem).wait()
            pltpu.sync_copy(rows_v, out_hbm.at[pl.ds(off, chunk)])

    return gather(y, idx)


def _experts_kernel(te_ref, used_ref, x_ref, wg_ref, wu_ref, wd_ref, o_ref):
    @pl.when(pl.program_id(0) < used_ref[0])
    def _():
        xb = _unpack_bf16_pairs(x_ref[...]).astype(BF16)
        hg = jnp.dot(xb, wg_ref[0].astype(BF16), preferred_element_type=F32)
        hu = jnp.dot(xb, wu_ref[0].astype(BF16), preferred_element_type=F32)
        h = jax.nn.silu(hg) * hu
        y = jnp.dot(h.astype(BF16), wd_ref[0].astype(BF16), preferred_element_type=F32)
        o_ref[...] = _pack_bf16_pairs(y)


def _experts(xg, tile_expert, n_used, wg, wu, wd):
    n_tiles = xg.shape[0] // MOE_TILE
    row = pl.BlockSpec((MOE_TILE, D_MODEL // 2), lambda i, te, used: (i, 0))
    grid_spec = pltpu.PrefetchScalarGridSpec(
        num_scalar_prefetch=2,
        grid=(n_tiles,),
        in_specs=[row,
                  pl.BlockSpec((1, D_MODEL, D_EXPERT), lambda i, te, used: (te[i], 0, 0)),
                  pl.BlockSpec((1, D_MODEL, D_EXPERT), lambda i, te, used: (te[i], 0, 0)),
                  pl.BlockSpec((1, D_EXPERT, D_MODEL), lambda i, te, used: (te[i], 0, 0))],
        out_specs=row,
    )
    return pl.pallas_call(
        _experts_kernel,
        grid_spec=grid_spec,
        out_shape=jax.ShapeDtypeStruct(xg.shape, jnp.int32),
        compiler_params=_cparams(("arbitrary",)),
        name="moe_experts",
    )(tile_expert, n_used, xg, wg, wu, wd)


def _combine_kernel(x_ref, y1_ref, y2_ref, meta_ref, g_ref, b_ref, o_ref):
    meta = meta_ref[...]
    moe = (meta[:, M_G1:M_G1 + 1] * _unpack_bf16_pairs(y1_ref[...])
           + meta[:, M_G2:M_G2 + 1] * _unpack_bf16_pairs(y2_ref[...]))
    o_ref[...] = _ln(ALPHA * x_ref[...] + moe, g_ref[...], b_ref[...])


def _combine(x, y1, y2, meta, g, b):
    n = x.shape[0]
    tm = _row_tile(n, 512)
    row = pl.BlockSpec((tm, D_MODEL), lambda i: (i, 0))
    half = pl.BlockSpec((tm, D_MODEL // 2), lambda i: (i, 0))
    return pl.pallas_call(
        _combine_kernel,
        grid=(n // tm,),
        in_specs=[row, half, half, pl.BlockSpec((tm, M_COLS), lambda i: (i, 0)), _full(g.shape), _full(b.shape)],
        out_specs=row,
        out_shape=jax.ShapeDtypeStruct((n, D_MODEL), F32),
        compiler_params=_cparams(("arbitrary",)),
        name="moe_combine",
    )(x, y1, y2, meta, g, b)


def _moe_sparse(x, rw, rb, wg, wu, wd, e0, g, b):
    n = x.shape[0]
    meta_t, counts, x_packed = _route(x, rw.T, rb.reshape(N_EXPERTS, 1))
    cnt = counts[:, 0].astype(jnp.int32)
    padded = (cnt + MOE_TILE - 1) // MOE_TILE * MOE_TILE
    seg_end = jnp.cumsum(padded)
    seg_start = seg_end - padded
    experts = jnp.arange(N_EXPERTS, dtype=jnp.int32)[:, None]
    start_of = lambda e: jnp.sum(jnp.where(experts == e[None, :], seg_start[:, None], 0), axis=0)
    e1, e2 = meta_t[M_I1].astype(jnp.int32), meta_t[M_I2].astype(jnp.int32)
    pos1 = start_of(e1) + meta_t[M_R1].astype(jnp.int32)
    pos2 = start_of(e2) + meta_t[M_R2].astype(jnp.int32)
    meta = meta_t.T
    n_tiles = -(-2 * n // MOE_TILE) + N_EXPERTS
    tile_start = jnp.arange(n_tiles, dtype=jnp.int32) * MOE_TILE
    tile_expert = e0 + jnp.minimum(jnp.sum(tile_start[:, None] >= seg_end[None, :], axis=1),
                                   N_EXPERTS - 1).astype(jnp.int32)
    n_used = (seg_end[-1:] // MOE_TILE).astype(jnp.int32)
    xg = _sc_scatter2(x_packed, pos1, pos2, n_tiles * MOE_TILE)
    yg = _experts(xg, tile_expert, n_used, wg, wu, wd)
    return _combine(x, _sc_gather(yg, pos1), _sc_gather(yg, pos2), meta, g, b)


def _head_sum(z, ones):
    hi, lo = _split2(z)
    parts = []
    for g in range(D_MODEL // LANES):
        sl = slice(g * LANES, (g + 1) * LANES)
        parts.append(jnp.dot(hi[:, sl], ones, preferred_element_type=F32)
                     + jnp.dot(lo[:, sl], ones, preferred_element_type=F32))
    return jnp.concatenate(parts, axis=1)


def _rwkv_proj_kernel(x_ref, sh0_ref, mu_ref, wr_ref, wk_ref, wv_ref, w0_ref, w1_ref, w2_ref,
                      a0_ref, a1_ref, a2_ref, g1_ref, g2_ref, kkw_ref, kaw_ref, ones_ref,
                      r_ref, lw_ref, k_ref, v_ref, kk_ref, a_ref, g_ref, prev_scr,
                      *, tm, start, end):
    t = pl.program_id(1)

    @pl.when(t == 0)
    def _():
        prev_scr[...] = jnp.zeros(prev_scr.shape, F32)

    x = x_ref[0]
    ext = jnp.concatenate([prev_scr[...], x], axis=0)
    x_prev = pltpu.roll(ext, 1, 0)[SUBLANES:]
    grow = t * tm + lax.broadcasted_iota(jnp.int32, (tm, D_MODEL), 0)
    x_prev = jnp.where(grow == start, sh0_ref[0], x_prev)
    prev_scr[...] = x[tm - SUBLANES:]
    xx = x_prev - x
    mu = mu_ref[...]
    xr, xw, xk, xv, xa, xg = (x + xx * mu[n:n + 1] for n in range(6))
    r = _bdot(xr, wr_ref[...])
    k = _bdot(xk, wk_ref[...])
    v = _bdot(xv, wv_ref[...])
    log_w = -DECAY_SCALE * jax.nn.sigmoid(w0_ref[...] + _bdot(jnp.tanh(_bdot(xw, w1_ref[...])), w2_ref[...]))
    a = jax.nn.sigmoid(a0_ref[...] + _bdot(_bdot(xa, a1_ref[...]), a2_ref[...]))
    g = _bdot(jax.nn.sigmoid(_bdot(xg, g1_ref[...])), g2_ref[...])
    kk = k * kkw_ref[...]
    norm = jnp.sqrt(_head_sum(kk * kk, ones_ref[...]))
    kk = kk / jnp.maximum(norm, 1e-12)
    k = k * (1.0 + (a - 1.0) * kaw_ref[...])
    live = (grow >= start) & (grow < end)
    r_ref[0] = r.astype(r_ref.dtype)
    lw_ref[0] = jnp.where(live, log_w, 0.0)
    k_ref[0] = jnp.where(live, k, 0.0).astype(k_ref.dtype)
    v_ref[0] = v.astype(v_ref.dtype)
    kk_ref[0] = jnp.where(live, kk, 0.0).astype(kk_ref.dtype)
    a_ref[0] = a.astype(a_ref.dtype)
    g_ref[0] = g.astype(g_ref.dtype)


def _rwkv_proj(x, sh0, od, start, end, act_dtype):
    b, l, _ = x.shape
    tm = _row_tile(l, 256)
    dtypes = [act_dtype, F32] + [act_dtype] * 5
    seq = pl.BlockSpec((1, tm, D_MODEL), lambda i, t: (i, t, 0))
    ws = [od[n] for n in ("mu", "w_r", "w_k", "w_v", "w0", "w1", "w2", "a0", "a1", "a2", "g1", "g2",
                          "k_k", "k_a", "ones")]
    return pl.pallas_call(
        functools.partial(_rwkv_proj_kernel, tm=tm, start=start, end=end),
        grid=(b, l // tm),
        in_specs=[seq, pl.BlockSpec((1, 1, D_MODEL), lambda i, t: (i, 0, 0))] + [_full(w.shape) for w in ws],
        out_specs=[seq] * 7,
        out_shape=[jax.ShapeDtypeStruct((b, l, D_MODEL), dt) for dt in dtypes],
        scratch_shapes=[pltpu.VMEM((SUBLANES, D_MODEL), F32)],
        compiler_params=_cparams(("arbitrary", "arbitrary")),
        name="rwkv_proj",
    )(x, sh0, *ws)


def _wkv_kernel(r_ref, lw_ref, k_ref, v_ref, kk_ref, a_ref, s0_ref, o_ref, sout_ref, s_scr,
                *, c, exact):
    t = pl.program_id(1)

    @pl.when(t == 0)
    def _():
        s_scr[...] = s0_ref[0]

    head0 = lax.broadcasted_iota(jnp.int32, (c, LANES), 1) < RWKV_HEAD
    c2 = 2 * c
    row = lax.broadcasted_iota(jnp.int32, (c2, c2), 0)
    col = lax.broadcasted_iota(jnp.int32, (c2, c2), 1)
    row_hi = jnp.where(row >= c, c, 0)
    col_hi = jnp.where(col >= c, c, 0)
    same = row_hi == col_hi
    rr = row - row_hi
    cc = col - col_hi
    strict = same & (rr > cc)
    incl = same & (rr >= cc)
    eye = (row == col).astype(F32)
    tri = (lax.broadcasted_iota(jnp.int32, (c, c), 0) >= lax.broadcasted_iota(jnp.int32, (c, c), 1)).astype(BF16)

    def stack(x):
        return jnp.concatenate([jnp.where(head0, x, 0.0), jnp.where(head0, 0.0, x)], axis=0)

    mm = functools.partial(_mm, exact=exact)
    pairs = range(RWKV_HEADS // 2)
    for sub in range(r_ref.shape[1] // c):
        rows = slice(sub * c, (sub + 1) * c)
        load = lambda ref: [ref[0, rows, p * LANES:(p + 1) * LANES].astype(F32) for p in pairs]
        r, lw, k, v, kk, a = (load(ref) for ref in (r_ref, lw_ref, k_ref, v_ref, kk_ref, a_ref))
        lc = [sum(jnp.dot(tri, part, preferred_element_type=F32) for part in _split3(x)) for x in lw]
        lc_end = [x[c - 1:c] for x in lc]
        b = [x * y for x, y in zip(kk, a)]
        lhs = [jnp.concatenate([stack(-kk[p] * jnp.exp(lc[p] - lw[p])), stack(r[p] * jnp.exp(lc[p]))], axis=0)
               for p in pairs]
        g_inv = [jnp.exp(-x) for x in lc]
        rhs = [jnp.concatenate([stack(b[p] * g_inv[p]), stack(k[p] * g_inv[p])], axis=0) for p in pairs]
        pm = [mm(x, y, _NT) for x, y in zip(lhs, rhs)]
        l_ab = [jnp.where(strict, x[:c2, :c2], 0.0) for x in pm]
        l_ak = [jnp.where(strict, x[:c2, c2:], 0.0) for x in pm]
        m_rb = [jnp.where(incl, x[c2:, :c2], 0.0) for x in pm]
        m_rk = [jnp.where(incl, x[c2:, c2:], 0.0) for x in pm]
        vs = [stack(x) for x in v]
        lakv = [mm(x, y, _NN) for x, y in zip(l_ak, vs)]
        mrkv = [mm(x, y, _NN) for x, y in zip(m_rk, vs)]
        tinv = [eye + x for x in l_ab]
        lp = l_ab
        n = 2
        while n < c:
            lp = [mm(x, x, _NN) for x in lp]
            tinv = [x + mm(x, y, _NN) for x, y in zip(tinv, lp)]
            n *= 2
        s = [s_scr[p] for p in pairs]
        xs = [mm(x, y, _NT) for x, y in zip(lhs, s)]
        u = [mm(tinv[p], xs[p][:c2] + lakv[p], _NN) for p in pairs]
        for p in pairs:
            os_ = xs[p][c2:] + mm(m_rb[p], u[p], _NN) + mrkv[p]
            o_ref[0, rows, p * LANES:(p + 1) * LANES] = os_[:c] + os_[c:]
        for p in pairs:
            g_rem = jnp.exp(lc_end[p] - lc[p])
            uv = jnp.concatenate([u[p], vs[p]], axis=0)
            bk = jnp.concatenate([stack(b[p] * g_rem), stack(k[p] * g_rem)], axis=0)
            s_scr[p] = s[p] * jnp.exp(lc_end[p]) + mm(uv, bk, _TN)

    @pl.when(t == pl.num_programs(1) - 1)
    def _():
        sout_ref[0] = s_scr[...]


def _wkv(r, lw, k, v, kk, a, s0, c, exact):
    b, l, _ = r.shape
    rows = 2 * c if l % (2 * c) == 0 else c
    seq = pl.BlockSpec((1, rows, D_MODEL), lambda i, t: (i, t, 0))
    st = pl.BlockSpec((1, RWKV_HEADS // 2, LANES, LANES), lambda i, t: (i, 0, 0, 0))
    return pl.pallas_call(
        functools.partial(_wkv_kernel, c=c, exact=exact),
        grid=(b, l // rows),
        in_specs=[seq] * 6 + [st],
        out_specs=[seq, st],
        out_shape=[jax.ShapeDtypeStruct((b, l, D_MODEL), F32),
                   jax.ShapeDtypeStruct((b, RWKV_HEADS // 2, LANES, LANES), F32)],
        scratch_shapes=[pltpu.VMEM((RWKV_HEADS // 2, LANES, LANES), F32)],
        compiler_params=_cparams(("arbitrary", "arbitrary")),
        name="wkv",
    )(r, lw, k, v, kk, a, s0)


def _rwkv_out_kernel(o_ref, r_ref, k_ref, v_ref, g_ref, x_ref, rk_ref, gng_ref, gnb_ref, wo_ref,
                     ones_ref, lg_ref, lb_ref, y_ref):
    ones = ones_ref[...]
    o = o_ref[...]
    inv = 1.0 / RWKV_HEAD
    mu = _head_sum(o, ones) * inv
    oc = o - mu
    var = _head_sum(oc * oc, ones) * inv
    on = oc * lax.rsqrt(var + GN_EPS) * gng_ref[...] + gnb_ref[...]
    r, k, v, g = (ref[...].astype(F32) for ref in (r_ref, k_ref, v_ref, g_ref))
    on = on + _head_sum(r * k * rk_ref[...], ones) * v
    out = _bdot(on * g, wo_ref[...])
    y_ref[...] = _ln(ALPHA * x_ref[...] + out, lg_ref[...], lb_ref[...])


def _rwkv_out(o, r, k, v, g, x, od, lg, lb):
    n = x.shape[0]
    tm = _row_tile(n, 256)
    row = pl.BlockSpec((tm, D_MODEL), lambda i: (i, 0))
    ws = [od["r_k"], od["ln_g"], od["ln_b"], od["w_o"], od["ones"], lg, lb]
    return pl.pallas_call(
        _rwkv_out_kernel,
        grid=(n // tm,),
        in_specs=[row] * 6 + [_full(w.shape) for w in ws],
        out_specs=row,
        out_shape=jax.ShapeDtypeStruct((n, D_MODEL), F32),
        compiler_params=_cparams(("arbitrary",)),
        name="rwkv_out",
    )(o, r, k, v, g, x, *ws)


def _rope_tables(pos):
    half = D_ROPE // 2
    freq = ROPE_BASE ** (-jnp.arange(half, dtype=F32) / half)
    ang_t = freq[:, None] * pos.astype(F32)[None, :]
    cos_t, sin_t = lax.optimization_barrier((jnp.cos(ang_t), jnp.sin(ang_t)))
    cos, sin = cos_t.T, sin_t.T
    n = pos.shape[0]
    ones = jnp.ones((n, D_NOPE), F32)
    zeros = jnp.zeros((n, D_NOPE), F32)
    z16 = jnp.zeros((n, half), F32)
    tail1 = jnp.ones((n, LANES - D_NOPE - D_ROPE), F32)
    tail0 = jnp.zeros((n, LANES - D_NOPE - D_ROPE), F32)
    c = jnp.concatenate([ones, cos, cos, tail1], axis=1)
    sa = jnp.concatenate([zeros, -sin, z16, tail0], axis=1)
    sb = jnp.concatenate([zeros, z16, sin, tail0], axis=1)
    return c, sa, sb


def _slot_cols(w, width):
    k, h, _ = w.shape
    return jnp.pad(w, ((0, 0), (0, 0), (0, HEAD_SLOT - width))).reshape(k, h * HEAD_SLOT)


def _block_diag(w):
    n, c, d = w.shape
    eye = jnp.eye(n, dtype=w.dtype)
    return (eye[:, None, :, None] * w[:, :, None, :]).reshape(n * c, n * d)


def _row2(v):
    return v.reshape(1, -1).astype(F32)


def _prep_even(w_in, g_q, w_uq, g_kv, w_uk, w_uv, conv_w, conv_b, w_rg, b_rg, w_ig, b_ig, lam, w_out):
    off_ckv, off_kr = D_CQ, D_CQ + D_C
    off_xr = off_kr + D_ROPE
    off_y = off_xr + D_RNN
    kr_cols = jnp.pad(w_in[:, off_kr:off_xr], ((0, 0), (D_NOPE, LANES - D_NOPE - D_ROPE)))
    w_in, w_uq, w_uk, w_uv, w_rg, w_ig, w_out = (
        w.astype(F32) for w in (w_in, w_uq, w_uk, w_uv, w_rg, w_ig, w_out))
    w1 = jnp.concatenate([w_in[:, :off_ckv], w_in[:, off_ckv:off_kr], w_in[:, off_xr:off_y],
                          w_in[:, off_y:], kr_cols], axis=1)
    wuq = _slot_cols(w_uq, D_NOPE + D_ROPE)
    wukv = jnp.concatenate([_slot_cols(w_uk, D_NOPE), _slot_cols(w_uv, D_V)], axis=1)
    place = np.zeros((LANES, D_ATT), np.float32)
    for h in range(MLA_HEADS):
        for cidx in range(D_ROPE):
            place[D_NOPE + cidx, h * HEAD_SLOT + D_NOPE + cidx] = 1.0
    wa = jnp.pad(w_out[:MLA_HEADS * D_V].reshape(MLA_HEADS, D_V, D_MODEL),
                 ((0, 0), (0, HEAD_SLOT - D_V), (0, 0))).reshape(D_ATT, D_MODEL)
    wr = w_out[MLA_HEADS * D_V:]
    keep_rope = np.zeros((HEAD_SLOT, LANES), np.float32)
    keep_rope[D_NOPE:D_NOPE + D_ROPE, D_NOPE:D_NOPE + D_ROPE] = np.eye(D_ROPE, dtype=np.float32)
    uk_t = jnp.pad(jnp.transpose(w_uk, (1, 2, 0)), ((0, 0), (0, HEAD_SLOT - D_NOPE), (0, 0)))
    wabs = jnp.concatenate([uk_t, jnp.broadcast_to(keep_rope, (MLA_HEADS, HEAD_SLOT, LANES))], axis=2)
    wuv = jnp.pad(jnp.transpose(w_uv, (1, 0, 2)), ((0, 0), (0, 0), (0, HEAD_SLOT - D_V)))
    return dict(
        wabs=wabs, wuv=wuv,
        w1=w1, gq=_row2(g_q), wuq=wuq, gkv=_row2(g_kv), wukv=wukv, place=jnp.asarray(place, F32),
        cw=conv_w.astype(F32), cb=_row2(conv_b), wrg=_block_diag(w_rg), brg=_row2(b_rg),
        wig=_block_diag(w_ig), big=_row2(b_ig), sp=_row2(jax.nn.softplus(-lam.astype(F32))),
        wa=wa, wr=wr)


_EVEN_MATMUL_WEIGHTS = ("w1", "wuq", "wukv", "place", "wrg", "wig", "wa", "wr")


def _single_pass(ev):
    return {n: (w.astype(BF16) if n in _EVEN_MATMUL_WEIGHTS else w) for n, w in ev.items()}


def _prep_odd(mu, w_r, w_k, w_v, w0, w1, w2, a0, a1, a2, g1, g2, k_k, k_a, r_k, ln_g, ln_b, w_o):
    ones = np.zeros((LANES, LANES), np.float32)
    ones[:RWKV_HEAD, :RWKV_HEAD] = 1.0
    ones[RWKV_HEAD:, RWKV_HEAD:] = 1.0
    return dict(
        mu=jnp.pad(mu.astype(F32), ((0, SUBLANES - mu.shape[0]), (0, 0))),
        w_r=w_r.astype(BF16), w_k=w_k.astype(BF16), w_v=w_v.astype(BF16), w0=_row2(w0),
        w1=w1.astype(BF16), w2=w2.astype(BF16), a0=_row2(a0), a1=a1.astype(BF16), a2=a2.astype(BF16),
        g1=g1.astype(BF16), g2=g2.astype(BF16), k_k=_row2(k_k), k_a=_row2(k_a), r_k=_row2(r_k),
        ln_g=_row2(ln_g), ln_b=_row2(ln_b), w_o=w_o.astype(BF16), ones=jnp.asarray(ones, BF16))


def _pair_states(s):
    b = s.shape[0]
    s = s.reshape(b, RWKV_HEADS // 2, 2, RWKV_HEAD, RWKV_HEAD).astype(F32)
    eye = jnp.eye(2, dtype=F32)
    out = s[:, :, :, :, None, :] * eye[None, None, :, None, :, None]
    return out.reshape(b, RWKV_HEADS // 2, LANES, LANES)


def _unpair_states(s):
    b = s.shape[0]
    s = s.reshape(b, RWKV_HEADS // 2, 2, RWKV_HEAD, 2, RWKV_HEAD)
    return jnp.stack([s[:, :, 0, :, 0, :], s[:, :, 1, :, 1, :]], axis=2).reshape(
        b, RWKV_HEADS, RWKV_HEAD, RWKV_HEAD)


def _round_up(n, m):
    return -(-n // m) * m


def kernel(x_prompt, x_sample, cache_ckv, cache_krope, state_conv, state_lru, state_shift, state_wkv,
           meta_tokens, ev_w_in, ev_g_q, ev_w_uq, ev_g_kv, ev_w_uk, ev_w_uv, ev_conv_w, ev_conv_b,
           ev_w_rg, ev_b_rg, ev_w_ig, ev_b_ig, ev_lru_lambda, ev_w_out, od_mu, od_w_r, od_w_k, od_w_v,
           od_w0, od_w1, od_w2, od_a0, od_a1, od_a2, od_g1, od_g2, od_k_k, od_k_a, od_r_k, od_ln_g,
           od_ln_b, od_w_o, ln_g, ln_b, router_w, router_b, exp_w_gate, exp_w_up, exp_w_down):
    assert x_prompt.shape[0] == 1 and x_prompt.shape[2] == D_MODEL
    seq = x_prompt.shape[1]
    assert seq % CHUNK == 0
    bs, ls, _ = x_sample.shape
    past = cache_ckv.shape[2]
    ns = bs * ls
    end = ROW0 + seq
    tp = _round_up(end, 512)

    ev = _prep_even(ev_w_in[0], ev_g_q[0], ev_w_uq[0], ev_g_kv[0], ev_w_uk[0], ev_w_uv[0], ev_conv_w[0],
                    ev_conv_b[0], ev_w_rg[0], ev_b_rg[0], ev_w_ig[0], ev_b_ig[0], ev_lru_lambda[0],
                    ev_w_out[0])
    od = _prep_odd(od_mu[0], od_w_r[0], od_w_k[0], od_w_v[0], od_w0[0], od_w1[0], od_w2[0], od_a0[0],
                   od_a1[0], od_a2[0], od_g1[0], od_g2[0], od_k_k[0], od_k_a[0], od_r_k[0], od_ln_g[0],
                   od_ln_b[0], od_w_o[0])
    rw = router_w.astype(F32)
    rb = _row2(router_b)
    wg, wu, wd = (w.reshape((DEPTH * N_EXPERTS,) + w.shape[2:]) for w in (exp_w_gate, exp_w_up, exp_w_down))
    lng = ln_g.astype(F32)[:, :, None, :]
    lnb = ln_b.astype(F32)[:, :, None, :]

    def moe(x, layer):
        sparse = x.shape[0] % (SC_WORKERS * SUBLANES) == 0 and x.shape[0] >= SPARSE_MIN_ROWS
        fn = _moe_sparse if sparse else _moe
        return fn(x, rw, rb, wg, wu, wd, layer * N_EXPERTS, lng[layer, 1], lnb[layer, 1])

    xp = jnp.concatenate([jnp.zeros((PAD_FRONT, D_MODEL), F32), meta_tokens.astype(F32),
                          x_prompt[0].astype(F32), jnp.zeros((tp - end, D_MODEL), F32)], axis=0)
    tabs_p = _rope_tables(jnp.maximum(jnp.arange(tp) - PAD_FRONT, 0))
    evb = _single_pass(ev)
    q_p, ckv_p, kr_p, xr_p, yg_p = _even_proj(xp, evb["w1"], ev["gq"], evb["wuq"], ev["gkv"], tabs_p,
                                              MLA_SCALE * LOG2E)
    ones_col = np.zeros((D_ATT, 1), np.float32)
    ones_col[D_V::HEAD_SLOT] = 1.0
    k_p, vt_p = _kv_proj_t(ckv_p, kr_p, evb["wukv"][:, :D_ATT], evb["place"], evb["wukv"][:, D_ATT:].T,
                           jnp.asarray(ones_col), _round_up(tp, FLASH_TK))
    attn_p = _flash_attention(q_p, k_p, vt_p, FLASH_TQ, FLASH_TK)
    rnn_p, tailx_p, tailh_p = _rglru(
        xr_p[None], yg_p[None], ev["cw"], ev["cb"], evb["wrg"], ev["brg"], evb["wig"], ev["big"], ev["sp"],
        jnp.zeros((1, SUBLANES, D_RNN), F32), jnp.zeros((1, 1, D_RNN), F32), PAD_FRONT, end)
    x1_p = _mix_out(attn_p, rnn_p[0], xp, evb["wa"], evb["wr"], lng[0, 0], lnb[0, 0])
    x2_p = moe(x1_p, 0)

    xs = x_sample.reshape(ns, D_MODEL).astype(F32)
    pos_s = jnp.tile(N_META + past + jnp.arange(ls), bs)
    q_s, ckv_s, kr_s, xr_s, yg_s = _even_proj(xs, ev["w1"], ev["gq"], ev["wuq"], ev["gkv"], _rope_tables(pos_s),
                                              MLA_SCALE)
    _, ckv_m, kr_m, _, _ = _even_proj(meta_tokens.astype(F32), ev["w1"], ev["gq"], ev["wuq"], ev["gkv"],
                                      _rope_tables(jnp.arange(N_META)), MLA_SCALE)
    n_keys = N_META + past + ls
    nk_pad = _round_up(n_keys, LANES)
    meta_ckv = jnp.broadcast_to(ckv_m[None], (bs, N_META, D_C))
    meta_kr = jnp.broadcast_to(kr_m[None], (bs, N_META, LANES))
    cache_kr = jnp.pad(cache_krope[0].astype(F32), ((0, 0), (0, 0), (D_NOPE, LANES - D_NOPE - D_ROPE)))
    all_ckv = jnp.concatenate([meta_ckv, cache_ckv[0].astype(F32), ckv_s.reshape(bs, ls, D_C),
                               jnp.zeros((bs, nk_pad - n_keys, D_C), F32)], axis=1)
    all_kr = jnp.concatenate([meta_kr, cache_kr, kr_s.reshape(bs, ls, LANES),
                              jnp.zeros((bs, nk_pad - n_keys, LANES), F32)], axis=1)
    attn_s = _sample_attention(q_s.reshape(bs, ls, D_ATT), jnp.concatenate([all_ckv, all_kr], axis=2),
                               ev["wabs"], ev["wuv"], n_keys)
    buf0_s = jnp.pad(state_conv[0].astype(F32), ((0, 0), (SUBLANES - (CONV_W - 1), 0), (0, 0)))
    rnn_s, tailx_s, tailh_s = _rglru(
        xr_s.reshape(bs, ls, D_RNN), yg_s.reshape(bs, ls, D_RNN), ev["cw"], ev["cb"], ev["wrg"], ev["brg"],
        ev["wig"], ev["big"], ev["sp"], buf0_s, state_lru[0].astype(F32)[:, None, :], 0, ls)
    x1_s = _mix_out(attn_s.reshape(ns, D_ATT), rnn_s.reshape(ns, D_RNN), xs, ev["wa"], ev["wr"],
                    lng[0, 0], lnb[0, 0])
    x2_s = moe(x1_s, 0)

    r_p, lw_p, kk_in_p, v1_p, kkn_p, a_p, g_p = _rwkv_proj(
        x2_p[None], jnp.zeros((1, 1, D_MODEL), F32), od, PAD_FRONT, end, BF16)
    o_p, s_p = _wkv(r_p, lw_p, kk_in_p, v1_p, kkn_p, a_p,
                    jnp.zeros((1, RWKV_HEADS // 2, LANES, LANES), F32), CHUNK, False)
    x3_p = _rwkv_out(o_p[0], r_p[0], kk_in_p[0], v1_p[0], g_p[0], x2_p, od, lng[1, 0], lnb[1, 0])
    x4_p = moe(x3_p, 1)

    x2_s3 = x2_s.reshape(bs, ls, D_MODEL)
    r_s, lw_s, kk_in_s, v1_s, kkn_s, a_s, g_s = _rwkv_proj(
        x2_s3, state_shift[0].astype(F32)[:, None, :], od, 0, ls, F32)
    to_chunk = lambda z: jnp.pad(z, ((0, 0), (0, _round_up(ls, CHUNK) - ls), (0, 0)))
    o_s, s_s = _wkv(*(to_chunk(z) for z in (r_s, lw_s, kk_in_s, v1_s, kkn_s, a_s)),
                    _pair_states(state_wkv[0]), CHUNK, True)
    o_s = o_s[:, :ls]
    flat = lambda z: z.reshape(ns, D_MODEL)
    x3_s = _rwkv_out(flat(o_s), flat(r_s), flat(kk_in_s), flat(v1_s), flat(g_s), x2_s, od,
                     lng[1, 0], lnb[1, 0])
    x4_s = moe(x3_s, 1)

    dt = x_prompt.dtype
    nb = CONV_W - 1
    return (
        x4_p[ROW0:end][None].astype(dt),
        x4_s.reshape(bs, ls, D_MODEL).astype(dt),
        ckv_p[PAD_FRONT:end][None, None].astype(dt),
        kr_p[PAD_FRONT:end, D_NOPE:D_NOPE + D_ROPE][None, None].astype(dt),
        tailx_p[:, SUBLANES - nb:][None].astype(dt),
        tailh_p[:, SUBLANES - 1][None].astype(dt),
        x2_p[end - 1][None, None].astype(dt),
        _unpair_states(s_p)[None].astype(dt),
        ckv_s.reshape(bs, ls, D_C)[None].astype(dt),
        kr_s.reshape(bs, ls, LANES)[:, :, D_NOPE:D_NOPE + D_ROPE][None].astype(dt),
        tailx_s[:, SUBLANES - nb:][None].astype(dt),
        tailh_s[:, SUBLANES - 1][None].astype(dt),
        x2_s3[:, ls - 1][None].astype(dt),
        _unpair_states(s_s)[None].astype(dt),
    )
```

```python
import functools

import numpy as np
import jax
import jax.numpy as jnp
from jax import lax
from jax.experimental import pallas as pl
from jax.experimental.pallas import tpu as pltpu
from jax.experimental.pallas import tpu_sc as plsc

F32 = jnp.float32
BF16 = jnp.bfloat16

D_MODEL = 1024
N_META = 16
CHUNK = 64
CHUNK_SHIFT = 6
LN_EPS = 1e-5
RMS_EPS = 1e-6
DEPTH = 2
ALPHA = (2 * DEPTH) ** 0.25
MLA_HEADS = 8
D_NOPE = 64
D_ROPE = 32
D_V = 64
D_C = 256
D_CQ = 384
ROPE_BASE = 10000.0
MLA_SCALE = (D_NOPE + D_ROPE) ** -0.5
D_RNN = 512
LRU_BLOCKS = 8
LRU_BLOCK_W = D_RNN // LRU_BLOCKS
CONV_W = 4
LRU_C = 8.0
RWKV_HEAD = 64
RWKV_HEADS = D_MODEL // RWKV_HEAD
DECAY_SCALE = float(np.exp(-0.5))
GN_EPS = 64e-5
N_EXPERTS = 16
N_GROUPS = 4
EXPERTS_PER_GROUP = N_EXPERTS // N_GROUPS
D_EXPERT = 512

LANES = 128
SUBLANES = 8
HEAD_SLOT = LANES
D_ATT = MLA_HEADS * HEAD_SLOT
PAD_FRONT = CHUNK - N_META
ROW0 = PAD_FRONT + N_META
NEG = -1e30
LOG2E = 1.4426950408889634
SC_CORES = 2
SC_SUBCORES = 16
SC_WORKERS = SC_CORES * SC_SUBCORES
MOE_TILE = 512
FLASH_TQ = 512
FLASH_TK = 1024
SPARSE_MIN_ROWS = 1024
VMEM_LIMIT = 56 * 1024 * 1024

C_CQ = 0
C_CKV = D_CQ
C_XR = C_CKV + D_C
C_YG = C_XR + D_RNN
C_KR = C_YG + D_RNN
N_COL = C_KR + LANES


def _cparams(sem):
    return pltpu.CompilerParams(dimension_semantics=sem, vmem_limit_bytes=VMEM_LIMIT)


def _row_tile(n, cap):
    for t in (1024, 512, 256, 128, 64, 32, 16, 8):
        if t <= cap and n % t == 0:
            return t
    return n


def _full(shape):
    zeros = (0,) * len(shape)
    return pl.BlockSpec(shape, lambda *_: zeros)


def _ln(x, g, b):
    mu = jnp.mean(x, axis=-1, keepdims=True)
    xc = x - mu
    var = jnp.mean(xc * xc, axis=-1, keepdims=True)
    return xc * lax.rsqrt(var + LN_EPS) * g + b


def _bdot(a, b):
    return jnp.dot(a.astype(BF16), b.astype(BF16), preferred_element_type=F32)


def _split2(x):
    hi = x.astype(BF16)
    return hi, (x - hi.astype(F32)).astype(BF16)


def _split3(x):
    hi = x.astype(BF16)
    r1 = x - hi.astype(F32)
    mid = r1.astype(BF16)
    return hi, mid, (r1 - mid.astype(F32)).astype(BF16)


_NN = ((1,), (0,))
_NT = ((1,), (1,))
_TN = ((0,), (0,))


def _mm(a, b, dims, exact):
    dn = (dims, ((), ()))
    if not exact:
        return lax.dot_general(a.astype(BF16), b.astype(BF16), dn, preferred_element_type=F32)
    ah, al = _split2(a)
    bh, bl = _split2(b)
    return (lax.dot_general(ah, bh, dn, preferred_element_type=F32)
            + lax.dot_general(al, bh, dn, preferred_element_type=F32)
            + lax.dot_general(ah, bl, dn, preferred_element_type=F32))


def _wdot(a, w):
    return _mm(a, w, _NN, exact=(w.dtype == F32))


def _act_dtype(w):
    return F32 if w.dtype == F32 else BF16


def _rope_slot(x, c, sa, sb):
    return x * c + pltpu.roll(x, LANES - D_ROPE // 2, 1) * sa + pltpu.roll(x, D_ROPE // 2, 1) * sb


def _even_proj_kernel(x_ref, w1_ref, gq_ref, wuq_ref, gkv_ref, c_ref, sa_ref, sb_ref,
                      q_ref, ckv_ref, kr_ref, xr_ref, yg_ref, *, q_scale):
    u = _wdot(x_ref[...], w1_ref[...])
    cq = u[:, C_CQ:C_CQ + D_CQ]
    cq = cq * lax.rsqrt(jnp.mean(cq * cq, axis=-1, keepdims=True) + RMS_EPS) * gq_ref[...]
    q = _wdot(cq, wuq_ref[...])
    c, sa, sb = c_ref[...], sa_ref[...], sb_ref[...]
    for h in range(MLA_HEADS):
        sl = slice(h * HEAD_SLOT, (h + 1) * HEAD_SLOT)
        q_ref[:, sl] = (_rope_slot(q[:, sl], c, sa, sb) * q_scale).astype(q_ref.dtype)
    ckv = u[:, C_CKV:C_CKV + D_C]
    ckv_ref[...] = ckv * lax.rsqrt(jnp.mean(ckv * ckv, axis=-1, keepdims=True) + RMS_EPS) * gkv_ref[...]
    kr_ref[...] = _rope_slot(u[:, C_KR:C_KR + LANES], c, sa, sb)
    xr_ref[...] = u[:, C_XR:C_XR + D_RNN]
    yg_ref[...] = u[:, C_YG:C_YG + D_RNN]


def _even_proj(x, w1, gq, wuq, gkv, tabs, q_scale):
    n = x.shape[0]
    tm = _row_tile(n, 512)
    row = lambda w: pl.BlockSpec((tm, w), lambda i: (i, 0))
    c, sa, sb = tabs
    return pl.pallas_call(
        functools.partial(_even_proj_kernel, q_scale=q_scale),
        grid=(n // tm,),
        in_specs=[row(D_MODEL), _full(w1.shape), _full(gq.shape), _full(wuq.shape), _full(gkv.shape),
                  row(LANES), row(LANES), row(LANES)],
        out_specs=[row(D_ATT), row(D_C), row(LANES), row(D_RNN), row(D_RNN)],
        out_shape=[jax.ShapeDtypeStruct((n, D_ATT), _act_dtype(w1)), jax.ShapeDtypeStruct((n, D_C), F32),
                   jax.ShapeDtypeStruct((n, LANES), F32), jax.ShapeDtypeStruct((n, D_RNN), F32),
                   jax.ShapeDtypeStruct((n, D_RNN), F32)],
        compiler_params=_cparams(("arbitrary",)),
        name="even_proj",
    )(x, w1, gq, wuq, gkv, c, sa, sb)


def _kv_proj_t_kernel(ckv_ref, kr_ref, wuk_ref, p_ref, wuvt_ref, ones_ref, k_ref, vt_ref):
    ckv = ckv_ref[...].astype(BF16)
    k = jnp.dot(ckv, wuk_ref[...], preferred_element_type=F32) + _bdot(kr_ref[...], p_ref[...])
    k_ref[...] = k.astype(BF16)
    vt = lax.dot_general(wuvt_ref[...], ckv, (_NT, ((), ())), preferred_element_type=F32)
    vt_ref[...] = (vt + ones_ref[...]).astype(BF16)


def _kv_proj_t(ckv, kr, wuk, place, wuvt, ones_col, n_out):
    n = ckv.shape[0]
    tm = _row_tile(n, 512)
    assert n_out % tm == 0
    last = n // tm - 1
    row_in = lambda w: pl.BlockSpec((tm, w), lambda i: (jnp.minimum(i, last), 0))
    return pl.pallas_call(
        _kv_proj_t_kernel,
        grid=(n_out // tm,),
        in_specs=[row_in(D_C), row_in(LANES), _full(wuk.shape), _full(place.shape), _full(wuvt.shape),
                  _full(ones_col.shape)],
        out_specs=[pl.BlockSpec((tm, D_ATT), lambda i: (i, 0)), pl.BlockSpec((D_ATT, tm), lambda i: (0, i))],
        out_shape=[jax.ShapeDtypeStruct((n_out, D_ATT), BF16), jax.ShapeDtypeStruct((D_ATT, n_out), BF16)],
        compiler_params=_cparams(("arbitrary",)),
        name="kv_proj_t",
    )(ckv, kr, wuk, place, wuvt, ones_col)


def _flash_kernel(qi_ref, kj_ref, last_ref, q_ref, k_ref, vt_ref, o_ref, m_scr, acc_scr, *, tq, tk):
    step = pl.program_id(0)
    i = qi_ref[step]
    j = kj_ref[step]

    @pl.when(j == 0)
    def _():
        m_scr[...] = jnp.full(m_scr.shape, NEG, F32)
        acc_scr[...] = jnp.zeros(acc_scr.shape, F32)

    heads = range(MLA_HEADS)
    slots = [slice(h * HEAD_SLOT, (h + 1) * HEAD_SLOT) for h in heads]

    def accumulate(masked):
        st = [lax.dot_general(k_ref[:, sl], q_ref[:, sl], (_NT, ((), ())), preferred_element_type=F32)
              for sl in slots]
        if masked:
            krow = j * tk + lax.broadcasted_iota(jnp.int32, (tk, tq), 0)
            qrow = i * tq + lax.broadcasted_iota(jnp.int32, (tk, tq), 1)
            keep = ((((qrow - ROW0) >> CHUNK_SHIFT) >= ((krow - ROW0) >> CHUNK_SHIFT))
                    & (krow >= PAD_FRONT))
            st = [jnp.where(keep, x, NEG) for x in st]
        m_prev = [m_scr[h:h + 1, :] for h in heads]
        m_new = [jnp.maximum(mp, jnp.max(x, axis=0, keepdims=True)) for mp, x in zip(m_prev, st)]
        alpha = [jnp.exp2(mp - mn) for mp, mn in zip(m_prev, m_new)]
        pt = [jnp.exp2(x - mn).astype(BF16) for x, mn in zip(st, m_new)]
        pv = [jnp.dot(vt_ref[sl, :], x, preferred_element_type=F32) for x, sl in zip(pt, slots)]
        for h in heads:
            acc_scr[slots[h], :] = alpha[h] * acc_scr[slots[h], :] + pv[h]
            m_scr[h:h + 1, :] = m_new[h]

    last = last_ref[step] == 1
    edge = last | (j == 0)

    @pl.when(edge)
    def _():
        accumulate(True)

    @pl.when(jnp.logical_not(edge))
    def _():
        accumulate(False)

    @pl.when(last)
    def _():
        for sl in slots:
            acc = acc_scr[sl, :]
            o_ref[:, sl] = (acc / acc[D_V:D_V + 1, :]).T.astype(BF16)


def _flash_attention(q, k, vt, tq, tk):
    n = q.shape[0]
    nq = n // tq
    n_kblocks = [-(-(i + 1) * tq // tk) for i in range(nq)]
    assert k.shape[0] >= n_kblocks[-1] * tk
    qi = np.concatenate([np.full(c, i, np.int32) for i, c in enumerate(n_kblocks)])
    kj = np.concatenate([np.arange(c, dtype=np.int32) for c in n_kblocks])
    last = np.concatenate([np.arange(c, dtype=np.int32) == c - 1 for c in n_kblocks]).astype(np.int32)
    grid_spec = pltpu.PrefetchScalarGridSpec(
        num_scalar_prefetch=3,
        grid=(len(qi),),
        in_specs=[pl.BlockSpec((tq, D_ATT), lambda s, qi, kj, last: (qi[s], 0)),
                  pl.BlockSpec((tk, D_ATT), lambda s, qi, kj, last: (kj[s], 0)),
                  pl.BlockSpec((D_ATT, tk), lambda s, qi, kj, last: (0, kj[s]))],
        out_specs=pl.BlockSpec((tq, D_ATT), lambda s, qi, kj, last: (qi[s], 0)),
        scratch_shapes=[pltpu.VMEM((MLA_HEADS, tq), F32), pltpu.VMEM((D_ATT, tq), F32)],
    )
    return pl.pallas_call(
        functools.partial(_flash_kernel, tq=tq, tk=tk),
        grid_spec=grid_spec,
        out_shape=jax.ShapeDtypeStruct((n, D_ATT), BF16),
        compiler_params=_cparams(("arbitrary",)),
        name="flash_attention",
    )(jnp.asarray(qi), jnp.asarray(kj), jnp.asarray(last), q, k, vt)


def _sample_attn_kernel(q_ref, kc_ref, wabs_ref, wuv_ref, o_ref, *, n_keys):
    kc = kc_ref[0]
    l = q_ref.shape[1]
    slots = [slice(h * HEAD_SLOT, (h + 1) * HEAD_SLOT) for h in range(MLA_HEADS)]
    qa = jnp.concatenate([_mm(q_ref[0, :, sl], wabs_ref[h], _NN, True) for h, sl in enumerate(slots)],
                         axis=0)
    keep = lax.broadcasted_iota(jnp.int32, (MLA_HEADS * l, kc.shape[0]), 1) < n_keys
    s = jnp.where(keep, _mm(qa, kc, _NT, True), NEG)
    p = jnp.exp(s - jnp.max(s, axis=-1, keepdims=True))
    p = p / jnp.sum(p, axis=-1, keepdims=True)
    pc = _mm(p, kc[:, :D_C], _NN, True)
    for h, sl in enumerate(slots):
        o_ref[0, :, sl] = _mm(pc[h * l:(h + 1) * l], wuv_ref[h], _NN, True)


def _sample_attention(q, kc, wabs, wuv, n_keys):
    b, l, _ = q.shape
    nk, dk = kc.shape[1:]
    return pl.pallas_call(
        functools.partial(_sample_attn_kernel, n_keys=n_keys),
        grid=(b,),
        in_specs=[pl.BlockSpec((1, l, D_ATT), lambda i: (i, 0, 0)),
                  pl.BlockSpec((1, nk, dk), lambda i: (i, 0, 0)),
                  _full(wabs.shape), _full(wuv.shape)],
        out_specs=pl.BlockSpec((1, l, D_ATT), lambda i: (i, 0, 0)),
        out_shape=jax.ShapeDtypeStruct((b, l, D_ATT), F32),
        compiler_params=_cparams(("arbitrary",)),
        name="sample_attention",
    )(q, kc, wabs, wuv)


def _expm1(x):
    series = x * (1.0 + x * (0.5 + x * (1.0 / 6.0 + x * (1.0 / 24.0 + x * (1.0 / 120.0)))))
    return jnp.where(jnp.abs(x) < 0.05, series, jnp.exp(x) - 1.0)


def _gelu_tanh(x):
    return 0.5 * x * (1.0 + jnp.tanh(0.7978845608028654 * (x + 0.044715 * x * x * x)))


def _rglru_kernel(xr_ref, yg_ref, cw_ref, cb_ref, wrg_ref, brg_ref, wig_ref, big_ref, sp_ref,
                  buf0_ref, h0_ref, rnn_ref, tailx_ref, tailh_ref, prev_scr, h_scr,
                  *, tm, start, end):
    t = pl.program_id(1)

    @pl.when(t == 0)
    def _():
        prev_scr[...] = buf0_ref[0]
        h_scr[...] = jnp.broadcast_to(h0_ref[0], h_scr.shape)

    x = xr_ref[0]
    ext = jnp.concatenate([prev_scr[...], x], axis=0)
    cw = cw_ref[...]
    xc = cb_ref[...] + cw[CONV_W - 1:CONV_W] * x
    for d in range(1, CONV_W):
        xc = xc + cw[CONV_W - 1 - d:CONV_W - d] * pltpu.roll(ext, d, 0)[SUBLANES:]
    prev_scr[...] = x[tm - SUBLANES:]

    r = jax.nn.sigmoid(_wdot(xc, wrg_ref[...]) + brg_ref[...])
    ig = jax.nn.sigmoid(_wdot(xc, wig_ref[...]) + big_ref[...])
    log_a = -LRU_C * r * sp_ref[...]
    a = jnp.exp(log_a)
    b = jnp.sqrt(-_expm1(2.0 * log_a)) * (ig * xc)
    row = lax.broadcasted_iota(jnp.int32, (tm, D_RNN), 0)
    if start > 0:
        live = (t * tm + row) >= start
        a = jnp.where(live, a, 1.0)
        b = jnp.where(live, b, 0.0)
    d = 1
    while d < tm:
        b = a * jnp.where(row >= d, pltpu.roll(b, d, 0), 0.0) + b
        a = a * jnp.where(row >= d, pltpu.roll(a, d, 0), 1.0)
        d *= 2
    h = a * h_scr[0:1] + b
    h_scr[...] = jnp.broadcast_to(h[tm - 1:tm], h_scr.shape)
    rnn_ref[0] = (h * _gelu_tanh(yg_ref[0])).astype(rnn_ref.dtype)

    t_end = (end - 1) // tm
    el = end - t_end * tm

    @pl.when(t == t_end)
    def _():
        tailx_ref[0] = ext[el:el + SUBLANES]
        tailh_ref[0] = h[el - SUBLANES:el]


def _rglru(xr, yg, cw, cb, wrg, brg, wig, big, sp, buf0, h0, start, end):
    b, l, _ = xr.shape
    tm = _row_tile(l, 512)
    seq = pl.BlockSpec((1, tm, D_RNN), lambda i, t: (i, t, 0))
    per_b = lambda r: pl.BlockSpec((1, r, D_RNN), lambda i, t: (i, 0, 0))
    return pl.pallas_call(
        functools.partial(_rglru_kernel, tm=tm, start=start, end=end),
        grid=(b, l // tm),
        in_specs=[seq, seq, _full(cw.shape), _full(cb.shape), _full(wrg.shape), _full(brg.shape),
                  _full(wig.shape), _full(big.shape), _full(sp.shape), per_b(SUBLANES), per_b(1)],
        out_specs=[seq, per_b(SUBLANES), per_b(SUBLANES)],
        out_shape=[jax.ShapeDtypeStruct((b, l, D_RNN), _act_dtype(wrg)),
                   jax.ShapeDtypeStruct((b, SUBLANES, D_RNN), F32),
                   jax.ShapeDtypeStruct((b, SUBLANES, D_RNN), F32)],
        scratch_shapes=[pltpu.VMEM((SUBLANES, D_RNN), F32), pltpu.VMEM((SUBLANES, D_RNN), F32)],
        compiler_params=_cparams(("arbitrary", "arbitrary")),
        name="rglru",
    )(xr, yg, cw, cb, wrg, brg, wig, big, sp, buf0, h0)


def _mix_out_kernel(attn_ref, rnn_ref, x_ref, wa_ref, wr_ref, g_ref, b_ref, o_ref):
    mix = _wdot(attn_ref[...], wa_ref[...]) + _wdot(rnn_ref[...], wr_ref[...])
    o_ref[...] = _ln(ALPHA * x_ref[...] + mix, g_ref[...], b_ref[...])


def _mix_out(attn, rnn, x, wa, wr, g, b):
    n = x.shape[0]
    tm = _row_tile(n, 512)
    row = lambda w: pl.BlockSpec((tm, w), lambda i: (i, 0))
    return pl.pallas_call(
        _mix_out_kernel,
        grid=(n // tm,),
        in_specs=[row(D_ATT), row(D_RNN), row(D_MODEL), _full(wa.shape), _full(wr.shape),
                  _full(g.shape), _full(b.shape)],
        out_specs=row(D_MODEL),
        out_shape=jax.ShapeDtypeStruct((n, D_MODEL), F32),
        compiler_params=_cparams(("arbitrary",)),
        name="mix_out",
    )(attn, rnn, x, wa, wr, g, b)


def _first_argmax(vals, lane):
    m = jnp.max(vals, axis=-1, keepdims=True)
    idx = jnp.min(jnp.where(vals == m, lane, N_EXPERTS), axis=-1, keepdims=True)
    return m, idx


def _router_top2(x, rw, rb):
    logits = _mm(x, rw, _NN, True)
    s = jax.nn.sigmoid(logits)
    sel = s + rb
    lane = lax.broadcasted_iota(jnp.int32, sel.shape, 1)
    grp = lane >> 2
    best = None
    g_best = None
    for g in range(N_GROUPS):
        vals = jnp.where(grp == g, sel, NEG)
        m1, i1 = _first_argmax(vals, lane)
        m2, _ = _first_argmax(jnp.where(lane == i1, NEG, vals), lane)
        score = m1 + m2
        if g == 0:
            best, g_best = score, jnp.zeros_like(i1)
        else:
            upd = score > best
            g_best = jnp.where(upd, g, g_best)
            best = jnp.where(upd, score, best)
    vals = jnp.where(grp == g_best, sel, NEG)
    _, i1 = _first_argmax(vals, lane)
    _, i2 = _first_argmax(jnp.where(lane == i1, NEG, vals), lane)
    w1 = jnp.sum(jnp.where(lane == i1, s, 0.0), axis=-1, keepdims=True)
    w2 = jnp.sum(jnp.where(lane == i2, s, 0.0), axis=-1, keepdims=True)
    den = w1 + w2
    return lane, i1, i2, w1 / den, w2 / den


def _router_gate(x, rw, rb):
    lane, i1, i2, g1, g2 = _router_top2(x, rw, rb)
    return jnp.where(lane == i1, g1, 0.0) + jnp.where(lane == i2, g2, 0.0)


def _moe_kernel(x_ref, rw_ref, rb_ref, wg_ref, wu_ref, wd_ref, g_ref, b_ref, o_ref,
                gate_scr, xb_scr, acc_scr):
    e = pl.program_id(1)

    @pl.when(e == 0)
    def _():
        x = x_ref[...]
        gate = _router_gate(x, rw_ref[...], rb_ref[...])
        for k in range(N_EXPERTS):
            gate_scr[k] = jnp.broadcast_to(gate[:, k:k + 1], gate_scr.shape[1:])
        xb_scr[...] = x.astype(BF16)
        acc_scr[...] = jnp.zeros(acc_scr.shape, F32)

    xb = xb_scr[...]
    hg = jnp.dot(xb, wg_ref[0].astype(BF16), preferred_element_type=F32)
    hu = jnp.dot(xb, wu_ref[0].astype(BF16), preferred_element_type=F32)
    gate_e = gate_scr[e]
    h = jax.nn.silu(hg) * hu * jnp.concatenate([gate_e] * (D_EXPERT // LANES), axis=1)
    acc_scr[...] += jnp.dot(h.astype(BF16), wd_ref[0].astype(BF16), preferred_element_type=F32)

    @pl.when(e == N_EXPERTS - 1)
    def _():
        o_ref[...] = _ln(ALPHA * x_ref[...] + acc_scr[...], g_ref[...], b_ref[...])


def _moe(x, rw, rb, wg, wu, wd, e0, g, b):
    n = x.shape[0]
    tm = _row_tile(n, 512)
    row = pl.BlockSpec((tm, D_MODEL), lambda i, e: (i, 0))
    return pl.pallas_call(
        _moe_kernel,
        grid=(n // tm, N_EXPERTS),
        in_specs=[row, _full(rw.shape), _full(rb.shape),
                  pl.BlockSpec((1, D_MODEL, D_EXPERT), lambda i, e: (e0 + e, 0, 0)),
                  pl.BlockSpec((1, D_MODEL, D_EXPERT), lambda i, e: (e0 + e, 0, 0)),
                  pl.BlockSpec((1, D_EXPERT, D_MODEL), lambda i, e: (e0 + e, 0, 0)),
                  _full(g.shape), _full(b.shape)],
        out_specs=row,
        out_shape=jax.ShapeDtypeStruct((n, D_MODEL), F32),
        scratch_shapes=[pltpu.VMEM((N_EXPERTS, tm, LANES), F32), pltpu.VMEM((tm, D_MODEL), BF16),
                        pltpu.VMEM((tm, D_MODEL), F32)],
        compiler_params=_cparams(("arbitrary", "arbitrary")),
        name="moe",
    )(x, rw, rb, wg, wu, wd, g, b)


M_I1, M_I2, M_R1, M_R2, M_G1, M_G2, M_COLS = 0, 1, 2, 3, 4, 5, 8


def _first_argmax_rows(vals, row):
    m = jnp.max(vals, axis=0, keepdims=True)
    idx = jnp.min(jnp.where(vals == m, row, N_EXPERTS), axis=0, keepdims=True)
    return m, idx


HI16 = -65536


def _pack_bf16_pairs(x):
    w = x.shape[1] // 2
    hi = lax.bitcast_convert_type(x[:, :w].astype(BF16).astype(F32), jnp.int32)
    lo = lax.bitcast_convert_type(x[:, w:].astype(BF16).astype(F32), jnp.int32)
    return (hi & HI16) | lax.shift_right_logical(lo, 16)


def _unpack_bf16_pairs(p):
    hi = lax.bitcast_convert_type(p & HI16, F32)
    lo = lax.bitcast_convert_type(lax.shift_left(p, 16), F32)
    return jnp.concatenate([hi, lo], axis=1)


def _route_kernel(x_ref, rwt_ref, rbc_ref, meta_ref, cnt_ref, xpk_ref, carry_scr, *, tm):
    @pl.when(pl.program_id(0) == 0)
    def _():
        carry_scr[...] = jnp.zeros(carry_scr.shape, F32)

    logits = _mm(rwt_ref[...], x_ref[...], _NT, True)
    s = jax.nn.sigmoid(logits)
    sel = s + rbc_ref[...]
    row = lax.broadcasted_iota(jnp.int32, sel.shape, 0)
    grp = row >> 2
    best = None
    g_best = None
    for g in range(N_GROUPS):
        vals = jnp.where(grp == g, sel, NEG)
        m1, i1 = _first_argmax_rows(vals, row)
        m2, _ = _first_argmax_rows(jnp.where(row == i1, NEG, vals), row)
        score = m1 + m2
        if g == 0:
            best, g_best = score, jnp.zeros_like(i1)
        else:
            upd = score > best
            g_best = jnp.where(upd, g, g_best)
            best = jnp.where(upd, score, best)
    vals = jnp.where(grp == g_best, sel, NEG)
    _, i1 = _first_argmax_rows(vals, row)
    _, i2 = _first_argmax_rows(jnp.where(row == i1, NEG, vals), row)
    w1 = jnp.sum(jnp.where(row == i1, s, 0.0), axis=0, keepdims=True)
    w2 = jnp.sum(jnp.where(row == i2, s, 0.0), axis=0, keepdims=True)
    den = w1 + w2

    chosen = jnp.where((row == i1) | (row == i2), 1.0, 0.0)
    earlier = (lax.broadcasted_iota(jnp.int32, (tm, tm), 0)
               < lax.broadcasted_iota(jnp.int32, (tm, tm), 1)).astype(BF16)
    seen = jnp.dot(chosen.astype(BF16), earlier, preferred_element_type=F32) + carry_scr[:, 0:1]
    r1 = jnp.sum(jnp.where(row == i1, seen, 0.0), axis=0, keepdims=True)
    r2 = jnp.sum(jnp.where(row == i2, seen, 0.0), axis=0, keepdims=True)
    carry_scr[...] = carry_scr[...] + jnp.sum(chosen, axis=1, keepdims=True)
    mrow = lax.broadcasted_iota(jnp.int32, (M_COLS, tm), 0)
    meta = jnp.zeros((M_COLS, tm), F32)
    for c, val in ((M_I1, i1.astype(F32)), (M_I2, i2.astype(F32)), (M_R1, r1), (M_R2, r2),
                   (M_G1, w1 / den), (M_G2, w2 / den)):
        meta = jnp.where(mrow == c, val, meta)
    meta_ref[...] = meta
    cnt_ref[...] = carry_scr[...]
    xpk_ref[...] = _pack_bf16_pairs(x_ref[...])


def _route(x, rwt, rbc):
    n = x.shape[0]
    tm = _row_tile(n, 512)
    return pl.pallas_call(
        functools.partial(_route_kernel, tm=tm),
        grid=(n // tm,),
        in_specs=[pl.BlockSpec((tm, D_MODEL), lambda i: (i, 0)), _full(rwt.shape), _full(rbc.shape)],
        out_specs=[pl.BlockSpec((M_COLS, tm), lambda i: (0, i)), _full((N_EXPERTS, LANES)),
                   pl.BlockSpec((tm, D_MODEL // 2), lambda i: (i, 0))],
        out_shape=[jax.ShapeDtypeStruct((M_COLS, n), F32), jax.ShapeDtypeStruct((N_EXPERTS, LANES), F32),
                   jax.ShapeDtypeStruct((n, D_MODEL // 2), jnp.int32)],
        scratch_shapes=[pltpu.VMEM((N_EXPERTS, LANES), F32)],
        compiler_params=_cparams(("arbitrary",)),
        name="moe_route",
    )(x, rwt, rbc)


def _sc_chunk(per_worker):
    for c in (64, 48, 32, 16, 8):
        if per_worker % c == 0:
            return c
    raise ValueError(per_worker)


def _sc_mesh():
    return plsc.VectorSubcoreMesh(core_axis_name="c", subcore_axis_name="s")


def _sc_scatter2(x, idx1, idx2, n_out):
    n, d = x.shape
    per_w = n // SC_WORKERS
    assert per_w * SC_WORKERS == n
    chunk = _sc_chunk(per_w)

    @functools.partial(
        pl.kernel, mesh=_sc_mesh(), out_type=jax.ShapeDtypeStruct((n_out, d), x.dtype),
        scratch_types=[pltpu.VMEM((chunk,), jnp.int32), pltpu.VMEM((chunk,), jnp.int32),
                       pltpu.VMEM((chunk, d), x.dtype), pltpu.SemaphoreType.DMA])
    def scatter(x_hbm, i1_hbm, i2_hbm, out_hbm, i1_v, i2_v, rows_v, sem):
        base = (lax.axis_index("s") * SC_CORES + lax.axis_index("c")) * per_w

        @pl.loop(0, per_w // chunk)
        def _(c):
            off = pl.multiple_of(base + c * chunk, SUBLANES)
            pltpu.sync_copy(i1_hbm.at[pl.ds(off, chunk)], i1_v)
            pltpu.sync_copy(i2_hbm.at[pl.ds(off, chunk)], i2_v)
            pltpu.sync_copy(x_hbm.at[pl.ds(off, chunk)], rows_v)
            pltpu.async_copy(rows_v, out_hbm.at[i1_v], sem).wait()
            pltpu.async_copy(rows_v, out_hbm.at[i2_v], sem).wait()

    return scatter(x, idx1, idx2)


def _sc_gather(y, idx):
    n = idx.shape[0]
    d = y.shape[1]
    per_w = n // SC_WORKERS
    assert per_w * SC_WORKERS == n
    chunk = _sc_chunk(per_w)

    @functools.partial(
        pl.kernel, mesh=_sc_mesh(), out_type=jax.ShapeDtypeStruct((n, d), y.dtype),
        scratch_types=[pltpu.VMEM((chunk,), jnp.int32), pltpu.VMEM((chunk, d), y.dtype),
                       pltpu.SemaphoreType.DMA])
    def gather(y_hbm, idx_hbm, out_hbm, idx_v, rows_v, sem):
        base = (lax.axis_index("s") * SC_CORES + lax.axis_index("c")) * per_w

        @pl.loop(0, per_w // chunk)
        def _(c):
            off = pl.multiple_of(base + c * chunk, SUBLANES)
            pltpu.sync_copy(idx_hbm.at[pl.ds(off, chunk)], idx_v)
            pltpu.async_copy(y_hbm.at[idx_v], rows_v, sem).wait()
            pltpu.sync_copy(rows_v, out_hbm.at[pl.ds(off, chunk)])

    return gather(y, idx)


def _experts_kernel(te_ref, used_ref, x_ref, wg_ref, wu_ref, wd_ref, o_ref):
    @pl.when(pl.program_id(0) < used_ref[0])
    def _():
        xb = _unpack_bf16_pairs(x_ref[...]).astype(BF16)
        hg = jnp.dot(xb, wg_ref[0].astype(BF16), preferred_element_type=F32)
        hu = jnp.dot(xb, wu_ref[0].astype(BF16), preferred_element_type=F32)
        h = jax.nn.silu(hg) * hu
        y = jnp.dot(h.astype(BF16), wd_ref[0].astype(BF16), preferred_element_type=F32)
        o_ref[...] = _pack_bf16_pairs(y)


def _experts(xg, tile_expert, n_used, wg, wu, wd):
    n_tiles = xg.shape[0] // MOE_TILE
    row = pl.BlockSpec((MOE_TILE, D_MODEL // 2), lambda i, te, used: (i, 0))
    grid_spec = pltpu.PrefetchScalarGridSpec(
        num_scalar_prefetch=2,
        grid=(n_tiles,),
        in_specs=[row,
                  pl.BlockSpec((1, D_MODEL, D_EXPERT), lambda i, te, used: (te[i], 0, 0)),
                  pl.BlockSpec((1, D_MODEL, D_EXPERT), lambda i, te, used: (te[i], 0, 0)),
                  pl.BlockSpec((1, D_EXPERT, D_MODEL), lambda i, te, used: (te[i], 0, 0))],
        out_specs=row,
    )
    return pl.pallas_call(
        _experts_kernel,
        grid_spec=grid_spec,
        out_shape=jax.ShapeDtypeStruct(xg.shape, jnp.int32),
        compiler_params=_cparams(("arbitrary",)),
        name="moe_experts",
    )(tile_expert, n_used, xg, wg, wu, wd)


def _combine_kernel(x_ref, y1_ref, y2_ref, meta_ref, g_ref, b_ref, o_ref):
    meta = meta_ref[...]
    moe = (meta[:, M_G1:M_G1 + 1] * _unpack_bf16_pairs(y1_ref[...])
           + meta[:, M_G2:M_G2 + 1] * _unpack_bf16_pairs(y2_ref[...]))
    o_ref[...] = _ln(ALPHA * x_ref[...] + moe, g_ref[...], b_ref[...])


def _combine(x, y1, y2, meta, g, b):
    n = x.shape[0]
    tm = _row_tile(n, 512)
    row = pl.BlockSpec((tm, D_MODEL), lambda i: (i, 0))
    half = pl.BlockSpec((tm, D_MODEL // 2), lambda i: (i, 0))
    return pl.pallas_call(
        _combine_kernel,
        grid=(n // tm,),
        in_specs=[row, half, half, pl.BlockSpec((tm, M_COLS), lambda i: (i, 0)), _full(g.shape), _full(b.shape)],
        out_specs=row,
        out_shape=jax.ShapeDtypeStruct((n, D_MODEL), F32),
        compiler_params=_cparams(("arbitrary",)),
        name="moe_combine",
    )(x, y1, y2, meta, g, b)


def _moe_sparse(x, rw, rb, wg, wu, wd, e0, g, b):
    n = x.shape[0]
    meta_t, counts, x_packed = _route(x, rw.T, rb.reshape(N_EXPERTS, 1))
    cnt = counts[:, 0].astype(jnp.int32)
    padded = (cnt + MOE_TILE - 1) // MOE_TILE * MOE_TILE
    seg_end = jnp.cumsum(padded)
    seg_start = seg_end - padded
    experts = jnp.arange(N_EXPERTS, dtype=jnp.int32)[:, None]
    start_of = lambda e: jnp.sum(jnp.where(experts == e[None, :], seg_start[:, None], 0), axis=0)
    e1, e2 = meta_t[M_I1].astype(jnp.int32), meta_t[M_I2].astype(jnp.int32)
    pos1 = start_of(e1) + meta_t[M_R1].astype(jnp.int32)
    pos2 = start_of(e2) + meta_t[M_R2].astype(jnp.int32)
    meta = meta_t.T
    n_tiles = -(-2 * n // MOE_TILE) + N_EXPERTS
    tile_start = jnp.arange(n_tiles, dtype=jnp.int32) * MOE_TILE
    tile_expert = e0 + jnp.minimum(jnp.sum(tile_start[:, None] >= seg_end[None, :], axis=1),
                                   N_EXPERTS - 1).astype(jnp.int32)
    n_used = (seg_end[-1:] // MOE_TILE).astype(jnp.int32)
    xg = _sc_scatter2(x_packed, pos1, pos2, n_tiles * MOE_TILE)
    yg = _experts(xg, tile_expert, n_used, wg, wu, wd)
    return _combine(x, _sc_gather(yg, pos1), _sc_gather(yg, pos2), meta, g, b)


def _head_sum(z, ones):
    hi, lo = _split2(z)
    parts = []
    for g in range(D_MODEL // LANES):
        sl = slice(g * LANES, (g + 1) * LANES)
        parts.append(jnp.dot(hi[:, sl], ones, preferred_element_type=F32)
                     + jnp.dot(lo[:, sl], ones, preferred_element_type=F32))
    return jnp.concatenate(parts, axis=1)


def _rwkv_proj_kernel(x_ref, sh0_ref, mu_ref, wr_ref, wk_ref, wv_ref, w0_ref, w1_ref, w2_ref,
                      a0_ref, a1_ref, a2_ref, g1_ref, g2_ref, kkw_ref, kaw_ref, ones_ref,
                      r_ref, lw_ref, k_ref, v_ref, kk_ref, a_ref, g_ref, prev_scr,
                      *, tm, start, end):
    t = pl.program_id(1)

    @pl.when(t == 0)
    def _():
        prev_scr[...] = jnp.zeros(prev_scr.shape, F32)

    x = x_ref[0]
    ext = jnp.concatenate([prev_scr[...], x], axis=0)
    x_prev = pltpu.roll(ext, 1, 0)[SUBLANES:]
    grow = t * tm + lax.broadcasted_iota(jnp.int32, (tm, D_MODEL), 0)
    x_prev = jnp.where(grow == start, sh0_ref[0], x_prev)
    prev_scr[...] = x[tm - SUBLANES:]
    xx = x_prev - x
    mu = mu_ref[...]
    xr, xw, xk, xv, xa, xg = (x + xx * mu[n:n + 1] for n in range(6))
    r = _bdot(xr, wr_ref[...])
    k = _bdot(xk, wk_ref[...])
    v = _bdot(xv, wv_ref[...])
    log_w = -DECAY_SCALE * jax.nn.sigmoid(w0_ref[...] + _bdot(jnp.tanh(_bdot(xw, w1_ref[...])), w2_ref[...]))
    a = jax.nn.sigmoid(a0_ref[...] + _bdot(_bdot(xa, a1_ref[...]), a2_ref[...]))
    g = _bdot(jax.nn.sigmoid(_bdot(xg, g1_ref[...])), g2_ref[...])
    kk = k * kkw_ref[...]
    norm = jnp.sqrt(_head_sum(kk * kk, ones_ref[...]))
    kk = kk / jnp.maximum(norm, 1e-12)
    k = k * (1.0 + (a - 1.0) * kaw_ref[...])
    live = (grow >= start) & (grow < end)
    r_ref[0] = r.astype(r_ref.dtype)
    lw_ref[0] = jnp.where(live, log_w, 0.0)
    k_ref[0] = jnp.where(live, k, 0.0).astype(k_ref.dtype)
    v_ref[0] = v.astype(v_ref.dtype)
    kk_ref[0] = jnp.where(live, kk, 0.0).astype(kk_ref.dtype)
    a_ref[0] = a.astype(a_ref.dtype)
    g_ref[0] = g.astype(g_ref.dtype)


def _rwkv_proj(x, sh0, od, start, end, act_dtype):
    b, l, _ = x.shape
    tm = _row_tile(l, 256)
    dtypes = [act_dtype, F32] + [act_dtype] * 5
    seq = pl.BlockSpec((1, tm, D_MODEL), lambda i, t: (i, t, 0))
    ws = [od[n] for n in ("mu", "w_r", "w_k", "w_v", "w0", "w1", "w2", "a0", "a1", "a2", "g1", "g2",
                          "k_k", "k_a", "ones")]
    return pl.pallas_call(
        functools.partial(_rwkv_proj_kernel, tm=tm, start=start, end=end),
        grid=(b, l // tm),
        in_specs=[seq, pl.BlockSpec((1, 1, D_MODEL), lambda i, t: (i, 0, 0))] + [_full(w.shape) for w in ws],
        out_specs=[seq] * 7,
        out_shape=[jax.ShapeDtypeStruct((b, l, D_MODEL), dt) for dt in dtypes],
        scratch_shapes=[pltpu.VMEM((SUBLANES, D_MODEL), F32)],
        compiler_params=_cparams(("arbitrary", "arbitrary")),
        name="rwkv_proj",
    )(x, sh0, *ws)


def _wkv_kernel(r_ref, lw_ref, k_ref, v_ref, kk_ref, a_ref, s0_ref, o_ref, sout_ref, s_scr,
                *, c, exact):
    t = pl.program_id(1)

    @pl.when(t == 0)
    def _():
        s_scr[...] = s0_ref[0]

    head0 = lax.broadcasted_iota(jnp.int32, (c, LANES), 1) < RWKV_HEAD
    c2 = 2 * c
    row = lax.broadcasted_iota(jnp.int32, (c2, c2), 0)
    col = lax.broadcasted_iota(jnp.int32, (c2, c2), 1)
    row_hi = jnp.where(row >= c, c, 0)
    col_hi = jnp.where(col >= c, c, 0)
    same = row_hi == col_hi
    rr = row - row_hi
    cc = col - col_hi
    strict = same & (rr > cc)
    incl = same & (rr >= cc)
    eye = (row == col).astype(F32)
    tri = (lax.broadcasted_iota(jnp.int32, (c, c), 0) >= lax.broadcasted_iota(jnp.int32, (c, c), 1)).astype(BF16)

    def stack(x):
        return jnp.concatenate([jnp.where(head0, x, 0.0), jnp.where(head0, 0.0, x)], axis=0)

    mm = functools.partial(_mm, exact=exact)
    pairs = range(RWKV_HEADS // 2)
    for sub in range(r_ref.shape[1] // c):
        rows = slice(sub * c, (sub + 1) * c)
        load = lambda ref: [ref[0, rows, p * LANES:(p + 1) * LANES].astype(F32) for p in pairs]
        r, lw, k, v, kk, a = (load(ref) for ref in (r_ref, lw_ref, k_ref, v_ref, kk_ref, a_ref))
        lc = [sum(jnp.dot(tri, part, preferred_element_type=F32) for part in _split3(x)) for x in lw]
        lc_end = [x[c - 1:c] for x in lc]
        b = [x * y for x, y in zip(kk, a)]
        lhs = [jnp.concatenate([stack(-kk[p] * jnp.exp(lc[p] - lw[p])), stack(r[p] * jnp.exp(lc[p]))], axis=0)
               for p in pairs]
        g_inv = [jnp.exp(-x) for x in lc]
        rhs = [jnp.concatenate([stack(b[p] * g_inv[p]), stack(k[p] * g_inv[p])], axis=0) for p in pairs]
        pm = [mm(x, y, _NT) for x, y in zip(lhs, rhs)]
        l_ab = [jnp.where(strict, x[:c2, :c2], 0.0) for x in pm]
        l_ak = [jnp.where(strict, x[:c2, c2:], 0.0) for x in pm]
        m_rb = [jnp.where(incl, x[c2:, :c2], 0.0) for x in pm]
        m_rk = [jnp.where(incl, x[c2:, c2:], 0.0) for x in pm]
        vs = [stack(x) for x in v]
        lakv = [mm(x, y, _NN) for x, y in zip(l_ak, vs)]
        mrkv = [mm(x, y, _NN) for x, y in zip(m_rk, vs)]
        tinv = [eye + x for x in l_ab]
        lp = l_ab
        n = 2
        while n < c:
            lp = [mm(x, x, _NN) for x in lp]
            tinv = [x + mm(x, y, _NN) for x, y in zip(tinv, lp)]
            n *= 2
        s = [s_scr[p] for p in pairs]
        xs = [mm(x, y, _NT) for x, y in zip(lhs, s)]
        u = [mm(tinv[p], xs[p][:c2] + lakv[p], _NN) for p in pairs]
        for p in pairs:
            os_ = xs[p][c2:] + mm(m_rb[p], u[p], _NN) + mrkv[p]
            o_ref[0, rows, p * LANES:(p + 1) * LANES] = os_[:c] + os_[c:]
        for p in pairs:
            g_rem = jnp.exp(lc_end[p] - lc[p])
            uv = jnp.concatenate([u[p], vs[p]], axis=0)
            bk = jnp.concatenate([stack(b[p] * g_rem), stack(k[p] * g_rem)], axis=0)
            s_scr[p] = s[p] * jnp.exp(lc_end[p]) + mm(uv, bk, _TN)

    @pl.when(t == pl.num_programs(1) - 1)
    def _():
        sout_ref[0] = s_scr[...]


def _wkv(r, lw, k, v, kk, a, s0, c, exact):
    b, l, _ = r.shape
    rows = 2 * c if l % (2 * c) == 0 else c
    seq = pl.BlockSpec((1, rows, D_MODEL), lambda i, t: (i, t, 0))
    st = pl.BlockSpec((1, RWKV_HEADS // 2, LANES, LANES), lambda i, t: (i, 0, 0, 0))
    return pl.pallas_call(
        functools.partial(_wkv_kernel, c=c, exact=exact),
        grid=(b, l // rows),
        in_specs=[seq] * 6 + [st],
        out_specs=[seq, st],
        out_shape=[jax.ShapeDtypeStruct((b, l, D_MODEL), F32),
                   jax.ShapeDtypeStruct((b, RWKV_HEADS // 2, LANES, LANES), F32)],
        scratch_shapes=[pltpu.VMEM((RWKV_HEADS // 2, LANES, LANES), F32)],
        compiler_params=_cparams(("arbitrary", "arbitrary")),
        name="wkv",
    )(r, lw, k, v, kk, a, s0)


def _rwkv_out_kernel(o_ref, r_ref, k_ref, v_ref, g_ref, x_ref, rk_ref, gng_ref, gnb_ref, wo_ref,
                     ones_ref, lg_ref, lb_ref, y_ref):
    ones = ones_ref[...]
    o = o_ref[...]
    inv = 1.0 / RWKV_HEAD
    mu = _head_sum(o, ones) * inv
    oc = o - mu
    var = _head_sum(oc * oc, ones) * inv
    on = oc * lax.rsqrt(var + GN_EPS) * gng_ref[...] + gnb_ref[...]
    r, k, v, g = (ref[...].astype(F32) for ref in (r_ref, k_ref, v_ref, g_ref))
    on = on + _head_sum(r * k * rk_ref[...], ones) * v
    out = _bdot(on * g, wo_ref[...])
    y_ref[...] = _ln(ALPHA * x_ref[...] + out, lg_ref[...], lb_ref[...])


def _rwkv_out(o, r, k, v, g, x, od, lg, lb):
    n = x.shape[0]
    tm = _row_tile(n, 256)
    row = pl.BlockSpec((tm, D_MODEL), lambda i: (i, 0))
    ws = [od["r_k"], od["ln_g"], od["ln_b"], od["w_o"], od["ones"], lg, lb]
    return pl.pallas_call(
        _rwkv_out_kernel,
        grid=(n // tm,),
        in_specs=[row] * 6 + [_full(w.shape) for w in ws],
        out_specs=row,
        out_shape=jax.ShapeDtypeStruct((n, D_MODEL), F32),
        compiler_params=_cparams(("arbitrary",)),
        name="rwkv_out",
    )(o, r, k, v, g, x, *ws)


def _rope_tables(pos):
    half = D_ROPE // 2
    freq = ROPE_BASE ** (-jnp.arange(half, dtype=F32) / half)
    ang_t = freq[:, None] * pos.astype(F32)[None, :]
    cos_t, sin_t = lax.optimization_barrier((jnp.cos(ang_t), jnp.sin(ang_t)))
    cos, sin = cos_t.T, sin_t.T
    n = pos.shape[0]
    ones = jnp.ones((n, D_NOPE), F32)
    zeros = jnp.zeros((n, D_NOPE), F32)
    z16 = jnp.zeros((n, half), F32)
    tail1 = jnp.ones((n, LANES - D_NOPE - D_ROPE), F32)
    tail0 = jnp.zeros((n, LANES - D_NOPE - D_ROPE), F32)
    c = jnp.concatenate([ones, cos, cos, tail1], axis=1)
    sa = jnp.concatenate([zeros, -sin, z16, tail0], axis=1)
    sb = jnp.concatenate([zeros, z16, sin, tail0], axis=1)
    return c, sa, sb


def _slot_cols(w, width):
    k, h, _ = w.shape
    return jnp.pad(w, ((0, 0), (0, 0), (0, HEAD_SLOT - width))).reshape(k, h * HEAD_SLOT)


def _block_diag(w):
    n, c, d = w.shape
    eye = jnp.eye(n, dtype=w.dtype)
    return (eye[:, None, :, None] * w[:, :, None, :]).reshape(n * c, n * d)


def _row2(v):
    return v.reshape(1, -1).astype(F32)


def _prep_even(w_in, g_q, w_uq, g_kv, w_uk, w_uv, conv_w, conv_b, w_rg, b_rg, w_ig, b_ig, lam, w_out):
    off_ckv, off_kr = D_CQ, D_CQ + D_C
    off_xr = off_kr + D_ROPE
    off_y = off_xr + D_RNN
    kr_cols = jnp.pad(w_in[:, off_kr:off_xr], ((0, 0), (D_NOPE, LANES - D_NOPE - D_ROPE)))
    w_in, w_uq, w_uk, w_uv, w_rg, w_ig, w_out = (
        w.astype(F32) for w in (w_in, w_uq, w_uk, w_uv, w_rg, w_ig, w_out))
    w1 = jnp.concatenate([w_in[:, :off_ckv], w_in[:, off_ckv:off_kr], w_in[:, off_xr:off_y],
                          w_in[:, off_y:], kr_cols], axis=1)
    wuq = _slot_cols(w_uq, D_NOPE + D_ROPE)
    wukv = jnp.concatenate([_slot_cols(w_uk, D_NOPE), _slot_cols(w_uv, D_V)], axis=1)
    place = np.zeros((LANES, D_ATT), np.float32)
    for h in range(MLA_HEADS):
        for cidx in range(D_ROPE):
            place[D_NOPE + cidx, h * HEAD_SLOT + D_NOPE + cidx] = 1.0
    wa = jnp.pad(w_out[:MLA_HEADS * D_V].reshape(MLA_HEADS, D_V, D_MODEL),
                 ((0, 0), (0, HEAD_SLOT - D_V), (0, 0))).reshape(D_ATT, D_MODEL)
    wr = w_out[MLA_HEADS * D_V:]
    keep_rope = np.zeros((HEAD_SLOT, LANES), np.float32)
    keep_rope[D_NOPE:D_NOPE + D_ROPE, D_NOPE:D_NOPE + D_ROPE] = np.eye(D_ROPE, dtype=np.float32)
    uk_t = jnp.pad(jnp.transpose(w_uk, (1, 2, 0)), ((0, 0), (0, HEAD_SLOT - D_NOPE), (0, 0)))
    wabs = jnp.concatenate([uk_t, jnp.broadcast_to(keep_rope, (MLA_HEADS, HEAD_SLOT, LANES))], axis=2)
    wuv = jnp.pad(jnp.transpose(w_uv, (1, 0, 2)), ((0, 0), (0, 0), (0, HEAD_SLOT - D_V)))
    return dict(
        wabs=wabs, wuv=wuv,
        w1=w1, gq=_row2(g_q), wuq=wuq, gkv=_row2(g_kv), wukv=wukv, place=jnp.asarray(place, F32),
        cw=conv_w.astype(F32), cb=_row2(conv_b), wrg=_block_diag(w_rg), brg=_row2(b_rg),
        wig=_block_diag(w_ig), big=_row2(b_ig), sp=_row2(jax.nn.softplus(-lam.astype(F32))),
        wa=wa, wr=wr)


_EVEN_MATMUL_WEIGHTS = ("w1", "wuq", "wukv", "place", "wrg", "wig", "wa", "wr")


def _single_pass(ev):
    return {n: (w.astype(BF16) if n in _EVEN_MATMUL_WEIGHTS else w) for n, w in ev.items()}


def _prep_odd(mu, w_r, w_k, w_v, w0, w1, w2, a0, a1, a2, g1, g2, k_k, k_a, r_k, ln_g, ln_b, w_o):
    ones = np.zeros((LANES, LANES), np.float32)
    ones[:RWKV_HEAD, :RWKV_HEAD] = 1.0
    ones[RWKV_HEAD:, RWKV_HEAD:] = 1.0
    return dict(
        mu=jnp.pad(mu.astype(F32), ((0, SUBLANES - mu.shape[0]), (0, 0))),
        w_r=w_r.astype(BF16), w_k=w_k.astype(BF16), w_v=w_v.astype(BF16), w0=_row2(w0),
        w1=w1.astype(BF16), w2=w2.astype(BF16), a0=_row2(a0), a1=a1.astype(BF16), a2=a2.astype(BF16),
        g1=g1.astype(BF16), g2=g2.astype(BF16), k_k=_row2(k_k), k_a=_row2(k_a), r_k=_row2(r_k),
        ln_g=_row2(ln_g), ln_b=_row2(ln_b), w_o=w_o.astype(BF16), ones=jnp.asarray(ones, BF16))


def _pair_states(s):
    b = s.shape[0]
    s = s.reshape(b, RWKV_HEADS // 2, 2, RWKV_HEAD, RWKV_HEAD).astype(F32)
    eye = jnp.eye(2, dtype=F32)
    out = s[:, :, :, :, None, :] * eye[None, None, :, None, :, None]
    return out.reshape(b, RWKV_HEADS // 2, LANES, LANES)


def _unpair_states(s):
    b = s.shape[0]
    s = s.reshape(b, RWKV_HEADS // 2, 2, RWKV_HEAD, 2, RWKV_HEAD)
    return jnp.stack([s[:, :, 0, :, 0, :], s[:, :, 1, :, 1, :]], axis=2).reshape(
        b, RWKV_HEADS, RWKV_HEAD, RWKV_HEAD)


def _round_up(n, m):
    return -(-n // m) * m


def kernel(x_prompt, x_sample, cache_ckv, cache_krope, state_conv, state_lru, state_shift, state_wkv,
           meta_tokens, ev_w_in, ev_g_q, ev_w_uq, ev_g_kv, ev_w_uk, ev_w_uv, ev_conv_w, ev_conv_b,
           ev_w_rg, ev_b_rg, ev_w_ig, ev_b_ig, ev_lru_lambda, ev_w_out, od_mu, od_w_r, od_w_k, od_w_v,
           od_w0, od_w1, od_w2, od_a0, od_a1, od_a2, od_g1, od_g2, od_k_k, od_k_a, od_r_k, od_ln_g,
           od_ln_b, od_w_o, ln_g, ln_b, router_w, router_b, exp_w_gate, exp_w_up, exp_w_down):
    assert x_prompt.shape[0] == 1 and x_prompt.shape[2] == D_MODEL
    seq = x_prompt.shape[1]
    assert seq % CHUNK == 0
    bs, ls, _ = x_sample.shape
    past = cache_ckv.shape[2]
    ns = bs * ls
    end = ROW0 + seq
    tp = _round_up(end, 512)

    ev = _prep_even(ev_w_in[0], ev_g_q[0], ev_w_uq[0], ev_g_kv[0], ev_w_uk[0], ev_w_uv[0], ev_conv_w[0],
                    ev_conv_b[0], ev_w_rg[0], ev_b_rg[0], ev_w_ig[0], ev_b_ig[0], ev_lru_lambda[0],
                    ev_w_out[0])
    od = _prep_odd(od_mu[0], od_w_r[0], od_w_k[0], od_w_v[0], od_w0[0], od_w1[0], od_w2[0], od_a0[0],
                   od_a1[0], od_a2[0], od_g1[0], od_g2[0], od_k_k[0], od_k_a[0], od_r_k[0], od_ln_g[0],
                   od_ln_b[0], od_w_o[0])
    rw = router_w.astype(F32)
    rb = _row2(router_b)
    wg, wu, wd = (w.reshape((DEPTH * N_EXPERTS,) + w.shape[2:]) for w in (exp_w_gate, exp_w_up, exp_w_down))
    lng = ln_g.astype(F32)[:, :, None, :]
    lnb = ln_b.astype(F32)[:, :, None, :]

    def moe(x, layer):
        sparse = x.shape[0] % (SC_WORKERS * SUBLANES) == 0 and x.shape[0] >= SPARSE_MIN_ROWS
        fn = _moe_sparse if sparse else _moe
        return fn(x, rw, rb, wg, wu, wd, layer * N_EXPERTS, lng[layer, 1], lnb[layer, 1])

    xp = jnp.concatenate([jnp.zeros((PAD_FRONT, D_MODEL), F32), meta_tokens.astype(F32),
                          x_prompt[0].astype(F32), jnp.zeros((tp - end, D_MODEL), F32)], axis=0)
    tabs_p = _rope_tables(jnp.maximum(jnp.arange(tp) - PAD_FRONT, 0))
    evb = _single_pass(ev)
    q_p, ckv_p, kr_p, xr_p, yg_p = _even_proj(xp, evb["w1"], ev["gq"], evb["wuq"], ev["gkv"], tabs_p,
                                              MLA_SCALE * LOG2E)
    ones_col = np.zeros((D_ATT, 1), np.float32)
    ones_col[D_V::HEAD_SLOT] = 1.0
    k_p, vt_p = _kv_proj_t(ckv_p, kr_p, evb["wukv"][:, :D_ATT], evb["place"], evb["wukv"][:, D_ATT:].T,
                           jnp.asarray(ones_col), _round_up(tp, FLASH_TK))
    attn_p = _flash_attention(q_p, k_p, vt_p, FLASH_TQ, FLASH_TK)
    rnn_p, tailx_p, tailh_p = _rglru(
        xr_p[None], yg_p[None], ev["cw"], ev["cb"], evb["wrg"], ev["brg"], evb["wig"], ev["big"], ev["sp"],
        jnp.zeros((1, SUBLANES, D_RNN), F32), jnp.zeros((1, 1, D_RNN), F32), PAD_FRONT, end)
    x1_p = _mix_out(attn_p, rnn_p[0], xp, evb["wa"], evb["wr"], lng[0, 0], lnb[0, 0])
    x2_p = moe(x1_p, 0)

    xs = x_sample.reshape(ns, D_MODEL).astype(F32)
    pos_s = jnp.tile(N_META + past + jnp.arange(ls), bs)
    q_s, ckv_s, kr_s, xr_s, yg_s = _even_proj(xs, ev["w1"], ev["gq"], ev["wuq"], ev["gkv"], _rope_tables(pos_s),
                                              MLA_SCALE)
    _, ckv_m, kr_m, _, _ = _even_proj(meta_tokens.astype(F32), ev["w1"], ev["gq"], ev["wuq"], ev["gkv"],
                                      _rope_tables(jnp.arange(N_META)), MLA_SCALE)
    n_keys = N_META + past + ls
    nk_pad = _round_up(n_keys, LANES)
    meta_ckv = jnp.broadcast_to(ckv_m[None], (bs, N_META, D_C))
    meta_kr = jnp.broadcast_to(kr_m[None], (bs, N_META, LANES))
    cache_kr = jnp.pad(cache_krope[0].astype(F32), ((0, 0), (0, 0), (D_NOPE, LANES - D_NOPE - D_ROPE)))
    all_ckv = jnp.concatenate([meta_ckv, cache_ckv[0].astype(F32), ckv_s.reshape(bs, ls, D_C),
                               jnp.zeros((bs, nk_pad - n_keys, D_C), F32)], axis=1)
    all_kr = jnp.concatenate([meta_kr, cache_kr, kr_s.reshape(bs, ls, LANES),
                              jnp.zeros((bs, nk_pad - n_keys, LANES), F32)], axis=1)
    attn_s = _sample_attention(q_s.reshape(bs, ls, D_ATT), jnp.concatenate([all_ckv, all_kr], axis=2),
                               ev["wabs"], ev["wuv"], n_keys)
    buf0_s = jnp.pad(state_conv[0].astype(F32), ((0, 0), (SUBLANES - (CONV_W - 1), 0), (0, 0)))
    rnn_s, tailx_s, tailh_s = _rglru(
        xr_s.reshape(bs, ls, D_RNN), yg_s.reshape(bs, ls, D_RNN), ev["cw"], ev["cb"], ev["wrg"], ev["brg"],
        ev["wig"], ev["big"], ev["sp"], buf0_s, state_lru[0].astype(F32)[:, None, :], 0, ls)
    x1_s = _mix_out(attn_s.reshape(ns, D_ATT), rnn_s.reshape(ns, D_RNN), xs, ev["wa"], ev["wr"],
                    lng[0, 0], lnb[0, 0])
    x2_s = moe(x1_s, 0)

    r_p, lw_p, kk_in_p, v1_p, kkn_p, a_p, g_p = _rwkv_proj(
        x2_p[None], jnp.zeros((1, 1, D_MODEL), F32), od, PAD_FRONT, end, BF16)
    o_p, s_p = _wkv(r_p, lw_p, kk_in_p, v1_p, kkn_p, a_p,
                    jnp.zeros((1, RWKV_HEADS // 2, LANES, LANES), F32), CHUNK, False)
    x3_p = _rwkv_out(o_p[0], r_p[0], kk_in_p[0], v1_p[0], g_p[0], x2_p, od, lng[1, 0], lnb[1, 0])
    x4_p = moe(x3_p, 1)

    x2_s3 = x2_s.reshape(bs, ls, D_MODEL)
    r_s, lw_s, kk_in_s, v1_s, kkn_s, a_s, g_s = _rwkv_proj(
        x2_s3, state_shift[0].astype(F32)[:, None, :], od, 0, ls, F32)
    to_chunk = lambda z: jnp.pad(z, ((0, 0), (0, _round_up(ls, CHUNK) - ls), (0, 0)))
    o_s, s_s = _wkv(*(to_chunk(z) for z in (r_s, lw_s, kk_in_s, v1_s, kkn_s, a_s)),
                    _pair_states(state_wkv[0]), CHUNK, True)
    o_s = o_s[:, :ls]
    flat = lambda z: z.reshape(ns, D_MODEL)
    x3_s = _rwkv_out(flat(o_s), flat(r_s), flat(kk_in_s), flat(v1_s), flat(g_s), x2_s, od,
                     lng[1, 0], lnb[1, 0])
    x4_s = moe(x3_s, 1)

    dt = x_prompt.dtype
    nb = CONV_W - 1
    return (
        x4_p[ROW0:end][None].astype(dt),
        x4_s.reshape(bs, ls, D_MODEL).astype(dt),
        ckv_p[PAD_FRONT:end][None, None].astype(dt),
        kr_p[PAD_FRONT:end, D_NOPE:D_NOPE + D_ROPE][None, None].astype(dt),
        tailx_p[:, SUBLANES - nb:][None].astype(dt),
        tailh_p[:, SUBLANES - 1][None].astype(dt),
        x2_p[end - 1][None, None].astype(dt),
        _unpair_states(s_p)[None].astype(dt),
        ckv_s.reshape(bs, ls, D_C)[None].astype(dt),
        kr_s.reshape(bs, ls, LANES)[:, :, D_NOPE:D_NOPE + D_ROPE][None].astype(dt),
        tailx_s[:, SUBLANES - nb:][None].astype(dt),
        tailh_s[:, SUBLANES - 1][None].astype(dt),
        x2_s3[:, ls - 1][None].astype(dt),
        _unpair_states(s_s)[None].astype(dt),
    )
```

```python
import functools

import numpy as np
import jax
import jax.numpy as jnp
from jax import lax
from jax.experimental import pallas as pl
from jax.experimental.pallas import tpu as pltpu
from jax.experimental.pallas import tpu_sc as plsc

F32 = jnp.float32
BF16 = jnp.bfloat16

D_MODEL = 1024
N_META = 16
CHUNK = 64
CHUNK_SHIFT = 6
LN_EPS = 1e-5
RMS_EPS = 1e-6
DEPTH = 2
ALPHA = (2 * DEPTH) ** 0.25
MLA_HEADS = 8
D_NOPE = 64
D_ROPE = 32
D_V = 64
D_C = 256
D_CQ = 384
ROPE_BASE = 10000.0
MLA_SCALE = (D_NOPE + D_ROPE) ** -0.5
D_RNN = 512
LRU_BLOCKS = 8
LRU_BLOCK_W = D_RNN // LRU_BLOCKS
CONV_W = 4
LRU_C = 8.0
RWKV_HEAD = 64
RWKV_HEADS = D_MODEL // RWKV_HEAD
DECAY_SCALE = float(np.exp(-0.5))
GN_EPS = 64e-5
N_EXPERTS = 16
N_GROUPS = 4
EXPERTS_PER_GROUP = N_EXPERTS // N_GROUPS
D_EXPERT = 512

LANES = 128
SUBLANES = 8
HEAD_SLOT = LANES
D_ATT = MLA_HEADS * HEAD_SLOT
PAD_FRONT = CHUNK - N_META
ROW0 = PAD_FRONT + N_META
NEG = -1e30
LOG2E = 1.4426950408889634
SC_CORES = 2
SC_SUBCORES = 16
SC_WORKERS = SC_CORES * SC_SUBCORES
MOE_TILE = 512
FLASH_TQ = 512
FLASH_TK = 1024
SPARSE_MIN_ROWS = 1024
VMEM_LIMIT = 56 * 1024 * 1024

C_CQ = 0
C_CKV = D_CQ
C_XR = C_CKV + D_C
C_YG = C_XR + D_RNN
C_KR = C_YG + D_RNN
N_COL = C_KR + LANES


def _cparams(sem):
    return pltpu.CompilerParams(dimension_semantics=sem, vmem_limit_bytes=VMEM_LIMIT)


def _row_tile(n, cap):
    for t in (1024, 512, 256, 128, 64, 32, 16, 8):
        if t <= cap and n % t == 0:
            return t
    return n


def _full(shape):
    zeros = (0,) * len(shape)
    return pl.BlockSpec(shape, lambda *_: zeros)


def _ln(x, g, b):
    mu = jnp.mean(x, axis=-1, keepdims=True)
    xc = x - mu
    var = jnp.mean(xc * xc, axis=-1, keepdims=True)
    return xc * lax.rsqrt(var + LN_EPS) * g + b


def _bdot(a, b):
    return jnp.dot(a.astype(BF16), b.astype(BF16), preferred_element_type=F32)


def _split2(x):
    hi = x.astype(BF16)
    return hi, (x - hi.astype(F32)).astype(BF16)


_NN = ((1,), (0,))
_NT = ((1,), (1,))
_TN = ((0,), (0,))


def _mm(a, b, dims, exact):
    dn = (dims, ((), ()))
    if not exact:
        return lax.dot_general(a.astype(BF16), b.astype(BF16), dn, preferred_element_type=F32)
    ah, al = _split2(a)
    bh, bl = _split2(b)
    return (lax.dot_general(ah, bh, dn, preferred_element_type=F32)
            + lax.dot_general(al, bh, dn, preferred_element_type=F32)
            + lax.dot_general(ah, bl, dn, preferred_element_type=F32))


def _wdot(a, w):
    return _mm(a, w, _NN, exact=(w.dtype == F32))


def _act_dtype(w):
    return F32 if w.dtype == F32 else BF16


def _rope_slot(x, c, sa, sb):
    return x * c + pltpu.roll(x, LANES - D_ROPE // 2, 1) * sa + pltpu.roll(x, D_ROPE // 2, 1) * sb


def _even_proj_kernel(x_ref, w1_ref, gq_ref, wuq_ref, gkv_ref, c_ref, sa_ref, sb_ref,
                      q_ref, ckv_ref, kr_ref, xr_ref, yg_ref, *, q_scale):
    u = _wdot(x_ref[...], w1_ref[...])
    cq = u[:, C_CQ:C_CQ + D_CQ]
    cq = cq * lax.rsqrt(jnp.mean(cq * cq, axis=-1, keepdims=True) + RMS_EPS) * gq_ref[...]
    q = _wdot(cq, wuq_ref[...])
    c, sa, sb = c_ref[...], sa_ref[...], sb_ref[...]
    for h in range(MLA_HEADS):
        sl = slice(h * HEAD_SLOT, (h + 1) * HEAD_SLOT)
        q_ref[:, sl] = (_rope_slot(q[:, sl], c, sa, sb) * q_scale).astype(q_ref.dtype)
    ckv = u[:, C_CKV:C_CKV + D_C]
    ckv_ref[...] = ckv * lax.rsqrt(jnp.mean(ckv * ckv, axis=-1, keepdims=True) + RMS_EPS) * gkv_ref[...]
    kr_ref[...] = _rope_slot(u[:, C_KR:C_KR + LANES], c, sa, sb)
    xr_ref[...] = u[:, C_XR:C_XR + D_RNN]
    yg_ref[...] = u[:, C_YG:C_YG + D_RNN]


def _even_proj(x, w1, gq, wuq, gkv, tabs, q_scale):
    n = x.shape[0]
    tm = _row_tile(n, 512)
    row = lambda w: pl.BlockSpec((tm, w), lambda i: (i, 0))
    c, sa, sb = tabs
    return pl.pallas_call(
        functools.partial(_even_proj_kernel, q_scale=q_scale),
        grid=(n // tm,),
        in_specs=[row(D_MODEL), _full(w1.shape), _full(gq.shape), _full(wuq.shape), _full(gkv.shape),
                  row(LANES), row(LANES), row(LANES)],
        out_specs=[row(D_ATT), row(D_C), row(LANES), row(D_RNN), row(D_RNN)],
        out_shape=[jax.ShapeDtypeStruct((n, D_ATT), _act_dtype(w1)), jax.ShapeDtypeStruct((n, D_C), F32),
                   jax.ShapeDtypeStruct((n, LANES), F32), jax.ShapeDtypeStruct((n, D_RNN), F32),
                   jax.ShapeDtypeStruct((n, D_RNN), F32)],
        compiler_params=_cparams(("arbitrary",)),
        name="even_proj",
    )(x, w1, gq, wuq, gkv, c, sa, sb)


def _kv_proj_t_kernel(ckv_ref, kr_ref, wuk_ref, p_ref, wuvt_ref, ones_ref, k_ref, vt_ref):
    ckv = ckv_ref[...].astype(BF16)
    k = jnp.dot(ckv, wuk_ref[...], preferred_element_type=F32) + _bdot(kr_ref[...], p_ref[...])
    k_ref[...] = k.astype(BF16)
    vt = lax.dot_general(wuvt_ref[...], ckv, (_NT, ((), ())), preferred_element_type=F32)
    vt_ref[...] = (vt + ones_ref[...]).astype(BF16)


def _kv_proj_t(ckv, kr, wuk, place, wuvt, ones_col, n_out):
    n = ckv.shape[0]
    tm = _row_tile(n, 512)
    assert n_out % tm == 0
    last = n // tm - 1
    row_in = lambda w: pl.BlockSpec((tm, w), lambda i: (jnp.minimum(i, last), 0))
    return pl.pallas_call(
        _kv_proj_t_kernel,
        grid=(n_out // tm,),
        in_specs=[row_in(D_C), row_in(LANES), _full(wuk.shape), _full(place.shape), _full(wuvt.shape),
                  _full(ones_col.shape)],
        out_specs=[pl.BlockSpec((tm, D_ATT), lambda i: (i, 0)), pl.BlockSpec((D_ATT, tm), lambda i: (0, i))],
        out_shape=[jax.ShapeDtypeStruct((n_out, D_ATT), BF16), jax.ShapeDtypeStruct((D_ATT, n_out), BF16)],
        compiler_params=_cparams(("arbitrary",)),
        name="kv_proj_t",
    )(ckv, kr, wuk, place, wuvt, ones_col)


def _flash_kernel(qi_ref, kj_ref, last_ref, q_ref, k_ref, vt_ref, o_ref, m_scr, acc_scr, *, tq, tk):
    step = pl.program_id(0)
    i = qi_ref[step]
    j = kj_ref[step]

    @pl.when(j == 0)
    def _():
        m_scr[...] = jnp.full(m_scr.shape, NEG, F32)
        acc_scr[...] = jnp.zeros(acc_scr.shape, F32)

    heads = range(MLA_HEADS)
    slots = [slice(h * HEAD_SLOT, (h + 1) * HEAD_SLOT) for h in heads]

    def accumulate(masked):
        st = [lax.dot_general(k_ref[:, sl], q_ref[:, sl], (_NT, ((), ())), preferred_element_type=F32)
              for sl in slots]
        if masked:
            krow = j * tk + lax.broadcasted_iota(jnp.int32, (tk, tq), 0)
            qrow = i * tq + lax.broadcasted_iota(jnp.int32, (tk, tq), 1)
            keep = ((((qrow - ROW0) >> CHUNK_SHIFT) >= ((krow - ROW0) >> CHUNK_SHIFT))
                    & (krow >= PAD_FRONT))
            st = [jnp.where(keep, x, NEG) for x in st]
        m_prev = [m_scr[h:h + 1, :] for h in heads]
        m_new = [jnp.maximum(mp, jnp.max(x, axis=0, keepdims=True)) for mp, x in zip(m_prev, st)]
        alpha = [jnp.exp2(mp - mn) for mp, mn in zip(m_prev, m_new)]
        pt = [jnp.exp2(x - mn).astype(BF16) for x, mn in zip(st, m_new)]
        pv = [jnp.dot(vt_ref[sl, :], x, preferred_element_type=F32) for x, sl in zip(pt, slots)]
        for h in heads:
            acc_scr[slots[h], :] = alpha[h] * acc_scr[slots[h], :] + pv[h]
            m_scr[h:h + 1, :] = m_new[h]

    last = last_ref[step] == 1
    edge = last | (j == 0)

    @pl.when(edge)
    def _():
        accumulate(True)

    @pl.when(jnp.logical_not(edge))
    def _():
        accumulate(False)

    @pl.when(last)
    def _():
        for sl in slots:
            acc = acc_scr[sl, :]
            o_ref[:, sl] = (acc / acc[D_V:D_V + 1, :]).T.astype(BF16)


def _flash_attention(q, k, vt, tq, tk):
    n = q.shape[0]
    nq = n // tq
    n_kblocks = [-(-(i + 1) * tq // tk) for i in range(nq)]
    assert k.shape[0] >= n_kblocks[-1] * tk
    qi = np.concatenate([np.full(c, i, np.int32) for i, c in enumerate(n_kblocks)])
    kj = np.concatenate([np.arange(c, dtype=np.int32) for c in n_kblocks])
    last = np.concatenate([np.arange(c, dtype=np.int32) == c - 1 for c in n_kblocks]).astype(np.int32)
    grid_spec = pltpu.PrefetchScalarGridSpec(
        num_scalar_prefetch=3,
        grid=(len(qi),),
        in_specs=[pl.BlockSpec((tq, D_ATT), lambda s, qi, kj, last: (qi[s], 0)),
                  pl.BlockSpec((tk, D_ATT), lambda s, qi, kj, last: (kj[s], 0)),
                  pl.BlockSpec((D_ATT, tk), lambda s, qi, kj, last: (0, kj[s]))],
        out_specs=pl.BlockSpec((tq, D_ATT), lambda s, qi, kj, last: (qi[s], 0)),
        scratch_shapes=[pltpu.VMEM((MLA_HEADS, tq), F32), pltpu.VMEM((D_ATT, tq), F32)],
    )
    return pl.pallas_call(
        functools.partial(_flash_kernel, tq=tq, tk=tk),
        grid_spec=grid_spec,
        out_shape=jax.ShapeDtypeStruct((n, D_ATT), BF16),
        compiler_params=_cparams(("arbitrary",)),
        name="flash_attention",
    )(jnp.asarray(qi), jnp.asarray(kj), jnp.asarray(last), q, k, vt)


def _sample_attn_kernel(q_ref, kc_ref, wabs_ref, wuv_ref, o_ref, *, n_keys):
    kc = kc_ref[0]
    l = q_ref.shape[1]
    slots = [slice(h * HEAD_SLOT, (h + 1) * HEAD_SLOT) for h in range(MLA_HEADS)]
    qa = jnp.concatenate([_mm(q_ref[0, :, sl], wabs_ref[h], _NN, True) for h, sl in enumerate(slots)],
                         axis=0)
    keep = lax.broadcasted_iota(jnp.int32, (MLA_HEADS * l, kc.shape[0]), 1) < n_keys
    s = jnp.where(keep, _mm(qa, kc, _NT, True), NEG)
    p = jnp.exp(s - jnp.max(s, axis=-1, keepdims=True))
    p = p / jnp.sum(p, axis=-1, keepdims=True)
    pc = _mm(p, kc[:, :D_C], _NN, True)
    for h, sl in enumerate(slots):
        o_ref[0, :, sl] = _mm(pc[h * l:(h + 1) * l], wuv_ref[h], _NN, True)


def _sample_attention(q, kc, wabs, wuv, n_keys):
    b, l, _ = q.shape
    nk, dk = kc.shape[1:]
    return pl.pallas_call(
        functools.partial(_sample_attn_kernel, n_keys=n_keys),
        grid=(b,),
        in_specs=[pl.BlockSpec((1, l, D_ATT), lambda i: (i, 0, 0)),
                  pl.BlockSpec((1, nk, dk), lambda i: (i, 0, 0)),
                  _full(wabs.shape), _full(wuv.shape)],
        out_specs=pl.BlockSpec((1, l, D_ATT), lambda i: (i, 0, 0)),
        out_shape=jax.ShapeDtypeStruct((b, l, D_ATT), F32),
        compiler_params=_cparams(("arbitrary",)),
        name="sample_attention",
    )(q, kc, wabs, wuv)


def _expm1(x):
    series = x * (1.0 + x * (0.5 + x * (1.0 / 6.0 + x * (1.0 / 24.0 + x * (1.0 / 120.0)))))
    return jnp.where(jnp.abs(x) < 0.05, series, jnp.exp(x) - 1.0)


def _gelu_tanh(x):
    return 0.5 * x * (1.0 + jnp.tanh(0.7978845608028654 * (x + 0.044715 * x * x * x)))


def _rglru_kernel(xr_ref, yg_ref, cw_ref, cb_ref, wrg_ref, brg_ref, wig_ref, big_ref, sp_ref,
                  buf0_ref, h0_ref, rnn_ref, tailx_ref, tailh_ref, prev_scr, h_scr,
                  *, tm, start, end):
    t = pl.program_id(1)

    @pl.when(t == 0)
    def _():
        prev_scr[...] = buf0_ref[0]
        h_scr[...] = jnp.broadcast_to(h0_ref[0], h_scr.shape)

    x = xr_ref[0]
    ext = jnp.concatenate([prev_scr[...], x], axis=0)
    cw = cw_ref[...]
    xc = cb_ref[...] + cw[CONV_W - 1:CONV_W] * x
    for d in range(1, CONV_W):
        xc = xc + cw[CONV_W - 1 - d:CONV_W - d] * pltpu.roll(ext, d, 0)[SUBLANES:]
    prev_scr[...] = x[tm - SUBLANES:]

    r = jax.nn.sigmoid(_wdot(xc, wrg_ref[...]) + brg_ref[...])
    ig = jax.nn.sigmoid(_wdot(xc, wig_ref[...]) + big_ref[...])
    log_a = -LRU_C * r * sp_ref[...]
    a = jnp.exp(log_a)
    b = jnp.sqrt(-_expm1(2.0 * log_a)) * (ig * xc)
    row = lax.broadcasted_iota(jnp.int32, (tm, D_RNN), 0)
    if start > 0:
        live = (t * tm + row) >= start
        a = jnp.where(live, a, 1.0)
        b = jnp.where(live, b, 0.0)
    d = 1
    while d < tm:
        b = a * jnp.where(row >= d, pltpu.roll(b, d, 0), 0.0) + b
        a = a * jnp.where(row >= d, pltpu.roll(a, d, 0), 1.0)
        d *= 2
    h = a * h_scr[0:1] + b
    h_scr[...] = jnp.broadcast_to(h[tm - 1:tm], h_scr.shape)
    rnn_ref[0] = (h * _gelu_tanh(yg_ref[0])).astype(rnn_ref.dtype)

    t_end = (end - 1) // tm
    el = end - t_end * tm

    @pl.when(t == t_end)
    def _():
        tailx_ref[0] = ext[el:el + SUBLANES]
        tailh_ref[0] = h[el - SUBLANES:el]


def _rglru(xr, yg, cw, cb, wrg, brg, wig, big, sp, buf0, h0, start, end):
    b, l, _ = xr.shape
    tm = _row_tile(l, 512)
    seq = pl.BlockSpec((1, tm, D_RNN), lambda i, t: (i, t, 0))
    per_b = lambda r: pl.BlockSpec((1, r, D_RNN), lambda i, t: (i, 0, 0))
    return pl.pallas_call(
        functools.partial(_rglru_kernel, tm=tm, start=start, end=end),
        grid=(b, l // tm),
        in_specs=[seq, seq, _full(cw.shape), _full(cb.shape), _full(wrg.shape), _full(brg.shape),
                  _full(wig.shape), _full(big.shape), _full(sp.shape), per_b(SUBLANES), per_b(1)],
        out_specs=[seq, per_b(SUBLANES), per_b(SUBLANES)],
        out_shape=[jax.ShapeDtypeStruct((b, l, D_RNN), _act_dtype(wrg)),
                   jax.ShapeDtypeStruct((b, SUBLANES, D_RNN), F32),
                   jax.ShapeDtypeStruct((b, SUBLANES, D_RNN), F32)],
        scratch_shapes=[pltpu.VMEM((SUBLANES, D_RNN), F32), pltpu.VMEM((SUBLANES, D_RNN), F32)],
        compiler_params=_cparams(("arbitrary", "arbitrary")),
        name="rglru",
    )(xr, yg, cw, cb, wrg, brg, wig, big, sp, buf0, h0)


def _mix_out_kernel(attn_ref, rnn_ref, x_ref, wa_ref, wr_ref, g_ref, b_ref, o_ref):
    mix = _wdot(attn_ref[...], wa_ref[...]) + _wdot(rnn_ref[...], wr_ref[...])
    o_ref[...] = _ln(ALPHA * x_ref[...] + mix, g_ref[...], b_ref[...])


def _mix_out(attn, rnn, x, wa, wr, g, b):
    n = x.shape[0]
    tm = _row_tile(n, 512)
    row = lambda w: pl.BlockSpec((tm, w), lambda i: (i, 0))
    return pl.pallas_call(
        _mix_out_kernel,
        grid=(n // tm,),
        in_specs=[row(D_ATT), row(D_RNN), row(D_MODEL), _full(wa.shape), _full(wr.shape),
                  _full(g.shape), _full(b.shape)],
        out_specs=row(D_MODEL),
        out_shape=jax.ShapeDtypeStruct((n, D_MODEL), F32),
        compiler_params=_cparams(("arbitrary",)),
        name="mix_out",
    )(attn, rnn, x, wa, wr, g, b)


def _first_argmax(vals, lane):
    m = jnp.max(vals, axis=-1, keepdims=True)
    idx = jnp.min(jnp.where(vals == m, lane, N_EXPERTS), axis=-1, keepdims=True)
    return m, idx


def _router_top2(x, rw, rb):
    logits = _mm(x, rw, _NN, True)
    s = jax.nn.sigmoid(logits)
    sel = s + rb
    lane = lax.broadcasted_iota(jnp.int32, sel.shape, 1)
    grp = lane >> 2
    best = None
    g_best = None
    for g in range(N_GROUPS):
        vals = jnp.where(grp == g, sel, NEG)
        m1, i1 = _first_argmax(vals, lane)
        m2, _ = _first_argmax(jnp.where(lane == i1, NEG, vals), lane)
        score = m1 + m2
        if g == 0:
            best, g_best = score, jnp.zeros_like(i1)
        else:
            upd = score > best
            g_best = jnp.where(upd, g, g_best)
            best = jnp.where(upd, score, best)
    vals = jnp.where(grp == g_best, sel, NEG)
    _, i1 = _first_argmax(vals, lane)
    _, i2 = _first_argmax(jnp.where(lane == i1, NEG, vals), lane)
    w1 = jnp.sum(jnp.where(lane == i1, s, 0.0), axis=-1, keepdims=True)
    w2 = jnp.sum(jnp.where(lane == i2, s, 0.0), axis=-1, keepdims=True)
    den = w1 + w2
    return lane, i1, i2, w1 / den, w2 / den


def _router_gate(x, rw, rb):
    lane, i1, i2, g1, g2 = _router_top2(x, rw, rb)
    return jnp.where(lane == i1, g1, 0.0) + jnp.where(lane == i2, g2, 0.0)


def _moe_kernel(x_ref, rw_ref, rb_ref, wg_ref, wu_ref, wd_ref, g_ref, b_ref, o_ref,
                gate_scr, xb_scr, acc_scr):
    e = pl.program_id(1)

    @pl.when(e == 0)
    def _():
        x = x_ref[...]
        gate = _router_gate(x, rw_ref[...], rb_ref[...])
        for k in range(N_EXPERTS):
            gate_scr[k] = jnp.broadcast_to(gate[:, k:k + 1], gate_scr.shape[1:])
        xb_scr[...] = x.astype(BF16)
        acc_scr[...] = jnp.zeros(acc_scr.shape, F32)

    xb = xb_scr[...]
    hg = jnp.dot(xb, wg_ref[0].astype(BF16), preferred_element_type=F32)
    hu = jnp.dot(xb, wu_ref[0].astype(BF16), preferred_element_type=F32)
    gate_e = gate_scr[e]
    h = jax.nn.silu(hg) * hu * jnp.concatenate([gate_e] * (D_EXPERT // LANES), axis=1)
    acc_scr[...] += jnp.dot(h.astype(BF16), wd_ref[0].astype(BF16), preferred_element_type=F32)

    @pl.when(e == N_EXPERTS - 1)
    def _():
        o_ref[...] = _ln(ALPHA * x_ref[...] + acc_scr[...], g_ref[...], b_ref[...])


def _moe(x, rw, rb, wg, wu, wd, e0, g, b):
    n = x.shape[0]
    tm = _row_tile(n, 512)
    row = pl.BlockSpec((tm, D_MODEL), lambda i, e: (i, 0))
    return pl.pallas_call(
        _moe_kernel,
        grid=(n // tm, N_EXPERTS),
        in_specs=[row, _full(rw.shape), _full(rb.shape),
                  pl.BlockSpec((1, D_MODEL, D_EXPERT), lambda i, e: (e0 + e, 0, 0)),
                  pl.BlockSpec((1, D_MODEL, D_EXPERT), lambda i, e: (e0 + e, 0, 0)),
                  pl.BlockSpec((1, D_EXPERT, D_MODEL), lambda i, e: (e0 + e, 0, 0)),
                  _full(g.shape), _full(b.shape)],
        out_specs=row,
        out_shape=jax.ShapeDtypeStruct((n, D_MODEL), F32),
        scratch_shapes=[pltpu.VMEM((N_EXPERTS, tm, LANES), F32), pltpu.VMEM((tm, D_MODEL), BF16),
                        pltpu.VMEM((tm, D_MODEL), F32)],
        compiler_params=_cparams(("arbitrary", "arbitrary")),
        name="moe",
    )(x, rw, rb, wg, wu, wd, g, b)


M_I1, M_I2, M_R1, M_R2, M_G1, M_G2, M_COLS = 0, 1, 2, 3, 4, 5, 8


def _first_argmax_rows(vals, row):
    m = jnp.max(vals, axis=0, keepdims=True)
    idx = jnp.min(jnp.where(vals == m, row, N_EXPERTS), axis=0, keepdims=True)
    return m, idx


HI16 = -65536


def _pack_bf16_pairs(x):
    w = x.shape[1] // 2
    hi = lax.bitcast_convert_type(x[:, :w].astype(BF16).astype(F32), jnp.int32)
    lo = lax.bitcast_convert_type(x[:, w:].astype(BF16).astype(F32), jnp.int32)
    return (hi & HI16) | lax.shift_right_logical(lo, 16)


def _unpack_bf16_pairs(p):
    hi = lax.bitcast_convert_type(p & HI16, F32)
    lo = lax.bitcast_convert_type(lax.shift_left(p, 16), F32)
    return jnp.concatenate([hi, lo], axis=1)


def _route_kernel(x_ref, rwt_ref, rbc_ref, meta_ref, cnt_ref, xpk_ref, carry_scr, *, tm):
    @pl.when(pl.program_id(0) == 0)
    def _():
        carry_scr[...] = jnp.zeros(carry_scr.shape, F32)

    logits = _mm(rwt_ref[...], x_ref[...], _NT, True)
    s = jax.nn.sigmoid(logits)
    sel = s + rbc_ref[...]
    row = lax.broadcasted_iota(jnp.int32, sel.shape, 0)
    grp = row >> 2
    best = None
    g_best = None
    for g in range(N_GROUPS):
        vals = jnp.where(grp == g, sel, NEG)
        m1, i1 = _first_argmax_rows(vals, row)
        m2, _ = _first_argmax_rows(jnp.where(row == i1, NEG, vals), row)
        score = m1 + m2
        if g == 0:
            best, g_best = score, jnp.zeros_like(i1)
        else:
            upd = score > best
            g_best = jnp.where(upd, g, g_best)
            best = jnp.where(upd, score, best)
    vals = jnp.where(grp == g_best, sel, NEG)
    _, i1 = _first_argmax_rows(vals, row)
    _, i2 = _first_argmax_rows(jnp.where(row == i1, NEG, vals), row)
    w1 = jnp.sum(jnp.where(row == i1, s, 0.0), axis=0, keepdims=True)
    w2 = jnp.sum(jnp.where(row == i2, s, 0.0), axis=0, keepdims=True)
    den = w1 + w2

    chosen = jnp.where((row == i1) | (row == i2), 1.0, 0.0)
    earlier = (lax.broadcasted_iota(jnp.int32, (tm, tm), 0)
               < lax.broadcasted_iota(jnp.int32, (tm, tm), 1)).astype(BF16)
    seen = jnp.dot(chosen.astype(BF16), earlier, preferred_element_type=F32) + carry_scr[:, 0:1]
    r1 = jnp.sum(jnp.where(row == i1, seen, 0.0), axis=0, keepdims=True)
    r2 = jnp.sum(jnp.where(row == i2, seen, 0.0), axis=0, keepdims=True)
    carry_scr[...] = carry_scr[...] + jnp.sum(chosen, axis=1, keepdims=True)
    mrow = lax.broadcasted_iota(jnp.int32, (M_COLS, tm), 0)
    meta = jnp.zeros((M_COLS, tm), F32)
    for c, val in ((M_I1, i1.astype(F32)), (M_I2, i2.astype(F32)), (M_R1, r1), (M_R2, r2),
                   (M_G1, w1 / den), (M_G2, w2 / den)):
        meta = jnp.where(mrow == c, val, meta)
    meta_ref[...] = meta
    cnt_ref[...] = carry_scr[...]
    xpk_ref[...] = _pack_bf16_pairs(x_ref[...])


def _route(x, rwt, rbc):
    n = x.shape[0]
    tm = _row_tile(n, 512)
    return pl.pallas_call(
        functools.partial(_route_kernel, tm=tm),
        grid=(n // tm,),
        in_specs=[pl.BlockSpec((tm, D_MODEL), lambda i: (i, 0)), _full(rwt.shape), _full(rbc.shape)],
        out_specs=[pl.BlockSpec((M_COLS, tm), lambda i: (0, i)), _full((N_EXPERTS, LANES)),
                   pl.BlockSpec((tm, D_MODEL // 2), lambda i: (i, 0))],
        out_shape=[jax.ShapeDtypeStruct((M_COLS, n), F32), jax.ShapeDtypeStruct((N_EXPERTS, LANES), F32),
                   jax.ShapeDtypeStruct((n, D_MODEL // 2), jnp.int32)],
        scratch_shapes=[pltpu.VMEM((N_EXPERTS, LANES), F32)],
        compiler_params=_cparams(("arbitrary",)),
        name="moe_route",
    )(x, rwt, rbc)


def _sc_chunk(per_worker):
    for c in (64, 48, 32, 16, 8):
        if per_worker % c == 0:
            return c
    raise ValueError(per_worker)


def _sc_mesh():
    return plsc.VectorSubcoreMesh(core_axis_name="c", subcore_axis_name="s")


def _sc_scatter2(x, idx1, idx2, n_out):
    n, d = x.shape
    per_w = n // SC_WORKERS
    assert per_w * SC_WORKERS == n
    chunk = _sc_chunk(per_w)

    @functools.partial(
        pl.kernel, mesh=_sc_mesh(), out_type=jax.ShapeDtypeStruct((n_out, d), x.dtype),
        scratch_types=[pltpu.VMEM((chunk,), jnp.int32), pltpu.VMEM((chunk,), jnp.int32),
                       pltpu.VMEM((chunk, d), x.dtype), pltpu.SemaphoreType.DMA, pltpu.SemaphoreType.DMA])
    def scatter(x_hbm, i1_hbm, i2_hbm, out_hbm, i1_v, i2_v, rows_v, sem1, sem2):
        base = (lax.axis_index("s") * SC_CORES + lax.axis_index("c")) * per_w

        @pl.loop(0, per_w // chunk)
        def _(c):
            off = pl.multiple_of(base + c * chunk, SUBLANES)
            pltpu.sync_copy(i1_hbm.at[pl.ds(off, chunk)], i1_v)
            pltpu.sync_copy(i2_hbm.at[pl.ds(off, chunk)], i2_v)
            pltpu.sync_copy(x_hbm.at[pl.ds(off, chunk)], rows_v)
            first = pltpu.async_copy(rows_v, out_hbm.at[i1_v], sem1)
            second = pltpu.async_copy(rows_v, out_hbm.at[i2_v], sem2)
            first.wait()
            second.wait()

    return scatter(x, idx1, idx2)


def _sc_gather(y, idx):
    n = idx.shape[0]
    d = y.shape[1]
    per_w = n // SC_WORKERS
    assert per_w * SC_WORKERS == n
    chunk = _sc_chunk(per_w)

    @functools.partial(
        pl.kernel, mesh=_sc_mesh(), out_type=jax.ShapeDtypeStruct((n, d), y.dtype),
        scratch_types=[pltpu.VMEM((chunk,), jnp.int32), pltpu.VMEM((chunk, d), y.dtype),
                       pltpu.SemaphoreType.DMA])
    def gather(y_hbm, idx_hbm, out_hbm, idx_v, rows_v, sem):
        base = (lax.axis_index("s") * SC_CORES + lax.axis_index("c")) * per_w

        @pl.loop(0, per_w // chunk)
        def _(c):
            off = pl.multiple_of(base + c * chunk, SUBLANES)
            pltpu.sync_copy(idx_hbm.at[pl.ds(off, chunk)], idx_v)
            pltpu.async_copy(y_hbm.at[idx_v], rows_v, sem).wait()
            pltpu.sync_copy(rows_v, out_hbm.at[pl.ds(off, chunk)])

    return gather(y, idx)


def _experts_kernel(te_ref, used_ref, x_ref, wg_ref, wu_ref, wd_ref, o_ref):
    @pl.when(pl.program_id(0) < used_ref[0])
    def _():
        xb = _unpack_bf16_pairs(x_ref[...]).astype(BF16)
        hg = jnp.dot(xb, wg_ref[0].astype(BF16), preferred_element_type=F32)
        hu = jnp.dot(xb, wu_ref[0].astype(BF16), preferred_element_type=F32)
        h = jax.nn.silu(hg) * hu
        y = jnp.dot(h.astype(BF16), wd_ref[0].astype(BF16), preferred_element_type=F32)
        o_ref[...] = _pack_bf16_pairs(y)


def _experts(xg, tile_expert, n_used, wg, wu, wd):
    n_tiles = xg.shape[0] // MOE_TILE
    row = pl.BlockSpec((MOE_TILE, D_MODEL // 2), lambda i, te, used: (i, 0))
    grid_spec = pltpu.PrefetchScalarGridSpec(
        num_scalar_prefetch=2,
        grid=(n_tiles,),
        in_specs=[row,
                  pl.BlockSpec((1, D_MODEL, D_EXPERT), lambda i, te, used: (te[i], 0, 0)),
                  pl.BlockSpec((1, D_MODEL, D_EXPERT), lambda i, te, used: (te[i], 0, 0)),
                  pl.BlockSpec((1, D_EXPERT, D_MODEL), lambda i, te, used: (te[i], 0, 0))],
        out_specs=row,
    )
    return pl.pallas_call(
        _experts_kernel,
        grid_spec=grid_spec,
        out_shape=jax.ShapeDtypeStruct(xg.shape, jnp.int32),
        compiler_params=_cparams(("arbitrary",)),
        name="moe_experts",
    )(tile_expert, n_used, xg, wg, wu, wd)


def _combine_kernel(x_ref, y1_ref, y2_ref, meta_ref, g_ref, b_ref, o_ref):
    meta = meta_ref[...]
    moe = (meta[:, M_G1:M_G1 + 1] * _unpack_bf16_pairs(y1_ref[...])
           + meta[:, M_G2:M_G2 + 1] * _unpack_bf16_pairs(y2_ref[...]))
    o_ref[...] = _ln(ALPHA * x_ref[...] + moe, g_ref[...], b_ref[...])


def _combine(x, y1, y2, meta, g, b):
    n = x.shape[0]
    tm = _row_tile(n, 512)
    row = pl.BlockSpec((tm, D_MODEL), lambda i: (i, 0))
    half = pl.BlockSpec((tm, D_MODEL // 2), lambda i: (i, 0))
    return pl.pallas_call(
        _combine_kernel,
        grid=(n // tm,),
        in_specs=[row, half, half, pl.BlockSpec((tm, M_COLS), lambda i: (i, 0)), _full(g.shape), _full(b.shape)],
        out_specs=row,
        out_shape=jax.ShapeDtypeStruct((n, D_MODEL), F32),
        compiler_params=_cparams(("arbitrary",)),
        name="moe_combine",
    )(x, y1, y2, meta, g, b)


def _moe_sparse(x, rw, rb, wg, wu, wd, e0, g, b):
    n = x.shape[0]
    meta_t, counts, x_packed = _route(x, rw.T, rb.reshape(N_EXPERTS, 1))
    cnt = counts[:, 0].astype(jnp.int32)
    padded = (cnt + MOE_TILE - 1) // MOE_TILE * MOE_TILE
    seg_end = jnp.cumsum(padded)
    seg_start = seg_end - padded
    experts = jnp.arange(N_EXPERTS, dtype=jnp.int32)[:, None]
    start_of = lambda e: jnp.sum(jnp.where(experts == e[None, :], seg_start[:, None], 0), axis=0)
    e1, e2 = meta_t[M_I1].astype(jnp.int32), meta_t[M_I2].astype(jnp.int32)
    pos1 = start_of(e1) + meta_t[M_R1].astype(jnp.int32)
    pos2 = start_of(e2) + meta_t[M_R2].astype(jnp.int32)
    meta = meta_t.T
    n_tiles = -(-2 * n // MOE_TILE) + N_EXPERTS
    tile_start = jnp.arange(n_tiles, dtype=jnp.int32) * MOE_TILE
    tile_expert = e0 + jnp.minimum(jnp.sum(tile_start[:, None] >= seg_end[None, :], axis=1),
                                   N_EXPERTS - 1).astype(jnp.int32)
    n_used = (seg_end[-1:] // MOE_TILE).astype(jnp.int32)
    xg = _sc_scatter2(x_packed, pos1, pos2, n_tiles * MOE_TILE)
    yg = _experts(xg, tile_expert, n_used, wg, wu, wd)
    return _combine(x, _sc_gather(yg, pos1), _sc_gather(yg, pos2), meta, g, b)


def _head_sum(z, ones):
    hi, lo = _split2(z)
    parts = []
    for g in range(D_MODEL // LANES):
        sl = slice(g * LANES, (g + 1) * LANES)
        parts.append(jnp.dot(hi[:, sl], ones, preferred_element_type=F32)
                     + jnp.dot(lo[:, sl], ones, preferred_element_type=F32))
    return jnp.concatenate(parts, axis=1)


def _rwkv_proj_kernel(x_ref, sh0_ref, mu_ref, wr_ref, wk_ref, wv_ref, w0_ref, w1_ref, w2_ref,
                      a0_ref, a1_ref, a2_ref, g1_ref, g2_ref, kkw_ref, kaw_ref, ones_ref,
                      r_ref, lw_ref, k_ref, v_ref, kk_ref, a_ref, g_ref, prev_scr,
                      *, tm, start, end):
    t = pl.program_id(1)

    @pl.when(t == 0)
    def _():
        prev_scr[...] = jnp.zeros(prev_scr.shape, F32)

    x = x_ref[0]
    ext = jnp.concatenate([prev_scr[...], x], axis=0)
    x_prev = pltpu.roll(ext, 1, 0)[SUBLANES:]
    grow = t * tm + lax.broadcasted_iota(jnp.int32, (tm, D_MODEL), 0)
    x_prev = jnp.where(grow == start, sh0_ref[0], x_prev)
    prev_scr[...] = x[tm - SUBLANES:]
    xx = x_prev - x
    mu = mu_ref[...]
    xr, xw, xk, xv, xa, xg = (x + xx * mu[n:n + 1] for n in range(6))
    r = _bdot(xr, wr_ref[...])
    k = _bdot(xk, wk_ref[...])
    v = _bdot(xv, wv_ref[...])
    log_w = -DECAY_SCALE * jax.nn.sigmoid(w0_ref[...] + _bdot(jnp.tanh(_bdot(xw, w1_ref[...])), w2_ref[...]))
    a = jax.nn.sigmoid(a0_ref[...] + _bdot(_bdot(xa, a1_ref[...]), a2_ref[...]))
    g = _bdot(jax.nn.sigmoid(_bdot(xg, g1_ref[...])), g2_ref[...])
    kk = k * kkw_ref[...]
    norm = jnp.sqrt(_head_sum(kk * kk, ones_ref[...]))
    kk = kk / jnp.maximum(norm, 1e-12)
    k = k * (1.0 + (a - 1.0) * kaw_ref[...])
    live = (grow >= start) & (grow < end)
    r_ref[0] = r.astype(r_ref.dtype)
    lw_ref[0] = jnp.where(live, log_w, 0.0)
    k_ref[0] = jnp.where(live, k, 0.0).astype(k_ref.dtype)
    v_ref[0] = v.astype(v_ref.dtype)
    kk_ref[0] = jnp.where(live, kk, 0.0).astype(kk_ref.dtype)
    a_ref[0] = a.astype(a_ref.dtype)
    g_ref[0] = g.astype(g_ref.dtype)


def _rwkv_proj(x, sh0, od, start, end, act_dtype):
    b, l, _ = x.shape
    tm = _row_tile(l, 256)
    dtypes = [act_dtype, F32] + [act_dtype] * 5
    seq = pl.BlockSpec((1, tm, D_MODEL), lambda i, t: (i, t, 0))
    ws = [od[n] for n in ("mu", "w_r", "w_k", "w_v", "w0", "w1", "w2", "a0", "a1", "a2", "g1", "g2",
                          "k_k", "k_a", "ones")]
    return pl.pallas_call(
        functools.partial(_rwkv_proj_kernel, tm=tm, start=start, end=end),
        grid=(b, l // tm),
        in_specs=[seq, pl.BlockSpec((1, 1, D_MODEL), lambda i, t: (i, 0, 0))] + [_full(w.shape) for w in ws],
        out_specs=[seq] * 7,
        out_shape=[jax.ShapeDtypeStruct((b, l, D_MODEL), dt) for dt in dtypes],
        scratch_shapes=[pltpu.VMEM((SUBLANES, D_MODEL), F32)],
        compiler_params=_cparams(("arbitrary", "arbitrary")),
        name="rwkv_proj",
    )(x, sh0, *ws)


def _wkv_kernel(r_ref, lw_ref, k_ref, v_ref, kk_ref, a_ref, s0_ref, o_ref, sout_ref, s_scr,
                *, c, exact):
    t = pl.program_id(1)

    @pl.when(t == 0)
    def _():
        s_scr[...] = s0_ref[0]

    head0 = lax.broadcasted_iota(jnp.int32, (c, LANES), 1) < RWKV_HEAD
    c2 = 2 * c
    row = lax.broadcasted_iota(jnp.int32, (c2, c2), 0)
    col = lax.broadcasted_iota(jnp.int32, (c2, c2), 1)
    row_hi = jnp.where(row >= c, c, 0)
    col_hi = jnp.where(col >= c, c, 0)
    same = row_hi == col_hi
    rr = row - row_hi
    cc = col - col_hi
    strict = same & (rr > cc)
    incl = same & (rr >= cc)
    eye = (row == col).astype(F32)
    crow = lax.broadcasted_iota(jnp.int32, (c, LANES), 0)

    def cumsum_rows(x):
        d = 1
        while d < c:
            x = x + jnp.where(crow >= d, pltpu.roll(x, d, 0), 0.0)
            d *= 2
        return x

    def stack(x):
        return jnp.concatenate([jnp.where(head0, x, 0.0), jnp.where(head0, 0.0, x)], axis=0)

    mm = functools.partial(_mm, exact=exact)
    pairs = range(RWKV_HEADS // 2)
    for sub in range(r_ref.shape[1] // c):
        rows = slice(sub * c, (sub + 1) * c)
        load = lambda ref: [ref[0, rows, p * LANES:(p + 1) * LANES].astype(F32) for p in pairs]
        r, lw, k, v, kk, a = (load(ref) for ref in (r_ref, lw_ref, k_ref, v_ref, kk_ref, a_ref))
        lc = [cumsum_rows(x) for x in lw]
        lc_end = [x[c - 1:c] for x in lc]
        b = [x * y for x, y in zip(kk, a)]
        lhs = [jnp.concatenate([stack(-kk[p] * jnp.exp(lc[p] - lw[p])), stack(r[p] * jnp.exp(lc[p]))], axis=0)
               for p in pairs]
        g_inv = [jnp.exp(-x) for x in lc]
        rhs = [jnp.concatenate([stack(b[p] * g_inv[p]), stack(k[p] * g_inv[p])], axis=0) for p in pairs]
        pm = [mm(x, y, _NT) for x, y in zip(lhs, rhs)]
        l_ab = [jnp.where(strict, x[:c2, :c2], 0.0) for x in pm]
        l_ak = [jnp.where(strict, x[:c2, c2:], 0.0) for x in pm]
        m_rb = [jnp.where(incl, x[c2:, :c2], 0.0) for x in pm]
        m_rk = [jnp.where(incl, x[c2:, c2:], 0.0) for x in pm]
        vs = [stack(x) for x in v]
        lakv = [mm(x, y, _NN) for x, y in zip(l_ak, vs)]
        mrkv = [mm(x, y, _NN) for x, y in zip(m_rk, vs)]
        tinv = [eye + x for x in l_ab]
        lp = l_ab
        n = 2
        while n < c:
            lp = [mm(x, x, _NN) for x in lp]
            tinv = [x + mm(x, y, _NN) for x, y in zip(tinv, lp)]
            n *= 2
        s = [s_scr[p] for p in pairs]
        xs = [mm(x, y, _NT) for x, y in zip(lhs, s)]
        u = [mm(tinv[p], xs[p][:c2] + lakv[p], _NN) for p in pairs]
        for p in pairs:
            os_ = xs[p][c2:] + mm(m_rb[p], u[p], _NN) + mrkv[p]
            o_ref[0, rows, p * LANES:(p + 1) * LANES] = os_[:c] + os_[c:]
        for p in pairs:
            g_rem = jnp.exp(lc_end[p] - lc[p])
            uv = jnp.concatenate([u[p], vs[p]], axis=0)
            bk = jnp.concatenate([stack(b[p] * g_rem), stack(k[p] * g_rem)], axis=0)
            s_scr[p] = s[p] * jnp.exp(lc_end[p]) + mm(uv, bk, _TN)

    @pl.when(t == pl.num_programs(1) - 1)
    def _():
        sout_ref[0] = s_scr[...]


def _wkv(r, lw, k, v, kk, a, s0, c, exact):
    b, l, _ = r.shape
    rows = 2 * c if l % (2 * c) == 0 else c
    seq = pl.BlockSpec((1, rows, D_MODEL), lambda i, t: (i, t, 0))
    st = pl.BlockSpec((1, RWKV_HEADS // 2, LANES, LANES), lambda i, t: (i, 0, 0, 0))
    return pl.pallas_call(
        functools.partial(_wkv_kernel, c=c, exact=exact),
        grid=(b, l // rows),
        in_specs=[seq] * 6 + [st],
        out_specs=[seq, st],
        out_shape=[jax.ShapeDtypeStruct((b, l, D_MODEL), F32),
                   jax.ShapeDtypeStruct((b, RWKV_HEADS // 2, LANES, LANES), F32)],
        scratch_shapes=[pltpu.VMEM((RWKV_HEADS // 2, LANES, LANES), F32)],
        compiler_params=_cparams(("arbitrary", "arbitrary")),
        name="wkv",
    )(r, lw, k, v, kk, a, s0)


def _rwkv_out_kernel(o_ref, r_ref, k_ref, v_ref, g_ref, x_ref, rk_ref, gng_ref, gnb_ref, wo_ref,
                     ones_ref, lg_ref, lb_ref, y_ref):
    ones = ones_ref[...]
    o = o_ref[...]
    inv = 1.0 / RWKV_HEAD
    mu = _head_sum(o, ones) * inv
    oc = o - mu
    var = _head_sum(oc * oc, ones) * inv
    on = oc * lax.rsqrt(var + GN_EPS) * gng_ref[...] + gnb_ref[...]
    r, k, v, g = (ref[...].astype(F32) for ref in (r_ref, k_ref, v_ref, g_ref))
    on = on + _head_sum(r * k * rk_ref[...], ones) * v
    out = _bdot(on * g, wo_ref[...])
    y_ref[...] = _ln(ALPHA * x_ref[...] + out, lg_ref[...], lb_ref[...])


def _rwkv_out(o, r, k, v, g, x, od, lg, lb):
    n = x.shape[0]
    tm = _row_tile(n, 256)
    row = pl.BlockSpec((tm, D_MODEL), lambda i: (i, 0))
    ws = [od["r_k"], od["ln_g"], od["ln_b"], od["w_o"], od["ones"], lg, lb]
    return pl.pallas_call(
        _rwkv_out_kernel,
        grid=(n // tm,),
        in_specs=[row] * 6 + [_full(w.shape) for w in ws],
        out_specs=row,
        out_shape=jax.ShapeDtypeStruct((n, D_MODEL), F32),
        compiler_params=_cparams(("arbitrary",)),
        name="rwkv_out",
    )(o, r, k, v, g, x, *ws)


def _rope_tables(pos):
    half = D_ROPE // 2
    freq = ROPE_BASE ** (-jnp.arange(half, dtype=F32) / half)
    ang_t = freq[:, None] * pos.astype(F32)[None, :]
    cos_t, sin_t = lax.optimization_barrier((jnp.cos(ang_t), jnp.sin(ang_t)))
    cos, sin = cos_t.T, sin_t.T
    n = pos.shape[0]
    ones = jnp.ones((n, D_NOPE), F32)
    zeros = jnp.zeros((n, D_NOPE), F32)
    z16 = jnp.zeros((n, half), F32)
    tail1 = jnp.ones((n, LANES - D_NOPE - D_ROPE), F32)
    tail0 = jnp.zeros((n, LANES - D_NOPE - D_ROPE), F32)
    c = jnp.concatenate([ones, cos, cos, tail1], axis=1)
    sa = jnp.concatenate([zeros, -sin, z16, tail0], axis=1)
    sb = jnp.concatenate([zeros, z16, sin, tail0], axis=1)
    return c, sa, sb


def _slot_cols(w, width):
    k, h, _ = w.shape
    return jnp.pad(w, ((0, 0), (0, 0), (0, HEAD_SLOT - width))).reshape(k, h * HEAD_SLOT)


def _block_diag(w):
    n, c, d = w.shape
    eye = jnp.eye(n, dtype=w.dtype)
    return (eye[:, None, :, None] * w[:, :, None, :]).reshape(n * c, n * d)


def _row2(v):
    return v.reshape(1, -1).astype(F32)


def _prep_even(w_in, g_q, w_uq, g_kv, w_uk, w_uv, conv_w, conv_b, w_rg, b_rg, w_ig, b_ig, lam, w_out):
    off_ckv, off_kr = D_CQ, D_CQ + D_C
    off_xr = off_kr + D_ROPE
    off_y = off_xr + D_RNN
    kr_cols = jnp.pad(w_in[:, off_kr:off_xr], ((0, 0), (D_NOPE, LANES - D_NOPE - D_ROPE)))
    w_in, w_uq, w_uk, w_uv, w_rg, w_ig, w_out = (
        w.astype(F32) for w in (w_in, w_uq, w_uk, w_uv, w_rg, w_ig, w_out))
    w1 = jnp.concatenate([w_in[:, :off_ckv], w_in[:, off_ckv:off_kr], w_in[:, off_xr:off_y],
                          w_in[:, off_y:], kr_cols], axis=1)
    wuq = _slot_cols(w_uq, D_NOPE + D_ROPE)
    wukv = jnp.concatenate([_slot_cols(w_uk, D_NOPE), _slot_cols(w_uv, D_V)], axis=1)
    place = np.zeros((LANES, D_ATT), np.float32)
    for h in range(MLA_HEADS):
        for cidx in range(D_ROPE):
            place[D_NOPE + cidx, h * HEAD_SLOT + D_NOPE + cidx] = 1.0
    wa = jnp.pad(w_out[:MLA_HEADS * D_V].reshape(MLA_HEADS, D_V, D_MODEL),
                 ((0, 0), (0, HEAD_SLOT - D_V), (0, 0))).reshape(D_ATT, D_MODEL)
    wr = w_out[MLA_HEADS * D_V:]
    keep_rope = np.zeros((HEAD_SLOT, LANES), np.float32)
    keep_rope[D_NOPE:D_NOPE + D_ROPE, D_NOPE:D_NOPE + D_ROPE] = np.eye(D_ROPE, dtype=np.float32)
    uk_t = jnp.pad(jnp.transpose(w_uk, (1, 2, 0)), ((0, 0), (0, HEAD_SLOT - D_NOPE), (0, 0)))
    wabs = jnp.concatenate([uk_t, jnp.broadcast_to(keep_rope, (MLA_HEADS, HEAD_SLOT, LANES))], axis=2)
    wuv = jnp.pad(jnp.transpose(w_uv, (1, 0, 2)), ((0, 0), (0, 0), (0, HEAD_SLOT - D_V)))
    return dict(
        wabs=wabs, wuv=wuv,
        w1=w1, gq=_row2(g_q), wuq=wuq, gkv=_row2(g_kv), wukv=wukv, place=jnp.asarray(place, F32),
        cw=conv_w.astype(F32), cb=_row2(conv_b), wrg=_block_diag(w_rg), brg=_row2(b_rg),
        wig=_block_diag(w_ig), big=_row2(b_ig), sp=_row2(jax.nn.softplus(-lam.astype(F32))),
        wa=wa, wr=wr)


_EVEN_MATMUL_WEIGHTS = ("w1", "wuq", "wukv", "place", "wrg", "wig", "wa", "wr")


def _single_pass(ev):
    return {n: (w.astype(BF16) if n in _EVEN_MATMUL_WEIGHTS else w) for n, w in ev.items()}


def _prep_odd(mu, w_r, w_k, w_v, w0, w1, w2, a0, a1, a2, g1, g2, k_k, k_a, r_k, ln_g, ln_b, w_o):
    ones = np.zeros((LANES, LANES), np.float32)
    ones[:RWKV_HEAD, :RWKV_HEAD] = 1.0
    ones[RWKV_HEAD:, RWKV_HEAD:] = 1.0
    return dict(
        mu=jnp.pad(mu.astype(F32), ((0, SUBLANES - mu.shape[0]), (0, 0))),
        w_r=w_r.astype(BF16), w_k=w_k.astype(BF16), w_v=w_v.astype(BF16), w0=_row2(w0),
        w1=w1.astype(BF16), w2=w2.astype(BF16), a0=_row2(a0), a1=a1.astype(BF16), a2=a2.astype(BF16),
        g1=g1.astype(BF16), g2=g2.astype(BF16), k_k=_row2(k_k), k_a=_row2(k_a), r_k=_row2(r_k),
        ln_g=_row2(ln_g), ln_b=_row2(ln_b), w_o=w_o.astype(BF16), ones=jnp.asarray(ones, BF16))


def _pair_states(s):
    b = s.shape[0]
    s = s.reshape(b, RWKV_HEADS // 2, 2, RWKV_HEAD, RWKV_HEAD).astype(F32)
    eye = jnp.eye(2, dtype=F32)
    out = s[:, :, :, :, None, :] * eye[None, None, :, None, :, None]
    return out.reshape(b, RWKV_HEADS // 2, LANES, LANES)


def _unpair_states(s):
    b = s.shape[0]
    s = s.reshape(b, RWKV_HEADS // 2, 2, RWKV_HEAD, 2, RWKV_HEAD)
    return jnp.stack([s[:, :, 0, :, 0, :], s[:, :, 1, :, 1, :]], axis=2).reshape(
        b, RWKV_HEADS, RWKV_HEAD, RWKV_HEAD)


def _round_up(n, m):
    return -(-n // m) * m


def kernel(x_prompt, x_sample, cache_ckv, cache_krope, state_conv, state_lru, state_shift, state_wkv,
           meta_tokens, ev_w_in, ev_g_q, ev_w_uq, ev_g_kv, ev_w_uk, ev_w_uv, ev_conv_w, ev_conv_b,
           ev_w_rg, ev_b_rg, ev_w_ig, ev_b_ig, ev_lru_lambda, ev_w_out, od_mu, od_w_r, od_w_k, od_w_v,
           od_w0, od_w1, od_w2, od_a0, od_a1, od_a2, od_g1, od_g2, od_k_k, od_k_a, od_r_k, od_ln_g,
           od_ln_b, od_w_o, ln_g, ln_b, router_w, router_b, exp_w_gate, exp_w_up, exp_w_down):
    assert x_prompt.shape[0] == 1 and x_prompt.shape[2] == D_MODEL
    seq = x_prompt.shape[1]
    assert seq % CHUNK == 0
    bs, ls, _ = x_sample.shape
    past = cache_ckv.shape[2]
    ns = bs * ls
    end = ROW0 + seq
    tp = _round_up(end, 512)

    ev = _prep_even(ev_w_in[0], ev_g_q[0], ev_w_uq[0], ev_g_kv[0], ev_w_uk[0], ev_w_uv[0], ev_conv_w[0],
                    ev_conv_b[0], ev_w_rg[0], ev_b_rg[0], ev_w_ig[0], ev_b_ig[0], ev_lru_lambda[0],
                    ev_w_out[0])
    od = _prep_odd(od_mu[0], od_w_r[0], od_w_k[0], od_w_v[0], od_w0[0], od_w1[0], od_w2[0], od_a0[0],
                   od_a1[0], od_a2[0], od_g1[0], od_g2[0], od_k_k[0], od_k_a[0], od_r_k[0], od_ln_g[0],
                   od_ln_b[0], od_w_o[0])
    rw = router_w.astype(F32)
    rb = _row2(router_b)
    wg, wu, wd = (w.reshape((DEPTH * N_EXPERTS,) + w.shape[2:]) for w in (exp_w_gate, exp_w_up, exp_w_down))
    lng = ln_g.astype(F32)[:, :, None, :]
    lnb = ln_b.astype(F32)[:, :, None, :]

    def moe(x, layer):
        sparse = x.shape[0] % (SC_WORKERS * SUBLANES) == 0 and x.shape[0] >= SPARSE_MIN_ROWS
        fn = _moe_sparse if sparse else _moe
        return fn(x, rw, rb, wg, wu, wd, layer * N_EXPERTS, lng[layer, 1], lnb[layer, 1])

    xp = jnp.concatenate([jnp.zeros((PAD_FRONT, D_MODEL), F32), meta_tokens.astype(F32),
                          x_prompt[0].astype(F32), jnp.zeros((tp - end, D_MODEL), F32)], axis=0)
    tabs_p = _rope_tables(jnp.maximum(jnp.arange(tp) - PAD_FRONT, 0))
    evb = _single_pass(ev)
    q_p, ckv_p, kr_p, xr_p, yg_p = _even_proj(xp, evb["w1"], ev["gq"], evb["wuq"], ev["gkv"], tabs_p,
                                              MLA_SCALE * LOG2E)
    ones_col = np.zeros((D_ATT, 1), np.float32)
    ones_col[D_V::HEAD_SLOT] = 1.0
    k_p, vt_p = _kv_proj_t(ckv_p, kr_p, evb["wukv"][:, :D_ATT], evb["place"], evb["wukv"][:, D_ATT:].T,
                           jnp.asarray(ones_col), _round_up(tp, FLASH_TK))
    attn_p = _flash_attention(q_p, k_p, vt_p, FLASH_TQ, FLASH_TK)
    rnn_p, tailx_p, tailh_p = _rglru(
        xr_p[None], yg_p[None], ev["cw"], ev["cb"], evb["wrg"], ev["brg"], evb["wig"], ev["big"], ev["sp"],
        jnp.zeros((1, SUBLANES, D_RNN), F32), jnp.zeros((1, 1, D_RNN), F32), PAD_FRONT, end)
    x1_p = _mix_out(attn_p, rnn_p[0], xp, evb["wa"], evb["wr"], lng[0, 0], lnb[0, 0])
    x2_p = moe(x1_p, 0)

    xs = x_sample.reshape(ns, D_MODEL).astype(F32)
    pos_s = jnp.tile(N_META + past + jnp.arange(ls), bs)
    q_s, ckv_s, kr_s, xr_s, yg_s = _even_proj(xs, ev["w1"], ev["gq"], ev["wuq"], ev["gkv"], _rope_tables(pos_s),
                                              MLA_SCALE)
    _, ckv_m, kr_m, _, _ = _even_proj(meta_tokens.astype(F32), ev["w1"], ev["gq"], ev["wuq"], ev["gkv"],
                                      _rope_tables(jnp.arange(N_META)), MLA_SCALE)
    n_keys = N_META + past + ls
    nk_pad = _round_up(n_keys, LANES)
    meta_ckv = jnp.broadcast_to(ckv_m[None], (bs, N_META, D_C))
    meta_kr = jnp.broadcast_to(kr_m[None], (bs, N_META, LANES))
    cache_kr = jnp.pad(cache_krope[0].astype(F32), ((0, 0), (0, 0), (D_NOPE, LANES - D_NOPE - D_ROPE)))
    all_ckv = jnp.concatenate([meta_ckv, cache_ckv[0].astype(F32), ckv_s.reshape(bs, ls, D_C),
                               jnp.zeros((bs, nk_pad - n_keys, D_C), F32)], axis=1)
    all_kr = jnp.concatenate([meta_kr, cache_kr, kr_s.reshape(bs, ls, LANES),
                              jnp.zeros((bs, nk_pad - n_keys, LANES), F32)], axis=1)
    attn_s = _sample_attention(q_s.reshape(bs, ls, D_ATT), jnp.concatenate([all_ckv, all_kr], axis=2),
                               ev["wabs"], ev["wuv"], n_keys)
    buf0_s = jnp.pad(state_conv[0].astype(F32), ((0, 0), (SUBLANES - (CONV_W - 1), 0), (0, 0)))
    rnn_s, tailx_s, tailh_s = _rglru(
        xr_s.reshape(bs, ls, D_RNN), yg_s.reshape(bs, ls, D_RNN), ev["cw"], ev["cb"], ev["wrg"], ev["brg"],
        ev["wig"], ev["big"], ev["sp"], buf0_s, state_lru[0].astype(F32)[:, None, :], 0, ls)
    x1_s = _mix_out(attn_s.reshape(ns, D_ATT), rnn_s.reshape(ns, D_RNN), xs, ev["wa"], ev["wr"],
                    lng[0, 0], lnb[0, 0])
    x2_s = moe(x1_s, 0)

    r_p, lw_p, kk_in_p, v1_p, kkn_p, a_p, g_p = _rwkv_proj(
        x2_p[None], jnp.zeros((1, 1, D_MODEL), F32), od, PAD_FRONT, end, BF16)
    o_p, s_p = _wkv(r_p, lw_p, kk_in_p, v1_p, kkn_p, a_p,
                    jnp.zeros((1, RWKV_HEADS // 2, LANES, LANES), F32), CHUNK, False)
    x3_p = _rwkv_out(o_p[0], r_p[0], kk_in_p[0], v1_p[0], g_p[0], x2_p, od, lng[1, 0], lnb[1, 0])
    x4_p = moe(x3_p, 1)

    x2_s3 = x2_s.reshape(bs, ls, D_MODEL)
    r_s, lw_s, kk_in_s, v1_s, kkn_s, a_s, g_s = _rwkv_proj(
        x2_s3, state_shift[0].astype(F32)[:, None, :], od, 0, ls, F32)
    to_chunk = lambda z: jnp.pad(z, ((0, 0), (0, _round_up(ls, CHUNK) - ls), (0, 0)))
    o_s, s_s = _wkv(*(to_chunk(z) for z in (r_s, lw_s, kk_in_s, v1_s, kkn_s, a_s)),
                    _pair_states(state_wkv[0]), CHUNK, True)
    o_s = o_s[:, :ls]
    flat = lambda z: z.reshape(ns, D_MODEL)
    x3_s = _rwkv_out(flat(o_s), flat(r_s), flat(kk_in_s), flat(v1_s), flat(g_s), x2_s, od,
                     lng[1, 0], lnb[1, 0])
    x4_s = moe(x3_s, 1)

    dt = x_prompt.dtype
    nb = CONV_W - 1
    return (
        x4_p[ROW0:end][None].astype(dt),
        x4_s.reshape(bs, ls, D_MODEL).astype(dt),
        ckv_p[PAD_FRONT:end][None, None].astype(dt),
        kr_p[PAD_FRONT:end, D_NOPE:D_NOPE + D_ROPE][None, None].astype(dt),
        tailx_p[:, SUBLANES - nb:][None].astype(dt),
        tailh_p[:, SUBLANES - 1][None].astype(dt),
        x2_p[end - 1][None, None].astype(dt),
        _unpair_states(s_p)[None].astype(dt),
        ckv_s.reshape(bs, ls, D_C)[None].astype(dt),
        kr_s.reshape(bs, ls, LANES)[:, :, D_NOPE:D_NOPE + D_ROPE][None].astype(dt),
        tailx_s[:, SUBLANES - nb:][None].astype(dt),
        tailh_s[:, SUBLANES - 1][None].astype(dt),
        x2_s3[:, ls - 1][None].astype(dt),
        _unpair_states(s_s)[None].astype(dt),
    )
```

```python
import functools

import numpy as np
import jax
import jax.numpy as jnp
from jax import lax
from jax.experimental import pallas as pl
from jax.experimental.pallas import tpu as pltpu
from jax.experimental.pallas import tpu_sc as plsc

F32 = jnp.float32
BF16 = jnp.bfloat16

D_MODEL = 1024
N_META = 16
CHUNK = 64
CHUNK_SHIFT = 6
LN_EPS = 1e-5
RMS_EPS = 1e-6
DEPTH = 2
ALPHA = (2 * DEPTH) ** 0.25
MLA_HEADS = 8
D_NOPE = 64
D_ROPE = 32
D_V = 64
D_C = 256
D_CQ = 384
ROPE_BASE = 10000.0
MLA_SCALE = (D_NOPE + D_ROPE) ** -0.5
D_RNN = 512
LRU_BLOCKS = 8
LRU_BLOCK_W = D_RNN // LRU_BLOCKS
CONV_W = 4
LRU_C = 8.0
RWKV_HEAD = 64
RWKV_HEADS = D_MODEL // RWKV_HEAD
DECAY_SCALE = float(np.exp(-0.5))
GN_EPS = 64e-5
N_EXPERTS = 16
N_GROUPS = 4
EXPERTS_PER_GROUP = N_EXPERTS // N_GROUPS
D_EXPERT = 512

LANES = 128
SUBLANES = 8
HEAD_SLOT = LANES
D_ATT = MLA_HEADS * HEAD_SLOT
VT_SLOT = 80
D_VT = MLA_HEADS * VT_SLOT
PAD_FRONT = CHUNK - N_META
ROW0 = PAD_FRONT + N_META
NEG = -1e30
LOG2E = 1.4426950408889634
SC_CORES = 2
SC_SUBCORES = 16
SC_WORKERS = SC_CORES * SC_SUBCORES
MOE_TILE = 512
FLASH_TQ = 512
FLASH_TK = 1024
SPARSE_MIN_ROWS = 1024
VMEM_LIMIT = 56 * 1024 * 1024

C_CQ = 0
C_CKV = D_CQ
C_XR = C_CKV + D_C
C_YG = C_XR + D_RNN
C_KR = C_YG + D_RNN
N_COL = C_KR + LANES


def _cparams(sem):
    return pltpu.CompilerParams(dimension_semantics=sem, vmem_limit_bytes=VMEM_LIMIT)


def _row_tile(n, cap):
    for t in (1024, 512, 256, 128, 64, 32, 16, 8):
        if t <= cap and n % t == 0:
            return t
    return n


def _full(shape):
    zeros = (0,) * len(shape)
    return pl.BlockSpec(shape, lambda *_: zeros)


def _ln(x, g, b):
    mu = jnp.mean(x, axis=-1, keepdims=True)
    xc = x - mu
    var = jnp.mean(xc * xc, axis=-1, keepdims=True)
    return xc * lax.rsqrt(var + LN_EPS) * g + b


def _bdot(a, b):
    return jnp.dot(a.astype(BF16), b.astype(BF16), preferred_element_type=F32)


def _split2(x):
    hi = x.astype(BF16)
    return hi, (x - hi.astype(F32)).astype(BF16)


_NN = ((1,), (0,))
_NT = ((1,), (1,))
_TN = ((0,), (0,))


def _mm(a, b, dims, exact):
    dn = (dims, ((), ()))
    if not exact:
        return lax.dot_general(a.astype(BF16), b.astype(BF16), dn, preferred_element_type=F32)
    ah, al = _split2(a)
    bh, bl = _split2(b)
    return (lax.dot_general(ah, bh, dn, preferred_element_type=F32)
            + lax.dot_general(al, bh, dn, preferred_element_type=F32)
            + lax.dot_general(ah, bl, dn, preferred_element_type=F32))


def _wdot(a, w):
    return _mm(a, w, _NN, exact=(w.dtype == F32))


def _act_dtype(w):
    return F32 if w.dtype == F32 else BF16


def _rope_slot(x, c, sa, sb):
    return x * c + pltpu.roll(x, LANES - D_ROPE // 2, 1) * sa + pltpu.roll(x, D_ROPE // 2, 1) * sb


def _even_proj_kernel(x_ref, w1_ref, gq_ref, wuq_ref, gkv_ref, c_ref, sa_ref, sb_ref,
                      q_ref, ckv_ref, kr_ref, xr_ref, yg_ref, *, q_scale):
    u = _wdot(x_ref[...], w1_ref[...])
    cq = u[:, C_CQ:C_CQ + D_CQ]
    cq = cq * lax.rsqrt(jnp.mean(cq * cq, axis=-1, keepdims=True) + RMS_EPS) * gq_ref[...]
    q = _wdot(cq, wuq_ref[...])
    c, sa, sb = c_ref[...], sa_ref[...], sb_ref[...]
    for h in range(MLA_HEADS):
        sl = slice(h * HEAD_SLOT, (h + 1) * HEAD_SLOT)
        q_ref[:, sl] = (_rope_slot(q[:, sl], c, sa, sb) * q_scale).astype(q_ref.dtype)
    ckv = u[:, C_CKV:C_CKV + D_C]
    ckv_ref[...] = ckv * lax.rsqrt(jnp.mean(ckv * ckv, axis=-1, keepdims=True) + RMS_EPS) * gkv_ref[...]
    kr_ref[...] = _rope_slot(u[:, C_KR:C_KR + LANES], c, sa, sb)
    xr_ref[...] = u[:, C_XR:C_XR + D_RNN]
    yg_ref[...] = u[:, C_YG:C_YG + D_RNN]


def _even_proj(x, w1, gq, wuq, gkv, tabs, q_scale):
    n = x.shape[0]
    tm = _row_tile(n, 512)
    row = lambda w: pl.BlockSpec((tm, w), lambda i: (i, 0))
    c, sa, sb = tabs
    return pl.pallas_call(
        functools.partial(_even_proj_kernel, q_scale=q_scale),
        grid=(n // tm,),
        in_specs=[row(D_MODEL), _full(w1.shape), _full(gq.shape), _full(wuq.shape), _full(gkv.shape),
                  row(LANES), row(LANES), row(LANES)],
        out_specs=[row(D_ATT), row(D_C), row(LANES), row(D_RNN), row(D_RNN)],
        out_shape=[jax.ShapeDtypeStruct((n, D_ATT), _act_dtype(w1)), jax.ShapeDtypeStruct((n, D_C), F32),
                   jax.ShapeDtypeStruct((n, LANES), F32), jax.ShapeDtypeStruct((n, D_RNN), F32),
                   jax.ShapeDtypeStruct((n, D_RNN), F32)],
        compiler_params=_cparams(("arbitrary",)),
        name="even_proj",
    )(x, w1, gq, wuq, gkv, c, sa, sb)


def _kv_proj_t_kernel(ckv_ref, kr_ref, wuk_ref, p_ref, wuvt_ref, ones_ref, k_ref, vt_ref):
    ckv = ckv_ref[...].astype(BF16)
    k = jnp.dot(ckv, wuk_ref[...], preferred_element_type=F32) + _bdot(kr_ref[...], p_ref[...])
    k_ref[...] = k.astype(BF16)
    vt = lax.dot_general(wuvt_ref[...], ckv, (_NT, ((), ())), preferred_element_type=F32)
    vt_ref[...] = (vt + ones_ref[...]).astype(BF16)


def _kv_proj_t(ckv, kr, wuk, place, wuvt, ones_col, n_out):
    n = ckv.shape[0]
    tm = _row_tile(n, 512)
    assert n_out % tm == 0
    last = n // tm - 1
    row_in = lambda w: pl.BlockSpec((tm, w), lambda i: (jnp.minimum(i, last), 0))
    return pl.pallas_call(
        _kv_proj_t_kernel,
        grid=(n_out // tm,),
        in_specs=[row_in(D_C), row_in(LANES), _full(wuk.shape), _full(place.shape), _full(wuvt.shape),
                  _full(ones_col.shape)],
        out_specs=[pl.BlockSpec((tm, D_ATT), lambda i: (i, 0)), pl.BlockSpec((D_VT, tm), lambda i: (0, i))],
        out_shape=[jax.ShapeDtypeStruct((n_out, D_ATT), BF16), jax.ShapeDtypeStruct((D_VT, n_out), BF16)],
        compiler_params=_cparams(("arbitrary",)),
        name="kv_proj_t",
    )(ckv, kr, wuk, place, wuvt, ones_col)


def _flash_kernel(qi_ref, kj_ref, last_ref, q_ref, k_ref, vt_ref, o_ref, m_scr, acc_scr, *, tq, tk):
    step = pl.program_id(0)
    i = qi_ref[step]
    j = kj_ref[step]

    @pl.when(j == 0)
    def _():
        m_scr[...] = jnp.full(m_scr.shape, NEG, F32)
        acc_scr[...] = jnp.zeros(acc_scr.shape, F32)

    heads = range(MLA_HEADS)
    slots = [slice(h * HEAD_SLOT, (h + 1) * HEAD_SLOT) for h in heads]
    vslots = [slice(h * VT_SLOT, (h + 1) * VT_SLOT) for h in heads]

    def accumulate(masked):
        st = [lax.dot_general(k_ref[:, sl], q_ref[:, sl], (_NT, ((), ())), preferred_element_type=F32)
              for sl in slots]
        if masked:
            krow = j * tk + lax.broadcasted_iota(jnp.int32, (tk, tq), 0)
            qrow = i * tq + lax.broadcasted_iota(jnp.int32, (tk, tq), 1)
            keep = ((((qrow - ROW0) >> CHUNK_SHIFT) >= ((krow - ROW0) >> CHUNK_SHIFT))
                    & (krow >= PAD_FRONT))
            st = [jnp.where(keep, x, NEG) for x in st]
        m_prev = [m_scr[h:h + 1, :] for h in heads]
        m_new = [jnp.maximum(mp, jnp.max(x, axis=0, keepdims=True)) for mp, x in zip(m_prev, st)]
        alpha = [jnp.exp2(mp - mn) for mp, mn in zip(m_prev, m_new)]
        pt = [jnp.exp2(x - mn).astype(BF16) for x, mn in zip(st, m_new)]
        pv = [jnp.dot(vt_ref[sl, :], x, preferred_element_type=F32) for x, sl in zip(pt, vslots)]
        for h in heads:
            acc_scr[vslots[h], :] = alpha[h] * acc_scr[vslots[h], :] + pv[h]
            m_scr[h:h + 1, :] = m_new[h]

    last = last_ref[step] == 1
    edge = last | (j == 0)

    @pl.when(edge)
    def _():
        accumulate(True)

    @pl.when(jnp.logical_not(edge))
    def _():
        accumulate(False)

    @pl.when(last)
    def _():
        fill = jnp.zeros((HEAD_SLOT - D_V, tq), F32)
        for sl, vsl in zip(slots, vslots):
            acc = acc_scr[vsl, :]
            out_t = jnp.concatenate([acc[:D_V] / acc[D_V:D_V + 1, :], fill], axis=0)
            o_ref[:, sl] = out_t.T.astype(BF16)


def _flash_attention(q, k, vt, tq, tk):
    n = q.shape[0]
    nq = n // tq
    n_kblocks = [-(-(i + 1) * tq // tk) for i in range(nq)]
    assert k.shape[0] >= n_kblocks[-1] * tk
    qi = np.concatenate([np.full(c, i, np.int32) for i, c in enumerate(n_kblocks)])
    kj = np.concatenate([np.arange(c, dtype=np.int32) for c in n_kblocks])
    last = np.concatenate([np.arange(c, dtype=np.int32) == c - 1 for c in n_kblocks]).astype(np.int32)
    grid_spec = pltpu.PrefetchScalarGridSpec(
        num_scalar_prefetch=3,
        grid=(len(qi),),
        in_specs=[pl.BlockSpec((tq, D_ATT), lambda s, qi, kj, last: (qi[s], 0)),
                  pl.BlockSpec((tk, D_ATT), lambda s, qi, kj, last: (kj[s], 0)),
                  pl.BlockSpec((D_VT, tk), lambda s, qi, kj, last: (0, kj[s]))],
        out_specs=pl.BlockSpec((tq, D_ATT), lambda s, qi, kj, last: (qi[s], 0)),
        scratch_shapes=[pltpu.VMEM((MLA_HEADS, tq), F32), pltpu.VMEM((D_VT, tq), F32)],
    )
    return pl.pallas_call(
        functools.partial(_flash_kernel, tq=tq, tk=tk),
        grid_spec=grid_spec,
        out_shape=jax.ShapeDtypeStruct((n, D_ATT), BF16),
        compiler_params=_cparams(("arbitrary",)),
        name="flash_attention",
    )(jnp.asarray(qi), jnp.asarray(kj), jnp.asarray(last), q, k, vt)


def _sample_attn_kernel(q_ref, kc_ref, wabs_ref, wuv_ref, o_ref, *, n_keys):
    kc = kc_ref[0]
    l = q_ref.shape[1]
    slots = [slice(h * HEAD_SLOT, (h + 1) * HEAD_SLOT) for h in range(MLA_HEADS)]
    qa = jnp.concatenate([_mm(q_ref[0, :, sl], wabs_ref[h], _NN, True) for h, sl in enumerate(slots)],
                         axis=0)
    keep = lax.broadcasted_iota(jnp.int32, (MLA_HEADS * l, kc.shape[0]), 1) < n_keys
    s = jnp.where(keep, _mm(qa, kc, _NT, True), NEG)
    p = jnp.exp(s - jnp.max(s, axis=-1, keepdims=True))
    p = p / jnp.sum(p, axis=-1, keepdims=True)
    pc = _mm(p, kc[:, :D_C], _NN, True)
    for h, sl in enumerate(slots):
        o_ref[0, :, sl] = _mm(pc[h * l:(h + 1) * l], wuv_ref[h], _NN, True)


def _sample_attention(q, kc, wabs, wuv, n_keys):
    b, l, _ = q.shape
    nk, dk = kc.shape[1:]
    return pl.pallas_call(
        functools.partial(_sample_attn_kernel, n_keys=n_keys),
        grid=(b,),
        in_specs=[pl.BlockSpec((1, l, D_ATT), lambda i: (i, 0, 0)),
                  pl.BlockSpec((1, nk, dk), lambda i: (i, 0, 0)),
                  _full(wabs.shape), _full(wuv.shape)],
        out_specs=pl.BlockSpec((1, l, D_ATT), lambda i: (i, 0, 0)),
        out_shape=jax.ShapeDtypeStruct((b, l, D_ATT), F32),
        compiler_params=_cparams(("arbitrary",)),
        name="sample_attention",
    )(q, kc, wabs, wuv)


def _expm1(x):
    series = x * (1.0 + x * (0.5 + x * (1.0 / 6.0 + x * (1.0 / 24.0 + x * (1.0 / 120.0)))))
    return jnp.where(jnp.abs(x) < 0.05, series, jnp.exp(x) - 1.0)


def _gelu_tanh(x):
    return 0.5 * x * (1.0 + jnp.tanh(0.7978845608028654 * (x + 0.044715 * x * x * x)))


def _rglru_kernel(xr_ref, yg_ref, cw_ref, cb_ref, wrg_ref, brg_ref, wig_ref, big_ref, sp_ref,
                  buf0_ref, h0_ref, rnn_ref, tailx_ref, tailh_ref, prev_scr, h_scr,
                  *, tm, start, end):
    t = pl.program_id(1)

    @pl.when(t == 0)
    def _():
        prev_scr[...] = buf0_ref[0]
        h_scr[...] = jnp.broadcast_to(h0_ref[0], h_scr.shape)

    x = xr_ref[0]
    ext = jnp.concatenate([prev_scr[...], x], axis=0)
    cw = cw_ref[...]
    xc = cb_ref[...] + cw[CONV_W - 1:CONV_W] * x
    for d in range(1, CONV_W):
        xc = xc + cw[CONV_W - 1 - d:CONV_W - d] * pltpu.roll(ext, d, 0)[SUBLANES:]
    prev_scr[...] = x[tm - SUBLANES:]

    r = jax.nn.sigmoid(_wdot(xc, wrg_ref[...]) + brg_ref[...])
    ig = jax.nn.sigmoid(_wdot(xc, wig_ref[...]) + big_ref[...])
    log_a = -LRU_C * r * sp_ref[...]
    a = jnp.exp(log_a)
    b = jnp.sqrt(-_expm1(2.0 * log_a)) * (ig * xc)
    row = lax.broadcasted_iota(jnp.int32, (tm, D_RNN), 0)
    if start > 0:
        live = (t * tm + row) >= start
        a = jnp.where(live, a, 1.0)
        b = jnp.where(live, b, 0.0)
    d = 1
    while d < tm:
        b = a * jnp.where(row >= d, pltpu.roll(b, d, 0), 0.0) + b
        a = a * jnp.where(row >= d, pltpu.roll(a, d, 0), 1.0)
        d *= 2
    h = a * h_scr[0:1] + b
    h_scr[...] = jnp.broadcast_to(h[tm - 1:tm], h_scr.shape)
    rnn_ref[0] = (h * _gelu_tanh(yg_ref[0])).astype(rnn_ref.dtype)

    t_end = (end - 1) // tm
    el = end - t_end * tm

    @pl.when(t == t_end)
    def _():
        tailx_ref[0] = ext[el:el + SUBLANES]
        tailh_ref[0] = h[el - SUBLANES:el]


def _rglru(xr, yg, cw, cb, wrg, brg, wig, big, sp, buf0, h0, start, end):
    b, l, _ = xr.shape
    tm = _row_tile(l, 512)
    seq = pl.BlockSpec((1, tm, D_RNN), lambda i, t: (i, t, 0))
    per_b = lambda r: pl.BlockSpec((1, r, D_RNN), lambda i, t: (i, 0, 0))
    return pl.pallas_call(
        functools.partial(_rglru_kernel, tm=tm, start=start, end=end),
        grid=(b, l // tm),
        in_specs=[seq, seq, _full(cw.shape), _full(cb.shape), _full(wrg.shape), _full(brg.shape),
                  _full(wig.shape), _full(big.shape), _full(sp.shape), per_b(SUBLANES), per_b(1)],
        out_specs=[seq, per_b(SUBLANES), per_b(SUBLANES)],
        out_shape=[jax.ShapeDtypeStruct((b, l, D_RNN), _act_dtype(wrg)),
                   jax.ShapeDtypeStruct((b, SUBLANES, D_RNN), F32),
                   jax.ShapeDtypeStruct((b, SUBLANES, D_RNN), F32)],
        scratch_shapes=[pltpu.VMEM((SUBLANES, D_RNN), F32), pltpu.VMEM((SUBLANES, D_RNN), F32)],
        compiler_params=_cparams(("arbitrary", "arbitrary")),
        name="rglru",
    )(xr, yg, cw, cb, wrg, brg, wig, big, sp, buf0, h0)


def _mix_out_kernel(attn_ref, rnn_ref, x_ref, wa_ref, wr_ref, g_ref, b_ref, o_ref):
    mix = _wdot(attn_ref[...], wa_ref[...]) + _wdot(rnn_ref[...], wr_ref[...])
    o_ref[...] = _ln(ALPHA * x_ref[...] + mix, g_ref[...], b_ref[...])


def _mix_out(attn, rnn, x, wa, wr, g, b):
    n = x.shape[0]
    tm = _row_tile(n, 512)
    row = lambda w: pl.BlockSpec((tm, w), lambda i: (i, 0))
    return pl.pallas_call(
        _mix_out_kernel,
        grid=(n // tm,),
        in_specs=[row(D_ATT), row(D_RNN), row(D_MODEL), _full(wa.shape), _full(wr.shape),
                  _full(g.shape), _full(b.shape)],
        out_specs=row(D_MODEL),
        out_shape=jax.ShapeDtypeStruct((n, D_MODEL), F32),
        compiler_params=_cparams(("arbitrary",)),
        name="mix_out",
    )(attn, rnn, x, wa, wr, g, b)


def _first_argmax(vals, lane):
    m = jnp.max(vals, axis=-1, keepdims=True)
    idx = jnp.min(jnp.where(vals == m, lane, N_EXPERTS), axis=-1, keepdims=True)
    return m, idx


def _router_top2(x, rw, rb):
    logits = _mm(x, rw, _NN, True)
    s = jax.nn.sigmoid(logits)
    sel = s + rb
    lane = lax.broadcasted_iota(jnp.int32, sel.shape, 1)
    grp = lane >> 2
    best = None
    g_best = None
    for g in range(N_GROUPS):
        vals = jnp.where(grp == g, sel, NEG)
        m1, i1 = _first_argmax(vals, lane)
        m2, _ = _first_argmax(jnp.where(lane == i1, NEG, vals), lane)
        score = m1 + m2
        if g == 0:
            best, g_best = score, jnp.zeros_like(i1)
        else:
            upd = score > best
            g_best = jnp.where(upd, g, g_best)
            best = jnp.where(upd, score, best)
    vals = jnp.where(grp == g_best, sel, NEG)
    _, i1 = _first_argmax(vals, lane)
    _, i2 = _first_argmax(jnp.where(lane == i1, NEG, vals), lane)
    w1 = jnp.sum(jnp.where(lane == i1, s, 0.0), axis=-1, keepdims=True)
    w2 = jnp.sum(jnp.where(lane == i2, s, 0.0), axis=-1, keepdims=True)
    den = w1 + w2
    return lane, i1, i2, w1 / den, w2 / den


def _router_gate(x, rw, rb):
    lane, i1, i2, g1, g2 = _router_top2(x, rw, rb)
    return jnp.where(lane == i1, g1, 0.0) + jnp.where(lane == i2, g2, 0.0)


def _moe_kernel(x_ref, rw_ref, rb_ref, wg_ref, wu_ref, wd_ref, g_ref, b_ref, o_ref,
                gate_scr, xb_scr, acc_scr):
    e = pl.program_id(1)

    @pl.when(e == 0)
    def _():
        x = x_ref[...]
        gate = _router_gate(x, rw_ref[...], rb_ref[...])
        for k in range(N_EXPERTS):
            gate_scr[k] = jnp.broadcast_to(gate[:, k:k + 1], gate_scr.shape[1:])
        xb_scr[...] = x.astype(BF16)
        acc_scr[...] = jnp.zeros(acc_scr.shape, F32)

    xb = xb_scr[...]
    hg = jnp.dot(xb, wg_ref[0].astype(BF16), preferred_element_type=F32)
    hu = jnp.dot(xb, wu_ref[0].astype(BF16), preferred_element_type=F32)
    gate_e = gate_scr[e]
    h = jax.nn.silu(hg) * hu * jnp.concatenate([gate_e] * (D_EXPERT // LANES), axis=1)
    acc_scr[...] += jnp.dot(h.astype(BF16), wd_ref[0].astype(BF16), preferred_element_type=F32)

    @pl.when(e == N_EXPERTS - 1)
    def _():
        o_ref[...] = _ln(ALPHA * x_ref[...] + acc_scr[...], g_ref[...], b_ref[...])


def _moe(x, rw, rb, wg, wu, wd, e0, g, b):
    n = x.shape[0]
    tm = _row_tile(n, 512)
    row = pl.BlockSpec((tm, D_MODEL), lambda i, e: (i, 0))
    return pl.pallas_call(
        _moe_kernel,
        grid=(n // tm, N_EXPERTS),
        in_specs=[row, _full(rw.shape), _full(rb.shape),
                  pl.BlockSpec((1, D_MODEL, D_EXPERT), lambda i, e: (e0 + e, 0, 0)),
                  pl.BlockSpec((1, D_MODEL, D_EXPERT), lambda i, e: (e0 + e, 0, 0)),
                  pl.BlockSpec((1, D_EXPERT, D_MODEL), lambda i, e: (e0 + e, 0, 0)),
                  _full(g.shape), _full(b.shape)],
        out_specs=row,
        out_shape=jax.ShapeDtypeStruct((n, D_MODEL), F32),
        scratch_shapes=[pltpu.VMEM((N_EXPERTS, tm, LANES), F32), pltpu.VMEM((tm, D_MODEL), BF16),
                        pltpu.VMEM((tm, D_MODEL), F32)],
        compiler_params=_cparams(("arbitrary", "arbitrary")),
        name="moe",
    )(x, rw, rb, wg, wu, wd, g, b)


M_I1, M_I2, M_R1, M_R2, M_G1, M_G2, M_COLS = 0, 1, 2, 3, 4, 5, 8


def _first_argmax_rows(vals, row):
    m = jnp.max(vals, axis=0, keepdims=True)
    idx = jnp.min(jnp.where(vals == m, row, N_EXPERTS), axis=0, keepdims=True)
    return m, idx


HI16 = -65536


def _pack_bf16_pairs(x):
    w = x.shape[1] // 2
    hi = lax.bitcast_convert_type(x[:, :w].astype(BF16).astype(F32), jnp.int32)
    lo = lax.bitcast_convert_type(x[:, w:].astype(BF16).astype(F32), jnp.int32)
    return (hi & HI16) | lax.shift_right_logical(lo, 16)


def _unpack_bf16_pairs(p):
    hi = lax.bitcast_convert_type(p & HI16, F32)
    lo = lax.bitcast_convert_type(lax.shift_left(p, 16), F32)
    return jnp.concatenate([hi, lo], axis=1)


def _route_kernel(x_ref, rwt_ref, rbc_ref, meta_ref, cnt_ref, xpk_ref, carry_scr, *, tm):
    @pl.when(pl.program_id(0) == 0)
    def _():
        carry_scr[...] = jnp.zeros(carry_scr.shape, F32)

    logits = _mm(rwt_ref[...], x_ref[...], _NT, True)
    s = jax.nn.sigmoid(logits)
    sel = s + rbc_ref[...]
    row = lax.broadcasted_iota(jnp.int32, sel.shape, 0)
    grp = row >> 2
    best = None
    g_best = None
    for g in range(N_GROUPS):
        vals = jnp.where(grp == g, sel, NEG)
        m1, i1 = _first_argmax_rows(vals, row)
        m2, _ = _first_argmax_rows(jnp.where(row == i1, NEG, vals), row)
        score = m1 + m2
        if g == 0:
            best, g_best = score, jnp.zeros_like(i1)
        else:
            upd = score > best
            g_best = jnp.where(upd, g, g_best)
            best = jnp.where(upd, score, best)
    vals = jnp.where(grp == g_best, sel, NEG)
    _, i1 = _first_argmax_rows(vals, row)
    _, i2 = _first_argmax_rows(jnp.where(row == i1, NEG, vals), row)
    w1 = jnp.sum(jnp.where(row == i1, s, 0.0), axis=0, keepdims=True)
    w2 = jnp.sum(jnp.where(row == i2, s, 0.0), axis=0, keepdims=True)
    den = w1 + w2

    chosen = jnp.where((row == i1) | (row == i2), 1.0, 0.0)
    earlier = (lax.broadcasted_iota(jnp.int32, (tm, tm), 0)
               < lax.broadcasted_iota(jnp.int32, (tm, tm), 1)).astype(BF16)
    seen = jnp.dot(chosen.astype(BF16), earlier, preferred_element_type=F32) + carry_scr[:, 0:1]
    r1 = jnp.sum(jnp.where(row == i1, seen, 0.0), axis=0, keepdims=True)
    r2 = jnp.sum(jnp.where(row == i2, seen, 0.0), axis=0, keepdims=True)
    carry_scr[...] = carry_scr[...] + jnp.sum(chosen, axis=1, keepdims=True)
    mrow = lax.broadcasted_iota(jnp.int32, (M_COLS, tm), 0)
    meta = jnp.zeros((M_COLS, tm), F32)
    for c, val in ((M_I1, i1.astype(F32)), (M_I2, i2.astype(F32)), (M_R1, r1), (M_R2, r2),
                   (M_G1, w1 / den), (M_G2, w2 / den)):
        meta = jnp.where(mrow == c, val, meta)
    meta_ref[...] = meta
    cnt_ref[...] = carry_scr[...]
    xpk_ref[...] = _pack_bf16_pairs(x_ref[...])


def _route(x, rwt, rbc):
    n = x.shape[0]
    tm = _row_tile(n, 512)
    return pl.pallas_call(
        functools.partial(_route_kernel, tm=tm),
        grid=(n // tm,),
        in_specs=[pl.BlockSpec((tm, D_MODEL), lambda i: (i, 0)), _full(rwt.shape), _full(rbc.shape)],
        out_specs=[pl.BlockSpec((M_COLS, tm), lambda i: (0, i)), _full((N_EXPERTS, LANES)),
                   pl.BlockSpec((tm, D_MODEL // 2), lambda i: (i, 0))],
        out_shape=[jax.ShapeDtypeStruct((M_COLS, n), F32), jax.ShapeDtypeStruct((N_EXPERTS, LANES), F32),
                   jax.ShapeDtypeStruct((n, D_MODEL // 2), jnp.int32)],
        scratch_shapes=[pltpu.VMEM((N_EXPERTS, LANES), F32)],
        compiler_params=_cparams(("arbitrary",)),
        name="moe_route",
    )(x, rwt, rbc)


def _sc_chunk(per_worker):
    for c in (64, 48, 32, 16, 8):
        if per_worker % c == 0:
            return c
    raise ValueError(per_worker)


def _sc_mesh():
    return plsc.VectorSubcoreMesh(core_axis_name="c", subcore_axis_name="s")


def _sc_scatter2(x, idx1, idx2, n_out):
    n, d = x.shape
    per_w = n // SC_WORKERS
    assert per_w * SC_WORKERS == n
    chunk = _sc_chunk(per_w)

    @functools.partial(
        pl.kernel, mesh=_sc_mesh(), out_type=jax.ShapeDtypeStruct((n_out, d), x.dtype),
        scratch_types=[pltpu.VMEM((chunk,), jnp.int32), pltpu.VMEM((chunk,), jnp.int32),
                       pltpu.VMEM((chunk, d), x.dtype), pltpu.SemaphoreType.DMA, pltpu.SemaphoreType.DMA])
    def scatter(x_hbm, i1_hbm, i2_hbm, out_hbm, i1_v, i2_v, rows_v, sem1, sem2):
        base = (lax.axis_index("s") * SC_CORES + lax.axis_index("c")) * per_w

        @pl.loop(0, per_w // chunk)
        def _(c):
            off = pl.multiple_of(base + c * chunk, SUBLANES)
            pltpu.sync_copy(i1_hbm.at[pl.ds(off, chunk)], i1_v)
            pltpu.sync_copy(i2_hbm.at[pl.ds(off, chunk)], i2_v)
            pltpu.sync_copy(x_hbm.at[pl.ds(off, chunk)], rows_v)
            first = pltpu.async_copy(rows_v, out_hbm.at[i1_v], sem1)
            second = pltpu.async_copy(rows_v, out_hbm.at[i2_v], sem2)
            first.wait()
            second.wait()

    return scatter(x, idx1, idx2)


def _sc_gather(y, idx):
    n = idx.shape[0]
    d = y.shape[1]
    per_w = n // SC_WORKERS
    assert per_w * SC_WORKERS == n
    chunk = _sc_chunk(per_w)

    @functools.partial(
        pl.kernel, mesh=_sc_mesh(), out_type=jax.ShapeDtypeStruct((n, d), y.dtype),
        scratch_types=[pltpu.VMEM((chunk,), jnp.int32), pltpu.VMEM((chunk, d), y.dtype),
                       pltpu.SemaphoreType.DMA])
    def gather(y_hbm, idx_hbm, out_hbm, idx_v, rows_v, sem):
        base = (lax.axis_index("s") * SC_CORES + lax.axis_index("c")) * per_w

        @pl.loop(0, per_w // chunk)
        def _(c):
            off = pl.multiple_of(base + c * chunk, SUBLANES)
            pltpu.sync_copy(idx_hbm.at[pl.ds(off, chunk)], idx_v)
            pltpu.async_copy(y_hbm.at[idx_v], rows_v, sem).wait()
            pltpu.sync_copy(rows_v, out_hbm.at[pl.ds(off, chunk)])

    return gather(y, idx)


def _experts_kernel(te_ref, used_ref, x_ref, wg_ref, wu_ref, wd_ref, o_ref):
    @pl.when(pl.program_id(0) < used_ref[0])
    def _():
        xb = _unpack_bf16_pairs(x_ref[...]).astype(BF16)
        hg = jnp.dot(xb, wg_ref[0].astype(BF16), preferred_element_type=F32)
        hu = jnp.dot(xb, wu_ref[0].astype(BF16), preferred_element_type=F32)
        h = jax.nn.silu(hg) * hu
        y = jnp.dot(h.astype(BF16), wd_ref[0].astype(BF16), preferred_element_type=F32)
        o_ref[...] = _pack_bf16_pairs(y)


def _experts(xg, tile_expert, n_used, wg, wu, wd):
    n_tiles = xg.shape[0] // MOE_TILE
    row = pl.BlockSpec((MOE_TILE, D_MODEL // 2), lambda i, te, used: (i, 0))
    grid_spec = pltpu.PrefetchScalarGridSpec(
        num_scalar_prefetch=2,
        grid=(n_tiles,),
        in_specs=[row,
                  pl.BlockSpec((1, D_MODEL, D_EXPERT), lambda i, te, used: (te[i], 0, 0)),
                  pl.BlockSpec((1, D_MODEL, D_EXPERT), lambda i, te, used: (te[i], 0, 0)),
                  pl.BlockSpec((1, D_EXPERT, D_MODEL), lambda i, te, used: (te[i], 0, 0))],
        out_specs=row,
    )
    return pl.pallas_call(
        _experts_kernel,
        grid_spec=grid_spec,
        out_shape=jax.ShapeDtypeStruct(xg.shape, jnp.int32),
        compiler_params=_cparams(("arbitrary",)),
        name="moe_experts",
    )(tile_expert, n_used, xg, wg, wu, wd)


def _combine_kernel(x_ref, y1_ref, y2_ref, meta_ref, g_ref, b_ref, o_ref):
    meta = meta_ref[...]
    moe = (meta[:, M_G1:M_G1 + 1] * _unpack_bf16_pairs(y1_ref[...])
           + meta[:, M_G2:M_G2 + 1] * _unpack_bf16_pairs(y2_ref[...]))
    o_ref[...] = _ln(ALPHA * x_ref[...] + moe, g_ref[...], b_ref[...])


def _combine(x, y1, y2, meta, g, b):
    n = x.shape[0]
    tm = _row_tile(n, 512)
    row = pl.BlockSpec((tm, D_MODEL), lambda i: (i, 0))
    half = pl.BlockSpec((tm, D_MODEL // 2), lambda i: (i, 0))
    return pl.pallas_call(
        _combine_kernel,
        grid=(n // tm,),
        in_specs=[row, half, half, pl.BlockSpec((tm, M_COLS), lambda i: (i, 0)), _full(g.shape), _full(b.shape)],
        out_specs=row,
        out_shape=jax.ShapeDtypeStruct((n, D_MODEL), F32),
        compiler_params=_cparams(("arbitrary",)),
        name="moe_combine",
    )(x, y1, y2, meta, g, b)


def _moe_sparse(x, rw, rb, wg, wu, wd, e0, g, b):
    n = x.shape[0]
    meta_t, counts, x_packed = _route(x, rw.T, rb.reshape(N_EXPERTS, 1))
    cnt = counts[:, 0].astype(jnp.int32)
    padded = (cnt + MOE_TILE - 1) // MOE_TILE * MOE_TILE
    seg_end = jnp.cumsum(padded)
    seg_start = seg_end - padded
    experts = jnp.arange(N_EXPERTS, dtype=jnp.int32)[:, None]
    start_of = lambda e: jnp.sum(jnp.where(experts == e[None, :], seg_start[:, None], 0), axis=0)
    e1, e2 = meta_t[M_I1].astype(jnp.int32), meta_t[M_I2].astype(jnp.int32)
    pos1 = start_of(e1) + meta_t[M_R1].astype(jnp.int32)
    pos2 = start_of(e2) + meta_t[M_R2].astype(jnp.int32)
    meta = meta_t.T
    n_tiles = -(-2 * n // MOE_TILE) + N_EXPERTS
    tile_start = jnp.arange(n_tiles, dtype=jnp.int32) * MOE_TILE
    tile_expert = e0 + jnp.minimum(jnp.sum(tile_start[:, None] >= seg_end[None, :], axis=1),
                                   N_EXPERTS - 1).astype(jnp.int32)
    n_used = (seg_end[-1:] // MOE_TILE).astype(jnp.int32)
    xg = _sc_scatter2(x_packed, pos1, pos2, n_tiles * MOE_TILE)
    yg = _experts(xg, tile_expert, n_used, wg, wu, wd)
    return _combine(x, _sc_gather(yg, pos1), _sc_gather(yg, pos2), meta, g, b)


def _head_sum(z, ones):
    hi, lo = _split2(z)
    parts = []
    for g in range(D_MODEL // LANES):
        sl = slice(g * LANES, (g + 1) * LANES)
        parts.append(jnp.dot(hi[:, sl], ones, preferred_element_type=F32)
                     + jnp.dot(lo[:, sl], ones, preferred_element_type=F32))
    return jnp.concatenate(parts, axis=1)


def _rwkv_proj_kernel(x_ref, sh0_ref, mu_ref, wr_ref, wk_ref, wv_ref, w0_ref, w1_ref, w2_ref,
                      a0_ref, a1_ref, a2_ref, g1_ref, g2_ref, kkw_ref, kaw_ref, ones_ref,
                      r_ref, lw_ref, k_ref, v_ref, kk_ref, a_ref, g_ref, prev_scr,
                      *, tm, start, end):
    t = pl.program_id(1)

    @pl.when(t == 0)
    def _():
        prev_scr[...] = jnp.zeros(prev_scr.shape, F32)

    x = x_ref[0]
    ext = jnp.concatenate([prev_scr[...], x], axis=0)
    x_prev = pltpu.roll(ext, 1, 0)[SUBLANES:]
    grow = t * tm + lax.broadcasted_iota(jnp.int32, (tm, D_MODEL), 0)
    x_prev = jnp.where(grow == start, sh0_ref[0], x_prev)
    prev_scr[...] = x[tm - SUBLANES:]
    xx = x_prev - x
    mu = mu_ref[...]
    xr, xw, xk, xv, xa, xg = (x + xx * mu[n:n + 1] for n in range(6))
    r = _bdot(xr, wr_ref[...])
    k = _bdot(xk, wk_ref[...])
    v = _bdot(xv, wv_ref[...])
    log_w = -DECAY_SCALE * jax.nn.sigmoid(w0_ref[...] + _bdot(jnp.tanh(_bdot(xw, w1_ref[...])), w2_ref[...]))
    a = jax.nn.sigmoid(a0_ref[...] + _bdot(_bdot(xa, a1_ref[...]), a2_ref[...]))
    g = _bdot(jax.nn.sigmoid(_bdot(xg, g1_ref[...])), g2_ref[...])
    kk = k * kkw_ref[...]
    norm = jnp.sqrt(_head_sum(kk * kk, ones_ref[...]))
    kk = kk / jnp.maximum(norm, 1e-12)
    k = k * (1.0 + (a - 1.0) * kaw_ref[...])
    live = (grow >= start) & (grow < end)
    r_ref[0] = r.astype(r_ref.dtype)
    lw_ref[0] = jnp.where(live, log_w, 0.0)
    k_ref[0] = jnp.where(live, k, 0.0).astype(k_ref.dtype)
    v_ref[0] = v.astype(v_ref.dtype)
    kk_ref[0] = jnp.where(live, kk, 0.0).astype(kk_ref.dtype)
    a_ref[0] = a.astype(a_ref.dtype)
    g_ref[0] = g.astype(g_ref.dtype)


def _rwkv_proj(x, sh0, od, start, end, act_dtype):
    b, l, _ = x.shape
    tm = _row_tile(l, 256)
    dtypes = [act_dtype, F32] + [act_dtype] * 5
    seq = pl.BlockSpec((1, tm, D_MODEL), lambda i, t: (i, t, 0))
    ws = [od[n] for n in ("mu", "w_r", "w_k", "w_v", "w0", "w1", "w2", "a0", "a1", "a2", "g1", "g2",
                          "k_k", "k_a", "ones")]
    return pl.pallas_call(
        functools.partial(_rwkv_proj_kernel, tm=tm, start=start, end=end),
        grid=(b, l // tm),
        in_specs=[seq, pl.BlockSpec((1, 1, D_MODEL), lambda i, t: (i, 0, 0))] + [_full(w.shape) for w in ws],
        out_specs=[seq] * 7,
        out_shape=[jax.ShapeDtypeStruct((b, l, D_MODEL), dt) for dt in dtypes],
        scratch_shapes=[pltpu.VMEM((SUBLANES, D_MODEL), F32)],
        compiler_params=_cparams(("arbitrary", "arbitrary")),
        name="rwkv_proj",
    )(x, sh0, *ws)


def _wkv_kernel(r_ref, lw_ref, k_ref, v_ref, kk_ref, a_ref, s0_ref, o_ref, sout_ref, s_scr,
                *, c, exact):
    t = pl.program_id(1)

    @pl.when(t == 0)
    def _():
        s_scr[...] = s0_ref[0]

    head0 = lax.broadcasted_iota(jnp.int32, (c, LANES), 1) < RWKV_HEAD
    c2 = 2 * c
    row = lax.broadcasted_iota(jnp.int32, (c2, c2), 0)
    col = lax.broadcasted_iota(jnp.int32, (c2, c2), 1)
    row_hi = jnp.where(row >= c, c, 0)
    col_hi = jnp.where(col >= c, c, 0)
    same = row_hi == col_hi
    rr = row - row_hi
    cc = col - col_hi
    strict = same & (rr > cc)
    incl = same & (rr >= cc)
    eye = (row == col).astype(F32)
    crow = lax.broadcasted_iota(jnp.int32, (c, LANES), 0)

    def cumsum_rows(x):
        d = 1
        while d < c:
            x = x + jnp.where(crow >= d, pltpu.roll(x, d, 0), 0.0)
            d *= 2
        return x

    def stack(x):
        return jnp.concatenate([jnp.where(head0, x, 0.0), jnp.where(head0, 0.0, x)], axis=0)

    mm = functools.partial(_mm, exact=exact)
    pairs = range(RWKV_HEADS // 2)
    for sub in range(r_ref.shape[1] // c):
        rows = slice(sub * c, (sub + 1) * c)
        load = lambda ref: [ref[0, rows, p * LANES:(p + 1) * LANES].astype(F32) for p in pairs]
        r, lw, k, v, kk, a = (load(ref) for ref in (r_ref, lw_ref, k_ref, v_ref, kk_ref, a_ref))
        lc = [cumsum_rows(x) for x in lw]
        lc_end = [x[c - 1:c] for x in lc]
        b = [x * y for x, y in zip(kk, a)]
        lhs = [jnp.concatenate([stack(-kk[p] * jnp.exp(lc[p] - lw[p])), stack(r[p] * jnp.exp(lc[p]))], axis=0)
               for p in pairs]
        g_inv = [jnp.exp(-x) for x in lc]
        rhs = [jnp.concatenate([stack(b[p] * g_inv[p]), stack(k[p] * g_inv[p])], axis=0) for p in pairs]
        pm = [mm(x, y, _NT) for x, y in zip(lhs, rhs)]
        l_ab = [jnp.where(strict, x[:c2, :c2], 0.0) for x in pm]
        l_ak = [jnp.where(strict, x[:c2, c2:], 0.0) for x in pm]
        m_rb = [jnp.where(incl, x[c2:, :c2], 0.0) for x in pm]
        m_rk = [jnp.where(incl, x[c2:, c2:], 0.0) for x in pm]
        vs = [stack(x) for x in v]
        lakv = [mm(x, y, _NN) for x, y in zip(l_ak, vs)]
        mrkv = [mm(x, y, _NN) for x, y in zip(m_rk, vs)]
        tinv = [eye + x for x in l_ab]
        lp = l_ab
        n = 2
        while n < c:
            lp = [mm(x, x, _NN) for x in lp]
            tinv = [x + mm(x, y, _NN) for x, y in zip(tinv, lp)]
            n *= 2
        s = [s_scr[p] for p in pairs]
        xs = [mm(x, y, _NT) for x, y in zip(lhs, s)]
        u = [mm(tinv[p], xs[p][:c2] + lakv[p], _NN) for p in pairs]
        for p in pairs:
            os_ = xs[p][c2:] + mm(m_rb[p], u[p], _NN) + mrkv[p]
            o_ref[0, rows, p * LANES:(p + 1) * LANES] = os_[:c] + os_[c:]
        for p in pairs:
            g_rem = jnp.exp(lc_end[p] - lc[p])
            uv = jnp.concatenate([u[p], vs[p]], axis=0)
            bk = jnp.concatenate([stack(b[p] * g_rem), stack(k[p] * g_rem)], axis=0)
            s_scr[p] = s[p] * jnp.exp(lc_end[p]) + mm(uv, bk, _TN)

    @pl.when(t == pl.num_programs(1) - 1)
    def _():
        sout_ref[0] = s_scr[...]


def _wkv(r, lw, k, v, kk, a, s0, c, exact):
    b, l, _ = r.shape
    rows = 2 * c if l % (2 * c) == 0 else c
    seq = pl.BlockSpec((1, rows, D_MODEL), lambda i, t: (i, t, 0))
    st = pl.BlockSpec((1, RWKV_HEADS // 2, LANES, LANES), lambda i, t: (i, 0, 0, 0))
    return pl.pallas_call(
        functools.partial(_wkv_kernel, c=c, exact=exact),
        grid=(b, l // rows),
        in_specs=[seq] * 6 + [st],
        out_specs=[seq, st],
        out_shape=[jax.ShapeDtypeStruct((b, l, D_MODEL), F32),
                   jax.ShapeDtypeStruct((b, RWKV_HEADS // 2, LANES, LANES), F32)],
        scratch_shapes=[pltpu.VMEM((RWKV_HEADS // 2, LANES, LANES), F32)],
        compiler_params=_cparams(("arbitrary", "arbitrary")),
        name="wkv",
    )(r, lw, k, v, kk, a, s0)


def _rwkv_out_kernel(o_ref, r_ref, k_ref, v_ref, g_ref, x_ref, rk_ref, gng_ref, gnb_ref, wo_ref,
                     ones_ref, lg_ref, lb_ref, y_ref):
    ones = ones_ref[...]
    o = o_ref[...]
    inv = 1.0 / RWKV_HEAD
    mu = _head_sum(o, ones) * inv
    oc = o - mu
    var = _head_sum(oc * oc, ones) * inv
    on = oc * lax.rsqrt(var + GN_EPS) * gng_ref[...] + gnb_ref[...]
    r, k, v, g = (ref[...].astype(F32) for ref in (r_ref, k_ref, v_ref, g_ref))
    on = on + _head_sum(r * k * rk_ref[...], ones) * v
    out = _bdot(on * g, wo_ref[...])
    y_ref[...] = _ln(ALPHA * x_ref[...] + out, lg_ref[...], lb_ref[...])


def _rwkv_out(o, r, k, v, g, x, od, lg, lb):
    n = x.shape[0]
    tm = _row_tile(n, 256)
    row = pl.BlockSpec((tm, D_MODEL), lambda i: (i, 0))
    ws = [od["r_k"], od["ln_g"], od["ln_b"], od["w_o"], od["ones"], lg, lb]
    return pl.pallas_call(
        _rwkv_out_kernel,
        grid=(n // tm,),
        in_specs=[row] * 6 + [_full(w.shape) for w in ws],
        out_specs=row,
        out_shape=jax.ShapeDtypeStruct((n, D_MODEL), F32),
        compiler_params=_cparams(("arbitrary",)),
        name="rwkv_out",
    )(o, r, k, v, g, x, *ws)


def _rope_tables(pos):
    half = D_ROPE // 2
    freq = ROPE_BASE ** (-jnp.arange(half, dtype=F32) / half)
    ang_t = freq[:, None] * pos.astype(F32)[None, :]
    cos_t, sin_t = lax.optimization_barrier((jnp.cos(ang_t), jnp.sin(ang_t)))
    cos, sin = cos_t.T, sin_t.T
    n = pos.shape[0]
    ones = jnp.ones((n, D_NOPE), F32)
    zeros = jnp.zeros((n, D_NOPE), F32)
    z16 = jnp.zeros((n, half), F32)
    tail1 = jnp.ones((n, LANES - D_NOPE - D_ROPE), F32)
    tail0 = jnp.zeros((n, LANES - D_NOPE - D_ROPE), F32)
    c = jnp.concatenate([ones, cos, cos, tail1], axis=1)
    sa = jnp.concatenate([zeros, -sin, z16, tail0], axis=1)
    sb = jnp.concatenate([zeros, z16, sin, tail0], axis=1)
    return c, sa, sb


def _slot_cols(w, width):
    k, h, _ = w.shape
    return jnp.pad(w, ((0, 0), (0, 0), (0, HEAD_SLOT - width))).reshape(k, h * HEAD_SLOT)


def _block_diag(w):
    n, c, d = w.shape
    eye = jnp.eye(n, dtype=w.dtype)
    return (eye[:, None, :, None] * w[:, :, None, :]).reshape(n * c, n * d)


def _row2(v):
    return v.reshape(1, -1).astype(F32)


def _prep_even(w_in, g_q, w_uq, g_kv, w_uk, w_uv, conv_w, conv_b, w_rg, b_rg, w_ig, b_ig, lam, w_out):
    off_ckv, off_kr = D_CQ, D_CQ + D_C
    off_xr = off_kr + D_ROPE
    off_y = off_xr + D_RNN
    kr_cols = jnp.pad(w_in[:, off_kr:off_xr], ((0, 0), (D_NOPE, LANES - D_NOPE - D_ROPE)))
    w_in, w_uq, w_uk, w_uv, w_rg, w_ig, w_out = (
        w.astype(F32) for w in (w_in, w_uq, w_uk, w_uv, w_rg, w_ig, w_out))
    w1 = jnp.concatenate([w_in[:, :off_ckv], w_in[:, off_ckv:off_kr], w_in[:, off_xr:off_y],
                          w_in[:, off_y:], kr_cols], axis=1)
    wuq = _slot_cols(w_uq, D_NOPE + D_ROPE)
    wukv = jnp.concatenate([_slot_cols(w_uk, D_NOPE), _slot_cols(w_uv, D_V)], axis=1)
    place = np.zeros((LANES, D_ATT), np.float32)
    for h in range(MLA_HEADS):
        for cidx in range(D_ROPE):
            place[D_NOPE + cidx, h * HEAD_SLOT + D_NOPE + cidx] = 1.0
    wa = jnp.pad(w_out[:MLA_HEADS * D_V].reshape(MLA_HEADS, D_V, D_MODEL),
                 ((0, 0), (0, HEAD_SLOT - D_V), (0, 0))).reshape(D_ATT, D_MODEL)
    wr = w_out[MLA_HEADS * D_V:]
    keep_rope = np.zeros((HEAD_SLOT, LANES), np.float32)
    keep_rope[D_NOPE:D_NOPE + D_ROPE, D_NOPE:D_NOPE + D_ROPE] = np.eye(D_ROPE, dtype=np.float32)
    uk_t = jnp.pad(jnp.transpose(w_uk, (1, 2, 0)), ((0, 0), (0, HEAD_SLOT - D_NOPE), (0, 0)))
    wabs = jnp.concatenate([uk_t, jnp.broadcast_to(keep_rope, (MLA_HEADS, HEAD_SLOT, LANES))], axis=2)
    wuv = jnp.pad(jnp.transpose(w_uv, (1, 0, 2)), ((0, 0), (0, 0), (0, HEAD_SLOT - D_V)))
    wuvt = jnp.pad(jnp.transpose(w_uv, (1, 2, 0)), ((0, 0), (0, VT_SLOT - D_V), (0, 0))).reshape(D_VT, D_C)
    return dict(
        wabs=wabs, wuv=wuv, wuvt=wuvt,
        w1=w1, gq=_row2(g_q), wuq=wuq, gkv=_row2(g_kv), wukv=wukv, place=jnp.asarray(place, F32),
        cw=conv_w.astype(F32), cb=_row2(conv_b), wrg=_block_diag(w_rg), brg=_row2(b_rg),
        wig=_block_diag(w_ig), big=_row2(b_ig), sp=_row2(jax.nn.softplus(-lam.astype(F32))),
        wa=wa, wr=wr)


_EVEN_MATMUL_WEIGHTS = ("w1", "wuq", "wukv", "wuvt", "place", "wrg", "wig", "wa", "wr")


def _single_pass(ev):
    return {n: (w.astype(BF16) if n in _EVEN_MATMUL_WEIGHTS else w) for n, w in ev.items()}


def _prep_odd(mu, w_r, w_k, w_v, w0, w1, w2, a0, a1, a2, g1, g2, k_k, k_a, r_k, ln_g, ln_b, w_o):
    ones = np.zeros((LANES, LANES), np.float32)
    ones[:RWKV_HEAD, :RWKV_HEAD] = 1.0
    ones[RWKV_HEAD:, RWKV_HEAD:] = 1.0
    return dict(
        mu=jnp.pad(mu.astype(F32), ((0, SUBLANES - mu.shape[0]), (0, 0))),
        w_r=w_r.astype(BF16), w_k=w_k.astype(BF16), w_v=w_v.astype(BF16), w0=_row2(w0),
        w1=w1.astype(BF16), w2=w2.astype(BF16), a0=_row2(a0), a1=a1.astype(BF16), a2=a2.astype(BF16),
        g1=g1.astype(BF16), g2=g2.astype(BF16), k_k=_row2(k_k), k_a=_row2(k_a), r_k=_row2(r_k),
        ln_g=_row2(ln_g), ln_b=_row2(ln_b), w_o=w_o.astype(BF16), ones=jnp.asarray(ones, BF16))


def _pair_states(s):
    b = s.shape[0]
    s = s.reshape(b, RWKV_HEADS // 2, 2, RWKV_HEAD, RWKV_HEAD).astype(F32)
    eye = jnp.eye(2, dtype=F32)
    out = s[:, :, :, :, None, :] * eye[None, None, :, None, :, None]
    return out.reshape(b, RWKV_HEADS // 2, LANES, LANES)


def _unpair_states(s):
    b = s.shape[0]
    s = s.reshape(b, RWKV_HEADS // 2, 2, RWKV_HEAD, 2, RWKV_HEAD)
    return jnp.stack([s[:, :, 0, :, 0, :], s[:, :, 1, :, 1, :]], axis=2).reshape(
        b, RWKV_HEADS, RWKV_HEAD, RWKV_HEAD)


def _round_up(n, m):
    return -(-n // m) * m


def kernel(x_prompt, x_sample, cache_ckv, cache_krope, state_conv, state_lru, state_shift, state_wkv,
           meta_tokens, ev_w_in, ev_g_q, ev_w_uq, ev_g_kv, ev_w_uk, ev_w_uv, ev_conv_w, ev_conv_b,
           ev_w_rg, ev_b_rg, ev_w_ig, ev_b_ig, ev_lru_lambda, ev_w_out, od_mu, od_w_r, od_w_k, od_w_v,
           od_w0, od_w1, od_w2, od_a0, od_a1, od_a2, od_g1, od_g2, od_k_k, od_k_a, od_r_k, od_ln_g,
           od_ln_b, od_w_o, ln_g, ln_b, router_w, router_b, exp_w_gate, exp_w_up, exp_w_down):
    assert x_prompt.shape[0] == 1 and x_prompt.shape[2] == D_MODEL
    seq = x_prompt.shape[1]
    assert seq % CHUNK == 0
    bs, ls, _ = x_sample.shape
    past = cache_ckv.shape[2]
    ns = bs * ls
    end = ROW0 + seq
    tp = _round_up(end, 512)

    ev = _prep_even(ev_w_in[0], ev_g_q[0], ev_w_uq[0], ev_g_kv[0], ev_w_uk[0], ev_w_uv[0], ev_conv_w[0],
                    ev_conv_b[0], ev_w_rg[0], ev_b_rg[0], ev_w_ig[0], ev_b_ig[0], ev_lru_lambda[0],
                    ev_w_out[0])
    od = _prep_odd(od_mu[0], od_w_r[0], od_w_k[0], od_w_v[0], od_w0[0], od_w1[0], od_w2[0], od_a0[0],
                   od_a1[0], od_a2[0], od_g1[0], od_g2[0], od_k_k[0], od_k_a[0], od_r_k[0], od_ln_g[0],
                   od_ln_b[0], od_w_o[0])
    rw = router_w.astype(F32)
    rb = _row2(router_b)
    wg, wu, wd = (w.reshape((DEPTH * N_EXPERTS,) + w.shape[2:]) for w in (exp_w_gate, exp_w_up, exp_w_down))
    lng = ln_g.astype(F32)[:, :, None, :]
    lnb = ln_b.astype(F32)[:, :, None, :]

    def moe(x, layer):
        sparse = x.shape[0] % (SC_WORKERS * SUBLANES) == 0 and x.shape[0] >= SPARSE_MIN_ROWS
        fn = _moe_sparse if sparse else _moe
        return fn(x, rw, rb, wg, wu, wd, layer * N_EXPERTS, lng[layer, 1], lnb[layer, 1])

    xs = x_sample.reshape(ns, D_MODEL).astype(F32)
    pos_s = jnp.tile(N_META + past + jnp.arange(ls), bs)
    q_s, ckv_s, kr_s, xr_s, yg_s = _even_proj(xs, ev["w1"], ev["gq"], ev["wuq"], ev["gkv"], _rope_tables(pos_s),
                                              MLA_SCALE)
    _, ckv_m, kr_m, _, _ = _even_proj(meta_tokens.astype(F32), ev["w1"], ev["gq"], ev["wuq"], ev["gkv"],
                                      _rope_tables(jnp.arange(N_META)), MLA_SCALE)
    n_keys = N_META + past + ls
    nk_pad = _round_up(n_keys, LANES)
    meta_ckv = jnp.broadcast_to(ckv_m[None], (bs, N_META, D_C))
    meta_kr = jnp.broadcast_to(kr_m[None], (bs, N_META, LANES))
    cache_kr = jnp.pad(cache_krope[0].astype(F32), ((0, 0), (0, 0), (D_NOPE, LANES - D_NOPE - D_ROPE)))
    all_ckv = jnp.concatenate([meta_ckv, cache_ckv[0].astype(F32), ckv_s.reshape(bs, ls, D_C),
                               jnp.zeros((bs, nk_pad - n_keys, D_C), F32)], axis=1)
    all_kr = jnp.concatenate([meta_kr, cache_kr, kr_s.reshape(bs, ls, LANES),
                              jnp.zeros((bs, nk_pad - n_keys, LANES), F32)], axis=1)
    attn_s = _sample_attention(q_s.reshape(bs, ls, D_ATT), jnp.concatenate([all_ckv, all_kr], axis=2),
                               ev["wabs"], ev["wuv"], n_keys)
    buf0_s = jnp.pad(state_conv[0].astype(F32), ((0, 0), (SUBLANES - (CONV_W - 1), 0), (0, 0)))
    rnn_s, tailx_s, tailh_s = _rglru(
        xr_s.reshape(bs, ls, D_RNN), yg_s.reshape(bs, ls, D_RNN), ev["cw"], ev["cb"], ev["wrg"], ev["brg"],
        ev["wig"], ev["big"], ev["sp"], buf0_s, state_lru[0].astype(F32)[:, None, :], 0, ls)
    x1_s = _mix_out(attn_s.reshape(ns, D_ATT), rnn_s.reshape(ns, D_RNN), xs, ev["wa"], ev["wr"],
                    lng[0, 0], lnb[0, 0])
    x2_s = moe(x1_s, 0)

    x2_s3 = x2_s.reshape(bs, ls, D_MODEL)
    r_s, lw_s, kk_in_s, v1_s, kkn_s, a_s, g_s = _rwkv_proj(
        x2_s3, state_shift[0].astype(F32)[:, None, :], od, 0, ls, F32)
    to_chunk = lambda z: jnp.pad(z, ((0, 0), (0, _round_up(ls, CHUNK) - ls), (0, 0)))
    o_s, s_s = _wkv(*(to_chunk(z) for z in (r_s, lw_s, kk_in_s, v1_s, kkn_s, a_s)),
                    _pair_states(state_wkv[0]), CHUNK, True)
    o_s = o_s[:, :ls]
    flat = lambda z: z.reshape(ns, D_MODEL)
    x3_s = _rwkv_out(flat(o_s), flat(r_s), flat(kk_in_s), flat(v1_s), flat(g_s), x2_s, od,
                     lng[1, 0], lnb[1, 0])
    x4_s = moe(x3_s, 1)

    xp = jnp.concatenate([jnp.zeros((PAD_FRONT, D_MODEL), F32), meta_tokens.astype(F32),
                          x_prompt[0].astype(F32), jnp.zeros((tp - end, D_MODEL), F32)], axis=0)
    tabs_p = _rope_tables(jnp.maximum(jnp.arange(tp) - PAD_FRONT, 0))
    evb = _single_pass(ev)
    q_p, ckv_p, kr_p, xr_p, yg_p = _even_proj(xp, evb["w1"], ev["gq"], evb["wuq"], ev["gkv"], tabs_p,
                                              MLA_SCALE * LOG2E)
    ones_col = np.zeros((D_VT, 1), np.float32)
    ones_col[D_V::VT_SLOT] = 1.0
    k_p, vt_p = _kv_proj_t(ckv_p, kr_p, evb["wukv"][:, :D_ATT], evb["place"], evb["wuvt"],
                           jnp.asarray(ones_col), _round_up(tp, FLASH_TK))
    attn_p = _flash_attention(q_p, k_p, vt_p, FLASH_TQ, FLASH_TK)
    rnn_p, tailx_p, tailh_p = _rglru(
        xr_p[None], yg_p[None], ev["cw"], ev["cb"], evb["wrg"], ev["brg"], evb["wig"], ev["big"], ev["sp"],
        jnp.zeros((1, SUBLANES, D_RNN), F32), jnp.zeros((1, 1, D_RNN), F32), PAD_FRONT, end)
    x1_p = _mix_out(attn_p, rnn_p[0], xp, evb["wa"], evb["wr"], lng[0, 0], lnb[0, 0])
    x2_p = moe(x1_p, 0)

    r_p, lw_p, kk_in_p, v1_p, kkn_p, a_p, g_p = _rwkv_proj(
        x2_p[None], jnp.zeros((1, 1, D_MODEL), F32), od, PAD_FRONT, end, BF16)
    o_p, s_p = _wkv(r_p, lw_p, kk_in_p, v1_p, kkn_p, a_p,
                    jnp.zeros((1, RWKV_HEADS // 2, LANES, LANES), F32), CHUNK, False)
    x3_p = _rwkv_out(o_p[0], r_p[0], kk_in_p[0], v1_p[0], g_p[0], x2_p, od, lng[1, 0], lnb[1, 0])
    x4_p = moe(x3_p, 1)

    dt = x_prompt.dtype
    nb = CONV_W - 1
    return (
        x4_p[ROW0:end][None].astype(dt),
        x4_s.reshape(bs, ls, D_MODEL).astype(dt),
        ckv_p[PAD_FRONT:end][None, None].astype(dt),
        kr_p[PAD_FRONT:end, D_NOPE:D_NOPE + D_ROPE][None, None].astype(dt),
        tailx_p[:, SUBLANES - nb:][None].astype(dt),
        tailh_p[:, SUBLANES - 1][None].astype(dt),
        x2_p[end - 1][None, None].astype(dt),
        _unpair_states(s_p)[None].astype(dt),
        ckv_s.reshape(bs, ls, D_C)[None].astype(dt),
        kr_s.reshape(bs, ls, LANES)[:, :, D_NOPE:D_NOPE + D_ROPE][None].astype(dt),
        tailx_s[:, SUBLANES - nb:][None].astype(dt),
        tailh_s[:, SUBLANES - 1][None].astype(dt),
        x2_s3[:, ls - 1][None].astype(dt),
        _unpair_states(s_s)[None].astype(dt),
    )
```

```python
import functools

import numpy as np
import jax
import jax.numpy as jnp
from jax import lax
from jax.experimental import pallas as pl
from jax.experimental.pallas import tpu as pltpu
from jax.experimental.pallas import tpu_sc as plsc

F32 = jnp.float32
BF16 = jnp.bfloat16

D_MODEL = 1024
N_META = 16
CHUNK = 64
CHUNK_SHIFT = 6
LN_EPS = 1e-5
RMS_EPS = 1e-6
DEPTH = 2
ALPHA = (2 * DEPTH) ** 0.25
MLA_HEADS = 8
D_NOPE = 64
D_ROPE = 32
D_V = 64
D_C = 256
D_CQ = 384
ROPE_BASE = 10000.0
MLA_SCALE = (D_NOPE + D_ROPE) ** -0.5
D_RNN = 512
LRU_BLOCKS = 8
LRU_BLOCK_W = D_RNN // LRU_BLOCKS
CONV_W = 4
LRU_C = 8.0
RWKV_HEAD = 64
RWKV_HEADS = D_MODEL // RWKV_HEAD
DECAY_SCALE = float(np.exp(-0.5))
GN_EPS = 64e-5
N_EXPERTS = 16
N_GROUPS = 4
EXPERTS_PER_GROUP = N_EXPERTS // N_GROUPS
D_EXPERT = 512

LANES = 128
SUBLANES = 8
HEAD_SLOT = LANES
D_ATT = MLA_HEADS * HEAD_SLOT
PAD_FRONT = CHUNK - N_META
ROW0 = PAD_FRONT + N_META
NEG = -1e30
LOG2E = 1.4426950408889634
SC_CORES = 2
SC_SUBCORES = 16
SC_WORKERS = SC_CORES * SC_SUBCORES
MOE_TILE = 512
FLASH_TQ = 512
FLASH_TK = 1024
SPARSE_MIN_ROWS = 1024
VMEM_LIMIT = 56 * 1024 * 1024

C_CQ = 0
C_CKV = D_CQ
C_XR = C_CKV + D_C
C_YG = C_XR + D_RNN
C_KR = C_YG + D_RNN
N_COL = C_KR + LANES


def _cparams(sem):
    return pltpu.CompilerParams(dimension_semantics=sem, vmem_limit_bytes=VMEM_LIMIT)


def _row_tile(n, cap):
    for t in (1024, 512, 256, 128, 64, 32, 16, 8):
        if t <= cap and n % t == 0:
            return t
    return n


def _full(shape):
    zeros = (0,) * len(shape)
    return pl.BlockSpec(shape, lambda *_: zeros)


def _ln(x, g, b):
    mu = jnp.mean(x, axis=-1, keepdims=True)
    xc = x - mu
    var = jnp.mean(xc * xc, axis=-1, keepdims=True)
    return xc * lax.rsqrt(var + LN_EPS) * g + b


def _bdot(a, b):
    return jnp.dot(a.astype(BF16), b.astype(BF16), preferred_element_type=F32)


def _split2(x):
    hi = x.astype(BF16)
    return hi, (x - hi.astype(F32)).astype(BF16)


_NN = ((1,), (0,))
_NT = ((1,), (1,))
_TN = ((0,), (0,))


def _mm(a, b, dims, exact):
    dn = (dims, ((), ()))
    if not exact:
        return lax.dot_general(a.astype(BF16), b.astype(BF16), dn, preferred_element_type=F32)
    ah, al = _split2(a)
    bh, bl = _split2(b)
    return (lax.dot_general(ah, bh, dn, preferred_element_type=F32)
            + lax.dot_general(al, bh, dn, preferred_element_type=F32)
            + lax.dot_general(ah, bl, dn, preferred_element_type=F32))


def _wdot(a, w):
    return _mm(a, w, _NN, exact=(w.dtype == F32))


def _act_dtype(w):
    return F32 if w.dtype == F32 else BF16


def _rope_slot(x, c, sa, sb):
    return x * c + pltpu.roll(x, LANES - D_ROPE // 2, 1) * sa + pltpu.roll(x, D_ROPE // 2, 1) * sb


def _even_proj_kernel(x_ref, w1_ref, gq_ref, wuq_ref, gkv_ref, c_ref, sa_ref, sb_ref,
                      q_ref, ckv_ref, kr_ref, xr_ref, yg_ref, *, q_scale):
    u = _wdot(x_ref[...], w1_ref[...])
    cq = u[:, C_CQ:C_CQ + D_CQ]
    cq = cq * lax.rsqrt(jnp.mean(cq * cq, axis=-1, keepdims=True) + RMS_EPS) * gq_ref[...]
    q = _wdot(cq, wuq_ref[...])
    c, sa, sb = c_ref[...], sa_ref[...], sb_ref[...]
    for h in range(MLA_HEADS):
        sl = slice(h * HEAD_SLOT, (h + 1) * HEAD_SLOT)
        q_ref[:, sl] = (_rope_slot(q[:, sl], c, sa, sb) * q_scale).astype(q_ref.dtype)
    ckv = u[:, C_CKV:C_CKV + D_C]
    ckv_ref[...] = ckv * lax.rsqrt(jnp.mean(ckv * ckv, axis=-1, keepdims=True) + RMS_EPS) * gkv_ref[...]
    kr_ref[...] = _rope_slot(u[:, C_KR:C_KR + LANES], c, sa, sb)
    xr_ref[...] = u[:, C_XR:C_XR + D_RNN]
    yg_ref[...] = u[:, C_YG:C_YG + D_RNN]


def _even_proj(x, w1, gq, wuq, gkv, tabs, q_scale):
    n = x.shape[0]
    tm = _row_tile(n, 512)
    row = lambda w: pl.BlockSpec((tm, w), lambda i: (i, 0))
    c, sa, sb = tabs
    return pl.pallas_call(
        functools.partial(_even_proj_kernel, q_scale=q_scale),
        grid=(n // tm,),
        in_specs=[row(D_MODEL), _full(w1.shape), _full(gq.shape), _full(wuq.shape), _full(gkv.shape),
                  row(LANES), row(LANES), row(LANES)],
        out_specs=[row(D_ATT), row(D_C), row(LANES), row(D_RNN), row(D_RNN)],
        out_shape=[jax.ShapeDtypeStruct((n, D_ATT), _act_dtype(w1)), jax.ShapeDtypeStruct((n, D_C), F32),
                   jax.ShapeDtypeStruct((n, LANES), F32), jax.ShapeDtypeStruct((n, D_RNN), F32),
                   jax.ShapeDtypeStruct((n, D_RNN), F32)],
        compiler_params=_cparams(("arbitrary",)),
        name="even_proj",
    )(x, w1, gq, wuq, gkv, c, sa, sb)


def _kv_proj_t_kernel(ckv_ref, kr_ref, wuk_ref, p_ref, wuvt_ref, ones_ref, k_ref, vt_ref):
    ckv = ckv_ref[...].astype(BF16)
    k = jnp.dot(ckv, wuk_ref[...], preferred_element_type=F32) + _bdot(kr_ref[...], p_ref[...])
    k_ref[...] = k.astype(BF16)
    vt = lax.dot_general(wuvt_ref[...], ckv, (_NT, ((), ())), preferred_element_type=F32)
    vt_ref[...] = (vt + ones_ref[...]).astype(BF16)


def _kv_proj_t(ckv, kr, wuk, place, wuvt, ones_col, n_out):
    n = ckv.shape[0]
    tm = _row_tile(n, 512)
    assert n_out % tm == 0
    last = n // tm - 1
    row_in = lambda w: pl.BlockSpec((tm, w), lambda i: (jnp.minimum(i, last), 0))
    return pl.pallas_call(
        _kv_proj_t_kernel,
        grid=(n_out // tm,),
        in_specs=[row_in(D_C), row_in(LANES), _full(wuk.shape), _full(place.shape), _full(wuvt.shape),
                  _full(ones_col.shape)],
        out_specs=[pl.BlockSpec((tm, D_ATT), lambda i: (i, 0)), pl.BlockSpec((D_ATT, tm), lambda i: (0, i))],
        out_shape=[jax.ShapeDtypeStruct((n_out, D_ATT), BF16), jax.ShapeDtypeStruct((D_ATT, n_out), BF16)],
        compiler_params=_cparams(("arbitrary",)),
        name="kv_proj_t",
    )(ckv, kr, wuk, place, wuvt, ones_col)


def _flash_kernel(qi_ref, kj_ref, last_ref, q_ref, k_ref, vt_ref, o_ref, m_scr, acc_scr, *, tq, tk):
    step = pl.program_id(0)
    i = qi_ref[step]
    j = kj_ref[step]

    @pl.when(j == 0)
    def _():
        m_scr[...] = jnp.full(m_scr.shape, NEG, F32)
        acc_scr[...] = jnp.zeros(acc_scr.shape, F32)

    heads = range(MLA_HEADS)
    slots = [slice(h * HEAD_SLOT, (h + 1) * HEAD_SLOT) for h in heads]

    def accumulate(masked):
        st = [lax.dot_general(k_ref[:, sl], q_ref[:, sl], (_NT, ((), ())), preferred_element_type=F32)
              for sl in slots]
        if masked:
            krow = j * tk + lax.broadcasted_iota(jnp.int32, (tk, tq), 0)
            qrow = i * tq + lax.broadcasted_iota(jnp.int32, (tk, tq), 1)
            keep = ((((qrow - ROW0) >> CHUNK_SHIFT) >= ((krow - ROW0) >> CHUNK_SHIFT))
                    & (krow >= PAD_FRONT))
            st = [jnp.where(keep, x, NEG) for x in st]
        m_prev = [m_scr[h:h + 1, :] for h in heads]
        m_new = [jnp.maximum(mp, jnp.max(x, axis=0, keepdims=True)) for mp, x in zip(m_prev, st)]
        alpha = [jnp.exp2(mp - mn) for mp, mn in zip(m_prev, m_new)]
        pt = [jnp.exp2(x - mn).astype(BF16) for x, mn in zip(st, m_new)]
        pv = [jnp.dot(vt_ref[sl, :], x, preferred_element_type=F32) for x, sl in zip(pt, slots)]
        for h in heads:
            acc_scr[slots[h], :] = alpha[h] * acc_scr[slots[h], :] + pv[h]
            m_scr[h:h + 1, :] = m_new[h]

    last = last_ref[step] == 1
    edge = last | (j == 0)

    @pl.when(edge)
    def _():
        accumulate(True)

    @pl.when(jnp.logical_not(edge))
    def _():
        accumulate(False)

    @pl.when(last)
    def _():
        for sl in slots:
            acc = acc_scr[sl, :]
            o_ref[:, sl] = (acc / acc[D_V:D_V + 1, :]).T.astype(BF16)


def _flash_attention(q, k, vt, tq, tk):
    n = q.shape[0]
    nq = n // tq
    n_kblocks = [-(-(i + 1) * tq // tk) for i in range(nq)]
    assert k.shape[0] >= n_kblocks[-1] * tk
    qi = np.concatenate([np.full(c, i, np.int32) for i, c in enumerate(n_kblocks)])
    kj = np.concatenate([np.arange(c, dtype=np.int32) for c in n_kblocks])
    last = np.concatenate([np.arange(c, dtype=np.int32) == c - 1 for c in n_kblocks]).astype(np.int32)
    grid_spec = pltpu.PrefetchScalarGridSpec(
        num_scalar_prefetch=3,
        grid=(len(qi),),
        in_specs=[pl.BlockSpec((tq, D_ATT), lambda s, qi, kj, last: (qi[s], 0)),
                  pl.BlockSpec((tk, D_ATT), lambda s, qi, kj, last: (kj[s], 0)),
                  pl.BlockSpec((D_ATT, tk), lambda s, qi, kj, last: (0, kj[s]))],
        out_specs=pl.BlockSpec((tq, D_ATT), lambda s, qi, kj, last: (qi[s], 0)),
        scratch_shapes=[pltpu.VMEM((MLA_HEADS, tq), F32), pltpu.VMEM((D_ATT, tq), F32)],
    )
    return pl.pallas_call(
        functools.partial(_flash_kernel, tq=tq, tk=tk),
        grid_spec=grid_spec,
        out_shape=jax.ShapeDtypeStruct((n, D_ATT), BF16),
        compiler_params=_cparams(("arbitrary",)),
        name="flash_attention",
    )(jnp.asarray(qi), jnp.asarray(kj), jnp.asarray(last), q, k, vt)


def _sample_attn_kernel(q_ref, kc_ref, wabs_ref, wuv_ref, o_ref, *, n_keys):
    kc = kc_ref[0]
    l = q_ref.shape[1]
    slots = [slice(h * HEAD_SLOT, (h + 1) * HEAD_SLOT) for h in range(MLA_HEADS)]
    qa = jnp.concatenate([_mm(q_ref[0, :, sl], wabs_ref[h], _NN, True) for h, sl in enumerate(slots)],
                         axis=0)
    keep = lax.broadcasted_iota(jnp.int32, (MLA_HEADS * l, kc.shape[0]), 1) < n_keys
    s = jnp.where(keep, _mm(qa, kc, _NT, True), NEG)
    p = jnp.exp(s - jnp.max(s, axis=-1, keepdims=True))
    p = p / jnp.sum(p, axis=-1, keepdims=True)
    pc = _mm(p, kc[:, :D_C], _NN, True)
    for h, sl in enumerate(slots):
        o_ref[0, :, sl] = _mm(pc[h * l:(h + 1) * l], wuv_ref[h], _NN, True)


def _sample_attention(q, kc, wabs, wuv, n_keys):
    b, l, _ = q.shape
    nk, dk = kc.shape[1:]
    return pl.pallas_call(
        functools.partial(_sample_attn_kernel, n_keys=n_keys),
        grid=(b,),
        in_specs=[pl.BlockSpec((1, l, D_ATT), lambda i: (i, 0, 0)),
                  pl.BlockSpec((1, nk, dk), lambda i: (i, 0, 0)),
                  _full(wabs.shape), _full(wuv.shape)],
        out_specs=pl.BlockSpec((1, l, D_ATT), lambda i: (i, 0, 0)),
        out_shape=jax.ShapeDtypeStruct((b, l, D_ATT), F32),
        compiler_params=_cparams(("arbitrary",)),
        name="sample_attention",
    )(q, kc, wabs, wuv)


def _expm1(x):
    series = x * (1.0 + x * (0.5 + x * (1.0 / 6.0 + x * (1.0 / 24.0 + x * (1.0 / 120.0)))))
    return jnp.where(jnp.abs(x) < 0.05, series, jnp.exp(x) - 1.0)


def _gelu_tanh(x):
    return 0.5 * x * (1.0 + jnp.tanh(0.7978845608028654 * (x + 0.044715 * x * x * x)))


def _rglru_kernel(xr_ref, yg_ref, cw_ref, cb_ref, wrg_ref, brg_ref, wig_ref, big_ref, sp_ref,
                  buf0_ref, h0_ref, rnn_ref, tailx_ref, tailh_ref, prev_scr, h_scr,
                  *, tm, start, end):
    t = pl.program_id(1)

    @pl.when(t == 0)
    def _():
        prev_scr[...] = buf0_ref[0]
        h_scr[...] = jnp.broadcast_to(h0_ref[0], h_scr.shape)

    x = xr_ref[0]
    ext = jnp.concatenate([prev_scr[...], x], axis=0)
    cw = cw_ref[...]
    xc = cb_ref[...] + cw[CONV_W - 1:CONV_W] * x
    for d in range(1, CONV_W):
        xc = xc + cw[CONV_W - 1 - d:CONV_W - d] * pltpu.roll(ext, d, 0)[SUBLANES:]
    prev_scr[...] = x[tm - SUBLANES:]

    r = jax.nn.sigmoid(_wdot(xc, wrg_ref[...]) + brg_ref[...])
    ig = jax.nn.sigmoid(_wdot(xc, wig_ref[...]) + big_ref[...])
    log_a = -LRU_C * r * sp_ref[...]
    a = jnp.exp(log_a)
    b = jnp.sqrt(-_expm1(2.0 * log_a)) * (ig * xc)
    row = lax.broadcasted_iota(jnp.int32, (tm, D_RNN), 0)
    if start > 0:
        live = (t * tm + row) >= start
        a = jnp.where(live, a, 1.0)
        b = jnp.where(live, b, 0.0)
    d = 1
    while d < tm:
        b = a * jnp.where(row >= d, pltpu.roll(b, d, 0), 0.0) + b
        a = a * jnp.where(row >= d, pltpu.roll(a, d, 0), 1.0)
        d *= 2
    h = a * h_scr[0:1] + b
    h_scr[...] = jnp.broadcast_to(h[tm - 1:tm], h_scr.shape)
    rnn_ref[0] = (h * _gelu_tanh(yg_ref[0])).astype(rnn_ref.dtype)

    t_end = (end - 1) // tm
    el = end - t_end * tm

    @pl.when(t == t_end)
    def _():
        tailx_ref[0] = ext[el:el + SUBLANES]
        tailh_ref[0] = h[el - SUBLANES:el]


def _rglru(xr, yg, cw, cb, wrg, brg, wig, big, sp, buf0, h0, start, end):
    b, l, _ = xr.shape
    tm = _row_tile(l, 512)
    seq = pl.BlockSpec((1, tm, D_RNN), lambda i, t: (i, t, 0))
    per_b = lambda r: pl.BlockSpec((1, r, D_RNN), lambda i, t: (i, 0, 0))
    return pl.pallas_call(
        functools.partial(_rglru_kernel, tm=tm, start=start, end=end),
        grid=(b, l // tm),
        in_specs=[seq, seq, _full(cw.shape), _full(cb.shape), _full(wrg.shape), _full(brg.shape),
                  _full(wig.shape), _full(big.shape), _full(sp.shape), per_b(SUBLANES), per_b(1)],
        out_specs=[seq, per_b(SUBLANES), per_b(SUBLANES)],
        out_shape=[jax.ShapeDtypeStruct((b, l, D_RNN), _act_dtype(wrg)),
                   jax.ShapeDtypeStruct((b, SUBLANES, D_RNN), F32),
                   jax.ShapeDtypeStruct((b, SUBLANES, D_RNN), F32)],
        scratch_shapes=[pltpu.VMEM((SUBLANES, D_RNN), F32), pltpu.VMEM((SUBLANES, D_RNN), F32)],
        compiler_params=_cparams(("arbitrary", "arbitrary")),
        name="rglru",
    )(xr, yg, cw, cb, wrg, brg, wig, big, sp, buf0, h0)


def _mix_out_kernel(attn_ref, rnn_ref, x_ref, wa_ref, wr_ref, g_ref, b_ref, o_ref):
    mix = _wdot(attn_ref[...], wa_ref[...]) + _wdot(rnn_ref[...], wr_ref[...])
    o_ref[...] = _ln(ALPHA * x_ref[...] + mix, g_ref[...], b_ref[...])


def _mix_out(attn, rnn, x, wa, wr, g, b):
    n = x.shape[0]
    tm = _row_tile(n, 512)
    row = lambda w: pl.BlockSpec((tm, w), lambda i: (i, 0))
    return pl.pallas_call(
        _mix_out_kernel,
        grid=(n // tm,),
        in_specs=[row(D_ATT), row(D_RNN), row(D_MODEL), _full(wa.shape), _full(wr.shape),
                  _full(g.shape), _full(b.shape)],
        out_specs=row(D_MODEL),
        out_shape=jax.ShapeDtypeStruct((n, D_MODEL), F32),
        compiler_params=_cparams(("arbitrary",)),
        name="mix_out",
    )(attn, rnn, x, wa, wr, g, b)


def _first_argmax(vals, lane):
    m = jnp.max(vals, axis=-1, keepdims=True)
    idx = jnp.min(jnp.where(vals == m, lane, N_EXPERTS), axis=-1, keepdims=True)
    return m, idx


def _router_top2(x, rw, rb):
    logits = _mm(x, rw, _NN, True)
    s = jax.nn.sigmoid(logits)
    sel = s + rb
    lane = lax.broadcasted_iota(jnp.int32, sel.shape, 1)
    grp = lane >> 2
    best = None
    g_best = None
    for g in range(N_GROUPS):
        vals = jnp.where(grp == g, sel, NEG)
        m1, i1 = _first_argmax(vals, lane)
        m2, _ = _first_argmax(jnp.where(lane == i1, NEG, vals), lane)
        score = m1 + m2
        if g == 0:
            best, g_best = score, jnp.zeros_like(i1)
        else:
            upd = score > best
            g_best = jnp.where(upd, g, g_best)
            best = jnp.where(upd, score, best)
    vals = jnp.where(grp == g_best, sel, NEG)
    _, i1 = _first_argmax(vals, lane)
    _, i2 = _first_argmax(jnp.where(lane == i1, NEG, vals), lane)
    w1 = jnp.sum(jnp.where(lane == i1, s, 0.0), axis=-1, keepdims=True)
    w2 = jnp.sum(jnp.where(lane == i2, s, 0.0), axis=-1, keepdims=True)
    den = w1 + w2
    return lane, i1, i2, w1 / den, w2 / den


def _router_gate(x, rw, rb):
    lane, i1, i2, g1, g2 = _router_top2(x, rw, rb)
    return jnp.where(lane == i1, g1, 0.0) + jnp.where(lane == i2, g2, 0.0)


def _moe_kernel(x_ref, rw_ref, rb_ref, wg_ref, wu_ref, wd_ref, g_ref, b_ref, o_ref,
                gate_scr, xb_scr, acc_scr):
    e = pl.program_id(1)

    @pl.when(e == 0)
    def _():
        x = x_ref[...]
        gate = _router_gate(x, rw_ref[...], rb_ref[...])
        for k in range(N_EXPERTS):
            gate_scr[k] = jnp.broadcast_to(gate[:, k:k + 1], gate_scr.shape[1:])
        xb_scr[...] = x.astype(BF16)
        acc_scr[...] = jnp.zeros(acc_scr.shape, F32)

    xb = xb_scr[...]
    hg = jnp.dot(xb, wg_ref[0].astype(BF16), preferred_element_type=F32)
    hu = jnp.dot(xb, wu_ref[0].astype(BF16), preferred_element_type=F32)
    gate_e = gate_scr[e]
    h = jax.nn.silu(hg) * hu * jnp.concatenate([gate_e] * (D_EXPERT // LANES), axis=1)
    acc_scr[...] += jnp.dot(h.astype(BF16), wd_ref[0].astype(BF16), preferred_element_type=F32)

    @pl.when(e == N_EXPERTS - 1)
    def _():
        o_ref[...] = _ln(ALPHA * x_ref[...] + acc_scr[...], g_ref[...], b_ref[...])


def _moe(x, rw, rb, wg, wu, wd, e0, g, b):
    n = x.shape[0]
    tm = _row_tile(n, 512)
    row = pl.BlockSpec((tm, D_MODEL), lambda i, e: (i, 0))
    return pl.pallas_call(
        _moe_kernel,
        grid=(n // tm, N_EXPERTS),
        in_specs=[row, _full(rw.shape), _full(rb.shape),
                  pl.BlockSpec((1, D_MODEL, D_EXPERT), lambda i, e: (e0 + e, 0, 0)),
                  pl.BlockSpec((1, D_MODEL, D_EXPERT), lambda i, e: (e0 + e, 0, 0)),
                  pl.BlockSpec((1, D_EXPERT, D_MODEL), lambda i, e: (e0 + e, 0, 0)),
                  _full(g.shape), _full(b.shape)],
        out_specs=row,
        out_shape=jax.ShapeDtypeStruct((n, D_MODEL), F32),
        scratch_shapes=[pltpu.VMEM((N_EXPERTS, tm, LANES), F32), pltpu.VMEM((tm, D_MODEL), BF16),
                        pltpu.VMEM((tm, D_MODEL), F32)],
        compiler_params=_cparams(("arbitrary", "arbitrary")),
        name="moe",
    )(x, rw, rb, wg, wu, wd, g, b)


M_I1, M_I2, M_R1, M_R2, M_G1, M_G2, M_COLS = 0, 1, 2, 3, 4, 5, 8


def _first_argmax_rows(vals, row):
    m = jnp.max(vals, axis=0, keepdims=True)
    idx = jnp.min(jnp.where(vals == m, row, N_EXPERTS), axis=0, keepdims=True)
    return m, idx


HI16 = -65536


def _pack_bf16_pairs(x):
    w = x.shape[1] // 2
    hi = lax.bitcast_convert_type(x[:, :w].astype(BF16).astype(F32), jnp.int32)
    lo = lax.bitcast_convert_type(x[:, w:].astype(BF16).astype(F32), jnp.int32)
    return (hi & HI16) | lax.shift_right_logical(lo, 16)


def _unpack_bf16_pairs(p):
    hi = lax.bitcast_convert_type(p & HI16, F32)
    lo = lax.bitcast_convert_type(lax.shift_left(p, 16), F32)
    return jnp.concatenate([hi, lo], axis=1)


def _route_kernel(x_ref, rwt_ref, rbc_ref, meta_ref, cnt_ref, xpk_ref, carry_scr, *, tm):
    @pl.when(pl.program_id(0) == 0)
    def _():
        carry_scr[...] = jnp.zeros(carry_scr.shape, F32)

    logits = _mm(rwt_ref[...], x_ref[...], _NT, True)
    s = jax.nn.sigmoid(logits)
    sel = s + rbc_ref[...]
    row = lax.broadcasted_iota(jnp.int32, sel.shape, 0)
    grp = row >> 2
    best = None
    g_best = None
    for g in range(N_GROUPS):
        vals = jnp.where(grp == g, sel, NEG)
        m1, i1 = _first_argmax_rows(vals, row)
        m2, _ = _first_argmax_rows(jnp.where(row == i1, NEG, vals), row)
        score = m1 + m2
        if g == 0:
            best, g_best = score, jnp.zeros_like(i1)
        else:
            upd = score > best
            g_best = jnp.where(upd, g, g_best)
            best = jnp.where(upd, score, best)
    vals = jnp.where(grp == g_best, sel, NEG)
    _, i1 = _first_argmax_rows(vals, row)
    _, i2 = _first_argmax_rows(jnp.where(row == i1, NEG, vals), row)
    w1 = jnp.sum(jnp.where(row == i1, s, 0.0), axis=0, keepdims=True)
    w2 = jnp.sum(jnp.where(row == i2, s, 0.0), axis=0, keepdims=True)
    den = w1 + w2

    chosen = jnp.where((row == i1) | (row == i2), 1.0, 0.0)
    earlier = (lax.broadcasted_iota(jnp.int32, (tm, tm), 0)
               < lax.broadcasted_iota(jnp.int32, (tm, tm), 1)).astype(BF16)
    seen = jnp.dot(chosen.astype(BF16), earlier, preferred_element_type=F32) + carry_scr[:, 0:1]
    r1 = jnp.sum(jnp.where(row == i1, seen, 0.0), axis=0, keepdims=True)
    r2 = jnp.sum(jnp.where(row == i2, seen, 0.0), axis=0, keepdims=True)
    carry_scr[...] = carry_scr[...] + jnp.sum(chosen, axis=1, keepdims=True)
    mrow = lax.broadcasted_iota(jnp.int32, (M_COLS, tm), 0)
    meta = jnp.zeros((M_COLS, tm), F32)
    for c, val in ((M_I1, i1.astype(F32)), (M_I2, i2.astype(F32)), (M_R1, r1), (M_R2, r2),
                   (M_G1, w1 / den), (M_G2, w2 / den)):
        meta = jnp.where(mrow == c, val, meta)
    meta_ref[...] = meta
    cnt_ref[...] = carry_scr[...]
    xpk_ref[...] = _pack_bf16_pairs(x_ref[...])


def _route(x, rwt, rbc):
    n = x.shape[0]
    tm = _row_tile(n, 512)
    return pl.pallas_call(
        functools.partial(_route_kernel, tm=tm),
        grid=(n // tm,),
        in_specs=[pl.BlockSpec((tm, D_MODEL), lambda i: (i, 0)), _full(rwt.shape), _full(rbc.shape)],
        out_specs=[pl.BlockSpec((M_COLS, tm), lambda i: (0, i)), _full((N_EXPERTS, LANES)),
                   pl.BlockSpec((tm, D_MODEL // 2), lambda i: (i, 0))],
        out_shape=[jax.ShapeDtypeStruct((M_COLS, n), F32), jax.ShapeDtypeStruct((N_EXPERTS, LANES), F32),
                   jax.ShapeDtypeStruct((n, D_MODEL // 2), jnp.int32)],
        scratch_shapes=[pltpu.VMEM((N_EXPERTS, LANES), F32)],
        compiler_params=_cparams(("arbitrary",)),
        name="moe_route",
    )(x, rwt, rbc)


def _sc_chunk(per_worker):
    for c in (64, 48, 32, 16, 8):
        if per_worker % c == 0:
            return c
    raise ValueError(per_worker)


def _sc_mesh():
    return plsc.VectorSubcoreMesh(core_axis_name="c", subcore_axis_name="s")


def _sc_scatter2(x, idx1, idx2, n_out):
    n, d = x.shape
    per_w = n // SC_WORKERS
    assert per_w * SC_WORKERS == n
    chunk = _sc_chunk(per_w)

    @functools.partial(
        pl.kernel, mesh=_sc_mesh(), out_type=jax.ShapeDtypeStruct((n_out, d), x.dtype),
        scratch_types=[pltpu.VMEM((chunk,), jnp.int32), pltpu.VMEM((chunk,), jnp.int32),
                       pltpu.VMEM((chunk, d), x.dtype), pltpu.SemaphoreType.DMA])
    def scatter(x_hbm, i1_hbm, i2_hbm, out_hbm, i1_v, i2_v, rows_v, sem):
        base = (lax.axis_index("s") * SC_CORES + lax.axis_index("c")) * per_w

        @pl.loop(0, per_w // chunk)
        def _(c):
            off = pl.multiple_of(base + c * chunk, SUBLANES)
            pltpu.sync_copy(i1_hbm.at[pl.ds(off, chunk)], i1_v)
            pltpu.sync_copy(i2_hbm.at[pl.ds(off, chunk)], i2_v)
            pltpu.sync_copy(x_hbm.at[pl.ds(off, chunk)], rows_v)
            pltpu.async_copy(rows_v, out_hbm.at[i1_v], sem).wait()
            pltpu.async_copy(rows_v, out_hbm.at[i2_v], sem).wait()

    return scatter(x, idx1, idx2)


def _sc_gather(y, idx):
    n = idx.shape[0]
    d = y.shape[1]
    per_w = n // SC_WORKERS
    assert per_w * SC_WORKERS == n
    chunk = _sc_chunk(per_w)

    @functools.partial(
        pl.kernel, mesh=_sc_mesh(), out_type=jax.ShapeDtypeStruct((n, d), y.dtype),
        scratch_types=[pltpu.VMEM((chunk,), jnp.int32), pltpu.VMEM((chunk, d), y.dtype),
                       pltpu.SemaphoreType.DMA])
    def gather(y_hbm, idx_hbm, out_hbm, idx_v, rows_v, sem):
        base = (lax.axis_index("s") * SC_CORES + lax.axis_index("c")) * per_w

        @pl.loop(0, per_w // chunk)
        def _(c):
            off = pl.multiple_of(base + c * chunk, SUBLANES)
            pltpu.sync_copy(idx_hbm.at[pl.ds(off, chunk)], idx_v)
            pltpu.async_copy(y_hbm.at[idx_v], rows_v, sem).wait()
            pltpu.sync_copy(rows_v, out_hbm.at[pl.ds(off, chunk)])

    return gather(y, idx)


def _experts_kernel(te_ref, used_ref, x_ref, wg_ref, wu_ref, wd_ref, o_ref):
    @pl.when(pl.program_id(0) < used_ref[0])
    def _():
        xb = _unpack_bf16_pairs(x_ref[...]).astype(BF16)
        hg = jnp.dot(xb, wg_ref[0].astype(BF16), preferred_element_type=F32)
        hu = jnp.dot(xb, wu_ref[0].astype(BF16), preferred_element_type=F32)
        h = jax.nn.silu(hg) * hu
        y = jnp.dot(h.astype(BF16), wd_ref[0].astype(BF16), preferred_element_type=F32)
        o_ref[...] = _pack_bf16_pairs(y)


def _experts(xg, tile_expert, n_used, wg, wu, wd):
    n_tiles = xg.shape[0] // MOE_TILE
    row = pl.BlockSpec((MOE_TILE, D_MODEL // 2), lambda i, te, used: (i, 0))
    grid_spec = pltpu.PrefetchScalarGridSpec(
        num_scalar_prefetch=2,
        grid=(n_tiles,),
        in_specs=[row,
                  pl.BlockSpec((1, D_MODEL, D_EXPERT), lambda i, te, used: (te[i], 0, 0)),
                  pl.BlockSpec((1, D_MODEL, D_EXPERT), lambda i, te, used: (te[i], 0, 0)),
                  pl.BlockSpec((1, D_EXPERT, D_MODEL), lambda i, te, used: (te[i], 0, 0))],
        out_specs=row,
    )
    return pl.pallas_call(
        _experts_kernel,
        grid_spec=grid_spec,
        out_shape=jax.ShapeDtypeStruct(xg.shape, jnp.int32),
        compiler_params=_cparams(("arbitrary",)),
        name="moe_experts",
    )(tile_expert, n_used, xg, wg, wu, wd)


def _combine_kernel(x_ref, y1_ref, y2_ref, meta_ref, g_ref, b_ref, o_ref):
    meta = meta_ref[...]
    moe = (meta[:, M_G1:M_G1 + 1] * _unpack_bf16_pairs(y1_ref[...])
           + meta[:, M_G2:M_G2 + 1] * _unpack_bf16_pairs(y2_ref[...]))
    o_ref[...] = _ln(ALPHA * x_ref[...] + moe, g_ref[...], b_ref[...])


def _combine(x, y1, y2, meta, g, b):
    n = x.shape[0]
    tm = _row_tile(n, 512)
    row = pl.BlockSpec((tm, D_MODEL), lambda i: (i, 0))
    half = pl.BlockSpec((tm, D_MODEL // 2), lambda i: (i, 0))
    return pl.pallas_call(
        _combine_kernel,
        grid=(n // tm,),
        in_specs=[row, half, half, pl.BlockSpec((tm, M_COLS), lambda i: (i, 0)), _full(g.shape), _full(b.shape)],
        out_specs=row,
        out_shape=jax.ShapeDtypeStruct((n, D_MODEL), F32),
        compiler_params=_cparams(("arbitrary",)),
        name="moe_combine",
    )(x, y1, y2, meta, g, b)


def _moe_sparse(x, rw, rb, wg, wu, wd, e0, g, b):
    n = x.shape[0]
    meta_t, counts, x_packed = _route(x, rw.T, rb.reshape(N_EXPERTS, 1))
    cnt = counts[:, 0].astype(jnp.int32)
    padded = (cnt + MOE_TILE - 1) // MOE_TILE * MOE_TILE
    seg_end = jnp.cumsum(padded)
    seg_start = seg_end - padded
    experts = jnp.arange(N_EXPERTS, dtype=jnp.int32)[:, None]
    start_of = lambda e: jnp.sum(jnp.where(experts == e[None, :], seg_start[:, None], 0), axis=0)
    e1, e2 = meta_t[M_I1].astype(jnp.int32), meta_t[M_I2].astype(jnp.int32)
    pos1 = start_of(e1) + meta_t[M_R1].astype(jnp.int32)
    pos2 = start_of(e2) + meta_t[M_R2].astype(jnp.int32)
    meta = meta_t.T
    n_tiles = -(-2 * n // MOE_TILE) + N_EXPERTS
    tile_start = jnp.arange(n_tiles, dtype=jnp.int32) * MOE_TILE
    tile_expert = e0 + jnp.minimum(jnp.sum(tile_start[:, None] >= seg_end[None, :], axis=1),
                                   N_EXPERTS - 1).astype(jnp.int32)
    n_used = (seg_end[-1:] // MOE_TILE).astype(jnp.int32)
    xg = _sc_scatter2(x_packed, pos1, pos2, n_tiles * MOE_TILE)
    yg = _experts(xg, tile_expert, n_used, wg, wu, wd)
    return _combine(x, _sc_gather(yg, pos1), _sc_gather(yg, pos2), meta, g, b)


def _head_sum(z, ones):
    hi, lo = _split2(z)
    parts = []
    for g in range(D_MODEL // LANES):
        sl = slice(g * LANES, (g + 1) * LANES)
        parts.append(jnp.dot(hi[:, sl], ones, preferred_element_type=F32)
                     + jnp.dot(lo[:, sl], ones, preferred_element_type=F32))
    return jnp.concatenate(parts, axis=1)


def _rwkv_proj_kernel(x_ref, sh0_ref, mu_ref, wr_ref, wk_ref, wv_ref, w0_ref, w1_ref, w2_ref,
                      a0_ref, a1_ref, a2_ref, g1_ref, g2_ref, kkw_ref, kaw_ref, ones_ref,
                      r_ref, lw_ref, k_ref, v_ref, kk_ref, a_ref, g_ref, prev_scr,
                      *, tm, start, end):
    t = pl.program_id(1)

    @pl.when(t == 0)
    def _():
        prev_scr[...] = jnp.zeros(prev_scr.shape, F32)

    x = x_ref[0]
    ext = jnp.concatenate([prev_scr[...], x], axis=0)
    x_prev = pltpu.roll(ext, 1, 0)[SUBLANES:]
    grow = t * tm + lax.broadcasted_iota(jnp.int32, (tm, D_MODEL), 0)
    x_prev = jnp.where(grow == start, sh0_ref[0], x_prev)
    prev_scr[...] = x[tm - SUBLANES:]
    xx = x_prev - x
    mu = mu_ref[...]
    xr, xw, xk, xv, xa, xg = (x + xx * mu[n:n + 1] for n in range(6))
    r = _bdot(xr, wr_ref[...])
    k = _bdot(xk, wk_ref[...])
    v = _bdot(xv, wv_ref[...])
    log_w = -DECAY_SCALE * jax.nn.sigmoid(w0_ref[...] + _bdot(jnp.tanh(_bdot(xw, w1_ref[...])), w2_ref[...]))
    a = jax.nn.sigmoid(a0_ref[...] + _bdot(_bdot(xa, a1_ref[...]), a2_ref[...]))
    g = _bdot(jax.nn.sigmoid(_bdot(xg, g1_ref[...])), g2_ref[...])
    kk = k * kkw_ref[...]
    kk = kk * lax.rsqrt(jnp.maximum(_head_sum(kk * kk, ones_ref[...]), 1e-24))
    k = k * (1.0 + (a - 1.0) * kaw_ref[...])
    live = (grow >= start) & (grow < end)
    r_ref[0] = r.astype(r_ref.dtype)
    lw_ref[0] = jnp.where(live, log_w, 0.0)
    k_ref[0] = jnp.where(live, k, 0.0).astype(k_ref.dtype)
    v_ref[0] = v.astype(v_ref.dtype)
    kk_ref[0] = jnp.where(live, kk, 0.0).astype(kk_ref.dtype)
    a_ref[0] = a.astype(a_ref.dtype)
    g_ref[0] = g.astype(g_ref.dtype)


def _rwkv_proj(x, sh0, od, start, end, act_dtype):
    b, l, _ = x.shape
    tm = _row_tile(l, 256)
    dtypes = [act_dtype, F32] + [act_dtype] * 5
    seq = pl.BlockSpec((1, tm, D_MODEL), lambda i, t: (i, t, 0))
    ws = [od[n] for n in ("mu", "w_r", "w_k", "w_v", "w0", "w1", "w2", "a0", "a1", "a2", "g1", "g2",
                          "k_k", "k_a", "ones")]
    return pl.pallas_call(
        functools.partial(_rwkv_proj_kernel, tm=tm, start=start, end=end),
        grid=(b, l // tm),
        in_specs=[seq, pl.BlockSpec((1, 1, D_MODEL), lambda i, t: (i, 0, 0))] + [_full(w.shape) for w in ws],
        out_specs=[seq] * 7,
        out_shape=[jax.ShapeDtypeStruct((b, l, D_MODEL), dt) for dt in dtypes],
        scratch_shapes=[pltpu.VMEM((SUBLANES, D_MODEL), F32)],
        compiler_params=_cparams(("arbitrary", "arbitrary")),
        name="rwkv_proj",
    )(x, sh0, *ws)


def _wkv_kernel(r_ref, lw_ref, k_ref, v_ref, kk_ref, a_ref, s0_ref, o_ref, sout_ref, s_scr,
                *, c, exact):
    t = pl.program_id(1)

    @pl.when(t == 0)
    def _():
        s_scr[...] = s0_ref[0]

    head0 = lax.broadcasted_iota(jnp.int32, (c, LANES), 1) < RWKV_HEAD
    c2 = 2 * c
    row = lax.broadcasted_iota(jnp.int32, (c2, c2), 0)
    col = lax.broadcasted_iota(jnp.int32, (c2, c2), 1)
    row_hi = jnp.where(row >= c, c, 0)
    col_hi = jnp.where(col >= c, c, 0)
    same = row_hi == col_hi
    rr = row - row_hi
    cc = col - col_hi
    strict = same & (rr > cc)
    incl = same & (rr >= cc)
    eye = (row == col).astype(F32)
    crow = lax.broadcasted_iota(jnp.int32, (c, LANES), 0)

    def cumsum_rows(x):
        d = 1
        while d < c:
            x = x + jnp.where(crow >= d, pltpu.roll(x, d, 0), 0.0)
            d *= 2
        return x

    def stack(x):
        return jnp.concatenate([jnp.where(head0, x, 0.0), jnp.where(head0, 0.0, x)], axis=0)

    mm = functools.partial(_mm, exact=exact)
    pairs = range(RWKV_HEADS // 2)
    for sub in range(r_ref.shape[1] // c):
        rows = slice(sub * c, (sub + 1) * c)
        load = lambda ref: [ref[0, rows, p * LANES:(p + 1) * LANES].astype(F32) for p in pairs]
        r, lw, k, v, kk, a = (load(ref) for ref in (r_ref, lw_ref, k_ref, v_ref, kk_ref, a_ref))
        lc = [cumsum_rows(x) for x in lw]
        lc_end = [x[c - 1:c] for x in lc]
        b = [x * y for x, y in zip(kk, a)]
        lhs = [jnp.concatenate([stack(-kk[p] * jnp.exp(lc[p] - lw[p])), stack(r[p] * jnp.exp(lc[p]))], axis=0)
               for p in pairs]
        g_inv = [jnp.exp(-x) for x in lc]
        rhs = [jnp.concatenate([stack(b[p] * g_inv[p]), stack(k[p] * g_inv[p])], axis=0) for p in pairs]
        pm = [mm(x, y, _NT) for x, y in zip(lhs, rhs)]
        l_ab = [jnp.where(strict, x[:c2, :c2], 0.0) for x in pm]
        l_ak = [jnp.where(strict, x[:c2, c2:], 0.0) for x in pm]
        m_rb = [jnp.where(incl, x[c2:, :c2], 0.0) for x in pm]
        m_rk = [jnp.where(incl, x[c2:, c2:], 0.0) for x in pm]
        vs = [stack(x) for x in v]
        lakv = [mm(x, y, _NN) for x, y in zip(l_ak, vs)]
        mrkv = [mm(x, y, _NN) for x, y in zip(m_rk, vs)]
        tinv = [eye + x for x in l_ab]
        lp = l_ab
        n = 2
        while n < c:
            lp = [mm(x, x, _NN) for x in lp]
            tinv = [x + mm(x, y, _NN) for x, y in zip(tinv, lp)]
            n *= 2
        s = [s_scr[p] for p in pairs]
        xs = [mm(x, y, _NT) for x, y in zip(lhs, s)]
        u = [mm(tinv[p], xs[p][:c2] + lakv[p], _NN) for p in pairs]
        for p in pairs:
            os_ = xs[p][c2:] + mm(m_rb[p], u[p], _NN) + mrkv[p]
            o_ref[0, rows, p * LANES:(p + 1) * LANES] = os_[:c] + os_[c:]
        for p in pairs:
            g_rem = jnp.exp(lc_end[p] - lc[p])
            uv = jnp.concatenate([u[p], vs[p]], axis=0)
            bk = jnp.concatenate([stack(b[p] * g_rem), stack(k[p] * g_rem)], axis=0)
            s_scr[p] = s[p] * jnp.exp(lc_end[p]) + mm(uv, bk, _TN)

    @pl.when(t == pl.num_programs(1) - 1)
    def _():
        sout_ref[0] = s_scr[...]


def _wkv(r, lw, k, v, kk, a, s0, c, exact):
    b, l, _ = r.shape
    rows = 2 * c if l % (2 * c) == 0 else c
    seq = pl.BlockSpec((1, rows, D_MODEL), lambda i, t: (i, t, 0))
    st = pl.BlockSpec((1, RWKV_HEADS // 2, LANES, LANES), lambda i, t: (i, 0, 0, 0))
    return pl.pallas_call(
        functools.partial(_wkv_kernel, c=c, exact=exact),
        grid=(b, l // rows),
        in_specs=[seq] * 6 + [st],
        out_specs=[seq, st],
        out_shape=[jax.ShapeDtypeStruct((b, l, D_MODEL), F32),
                   jax.ShapeDtypeStruct((b, RWKV_HEADS // 2, LANES, LANES), F32)],
        scratch_shapes=[pltpu.VMEM((RWKV_HEADS // 2, LANES, LANES), F32)],
        compiler_params=_cparams(("arbitrary", "arbitrary")),
        name="wkv",
    )(r, lw, k, v, kk, a, s0)


def _rwkv_out_kernel(o_ref, r_ref, k_ref, v_ref, g_ref, x_ref, rk_ref, gng_ref, gnb_ref, wo_ref,
                     ones_ref, lg_ref, lb_ref, y_ref):
    ones = ones_ref[...]
    o = o_ref[...]
    inv = 1.0 / RWKV_HEAD
    mu = _head_sum(o, ones) * inv
    oc = o - mu
    var = _head_sum(oc * oc, ones) * inv
    on = oc * lax.rsqrt(var + GN_EPS) * gng_ref[...] + gnb_ref[...]
    r, k, v, g = (ref[...].astype(F32) for ref in (r_ref, k_ref, v_ref, g_ref))
    on = on + _head_sum(r * k * rk_ref[...], ones) * v
    out = _bdot(on * g, wo_ref[...])
    y_ref[...] = _ln(ALPHA * x_ref[...] + out, lg_ref[...], lb_ref[...])


def _rwkv_out(o, r, k, v, g, x, od, lg, lb):
    n = x.shape[0]
    tm = _row_tile(n, 512)
    row = pl.BlockSpec((tm, D_MODEL), lambda i: (i, 0))
    ws = [od["r_k"], od["ln_g"], od["ln_b"], od["w_o"], od["ones"], lg, lb]
    return pl.pallas_call(
        _rwkv_out_kernel,
        grid=(n // tm,),
        in_specs=[row] * 6 + [_full(w.shape) for w in ws],
        out_specs=row,
        out_shape=jax.ShapeDtypeStruct((n, D_MODEL), F32),
        compiler_params=_cparams(("arbitrary",)),
        name="rwkv_out",
    )(o, r, k, v, g, x, *ws)


def _rope_tables(pos):
    half = D_ROPE // 2
    freq = ROPE_BASE ** (-jnp.arange(half, dtype=F32) / half)
    ang_t = freq[:, None] * pos.astype(F32)[None, :]
    cos_t, sin_t = lax.optimization_barrier((jnp.cos(ang_t), jnp.sin(ang_t)))
    cos, sin = cos_t.T, sin_t.T
    n = pos.shape[0]
    ones = jnp.ones((n, D_NOPE), F32)
    zeros = jnp.zeros((n, D_NOPE), F32)
    z16 = jnp.zeros((n, half), F32)
    tail1 = jnp.ones((n, LANES - D_NOPE - D_ROPE), F32)
    tail0 = jnp.zeros((n, LANES - D_NOPE - D_ROPE), F32)
    c = jnp.concatenate([ones, cos, cos, tail1], axis=1)
    sa = jnp.concatenate([zeros, -sin, z16, tail0], axis=1)
    sb = jnp.concatenate([zeros, z16, sin, tail0], axis=1)
    return c, sa, sb


def _slot_cols(w, width):
    k, h, _ = w.shape
    return jnp.pad(w, ((0, 0), (0, 0), (0, HEAD_SLOT - width))).reshape(k, h * HEAD_SLOT)


def _block_diag(w):
    n, c, d = w.shape
    eye = jnp.eye(n, dtype=w.dtype)
    return (eye[:, None, :, None] * w[:, :, None, :]).reshape(n * c, n * d)


def _row2(v):
    return v.reshape(1, -1).astype(F32)


def _prep_even(w_in, g_q, w_uq, g_kv, w_uk, w_uv, conv_w, conv_b, w_rg, b_rg, w_ig, b_ig, lam, w_out):
    off_ckv, off_kr = D_CQ, D_CQ + D_C
    off_xr = off_kr + D_ROPE
    off_y = off_xr + D_RNN
    kr_cols = jnp.pad(w_in[:, off_kr:off_xr], ((0, 0), (D_NOPE, LANES - D_NOPE - D_ROPE)))
    w_in, w_uq, w_uk, w_uv, w_rg, w_ig, w_out = (
        w.astype(F32) for w in (w_in, w_uq, w_uk, w_uv, w_rg, w_ig, w_out))
    w1 = jnp.concatenate([w_in[:, :off_ckv], w_in[:, off_ckv:off_kr], w_in[:, off_xr:off_y],
                          w_in[:, off_y:], kr_cols], axis=1)
    wuq = _slot_cols(w_uq, D_NOPE + D_ROPE)
    wukv = jnp.concatenate([_slot_cols(w_uk, D_NOPE), _slot_cols(w_uv, D_V)], axis=1)
    place = np.zeros((LANES, D_ATT), np.float32)
    for h in range(MLA_HEADS):
        for cidx in range(D_ROPE):
            place[D_NOPE + cidx, h * HEAD_SLOT + D_NOPE + cidx] = 1.0
    wa = jnp.pad(w_out[:MLA_HEADS * D_V].reshape(MLA_HEADS, D_V, D_MODEL),
                 ((0, 0), (0, HEAD_SLOT - D_V), (0, 0))).reshape(D_ATT, D_MODEL)
    wr = w_out[MLA_HEADS * D_V:]
    keep_rope = np.zeros((HEAD_SLOT, LANES), np.float32)
    keep_rope[D_NOPE:D_NOPE + D_ROPE, D_NOPE:D_NOPE + D_ROPE] = np.eye(D_ROPE, dtype=np.float32)
    uk_t = jnp.pad(jnp.transpose(w_uk, (1, 2, 0)), ((0, 0), (0, HEAD_SLOT - D_NOPE), (0, 0)))
    wabs = jnp.concatenate([uk_t, jnp.broadcast_to(keep_rope, (MLA_HEADS, HEAD_SLOT, LANES))], axis=2)
    wuv = jnp.pad(jnp.transpose(w_uv, (1, 0, 2)), ((0, 0), (0, 0), (0, HEAD_SLOT - D_V)))
    return dict(
        wabs=wabs, wuv=wuv,
        w1=w1, gq=_row2(g_q), wuq=wuq, gkv=_row2(g_kv), wukv=wukv, place=jnp.asarray(place, F32),
        cw=conv_w.astype(F32), cb=_row2(conv_b), wrg=_block_diag(w_rg), brg=_row2(b_rg),
        wig=_block_diag(w_ig), big=_row2(b_ig), sp=_row2(jax.nn.softplus(-lam.astype(F32))),
        wa=wa, wr=wr)


_EVEN_MATMUL_WEIGHTS = ("w1", "wuq", "wukv", "place", "wrg", "wig", "wa", "wr")


def _single_pass(ev):
    return {n: (w.astype(BF16) if n in _EVEN_MATMUL_WEIGHTS else w) for n, w in ev.items()}


def _prep_odd(mu, w_r, w_k, w_v, w0, w1, w2, a0, a1, a2, g1, g2, k_k, k_a, r_k, ln_g, ln_b, w_o):
    ones = np.zeros((LANES, LANES), np.float32)
    ones[:RWKV_HEAD, :RWKV_HEAD] = 1.0
    ones[RWKV_HEAD:, RWKV_HEAD:] = 1.0
    return dict(
        mu=jnp.pad(mu.astype(F32), ((0, SUBLANES - mu.shape[0]), (0, 0))),
        w_r=w_r.astype(BF16), w_k=w_k.astype(BF16), w_v=w_v.astype(BF16), w0=_row2(w0),
        w1=w1.astype(BF16), w2=w2.astype(BF16), a0=_row2(a0), a1=a1.astype(BF16), a2=a2.astype(BF16),
        g1=g1.astype(BF16), g2=g2.astype(BF16), k_k=_row2(k_k), k_a=_row2(k_a), r_k=_row2(r_k),
        ln_g=_row2(ln_g), ln_b=_row2(ln_b), w_o=w_o.astype(BF16), ones=jnp.asarray(ones, BF16))


def _pair_states(s):
    b = s.shape[0]
    s = s.reshape(b, RWKV_HEADS // 2, 2, RWKV_HEAD, RWKV_HEAD).astype(F32)
    eye = jnp.eye(2, dtype=F32)
    out = s[:, :, :, :, None, :] * eye[None, None, :, None, :, None]
    return out.reshape(b, RWKV_HEADS // 2, LANES, LANES)


def _unpair_states(s):
    b = s.shape[0]
    s = s.reshape(b, RWKV_HEADS // 2, 2, RWKV_HEAD, 2, RWKV_HEAD)
    return jnp.stack([s[:, :, 0, :, 0, :], s[:, :, 1, :, 1, :]], axis=2).reshape(
        b, RWKV_HEADS, RWKV_HEAD, RWKV_HEAD)


def _round_up(n, m):
    return -(-n // m) * m


def kernel(x_prompt, x_sample, cache_ckv, cache_krope, state_conv, state_lru, state_shift, state_wkv,
           meta_tokens, ev_w_in, ev_g_q, ev_w_uq, ev_g_kv, ev_w_uk, ev_w_uv, ev_conv_w, ev_conv_b,
           ev_w_rg, ev_b_rg, ev_w_ig, ev_b_ig, ev_lru_lambda, ev_w_out, od_mu, od_w_r, od_w_k, od_w_v,
           od_w0, od_w1, od_w2, od_a0, od_a1, od_a2, od_g1, od_g2, od_k_k, od_k_a, od_r_k, od_ln_g,
           od_ln_b, od_w_o, ln_g, ln_b, router_w, router_b, exp_w_gate, exp_w_up, exp_w_down):
    assert x_prompt.shape[0] == 1 and x_prompt.shape[2] == D_MODEL
    seq = x_prompt.shape[1]
    assert seq % CHUNK == 0
    bs, ls, _ = x_sample.shape
    past = cache_ckv.shape[2]
    ns = bs * ls
    end = ROW0 + seq
    tp = _round_up(end, 512)

    ev = _prep_even(ev_w_in[0], ev_g_q[0], ev_w_uq[0], ev_g_kv[0], ev_w_uk[0], ev_w_uv[0], ev_conv_w[0],
                    ev_conv_b[0], ev_w_rg[0], ev_b_rg[0], ev_w_ig[0], ev_b_ig[0], ev_lru_lambda[0],
                    ev_w_out[0])
    od = _prep_odd(od_mu[0], od_w_r[0], od_w_k[0], od_w_v[0], od_w0[0], od_w1[0], od_w2[0], od_a0[0],
                   od_a1[0], od_a2[0], od_g1[0], od_g2[0], od_k_k[0], od_k_a[0], od_r_k[0], od_ln_g[0],
                   od_ln_b[0], od_w_o[0])
    rw = router_w.astype(F32)
    rb = _row2(router_b)
    wg, wu, wd = (w.reshape((DEPTH * N_EXPERTS,) + w.shape[2:]) for w in (exp_w_gate, exp_w_up, exp_w_down))
    lng = ln_g.astype(F32)[:, :, None, :]
    lnb = ln_b.astype(F32)[:, :, None, :]

    def moe(x, layer):
        sparse = x.shape[0] % (SC_WORKERS * SUBLANES) == 0 and x.shape[0] >= SPARSE_MIN_ROWS
        fn = _moe_sparse if sparse else _moe
        return fn(x, rw, rb, wg, wu, wd, layer * N_EXPERTS, lng[layer, 1], lnb[layer, 1])

    xp = jnp.concatenate([jnp.zeros((PAD_FRONT, D_MODEL), F32), meta_tokens.astype(F32),
                          x_prompt[0].astype(F32), jnp.zeros((tp - end, D_MODEL), F32)], axis=0)
    tabs_p = _rope_tables(jnp.maximum(jnp.arange(tp) - PAD_FRONT, 0))
    evb = _single_pass(ev)
    q_p, ckv_p, kr_p, xr_p, yg_p = _even_proj(xp, evb["w1"], ev["gq"], evb["wuq"], ev["gkv"], tabs_p,
                                              MLA_SCALE * LOG2E)
    ones_col = np.zeros((D_ATT, 1), np.float32)
    ones_col[D_V::HEAD_SLOT] = 1.0
    k_p, vt_p = _kv_proj_t(ckv_p, kr_p, evb["wukv"][:, :D_ATT], evb["place"], evb["wukv"][:, D_ATT:].T,
                           jnp.asarray(ones_col), _round_up(tp, FLASH_TK))
    attn_p = _flash_attention(q_p, k_p, vt_p, FLASH_TQ, FLASH_TK)
    rnn_p, tailx_p, tailh_p = _rglru(
        xr_p[None], yg_p[None], ev["cw"], ev["cb"], evb["wrg"], ev["brg"], evb["wig"], ev["big"], ev["sp"],
        jnp.zeros((1, SUBLANES, D_RNN), F32), jnp.zeros((1, 1, D_RNN), F32), PAD_FRONT, end)
    x1_p = _mix_out(attn_p, rnn_p[0], xp, evb["wa"], evb["wr"], lng[0, 0], lnb[0, 0])
    x2_p = moe(x1_p, 0)

    xs = x_sample.reshape(ns, D_MODEL).astype(F32)
    pos_s = jnp.tile(N_META + past + jnp.arange(ls), bs)
    q_s, ckv_s, kr_s, xr_s, yg_s = _even_proj(xs, ev["w1"], ev["gq"], ev["wuq"], ev["gkv"], _rope_tables(pos_s),
                                              MLA_SCALE)
    _, ckv_m, kr_m, _, _ = _even_proj(meta_tokens.astype(F32), ev["w1"], ev["gq"], ev["wuq"], ev["gkv"],
                                      _rope_tables(jnp.arange(N_META)), MLA_SCALE)
    n_keys = N_META + past + ls
    nk_pad = _round_up(n_keys, LANES)
    meta_ckv = jnp.broadcast_to(ckv_m[None], (bs, N_META, D_C))
    meta_kr = jnp.broadcast_to(kr_m[None], (bs, N_META, LANES))
    cache_kr = jnp.pad(cache_krope[0].astype(F32), ((0, 0), (0, 0), (D_NOPE, LANES - D_NOPE - D_ROPE)))
    all_ckv = jnp.concatenate([meta_ckv, cache_ckv[0].astype(F32), ckv_s.reshape(bs, ls, D_C),
                               jnp.zeros((bs, nk_pad - n_keys, D_C), F32)], axis=1)
    all_kr = jnp.concatenate([meta_kr, cache_kr, kr_s.reshape(bs, ls, LANES),
                              jnp.zeros((bs, nk_pad - n_keys, LANES), F32)], axis=1)
    attn_s = _sample_attention(q_s.reshape(bs, ls, D_ATT), jnp.concatenate([all_ckv, all_kr], axis=2),
                               ev["wabs"], ev["wuv"], n_keys)
    buf0_s = jnp.pad(state_conv[0].astype(F32), ((0, 0), (SUBLANES - (CONV_W - 1), 0), (0, 0)))
    rnn_s, tailx_s, tailh_s = _rglru(
        xr_s.reshape(bs, ls, D_RNN), yg_s.reshape(bs, ls, D_RNN), ev["cw"], ev["cb"], ev["wrg"], ev["brg"],
        ev["wig"], ev["big"], ev["sp"], buf0_s, state_lru[0].astype(F32)[:, None, :], 0, ls)
    x1_s = _mix_out(attn_s.reshape(ns, D_ATT), rnn_s.reshape(ns, D_RNN), xs, ev["wa"], ev["wr"],
                    lng[0, 0], lnb[0, 0])
    x2_s = moe(x1_s, 0)

    r_p, lw_p, kk_in_p, v1_p, kkn_p, a_p, g_p = _rwkv_proj(
        x2_p[None], jnp.zeros((1, 1, D_MODEL), F32), od, PAD_FRONT, end, BF16)
    o_p, s_p = _wkv(r_p, lw_p, kk_in_p, v1_p, kkn_p, a_p,
                    jnp.zeros((1, RWKV_HEADS // 2, LANES, LANES), F32), CHUNK, False)
    x3_p = _rwkv_out(o_p[0], r_p[0], kk_in_p[0], v1_p[0], g_p[0], x2_p, od, lng[1, 0], lnb[1, 0])
    x4_p = moe(x3_p, 1)

    x2_s3 = x2_s.reshape(bs, ls, D_MODEL)
    r_s, lw_s, kk_in_s, v1_s, kkn_s, a_s, g_s = _rwkv_proj(
        x2_s3, state_shift[0].astype(F32)[:, None, :], od, 0, ls, F32)
    to_chunk = lambda z: jnp.pad(z, ((0, 0), (0, _round_up(ls, CHUNK) - ls), (0, 0)))
    o_s, s_s = _wkv(*(to_chunk(z) for z in (r_s, lw_s, kk_in_s, v1_s, kkn_s, a_s)),
                    _pair_states(state_wkv[0]), CHUNK, True)
    o_s = o_s[:, :ls]
    flat = lambda z: z.reshape(ns, D_MODEL)
    x3_s = _rwkv_out(flat(o_s), flat(r_s), flat(kk_in_s), flat(v1_s), flat(g_s), x2_s, od,
                     lng[1, 0], lnb[1, 0])
    x4_s = moe(x3_s, 1)

    dt = x_prompt.dtype
    nb = CONV_W - 1
    return (
        x4_p[ROW0:end][None].astype(dt),
        x4_s.reshape(bs, ls, D_MODEL).astype(dt),
        ckv_p[PAD_FRONT:end][None, None].astype(dt),
        kr_p[PAD_FRONT:end, D_NOPE:D_NOPE + D_ROPE][None, None].astype(dt),
        tailx_p[:, SUBLANES - nb:][None].astype(dt),
        tailh_p[:, SUBLANES - 1][None].astype(dt),
        x2_p[end - 1][None, None].astype(dt),
        _unpair_states(s_p)[None].astype(dt),
        ckv_s.reshape(bs, ls, D_C)[None].astype(dt),
        kr_s.reshape(bs, ls, LANES)[:, :, D_NOPE:D_NOPE + D_ROPE][None].astype(dt),
        tailx_s[:, SUBLANES - nb:][None].astype(dt),
        tailh_s[:, SUBLANES - 1][None].astype(dt),
        x2_s3[:, ls - 1][None].astype(dt),
        _unpair_states(s_s)[None].astype(dt),
    )
```

```python
import functools

import numpy as np
import jax
import jax.numpy as jnp
from jax import lax
from jax.experimental import pallas as pl
from jax.experimental.pallas import tpu as pltpu
from jax.experimental.pallas import tpu_sc as plsc

F32 = jnp.float32
BF16 = jnp.bfloat16

D_MODEL = 1024
N_META = 16
CHUNK = 64
CHUNK_SHIFT = 6
LN_EPS = 1e-5
RMS_EPS = 1e-6
DEPTH = 2
ALPHA = (2 * DEPTH) ** 0.25
MLA_HEADS = 8
D_NOPE = 64
D_ROPE = 32
D_V = 64
D_C = 256
D_CQ = 384
ROPE_BASE = 10000.0
MLA_SCALE = (D_NOPE + D_ROPE) ** -0.5
D_RNN = 512
LRU_BLOCKS = 8
LRU_BLOCK_W = D_RNN // LRU_BLOCKS
CONV_W = 4
LRU_C = 8.0
RWKV_HEAD = 64
RWKV_HEADS = D_MODEL // RWKV_HEAD
DECAY_SCALE = float(np.exp(-0.5))
GN_EPS = 64e-5
N_EXPERTS = 16
N_GROUPS = 4
EXPERTS_PER_GROUP = N_EXPERTS // N_GROUPS
D_EXPERT = 512

LANES = 128
SUBLANES = 8
HEAD_SLOT = LANES
D_ATT = MLA_HEADS * HEAD_SLOT
PAD_FRONT = CHUNK - N_META
ROW0 = PAD_FRONT + N_META
NEG = -1e30
LOG2E = 1.4426950408889634
SC_CORES = 2
SC_SUBCORES = 16
SC_WORKERS = SC_CORES * SC_SUBCORES
MOE_TILE = 512
FLASH_TQ = 512
FLASH_TK = 1024
SPARSE_MIN_ROWS = 1024
VMEM_LIMIT = 56 * 1024 * 1024

C_CQ = 0
C_CKV = D_CQ
C_XR = C_CKV + D_C
C_YG = C_XR + D_RNN
C_KR = C_YG + D_RNN
N_COL = C_KR + LANES


def _cparams(sem):
    return pltpu.CompilerParams(dimension_semantics=sem, vmem_limit_bytes=VMEM_LIMIT)


def _row_tile(n, cap):
    for t in (1024, 512, 256, 128, 64, 32, 16, 8):
        if t <= cap and n % t == 0:
            return t
    return n


def _full(shape):
    zeros = (0,) * len(shape)
    return pl.BlockSpec(shape, lambda *_: zeros)


def _ln(x, g, b):
    mu = jnp.mean(x, axis=-1, keepdims=True)
    xc = x - mu
    var = jnp.mean(xc * xc, axis=-1, keepdims=True)
    return xc * lax.rsqrt(var + LN_EPS) * g + b


def _bdot(a, b):
    return jnp.dot(a.astype(BF16), b.astype(BF16), preferred_element_type=F32)


def _split2(x):
    hi = x.astype(BF16)
    return hi, (x - hi.astype(F32)).astype(BF16)


_NN = ((1,), (0,))
_NT = ((1,), (1,))
_TN = ((0,), (0,))


def _mm(a, b, dims, exact):
    dn = (dims, ((), ()))
    if not exact:
        return lax.dot_general(a.astype(BF16), b.astype(BF16), dn, preferred_element_type=F32)
    ah, al = _split2(a)
    bh, bl = _split2(b)
    return (lax.dot_general(ah, bh, dn, preferred_element_type=F32)
            + lax.dot_general(al, bh, dn, preferred_element_type=F32)
            + lax.dot_general(ah, bl, dn, preferred_element_type=F32))


def _wdot(a, w):
    return _mm(a, w, _NN, exact=(w.dtype == F32))


def _act_dtype(w):
    return F32 if w.dtype == F32 else BF16


def _rope_slot(x, c, sa, sb):
    return x * c + pltpu.roll(x, LANES - D_ROPE // 2, 1) * sa + pltpu.roll(x, D_ROPE // 2, 1) * sb


def _even_proj_kernel(x_ref, w1_ref, gq_ref, wuq_ref, gkv_ref, c_ref, sa_ref, sb_ref,
                      q_ref, ckv_ref, kr_ref, xr_ref, yg_ref, *, q_scale):
    u = _wdot(x_ref[...], w1_ref[...])
    cq = u[:, C_CQ:C_CQ + D_CQ]
    cq = cq * lax.rsqrt(jnp.mean(cq * cq, axis=-1, keepdims=True) + RMS_EPS) * gq_ref[...]
    q = _wdot(cq, wuq_ref[...])
    c, sa, sb = c_ref[...], sa_ref[...], sb_ref[...]
    for h in range(MLA_HEADS):
        sl = slice(h * HEAD_SLOT, (h + 1) * HEAD_SLOT)
        q_ref[:, sl] = (_rope_slot(q[:, sl], c, sa, sb) * q_scale).astype(q_ref.dtype)
    ckv = u[:, C_CKV:C_CKV + D_C]
    ckv_ref[...] = ckv * lax.rsqrt(jnp.mean(ckv * ckv, axis=-1, keepdims=True) + RMS_EPS) * gkv_ref[...]
    kr_ref[...] = _rope_slot(u[:, C_KR:C_KR + LANES], c, sa, sb)
    xr_ref[...] = u[:, C_XR:C_XR + D_RNN]
    yg_ref[...] = u[:, C_YG:C_YG + D_RNN]


def _even_proj(x, w1, gq, wuq, gkv, tabs, q_scale):
    n = x.shape[0]
    tm = _row_tile(n, 512)
    row = lambda w: pl.BlockSpec((tm, w), lambda i: (i, 0))
    c, sa, sb = tabs
    return pl.pallas_call(
        functools.partial(_even_proj_kernel, q_scale=q_scale),
        grid=(n // tm,),
        in_specs=[row(D_MODEL), _full(w1.shape), _full(gq.shape), _full(wuq.shape), _full(gkv.shape),
                  row(LANES), row(LANES), row(LANES)],
        out_specs=[row(D_ATT), row(D_C), row(LANES), row(D_RNN), row(D_RNN)],
        out_shape=[jax.ShapeDtypeStruct((n, D_ATT), _act_dtype(w1)), jax.ShapeDtypeStruct((n, D_C), F32),
                   jax.ShapeDtypeStruct((n, LANES), F32), jax.ShapeDtypeStruct((n, D_RNN), F32),
                   jax.ShapeDtypeStruct((n, D_RNN), F32)],
        compiler_params=_cparams(("arbitrary",)),
        name="even_proj",
    )(x, w1, gq, wuq, gkv, c, sa, sb)


def _kv_proj_t_kernel(ckv_ref, kr_ref, wuk_ref, p_ref, wuvt_ref, ones_ref, k_ref, vt_ref):
    ckv = ckv_ref[...].astype(BF16)
    k = jnp.dot(ckv, wuk_ref[...], preferred_element_type=F32) + _bdot(kr_ref[...], p_ref[...])
    k_ref[...] = k.astype(BF16)
    vt = lax.dot_general(wuvt_ref[...], ckv, (_NT, ((), ())), preferred_element_type=F32)
    vt_ref[...] = (vt + ones_ref[...]).astype(BF16)


def _kv_proj_t(ckv, kr, wuk, place, wuvt, ones_col, n_out):
    n = ckv.shape[0]
    tm = _row_tile(n, 512)
    assert n_out % tm == 0
    last = n // tm - 1
    row_in = lambda w: pl.BlockSpec((tm, w), lambda i: (jnp.minimum(i, last), 0))
    return pl.pallas_call(
        _kv_proj_t_kernel,
        grid=(n_out // tm,),
        in_specs=[row_in(D_C), row_in(LANES), _full(wuk.shape), _full(place.shape), _full(wuvt.shape),
                  _full(ones_col.shape)],
        out_specs=[pl.BlockSpec((tm, D_ATT), lambda i: (i, 0)), pl.BlockSpec((D_ATT, tm), lambda i: (0, i))],
        out_shape=[jax.ShapeDtypeStruct((n_out, D_ATT), BF16), jax.ShapeDtypeStruct((D_ATT, n_out), BF16)],
        compiler_params=_cparams(("arbitrary",)),
        name="kv_proj_t",
    )(ckv, kr, wuk, place, wuvt, ones_col)


def _flash_kernel(qi_ref, kj_ref, last_ref, q_ref, k_ref, vt_ref, o_ref, m_scr, acc_scr, *, tq, tk):
    step = pl.program_id(0)
    i = qi_ref[step]
    j = kj_ref[step]

    @pl.when(j == 0)
    def _():
        m_scr[...] = jnp.full(m_scr.shape, NEG, F32)
        acc_scr[...] = jnp.zeros(acc_scr.shape, F32)

    heads = range(MLA_HEADS)
    slots = [slice(h * HEAD_SLOT, (h + 1) * HEAD_SLOT) for h in heads]

    def accumulate(mask):
        st = [lax.dot_general(k_ref[:, sl], q_ref[:, sl], (_NT, ((), ())), preferred_element_type=F32)
              for sl in slots]
        if mask == "front":
            hidden = jnp.full((PAD_FRONT, tq), NEG, F32)
            st = [jnp.concatenate([hidden, x[PAD_FRONT:]], axis=0) for x in st]
        if mask == "full":
            krow = j * tk + lax.broadcasted_iota(jnp.int32, (tk, tq), 0)
            qrow = i * tq + lax.broadcasted_iota(jnp.int32, (tk, tq), 1)
            keep = ((((qrow - ROW0) >> CHUNK_SHIFT) >= ((krow - ROW0) >> CHUNK_SHIFT))
                    & (krow >= PAD_FRONT))
            st = [jnp.where(keep, x, NEG) for x in st]
        m_prev = [m_scr[h:h + 1, :] for h in heads]
        m_new = [jnp.maximum(mp, jnp.max(x, axis=0, keepdims=True)) for mp, x in zip(m_prev, st)]
        alpha = [jnp.exp2(mp - mn) for mp, mn in zip(m_prev, m_new)]
        pt = [jnp.exp2(x - mn).astype(BF16) for x, mn in zip(st, m_new)]
        pv = [jnp.dot(vt_ref[sl, :], x, preferred_element_type=F32) for x, sl in zip(pt, slots)]
        for h in heads:
            acc_scr[slots[h], :] = alpha[h] * acc_scr[slots[h], :] + pv[h]
            m_scr[h:h + 1, :] = m_new[h]

    last = last_ref[step] == 1
    first = (j == 0) & jnp.logical_not(last)

    @pl.when(last)
    def _():
        accumulate("full")

    @pl.when(first)
    def _():
        accumulate("front")

    @pl.when(jnp.logical_not(last | first))
    def _():
        accumulate("none")

    @pl.when(last)
    def _():
        for sl in slots:
            acc = acc_scr[sl, :]
            o_ref[:, sl] = (acc / acc[D_V:D_V + 1, :]).T.astype(BF16)


def _flash_attention(q, k, vt, tq, tk):
    n = q.shape[0]
    nq = n // tq
    n_kblocks = [-(-(i + 1) * tq // tk) for i in range(nq)]
    assert k.shape[0] >= n_kblocks[-1] * tk
    qi = np.concatenate([np.full(c, i, np.int32) for i, c in enumerate(n_kblocks)])
    kj = np.concatenate([np.arange(c, dtype=np.int32) for c in n_kblocks])
    last = np.concatenate([np.arange(c, dtype=np.int32) == c - 1 for c in n_kblocks]).astype(np.int32)
    grid_spec = pltpu.PrefetchScalarGridSpec(
        num_scalar_prefetch=3,
        grid=(len(qi),),
        in_specs=[pl.BlockSpec((tq, D_ATT), lambda s, qi, kj, last: (qi[s], 0)),
                  pl.BlockSpec((tk, D_ATT), lambda s, qi, kj, last: (kj[s], 0)),
                  pl.BlockSpec((D_ATT, tk), lambda s, qi, kj, last: (0, kj[s]))],
        out_specs=pl.BlockSpec((tq, D_ATT), lambda s, qi, kj, last: (qi[s], 0)),
        scratch_shapes=[pltpu.VMEM((MLA_HEADS, tq), F32), pltpu.VMEM((D_ATT, tq), F32)],
    )
    return pl.pallas_call(
        functools.partial(_flash_kernel, tq=tq, tk=tk),
        grid_spec=grid_spec,
        out_shape=jax.ShapeDtypeStruct((n, D_ATT), BF16),
        compiler_params=_cparams(("arbitrary",)),
        name="flash_attention",
    )(jnp.asarray(qi), jnp.asarray(kj), jnp.asarray(last), q, k, vt)


def _sample_attn_kernel(q_ref, kc_ref, wabs_ref, wuv_ref, o_ref, *, n_keys):
    kc = kc_ref[0]
    l = q_ref.shape[1]
    slots = [slice(h * HEAD_SLOT, (h + 1) * HEAD_SLOT) for h in range(MLA_HEADS)]
    qa = jnp.concatenate([_mm(q_ref[0, :, sl], wabs_ref[h], _NN, True) for h, sl in enumerate(slots)],
                         axis=0)
    keep = lax.broadcasted_iota(jnp.int32, (MLA_HEADS * l, kc.shape[0]), 1) < n_keys
    s = jnp.where(keep, _mm(qa, kc, _NT, True), NEG)
    p = jnp.exp(s - jnp.max(s, axis=-1, keepdims=True))
    p = p / jnp.sum(p, axis=-1, keepdims=True)
    pc = _mm(p, kc[:, :D_C], _NN, True)
    for h, sl in enumerate(slots):
        o_ref[0, :, sl] = _mm(pc[h * l:(h + 1) * l], wuv_ref[h], _NN, True)


def _sample_attention(q, kc, wabs, wuv, n_keys):
    b, l, _ = q.shape
    nk, dk = kc.shape[1:]
    return pl.pallas_call(
        functools.partial(_sample_attn_kernel, n_keys=n_keys),
        grid=(b,),
        in_specs=[pl.BlockSpec((1, l, D_ATT), lambda i: (i, 0, 0)),
                  pl.BlockSpec((1, nk, dk), lambda i: (i, 0, 0)),
                  _full(wabs.shape), _full(wuv.shape)],
        out_specs=pl.BlockSpec((1, l, D_ATT), lambda i: (i, 0, 0)),
        out_shape=jax.ShapeDtypeStruct((b, l, D_ATT), F32),
        compiler_params=_cparams(("arbitrary",)),
        name="sample_attention",
    )(q, kc, wabs, wuv)


def _expm1(x):
    series = x * (1.0 + x * (0.5 + x * (1.0 / 6.0 + x * (1.0 / 24.0 + x * (1.0 / 120.0)))))
    return jnp.where(jnp.abs(x) < 0.05, series, jnp.exp(x) - 1.0)


def _gelu_tanh(x):
    return 0.5 * x * (1.0 + jnp.tanh(0.7978845608028654 * (x + 0.044715 * x * x * x)))


def _rglru_kernel(xr_ref, yg_ref, cw_ref, cb_ref, wrg_ref, brg_ref, wig_ref, big_ref, sp_ref,
                  buf0_ref, h0_ref, rnn_ref, tailx_ref, tailh_ref, prev_scr, h_scr,
                  *, tm, start, end):
    t = pl.program_id(1)

    @pl.when(t == 0)
    def _():
        prev_scr[...] = buf0_ref[0]
        h_scr[...] = jnp.broadcast_to(h0_ref[0], h_scr.shape)

    x = xr_ref[0]
    ext = jnp.concatenate([prev_scr[...], x], axis=0)
    cw = cw_ref[...]
    xc = cb_ref[...] + cw[CONV_W - 1:CONV_W] * x
    for d in range(1, CONV_W):
        xc = xc + cw[CONV_W - 1 - d:CONV_W - d] * pltpu.roll(ext, d, 0)[SUBLANES:]
    prev_scr[...] = x[tm - SUBLANES:]

    r = jax.nn.sigmoid(_wdot(xc, wrg_ref[...]) + brg_ref[...])
    ig = jax.nn.sigmoid(_wdot(xc, wig_ref[...]) + big_ref[...])
    log_a = -LRU_C * r * sp_ref[...]
    a = jnp.exp(log_a)
    b = jnp.sqrt(-_expm1(2.0 * log_a)) * (ig * xc)
    row = lax.broadcasted_iota(jnp.int32, (tm, D_RNN), 0)
    if start > 0:
        live = (t * tm + row) >= start
        a = jnp.where(live, a, 1.0)
        b = jnp.where(live, b, 0.0)
    d = 1
    while d < tm:
        b = a * jnp.where(row >= d, pltpu.roll(b, d, 0), 0.0) + b
        a = a * jnp.where(row >= d, pltpu.roll(a, d, 0), 1.0)
        d *= 2
    h = a * h_scr[0:1] + b
    h_scr[...] = jnp.broadcast_to(h[tm - 1:tm], h_scr.shape)
    rnn_ref[0] = (h * _gelu_tanh(yg_ref[0])).astype(rnn_ref.dtype)

    t_end = (end - 1) // tm
    el = end - t_end * tm

    @pl.when(t == t_end)
    def _():
        tailx_ref[0] = ext[el:el + SUBLANES]
        tailh_ref[0] = h[el - SUBLANES:el]


def _rglru(xr, yg, cw, cb, wrg, brg, wig, big, sp, buf0, h0, start, end):
    b, l, _ = xr.shape
    tm = _row_tile(l, 512)
    seq = pl.BlockSpec((1, tm, D_RNN), lambda i, t: (i, t, 0))
    per_b = lambda r: pl.BlockSpec((1, r, D_RNN), lambda i, t: (i, 0, 0))
    return pl.pallas_call(
        functools.partial(_rglru_kernel, tm=tm, start=start, end=end),
        grid=(b, l // tm),
        in_specs=[seq, seq, _full(cw.shape), _full(cb.shape), _full(wrg.shape), _full(brg.shape),
                  _full(wig.shape), _full(big.shape), _full(sp.shape), per_b(SUBLANES), per_b(1)],
        out_specs=[seq, per_b(SUBLANES), per_b(SUBLANES)],
        out_shape=[jax.ShapeDtypeStruct((b, l, D_RNN), _act_dtype(wrg)),
                   jax.ShapeDtypeStruct((b, SUBLANES, D_RNN), F32),
                   jax.ShapeDtypeStruct((b, SUBLANES, D_RNN), F32)],
        scratch_shapes=[pltpu.VMEM((SUBLANES, D_RNN), F32), pltpu.VMEM((SUBLANES, D_RNN), F32)],
        compiler_params=_cparams(("arbitrary", "arbitrary")),
        name="rglru",
    )(xr, yg, cw, cb, wrg, brg, wig, big, sp, buf0, h0)


def _mix_out_kernel(attn_ref, rnn_ref, x_ref, wa_ref, wr_ref, g_ref, b_ref, o_ref):
    mix = _wdot(attn_ref[...], wa_ref[...]) + _wdot(rnn_ref[...], wr_ref[...])
    o_ref[...] = _ln(ALPHA * x_ref[...] + mix, g_ref[...], b_ref[...])


def _mix_out(attn, rnn, x, wa, wr, g, b):
    n = x.shape[0]
    tm = _row_tile(n, 512)
    row = lambda w: pl.BlockSpec((tm, w), lambda i: (i, 0))
    return pl.pallas_call(
        _mix_out_kernel,
        grid=(n // tm,),
        in_specs=[row(D_ATT), row(D_RNN), row(D_MODEL), _full(wa.shape), _full(wr.shape),
                  _full(g.shape), _full(b.shape)],
        out_specs=row(D_MODEL),
        out_shape=jax.ShapeDtypeStruct((n, D_MODEL), F32),
        compiler_params=_cparams(("arbitrary",)),
        name="mix_out",
    )(attn, rnn, x, wa, wr, g, b)


def _first_argmax(vals, lane):
    m = jnp.max(vals, axis=-1, keepdims=True)
    idx = jnp.min(jnp.where(vals == m, lane, N_EXPERTS), axis=-1, keepdims=True)
    return m, idx


def _router_top2(x, rw, rb):
    logits = _mm(x, rw, _NN, True)
    s = jax.nn.sigmoid(logits)
    sel = s + rb
    lane = lax.broadcasted_iota(jnp.int32, sel.shape, 1)
    grp = lane >> 2
    best = None
    g_best = None
    for g in range(N_GROUPS):
        vals = jnp.where(grp == g, sel, NEG)
        m1, i1 = _first_argmax(vals, lane)
        m2, _ = _first_argmax(jnp.where(lane == i1, NEG, vals), lane)
        score = m1 + m2
        if g == 0:
            best, g_best = score, jnp.zeros_like(i1)
        else:
            upd = score > best
            g_best = jnp.where(upd, g, g_best)
            best = jnp.where(upd, score, best)
    vals = jnp.where(grp == g_best, sel, NEG)
    _, i1 = _first_argmax(vals, lane)
    _, i2 = _first_argmax(jnp.where(lane == i1, NEG, vals), lane)
    w1 = jnp.sum(jnp.where(lane == i1, s, 0.0), axis=-1, keepdims=True)
    w2 = jnp.sum(jnp.where(lane == i2, s, 0.0), axis=-1, keepdims=True)
    den = w1 + w2
    return lane, i1, i2, w1 / den, w2 / den


def _router_gate(x, rw, rb):
    lane, i1, i2, g1, g2 = _router_top2(x, rw, rb)
    return jnp.where(lane == i1, g1, 0.0) + jnp.where(lane == i2, g2, 0.0)


def _moe_kernel(x_ref, rw_ref, rb_ref, wg_ref, wu_ref, wd_ref, g_ref, b_ref, o_ref,
                gate_scr, xb_scr, acc_scr):
    e = pl.program_id(1)

    @pl.when(e == 0)
    def _():
        x = x_ref[...]
        gate = _router_gate(x, rw_ref[...], rb_ref[...])
        for k in range(N_EXPERTS):
            gate_scr[k] = jnp.broadcast_to(gate[:, k:k + 1], gate_scr.shape[1:])
        xb_scr[...] = x.astype(BF16)
        acc_scr[...] = jnp.zeros(acc_scr.shape, F32)

    xb = xb_scr[...]
    hg = jnp.dot(xb, wg_ref[0].astype(BF16), preferred_element_type=F32)
    hu = jnp.dot(xb, wu_ref[0].astype(BF16), preferred_element_type=F32)
    gate_e = gate_scr[e]
    h = jax.nn.silu(hg) * hu * jnp.concatenate([gate_e] * (D_EXPERT // LANES), axis=1)
    acc_scr[...] += jnp.dot(h.astype(BF16), wd_ref[0].astype(BF16), preferred_element_type=F32)

    @pl.when(e == N_EXPERTS - 1)
    def _():
        o_ref[...] = _ln(ALPHA * x_ref[...] + acc_scr[...], g_ref[...], b_ref[...])


def _moe(x, rw, rb, wg, wu, wd, e0, g, b):
    n = x.shape[0]
    tm = _row_tile(n, 512)
    row = pl.BlockSpec((tm, D_MODEL), lambda i, e: (i, 0))
    return pl.pallas_call(
        _moe_kernel,
        grid=(n // tm, N_EXPERTS),
        in_specs=[row, _full(rw.shape), _full(rb.shape),
                  pl.BlockSpec((1, D_MODEL, D_EXPERT), lambda i, e: (e0 + e, 0, 0)),
                  pl.BlockSpec((1, D_MODEL, D_EXPERT), lambda i, e: (e0 + e, 0, 0)),
                  pl.BlockSpec((1, D_EXPERT, D_MODEL), lambda i, e: (e0 + e, 0, 0)),
                  _full(g.shape), _full(b.shape)],
        out_specs=row,
        out_shape=jax.ShapeDtypeStruct((n, D_MODEL), F32),
        scratch_shapes=[pltpu.VMEM((N_EXPERTS, tm, LANES), F32), pltpu.VMEM((tm, D_MODEL), BF16),
                        pltpu.VMEM((tm, D_MODEL), F32)],
        compiler_params=_cparams(("arbitrary", "arbitrary")),
        name="moe",
    )(x, rw, rb, wg, wu, wd, g, b)


M_I1, M_I2, M_R1, M_R2, M_G1, M_G2, M_COLS = 0, 1, 2, 3, 4, 5, 8


def _first_argmax_rows(vals, row):
    m = jnp.max(vals, axis=0, keepdims=True)
    idx = jnp.min(jnp.where(vals == m, row, N_EXPERTS), axis=0, keepdims=True)
    return m, idx


HI16 = -65536


def _pack_bf16_pairs(x):
    w = x.shape[1] // 2
    hi = lax.bitcast_convert_type(x[:, :w].astype(BF16).astype(F32), jnp.int32)
    lo = lax.bitcast_convert_type(x[:, w:].astype(BF16).astype(F32), jnp.int32)
    return (hi & HI16) | lax.shift_right_logical(lo, 16)


def _unpack_bf16_pairs(p):
    hi = lax.bitcast_convert_type(p & HI16, F32)
    lo = lax.bitcast_convert_type(lax.shift_left(p, 16), F32)
    return jnp.concatenate([hi, lo], axis=1)


def _route_kernel(x_ref, rwt_ref, rbc_ref, meta_ref, cnt_ref, xpk_ref, carry_scr, *, tm):
    @pl.when(pl.program_id(0) == 0)
    def _():
        carry_scr[...] = jnp.zeros(carry_scr.shape, F32)

    logits = _mm(rwt_ref[...], x_ref[...], _NT, True)
    s = jax.nn.sigmoid(logits)
    sel = s + rbc_ref[...]
    row = lax.broadcasted_iota(jnp.int32, sel.shape, 0)
    grp = row >> 2
    best = None
    g_best = None
    for g in range(N_GROUPS):
        vals = jnp.where(grp == g, sel, NEG)
        m1, i1 = _first_argmax_rows(vals, row)
        m2, _ = _first_argmax_rows(jnp.where(row == i1, NEG, vals), row)
        score = m1 + m2
        if g == 0:
            best, g_best = score, jnp.zeros_like(i1)
        else:
            upd = score > best
            g_best = jnp.where(upd, g, g_best)
            best = jnp.where(upd, score, best)
    vals = jnp.where(grp == g_best, sel, NEG)
    _, i1 = _first_argmax_rows(vals, row)
    _, i2 = _first_argmax_rows(jnp.where(row == i1, NEG, vals), row)
    w1 = jnp.sum(jnp.where(row == i1, s, 0.0), axis=0, keepdims=True)
    w2 = jnp.sum(jnp.where(row == i2, s, 0.0), axis=0, keepdims=True)
    den = w1 + w2

    chosen = jnp.where((row == i1) | (row == i2), 1.0, 0.0)
    earlier = (lax.broadcasted_iota(jnp.int32, (tm, tm), 0)
               < lax.broadcasted_iota(jnp.int32, (tm, tm), 1)).astype(BF16)
    seen = jnp.dot(chosen.astype(BF16), earlier, preferred_element_type=F32) + carry_scr[:, 0:1]
    r1 = jnp.sum(jnp.where(row == i1, seen, 0.0), axis=0, keepdims=True)
    r2 = jnp.sum(jnp.where(row == i2, seen, 0.0), axis=0, keepdims=True)
    carry_scr[...] = carry_scr[...] + jnp.sum(chosen, axis=1, keepdims=True)
    mrow = lax.broadcasted_iota(jnp.int32, (M_COLS, tm), 0)
    meta = jnp.zeros((M_COLS, tm), F32)
    for c, val in ((M_I1, i1.astype(F32)), (M_I2, i2.astype(F32)), (M_R1, r1), (M_R2, r2),
                   (M_G1, w1 / den), (M_G2, w2 / den)):
        meta = jnp.where(mrow == c, val, meta)
    meta_ref[...] = meta
    cnt_ref[...] = carry_scr[...]
    xpk_ref[...] = _pack_bf16_pairs(x_ref[...])


def _route(x, rwt, rbc):
    n = x.shape[0]
    tm = _row_tile(n, 512)
    return pl.pallas_call(
        functools.partial(_route_kernel, tm=tm),
        grid=(n // tm,),
        in_specs=[pl.BlockSpec((tm, D_MODEL), lambda i: (i, 0)), _full(rwt.shape), _full(rbc.shape)],
        out_specs=[pl.BlockSpec((M_COLS, tm), lambda i: (0, i)), _full((N_EXPERTS, LANES)),
                   pl.BlockSpec((tm, D_MODEL // 2), lambda i: (i, 0))],
        out_shape=[jax.ShapeDtypeStruct((M_COLS, n), F32), jax.ShapeDtypeStruct((N_EXPERTS, LANES), F32),
                   jax.ShapeDtypeStruct((n, D_MODEL // 2), jnp.int32)],
        scratch_shapes=[pltpu.VMEM((N_EXPERTS, LANES), F32)],
        compiler_params=_cparams(("arbitrary",)),
        name="moe_route",
    )(x, rwt, rbc)


def _sc_chunk(per_worker):
    for c in (64, 48, 32, 16, 8):
        if per_worker % c == 0:
            return c
    raise ValueError(per_worker)


def _sc_mesh():
    return plsc.VectorSubcoreMesh(core_axis_name="c", subcore_axis_name="s")


def _sc_scatter2(x, idx1, idx2, n_out):
    n, d = x.shape
    per_w = n // SC_WORKERS
    assert per_w * SC_WORKERS == n
    chunk = _sc_chunk(per_w)

    @functools.partial(
        pl.kernel, mesh=_sc_mesh(), out_type=jax.ShapeDtypeStruct((n_out, d), x.dtype),
        scratch_types=[pltpu.VMEM((chunk,), jnp.int32), pltpu.VMEM((chunk,), jnp.int32),
                       pltpu.VMEM((chunk, d), x.dtype), pltpu.SemaphoreType.DMA])
    def scatter(x_hbm, i1_hbm, i2_hbm, out_hbm, i1_v, i2_v, rows_v, sem):
        base = (lax.axis_index("s") * SC_CORES + lax.axis_index("c")) * per_w

        @pl.loop(0, per_w // chunk)
        def _(c):
            off = pl.multiple_of(base + c * chunk, SUBLANES)
            pltpu.sync_copy(i1_hbm.at[pl.ds(off, chunk)], i1_v)
            pltpu.sync_copy(i2_hbm.at[pl.ds(off, chunk)], i2_v)
            pltpu.sync_copy(x_hbm.at[pl.ds(off, chunk)], rows_v)
            pltpu.async_copy(rows_v, out_hbm.at[i1_v], sem).wait()
            pltpu.async_copy(rows_v, out_hbm.at[i2_v], sem).wait()

    return scatter(x, idx1, idx2)


def _sc_gather(y, idx):
    n = idx.shape[0]
    d = y.shape[1]
    per_w = n // SC_WORKERS
    assert per_w * SC_WORKERS == n
    chunk = _sc_chunk(per_w)

    @functools.partial(
        pl.kernel, mesh=_sc_mesh(), out_type=jax.ShapeDtypeStruct((n, d), y.dtype),
        scratch_types=[pltpu.VMEM((chunk,), jnp.int32), pltpu.VMEM((chunk, d), y.dtype),
                       pltpu.SemaphoreType.DMA])
    def gather(y_hbm, idx_hbm, out_hbm, idx_v, rows_v, sem):
        base = (lax.axis_index("s") * SC_CORES + lax.axis_index("c")) * per_w

        @pl.loop(0, per_w // chunk)
        def _(c):
            off = pl.multiple_of(base + c * chunk, SUBLANES)
            pltpu.sync_copy(idx_hbm.at[pl.ds(off, chunk)], idx_v)
            pltpu.async_copy(y_hbm.at[idx_v], rows_v, sem).wait()
            pltpu.sync_copy(rows_v, out_hbm.at[pl.ds(off, chunk)])

    return gather(y, idx)


def _experts_kernel(te_ref, used_ref, x_ref, wg_ref, wu_ref, wd_ref, o_ref):
    @pl.when(pl.program_id(0) < used_ref[0])
    def _():
        xb = _unpack_bf16_pairs(x_ref[...]).astype(BF16)
        hg = jnp.dot(xb, wg_ref[0].astype(BF16), preferred_element_type=F32)
        hu = jnp.dot(xb, wu_ref[0].astype(BF16), preferred_element_type=F32)
        h = jax.nn.silu(hg) * hu
        y = jnp.dot(h.astype(BF16), wd_ref[0].astype(BF16), preferred_element_type=F32)
        o_ref[...] = _pack_bf16_pairs(y)


def _experts(xg, tile_expert, n_used, wg, wu, wd):
    n_tiles = xg.shape[0] // MOE_TILE
    row = pl.BlockSpec((MOE_TILE, D_MODEL // 2), lambda i, te, used: (i, 0))
    grid_spec = pltpu.PrefetchScalarGridSpec(
        num_scalar_prefetch=2,
        grid=(n_tiles,),
        in_specs=[row,
                  pl.BlockSpec((1, D_MODEL, D_EXPERT), lambda i, te, used: (te[i], 0, 0)),
                  pl.BlockSpec((1, D_MODEL, D_EXPERT), lambda i, te, used: (te[i], 0, 0)),
                  pl.BlockSpec((1, D_EXPERT, D_MODEL), lambda i, te, used: (te[i], 0, 0))],
        out_specs=row,
    )
    return pl.pallas_call(
        _experts_kernel,
        grid_spec=grid_spec,
        out_shape=jax.ShapeDtypeStruct(xg.shape, jnp.int32),
        compiler_params=_cparams(("arbitrary",)),
        name="moe_experts",
    )(tile_expert, n_used, xg, wg, wu, wd)


def _combine_kernel(x_ref, y1_ref, y2_ref, meta_ref, g_ref, b_ref, o_ref):
    meta = meta_ref[...]
    moe = (meta[:, M_G1:M_G1 + 1] * _unpack_bf16_pairs(y1_ref[...])
           + meta[:, M_G2:M_G2 + 1] * _unpack_bf16_pairs(y2_ref[...]))
    o_ref[...] = _ln(ALPHA * x_ref[...] + moe, g_ref[...], b_ref[...])


def _combine(x, y1, y2, meta, g, b):
    n = x.shape[0]
    tm = _row_tile(n, 512)
    row = pl.BlockSpec((tm, D_MODEL), lambda i: (i, 0))
    half = pl.BlockSpec((tm, D_MODEL // 2), lambda i: (i, 0))
    return pl.pallas_call(
        _combine_kernel,
        grid=(n // tm,),
        in_specs=[row, half, half, pl.BlockSpec((tm, M_COLS), lambda i: (i, 0)), _full(g.shape), _full(b.shape)],
        out_specs=row,
        out_shape=jax.ShapeDtypeStruct((n, D_MODEL), F32),
        compiler_params=_cparams(("arbitrary",)),
        name="moe_combine",
    )(x, y1, y2, meta, g, b)


def _moe_sparse(x, rw, rb, wg, wu, wd, e0, g, b):
    n = x.shape[0]
    meta_t, counts, x_packed = _route(x, rw.T, rb.reshape(N_EXPERTS, 1))
    cnt = counts[:, 0].astype(jnp.int32)
    padded = (cnt + MOE_TILE - 1) // MOE_TILE * MOE_TILE
    seg_end = jnp.cumsum(padded)
    seg_start = seg_end - padded
    experts = jnp.arange(N_EXPERTS, dtype=jnp.int32)[:, None]
    start_of = lambda e: jnp.sum(jnp.where(experts == e[None, :], seg_start[:, None], 0), axis=0)
    e1, e2 = meta_t[M_I1].astype(jnp.int32), meta_t[M_I2].astype(jnp.int32)
    pos1 = start_of(e1) + meta_t[M_R1].astype(jnp.int32)
    pos2 = start_of(e2) + meta_t[M_R2].astype(jnp.int32)
    meta = meta_t.T
    n_tiles = -(-2 * n // MOE_TILE) + N_EXPERTS
    tile_start = jnp.arange(n_tiles, dtype=jnp.int32) * MOE_TILE
    tile_expert = e0 + jnp.minimum(jnp.sum(tile_start[:, None] >= seg_end[None, :], axis=1),
                                   N_EXPERTS - 1).astype(jnp.int32)
    n_used = (seg_end[-1:] // MOE_TILE).astype(jnp.int32)
    xg = _sc_scatter2(x_packed, pos1, pos2, n_tiles * MOE_TILE)
    yg = _experts(xg, tile_expert, n_used, wg, wu, wd)
    return _combine(x, _sc_gather(yg, pos1), _sc_gather(yg, pos2), meta, g, b)


def _head_sum(z, ones):
    hi, lo = _split2(z)
    parts = []
    for g in range(D_MODEL // LANES):
        sl = slice(g * LANES, (g + 1) * LANES)
        parts.append(jnp.dot(hi[:, sl], ones, preferred_element_type=F32)
                     + jnp.dot(lo[:, sl], ones, preferred_element_type=F32))
    return jnp.concatenate(parts, axis=1)


def _rwkv_proj_kernel(x_ref, sh0_ref, mu_ref, wr_ref, wk_ref, wv_ref, w0_ref, w1_ref, w2_ref,
                      a0_ref, a1_ref, a2_ref, g1_ref, g2_ref, kkw_ref, kaw_ref, ones_ref,
                      r_ref, lw_ref, k_ref, v_ref, kk_ref, a_ref, g_ref, prev_scr,
                      *, tm, start, end):
    t = pl.program_id(1)

    @pl.when(t == 0)
    def _():
        prev_scr[...] = jnp.zeros(prev_scr.shape, F32)

    x = x_ref[0]
    ext = jnp.concatenate([prev_scr[...], x], axis=0)
    x_prev = pltpu.roll(ext, 1, 0)[SUBLANES:]
    grow = t * tm + lax.broadcasted_iota(jnp.int32, (tm, D_MODEL), 0)
    x_prev = jnp.where(grow == start, sh0_ref[0], x_prev)
    prev_scr[...] = x[tm - SUBLANES:]
    xx = x_prev - x
    mu = mu_ref[...]
    xr, xw, xk, xv, xa, xg = (x + xx * mu[n:n + 1] for n in range(6))
    r = _bdot(xr, wr_ref[...])
    k = _bdot(xk, wk_ref[...])
    v = _bdot(xv, wv_ref[...])
    log_w = -DECAY_SCALE * jax.nn.sigmoid(w0_ref[...] + _bdot(jnp.tanh(_bdot(xw, w1_ref[...])), w2_ref[...]))
    a = jax.nn.sigmoid(a0_ref[...] + _bdot(_bdot(xa, a1_ref[...]), a2_ref[...]))
    g = _bdot(jax.nn.sigmoid(_bdot(xg, g1_ref[...])), g2_ref[...])
    kk = k * kkw_ref[...]
    kk = kk * lax.rsqrt(jnp.maximum(_head_sum(kk * kk, ones_ref[...]), 1e-24))
    k = k * (1.0 + (a - 1.0) * kaw_ref[...])
    live = (grow >= start) & (grow < end)
    r_ref[0] = r.astype(r_ref.dtype)
    lw_ref[0] = jnp.where(live, log_w, 0.0)
    k_ref[0] = jnp.where(live, k, 0.0).astype(k_ref.dtype)
    v_ref[0] = v.astype(v_ref.dtype)
    kk_ref[0] = jnp.where(live, kk, 0.0).astype(kk_ref.dtype)
    a_ref[0] = a.astype(a_ref.dtype)
    g_ref[0] = g.astype(g_ref.dtype)


def _rwkv_proj(x, sh0, od, start, end, act_dtype):
    b, l, _ = x.shape
    tm = _row_tile(l, 256)
    dtypes = [act_dtype, F32] + [act_dtype] * 5
    seq = pl.BlockSpec((1, tm, D_MODEL), lambda i, t: (i, t, 0))
    ws = [od[n] for n in ("mu", "w_r", "w_k", "w_v", "w0", "w1", "w2", "a0", "a1", "a2", "g1", "g2",
                          "k_k", "k_a", "ones")]
    return pl.pallas_call(
        functools.partial(_rwkv_proj_kernel, tm=tm, start=start, end=end),
        grid=(b, l // tm),
        in_specs=[seq, pl.BlockSpec((1, 1, D_MODEL), lambda i, t: (i, 0, 0))] + [_full(w.shape) for w in ws],
        out_specs=[seq] * 7,
        out_shape=[jax.ShapeDtypeStruct((b, l, D_MODEL), dt) for dt in dtypes],
        scratch_shapes=[pltpu.VMEM((SUBLANES, D_MODEL), F32)],
        compiler_params=_cparams(("arbitrary", "arbitrary")),
        name="rwkv_proj",
    )(x, sh0, *ws)


def _wkv_kernel(r_ref, lw_ref, k_ref, v_ref, kk_ref, a_ref, s0_ref, o_ref, sout_ref, s_scr,
                *, c, exact):
    t = pl.program_id(1)

    @pl.when(t == 0)
    def _():
        s_scr[...] = s0_ref[0]

    head0 = lax.broadcasted_iota(jnp.int32, (c, LANES), 1) < RWKV_HEAD
    c2 = 2 * c
    row = lax.broadcasted_iota(jnp.int32, (c2, c2), 0)
    col = lax.broadcasted_iota(jnp.int32, (c2, c2), 1)
    row_hi = jnp.where(row >= c, c, 0)
    col_hi = jnp.where(col >= c, c, 0)
    same = row_hi == col_hi
    rr = row - row_hi
    cc = col - col_hi
    strict = same & (rr > cc)
    incl = same & (rr >= cc)
    eye = (row == col).astype(F32)
    crow = lax.broadcasted_iota(jnp.int32, (c, LANES), 0)

    def cumsum_rows(x):
        d = 1
        while d < c:
            x = x + jnp.where(crow >= d, pltpu.roll(x, d, 0), 0.0)
            d *= 2
        return x

    def stack(x):
        return jnp.concatenate([jnp.where(head0, x, 0.0), jnp.where(head0, 0.0, x)], axis=0)

    mm = functools.partial(_mm, exact=exact)
    pairs = range(RWKV_HEADS // 2)
    for sub in range(r_ref.shape[1] // c):
        rows = slice(sub * c, (sub + 1) * c)
        load = lambda ref: [ref[0, rows, p * LANES:(p + 1) * LANES].astype(F32) for p in pairs]
        r, lw, k, v, kk, a = (load(ref) for ref in (r_ref, lw_ref, k_ref, v_ref, kk_ref, a_ref))
        lc = [cumsum_rows(x) for x in lw]
        lc_end = [x[c - 1:c] for x in lc]
        b = [x * y for x, y in zip(kk, a)]
        lhs = [jnp.concatenate([stack(-kk[p] * jnp.exp(lc[p] - lw[p])), stack(r[p] * jnp.exp(lc[p]))], axis=0)
               for p in pairs]
        g_inv = [jnp.exp(-x) for x in lc]
        rhs = [jnp.concatenate([stack(b[p] * g_inv[p]), stack(k[p] * g_inv[p])], axis=0) for p in pairs]
        pm = [mm(x, y, _NT) for x, y in zip(lhs, rhs)]
        l_ab = [jnp.where(strict, x[:c2, :c2], 0.0) for x in pm]
        l_ak = [jnp.where(strict, x[:c2, c2:], 0.0) for x in pm]
        m_rb = [jnp.where(incl, x[c2:, :c2], 0.0) for x in pm]
        m_rk = [jnp.where(incl, x[c2:, c2:], 0.0) for x in pm]
        vs = [stack(x) for x in v]
        lakv = [mm(x, y, _NN) for x, y in zip(l_ak, vs)]
        mrkv = [mm(x, y, _NN) for x, y in zip(m_rk, vs)]
        tinv = [eye + x for x in l_ab]
        lp = l_ab
        n = 2
        while n < c:
            lp = [mm(x, x, _NN) for x in lp]
            tinv = [x + mm(x, y, _NN) for x, y in zip(tinv, lp)]
            n *= 2
        s = [s_scr[p] for p in pairs]
        xs = [mm(x, y, _NT) for x, y in zip(lhs, s)]
        u = [mm(tinv[p], xs[p][:c2] + lakv[p], _NN) for p in pairs]
        for p in pairs:
            os_ = xs[p][c2:] + mm(m_rb[p], u[p], _NN) + mrkv[p]
            o_ref[0, rows, p * LANES:(p + 1) * LANES] = os_[:c] + os_[c:]
        for p in pairs:
            g_rem = jnp.exp(lc_end[p] - lc[p])
            uv = jnp.concatenate([u[p], vs[p]], axis=0)
            bk = jnp.concatenate([stack(b[p] * g_rem), stack(k[p] * g_rem)], axis=0)
            s_scr[p] = s[p] * jnp.exp(lc_end[p]) + mm(uv, bk, _TN)

    @pl.when(t == pl.num_programs(1) - 1)
    def _():
        sout_ref[0] = s_scr[...]


def _wkv(r, lw, k, v, kk, a, s0, c, exact):
    b, l, _ = r.shape
    rows = 2 * c if l % (2 * c) == 0 else c
    seq = pl.BlockSpec((1, rows, D_MODEL), lambda i, t: (i, t, 0))
    st = pl.BlockSpec((1, RWKV_HEADS // 2, LANES, LANES), lambda i, t: (i, 0, 0, 0))
    return pl.pallas_call(
        functools.partial(_wkv_kernel, c=c, exact=exact),
        grid=(b, l // rows),
        in_specs=[seq] * 6 + [st],
        out_specs=[seq, st],
        out_shape=[jax.ShapeDtypeStruct((b, l, D_MODEL), F32),
                   jax.ShapeDtypeStruct((b, RWKV_HEADS // 2, LANES, LANES), F32)],
        scratch_shapes=[pltpu.VMEM((RWKV_HEADS // 2, LANES, LANES), F32)],
        compiler_params=_cparams(("arbitrary", "arbitrary")),
        name="wkv",
    )(r, lw, k, v, kk, a, s0)


def _rwkv_out_kernel(o_ref, r_ref, k_ref, v_ref, g_ref, x_ref, rk_ref, gng_ref, gnb_ref, wo_ref,
                     ones_ref, lg_ref, lb_ref, y_ref):
    ones = ones_ref[...]
    o = o_ref[...]
    inv = 1.0 / RWKV_HEAD
    mu = _head_sum(o, ones) * inv
    oc = o - mu
    var = _head_sum(oc * oc, ones) * inv
    on = oc * lax.rsqrt(var + GN_EPS) * gng_ref[...] + gnb_ref[...]
    r, k, v, g = (ref[...].astype(F32) for ref in (r_ref, k_ref, v_ref, g_ref))
    on = on + _head_sum(r * k * rk_ref[...], ones) * v
    out = _bdot(on * g, wo_ref[...])
    y_ref[...] = _ln(ALPHA * x_ref[...] + out, lg_ref[...], lb_ref[...])


def _rwkv_out(o, r, k, v, g, x, od, lg, lb):
    n = x.shape[0]
    tm = _row_tile(n, 512)
    row = pl.BlockSpec((tm, D_MODEL), lambda i: (i, 0))
    ws = [od["r_k"], od["ln_g"], od["ln_b"], od["w_o"], od["ones"], lg, lb]
    return pl.pallas_call(
        _rwkv_out_kernel,
        grid=(n // tm,),
        in_specs=[row] * 6 + [_full(w.shape) for w in ws],
        out_specs=row,
        out_shape=jax.ShapeDtypeStruct((n, D_MODEL), F32),
        compiler_params=_cparams(("arbitrary",)),
        name="rwkv_out",
    )(o, r, k, v, g, x, *ws)


def _rope_tables(pos):
    half = D_ROPE // 2
    freq = ROPE_BASE ** (-jnp.arange(half, dtype=F32) / half)
    ang_t = freq[:, None] * pos.astype(F32)[None, :]
    cos_t, sin_t = lax.optimization_barrier((jnp.cos(ang_t), jnp.sin(ang_t)))
    cos, sin = cos_t.T, sin_t.T
    n = pos.shape[0]
    ones = jnp.ones((n, D_NOPE), F32)
    zeros = jnp.zeros((n, D_NOPE), F32)
    z16 = jnp.zeros((n, half), F32)
    tail1 = jnp.ones((n, LANES - D_NOPE - D_ROPE), F32)
    tail0 = jnp.zeros((n, LANES - D_NOPE - D_ROPE), F32)
    c = jnp.concatenate([ones, cos, cos, tail1], axis=1)
    sa = jnp.concatenate([zeros, -sin, z16, tail0], axis=1)
    sb = jnp.concatenate([zeros, z16, sin, tail0], axis=1)
    return c, sa, sb


def _slot_cols(w, width):
    k, h, _ = w.shape
    return jnp.pad(w, ((0, 0), (0, 0), (0, HEAD_SLOT - width))).reshape(k, h * HEAD_SLOT)


def _block_diag(w):
    n, c, d = w.shape
    eye = jnp.eye(n, dtype=w.dtype)
    return (eye[:, None, :, None] * w[:, :, None, :]).reshape(n * c, n * d)


def _row2(v):
    return v.reshape(1, -1).astype(F32)


def _prep_even(w_in, g_q, w_uq, g_kv, w_uk, w_uv, conv_w, conv_b, w_rg, b_rg, w_ig, b_ig, lam, w_out):
    off_ckv, off_kr = D_CQ, D_CQ + D_C
    off_xr = off_kr + D_ROPE
    off_y = off_xr + D_RNN
    kr_cols = jnp.pad(w_in[:, off_kr:off_xr], ((0, 0), (D_NOPE, LANES - D_NOPE - D_ROPE)))
    w_in, w_uq, w_uk, w_uv, w_rg, w_ig, w_out = (
        w.astype(F32) for w in (w_in, w_uq, w_uk, w_uv, w_rg, w_ig, w_out))
    w1 = jnp.concatenate([w_in[:, :off_ckv], w_in[:, off_ckv:off_kr], w_in[:, off_xr:off_y],
                          w_in[:, off_y:], kr_cols], axis=1)
    wuq = _slot_cols(w_uq, D_NOPE + D_ROPE)
    wukv = jnp.concatenate([_slot_cols(w_uk, D_NOPE), _slot_cols(w_uv, D_V)], axis=1)
    place = np.zeros((LANES, D_ATT), np.float32)
    for h in range(MLA_HEADS):
        for cidx in range(D_ROPE):
            place[D_NOPE + cidx, h * HEAD_SLOT + D_NOPE + cidx] = 1.0
    wa = jnp.pad(w_out[:MLA_HEADS * D_V].reshape(MLA_HEADS, D_V, D_MODEL),
                 ((0, 0), (0, HEAD_SLOT - D_V), (0, 0))).reshape(D_ATT, D_MODEL)
    wr = w_out[MLA_HEADS * D_V:]
    keep_rope = np.zeros((HEAD_SLOT, LANES), np.float32)
    keep_rope[D_NOPE:D_NOPE + D_ROPE, D_NOPE:D_NOPE + D_ROPE] = np.eye(D_ROPE, dtype=np.float32)
    uk_t = jnp.pad(jnp.transpose(w_uk, (1, 2, 0)), ((0, 0), (0, HEAD_SLOT - D_NOPE), (0, 0)))
    wabs = jnp.concatenate([uk_t, jnp.broadcast_to(keep_rope, (MLA_HEADS, HEAD_SLOT, LANES))], axis=2)
    wuv = jnp.pad(jnp.transpose(w_uv, (1, 0, 2)), ((0, 0), (0, 0), (0, HEAD_SLOT - D_V)))
    return dict(
        wabs=wabs, wuv=wuv,
        w1=w1, gq=_row2(g_q), wuq=wuq, gkv=_row2(g_kv), wukv=wukv, place=jnp.asarray(place, F32),
        cw=conv_w.astype(F32), cb=_row2(conv_b), wrg=_block_diag(w_rg), brg=_row2(b_rg),
        wig=_block_diag(w_ig), big=_row2(b_ig), sp=_row2(jax.nn.softplus(-lam.astype(F32))),
        wa=wa, wr=wr)


_EVEN_MATMUL_WEIGHTS = ("w1", "wuq", "wukv", "place", "wrg", "wig", "wa", "wr")


def _single_pass(ev):
    return {n: (w.astype(BF16) if n in _EVEN_MATMUL_WEIGHTS else w) for n, w in ev.items()}


def _prep_odd(mu, w_r, w_k, w_v, w0, w1, w2, a0, a1, a2, g1, g2, k_k, k_a, r_k, ln_g, ln_b, w_o):
    ones = np.zeros((LANES, LANES), np.float32)
    ones[:RWKV_HEAD, :RWKV_HEAD] = 1.0
    ones[RWKV_HEAD:, RWKV_HEAD:] = 1.0
    return dict(
        mu=jnp.pad(mu.astype(F32), ((0, SUBLANES - mu.shape[0]), (0, 0))),
        w_r=w_r.astype(BF16), w_k=w_k.astype(BF16), w_v=w_v.astype(BF16), w0=_row2(w0),
        w1=w1.astype(BF16), w2=w2.astype(BF16), a0=_row2(a0), a1=a1.astype(BF16), a2=a2.astype(BF16),
        g1=g1.astype(BF16), g2=g2.astype(BF16), k_k=_row2(k_k), k_a=_row2(k_a), r_k=_row2(r_k),
        ln_g=_row2(ln_g), ln_b=_row2(ln_b), w_o=w_o.astype(BF16), ones=jnp.asarray(ones, BF16))


def _pair_states(s):
    b = s.shape[0]
    s = s.reshape(b, RWKV_HEADS // 2, 2, RWKV_HEAD, RWKV_HEAD).astype(F32)
    eye = jnp.eye(2, dtype=F32)
    out = s[:, :, :, :, None, :] * eye[None, None, :, None, :, None]
    return out.reshape(b, RWKV_HEADS // 2, LANES, LANES)


def _unpair_states(s):
    b = s.shape[0]
    s = s.reshape(b, RWKV_HEADS // 2, 2, RWKV_HEAD, 2, RWKV_HEAD)
    return jnp.stack([s[:, :, 0, :, 0, :], s[:, :, 1, :, 1, :]], axis=2).reshape(
        b, RWKV_HEADS, RWKV_HEAD, RWKV_HEAD)


def _round_up(n, m):
    return -(-n // m) * m


def kernel(x_prompt, x_sample, cache_ckv, cache_krope, state_conv, state_lru, state_shift, state_wkv,
           meta_tokens, ev_w_in, ev_g_q, ev_w_uq, ev_g_kv, ev_w_uk, ev_w_uv, ev_conv_w, ev_conv_b,
           ev_w_rg, ev_b_rg, ev_w_ig, ev_b_ig, ev_lru_lambda, ev_w_out, od_mu, od_w_r, od_w_k, od_w_v,
           od_w0, od_w1, od_w2, od_a0, od_a1, od_a2, od_g1, od_g2, od_k_k, od_k_a, od_r_k, od_ln_g,
           od_ln_b, od_w_o, ln_g, ln_b, router_w, router_b, exp_w_gate, exp_w_up, exp_w_down):
    assert x_prompt.shape[0] == 1 and x_prompt.shape[2] == D_MODEL
    seq = x_prompt.shape[1]
    assert seq % CHUNK == 0
    bs, ls, _ = x_sample.shape
    past = cache_ckv.shape[2]
    ns = bs * ls
    end = ROW0 + seq
    tp = _round_up(end, 512)

    ev = _prep_even(ev_w_in[0], ev_g_q[0], ev_w_uq[0], ev_g_kv[0], ev_w_uk[0], ev_w_uv[0], ev_conv_w[0],
                    ev_conv_b[0], ev_w_rg[0], ev_b_rg[0], ev_w_ig[0], ev_b_ig[0], ev_lru_lambda[0],
                    ev_w_out[0])
    od = _prep_odd(od_mu[0], od_w_r[0], od_w_k[0], od_w_v[0], od_w0[0], od_w1[0], od_w2[0], od_a0[0],
                   od_a1[0], od_a2[0], od_g1[0], od_g2[0], od_k_k[0], od_k_a[0], od_r_k[0], od_ln_g[0],
                   od_ln_b[0], od_w_o[0])
    rw = router_w.astype(F32)
    rb = _row2(router_b)
    wg, wu, wd = (w.reshape((DEPTH * N_EXPERTS,) + w.shape[2:]) for w in (exp_w_gate, exp_w_up, exp_w_down))
    lng = ln_g.astype(F32)[:, :, None, :]
    lnb = ln_b.astype(F32)[:, :, None, :]

    def moe(x, layer):
        sparse = x.shape[0] % (SC_WORKERS * SUBLANES) == 0 and x.shape[0] >= SPARSE_MIN_ROWS
        fn = _moe_sparse if sparse else _moe
        return fn(x, rw, rb, wg, wu, wd, layer * N_EXPERTS, lng[layer, 1], lnb[layer, 1])

    xp = jnp.concatenate([jnp.zeros((PAD_FRONT, D_MODEL), F32), meta_tokens.astype(F32),
                          x_prompt[0].astype(F32), jnp.zeros((tp - end, D_MODEL), F32)], axis=0)
    tabs_p = _rope_tables(jnp.maximum(jnp.arange(tp) - PAD_FRONT, 0))
    evb = _single_pass(ev)
    q_p, ckv_p, kr_p, xr_p, yg_p = _even_proj(xp, evb["w1"], ev["gq"], evb["wuq"], ev["gkv"], tabs_p,
                                              MLA_SCALE * LOG2E)
    ones_col = np.zeros((D_ATT, 1), np.float32)
    ones_col[D_V::HEAD_SLOT] = 1.0
    k_p, vt_p = _kv_proj_t(ckv_p, kr_p, evb["wukv"][:, :D_ATT], evb["place"], evb["wukv"][:, D_ATT:].T,
                           jnp.asarray(ones_col), _round_up(tp, FLASH_TK))
    attn_p = _flash_attention(q_p, k_p, vt_p, FLASH_TQ, FLASH_TK)
    rnn_p, tailx_p, tailh_p = _rglru(
        xr_p[None], yg_p[None], ev["cw"], ev["cb"], evb["wrg"], ev["brg"], evb["wig"], ev["big"], ev["sp"],
        jnp.zeros((1, SUBLANES, D_RNN), F32), jnp.zeros((1, 1, D_RNN), F32), PAD_FRONT, end)
    x1_p = _mix_out(attn_p, rnn_p[0], xp, evb["wa"], evb["wr"], lng[0, 0], lnb[0, 0])
    x2_p = moe(x1_p, 0)

    xs = x_sample.reshape(ns, D_MODEL).astype(F32)
    pos_s = jnp.tile(N_META + past + jnp.arange(ls), bs)
    q_s, ckv_s, kr_s, xr_s, yg_s = _even_proj(xs, ev["w1"], ev["gq"], ev["wuq"], ev["gkv"], _rope_tables(pos_s),
                                              MLA_SCALE)
    _, ckv_m, kr_m, _, _ = _even_proj(meta_tokens.astype(F32), ev["w1"], ev["gq"], ev["wuq"], ev["gkv"],
                                      _rope_tables(jnp.arange(N_META)), MLA_SCALE)
    n_keys = N_META + past + ls
    nk_pad = _round_up(n_keys, LANES)
    meta_ckv = jnp.broadcast_to(ckv_m[None], (bs, N_META, D_C))
    meta_kr = jnp.broadcast_to(kr_m[None], (bs, N_META, LANES))
    cache_kr = jnp.pad(cache_krope[0].astype(F32), ((0, 0), (0, 0), (D_NOPE, LANES - D_NOPE - D_ROPE)))
    all_ckv = jnp.concatenate([meta_ckv, cache_ckv[0].astype(F32), ckv_s.reshape(bs, ls, D_C),
                               jnp.zeros((bs, nk_pad - n_keys, D_C), F32)], axis=1)
    all_kr = jnp.concatenate([meta_kr, cache_kr, kr_s.reshape(bs, ls, LANES),
                              jnp.zeros((bs, nk_pad - n_keys, LANES), F32)], axis=1)
    attn_s = _sample_attention(q_s.reshape(bs, ls, D_ATT), jnp.concatenate([all_ckv, all_kr], axis=2),
                               ev["wabs"], ev["wuv"], n_keys)
    buf0_s = jnp.pad(state_conv[0].astype(F32), ((0, 0), (SUBLANES - (CONV_W - 1), 0), (0, 0)))
    rnn_s, tailx_s, tailh_s = _rglru(
        xr_s.reshape(bs, ls, D_RNN), yg_s.reshape(bs, ls, D_RNN), ev["cw"], ev["cb"], ev["wrg"], ev["brg"],
        ev["wig"], ev["big"], ev["sp"], buf0_s, state_lru[0].astype(F32)[:, None, :], 0, ls)
    x1_s = _mix_out(attn_s.reshape(ns, D_ATT), rnn_s.reshape(ns, D_RNN), xs, ev["wa"], ev["wr"],
                    lng[0, 0], lnb[0, 0])
    x2_s = moe(x1_s, 0)

    r_p, lw_p, kk_in_p, v1_p, kkn_p, a_p, g_p = _rwkv_proj(
        x2_p[None], jnp.zeros((1, 1, D_MODEL), F32), od, PAD_FRONT, end, BF16)
    o_p, s_p = _wkv(r_p, lw_p, kk_in_p, v1_p, kkn_p, a_p,
                    jnp.zeros((1, RWKV_HEADS // 2, LANES, LANES), F32), CHUNK, False)
    x3_p = _rwkv_out(o_p[0], r_p[0], kk_in_p[0], v1_p[0], g_p[0], x2_p, od, lng[1, 0], lnb[1, 0])
    x4_p = moe(x3_p, 1)

    x2_s3 = x2_s.reshape(bs, ls, D_MODEL)
    r_s, lw_s, kk_in_s, v1_s, kkn_s, a_s, g_s = _rwkv_proj(
        x2_s3, state_shift[0].astype(F32)[:, None, :], od, 0, ls, F32)
    to_chunk = lambda z: jnp.pad(z, ((0, 0), (0, _round_up(ls, CHUNK) - ls), (0, 0)))
    o_s, s_s = _wkv(*(to_chunk(z) for z in (r_s, lw_s, kk_in_s, v1_s, kkn_s, a_s)),
                    _pair_states(state_wkv[0]), CHUNK, True)
    o_s = o_s[:, :ls]
    flat = lambda z: z.reshape(ns, D_MODEL)
    x3_s = _rwkv_out(flat(o_s), flat(r_s), flat(kk_in_s), flat(v1_s), flat(g_s), x2_s, od,
                     lng[1, 0], lnb[1, 0])
    x4_s = moe(x3_s, 1)

    dt = x_prompt.dtype
    nb = CONV_W - 1
    return (
        x4_p[ROW0:end][None].astype(dt),
        x4_s.reshape(bs, ls, D_MODEL).astype(dt),
        ckv_p[PAD_FRONT:end][None, None].astype(dt),
        kr_p[PAD_FRONT:end, D_NOPE:D_NOPE + D_ROPE][None, None].astype(dt),
        tailx_p[:, SUBLANES - nb:][None].astype(dt),
        tailh_p[:, SUBLANES - 1][None].astype(dt),
        x2_p[end - 1][None, None].astype(dt),
        _unpair_states(s_p)[None].astype(dt),
        ckv_s.reshape(bs, ls, D_C)[None].astype(dt),
        kr_s.reshape(bs, ls, LANES)[:, :, D_NOPE:D_NOPE + D_ROPE][None].astype(dt),
        tailx_s[:, SUBLANES - nb:][None].astype(dt),
        tailh_s[:, SUBLANES - 1][None].astype(dt),
        x2_s3[:, ls - 1][None].astype(dt),
        _unpair_states(s_s)[None].astype(dt),
    )
```

```python
import functools

import numpy as np
import jax
import jax.numpy as jnp
from jax import lax
from jax.experimental import pallas as pl
from jax.experimental.pallas import tpu as pltpu
from jax.experimental.pallas import tpu_sc as plsc

F32 = jnp.float32
BF16 = jnp.bfloat16

D_MODEL = 1024
N_META = 16
CHUNK = 64
CHUNK_SHIFT = 6
LN_EPS = 1e-5
RMS_EPS = 1e-6
DEPTH = 2
ALPHA = (2 * DEPTH) ** 0.25
MLA_HEADS = 8
D_NOPE = 64
D_ROPE = 32
D_V = 64
D_C = 256
D_CQ = 384
ROPE_BASE = 10000.0
MLA_SCALE = (D_NOPE + D_ROPE) ** -0.5
D_RNN = 512
LRU_BLOCKS = 8
LRU_BLOCK_W = D_RNN // LRU_BLOCKS
CONV_W = 4
LRU_C = 8.0
RWKV_HEAD = 64
RWKV_HEADS = D_MODEL // RWKV_HEAD
DECAY_SCALE = float(np.exp(-0.5))
GN_EPS = 64e-5
N_EXPERTS = 16
N_GROUPS = 4
EXPERTS_PER_GROUP = N_EXPERTS // N_GROUPS
D_EXPERT = 512

LANES = 128
SUBLANES = 8
HEAD_SLOT = LANES
D_ATT = MLA_HEADS * HEAD_SLOT
PAD_FRONT = CHUNK - N_META
ROW0 = PAD_FRONT + N_META
NEG = -1e30
LOG2E = 1.4426950408889634
SC_CORES = 2
SC_SUBCORES = 16
SC_WORKERS = SC_CORES * SC_SUBCORES
MOE_TILE = 512
FLASH_TQ = 512
FLASH_TK = 1024
SPARSE_MIN_ROWS = 1024
VMEM_LIMIT = 56 * 1024 * 1024

C_CQ = 0
C_CKV = D_CQ
C_XR = C_CKV + D_C
C_YG = C_XR + D_RNN
C_KR = C_YG + D_RNN
N_COL = C_KR + LANES


def _cparams(sem):
    return pltpu.CompilerParams(dimension_semantics=sem, vmem_limit_bytes=VMEM_LIMIT)


def _row_tile(n, cap):
    for t in (1024, 512, 256, 128, 64, 32, 16, 8):
        if t <= cap and n % t == 0:
            return t
    return n


def _full(shape):
    zeros = (0,) * len(shape)
    return pl.BlockSpec(shape, lambda *_: zeros)


def _ln(x, g, b):
    mu = jnp.mean(x, axis=-1, keepdims=True)
    xc = x - mu
    var = jnp.mean(xc * xc, axis=-1, keepdims=True)
    return xc * lax.rsqrt(var + LN_EPS) * g + b


def _bdot(a, b):
    return jnp.dot(a.astype(BF16), b.astype(BF16), preferred_element_type=F32)


def _split2(x):
    hi = x.astype(BF16)
    return hi, (x - hi.astype(F32)).astype(BF16)


_NN = ((1,), (0,))
_NT = ((1,), (1,))
_TN = ((0,), (0,))


def _mm(a, b, dims, exact):
    dn = (dims, ((), ()))
    if not exact:
        return lax.dot_general(a.astype(BF16), b.astype(BF16), dn, preferred_element_type=F32)
    ah, al = _split2(a)
    bh, bl = _split2(b)
    return (lax.dot_general(ah, bh, dn, preferred_element_type=F32)
            + lax.dot_general(al, bh, dn, preferred_element_type=F32)
            + lax.dot_general(ah, bl, dn, preferred_element_type=F32))


def _wdot(a, w):
    return _mm(a, w, _NN, exact=(w.dtype == F32))


def _act_dtype(w):
    return F32 if w.dtype == F32 else BF16


def _rope_slot(x, c, sa, sb):
    return x * c + pltpu.roll(x, LANES - D_ROPE // 2, 1) * sa + pltpu.roll(x, D_ROPE // 2, 1) * sb


def _even_proj_kernel(x_ref, w1_ref, gq_ref, wuq_ref, gkv_ref, c_ref, sa_ref, sb_ref,
                      q_ref, ckv_ref, kr_ref, xr_ref, yg_ref, *, q_scale):
    u = _wdot(x_ref[...], w1_ref[...])
    cq = u[:, C_CQ:C_CQ + D_CQ]
    cq = cq * lax.rsqrt(jnp.mean(cq * cq, axis=-1, keepdims=True) + RMS_EPS) * gq_ref[...]
    q = _wdot(cq, wuq_ref[...])
    c, sa, sb = c_ref[...], sa_ref[...], sb_ref[...]
    for h in range(MLA_HEADS):
        sl = slice(h * HEAD_SLOT, (h + 1) * HEAD_SLOT)
        q_ref[:, sl] = (_rope_slot(q[:, sl], c, sa, sb) * q_scale).astype(q_ref.dtype)
    ckv = u[:, C_CKV:C_CKV + D_C]
    ckv_ref[...] = ckv * lax.rsqrt(jnp.mean(ckv * ckv, axis=-1, keepdims=True) + RMS_EPS) * gkv_ref[...]
    kr_ref[...] = _rope_slot(u[:, C_KR:C_KR + LANES], c, sa, sb)
    xr_ref[...] = u[:, C_XR:C_XR + D_RNN]
    yg_ref[...] = u[:, C_YG:C_YG + D_RNN]


def _even_proj(x, w1, gq, wuq, gkv, tabs, q_scale):
    n = x.shape[0]
    tm = _row_tile(n, 512)
    row = lambda w: pl.BlockSpec((tm, w), lambda i: (i, 0))
    c, sa, sb = tabs
    return pl.pallas_call(
        functools.partial(_even_proj_kernel, q_scale=q_scale),
        grid=(n // tm,),
        in_specs=[row(D_MODEL), _full(w1.shape), _full(gq.shape), _full(wuq.shape), _full(gkv.shape),
                  row(LANES), row(LANES), row(LANES)],
        out_specs=[row(D_ATT), row(D_C), row(LANES), row(D_RNN), row(D_RNN)],
        out_shape=[jax.ShapeDtypeStruct((n, D_ATT), _act_dtype(w1)), jax.ShapeDtypeStruct((n, D_C), F32),
                   jax.ShapeDtypeStruct((n, LANES), F32), jax.ShapeDtypeStruct((n, D_RNN), F32),
                   jax.ShapeDtypeStruct((n, D_RNN), F32)],
        compiler_params=_cparams(("arbitrary",)),
        name="even_proj",
    )(x, w1, gq, wuq, gkv, c, sa, sb)


def _kv_proj_t_kernel(ckv_ref, kr_ref, wuk_ref, p_ref, wuvt_ref, ones_ref, k_ref, vt_ref):
    ckv = ckv_ref[...].astype(BF16)
    k = jnp.dot(ckv, wuk_ref[...], preferred_element_type=F32) + _bdot(kr_ref[...], p_ref[...])
    k_ref[...] = k.astype(BF16)
    vt = lax.dot_general(wuvt_ref[...], ckv, (_NT, ((), ())), preferred_element_type=F32)
    vt_ref[...] = (vt + ones_ref[...]).astype(BF16)


def _kv_proj_t(ckv, kr, wuk, place, wuvt, ones_col, n_out):
    n = ckv.shape[0]
    tm = _row_tile(n, 512)
    assert n_out % tm == 0
    last = n // tm - 1
    row_in = lambda w: pl.BlockSpec((tm, w), lambda i: (jnp.minimum(i, last), 0))
    return pl.pallas_call(
        _kv_proj_t_kernel,
        grid=(n_out // tm,),
        in_specs=[row_in(D_C), row_in(LANES), _full(wuk.shape), _full(place.shape), _full(wuvt.shape),
                  _full(ones_col.shape)],
        out_specs=[pl.BlockSpec((tm, D_ATT), lambda i: (i, 0)), pl.BlockSpec((D_ATT, tm), lambda i: (0, i))],
        out_shape=[jax.ShapeDtypeStruct((n_out, D_ATT), BF16), jax.ShapeDtypeStruct((D_ATT, n_out), BF16)],
        compiler_params=_cparams(("arbitrary",)),
        name="kv_proj_t",
    )(ckv, kr, wuk, place, wuvt, ones_col)


def _flash_kernel(qi_ref, kj_ref, last_ref, q_ref, k_ref, vt_ref, o_ref, m_scr, acc_scr, *, tq, tk):
    step = pl.program_id(0)
    i = qi_ref[step]
    j = kj_ref[step]

    @pl.when(j == 0)
    def _():
        m_scr[...] = jnp.full(m_scr.shape, NEG, F32)
        acc_scr[...] = jnp.zeros(acc_scr.shape, F32)

    heads = range(MLA_HEADS)
    slots = [slice(h * HEAD_SLOT, (h + 1) * HEAD_SLOT) for h in heads]

    def accumulate(mask):
        st = [lax.dot_general(k_ref[:, sl], q_ref[:, sl], (_NT, ((), ())), preferred_element_type=F32)
              for sl in slots]
        if mask == "front":
            hidden = jnp.full((PAD_FRONT, tq), NEG, F32)
            st = [jnp.concatenate([hidden, x[PAD_FRONT:]], axis=0) for x in st]
        if mask == "full":
            krow = j * tk + lax.broadcasted_iota(jnp.int32, (tk, tq), 0)
            qrow = i * tq + lax.broadcasted_iota(jnp.int32, (tk, tq), 1)
            keep = ((((qrow - ROW0) >> CHUNK_SHIFT) >= ((krow - ROW0) >> CHUNK_SHIFT))
                    & (krow >= PAD_FRONT))
            st = [jnp.where(keep, x, NEG) for x in st]
        m_prev = [m_scr[h:h + 1, :] for h in heads]
        m_new = [jnp.maximum(mp, jnp.max(x, axis=0, keepdims=True)) for mp, x in zip(m_prev, st)]
        alpha = [jnp.exp2(mp - mn) for mp, mn in zip(m_prev, m_new)]
        pt = [jnp.exp2(x - mn).astype(BF16) for x, mn in zip(st, m_new)]
        pv = [jnp.dot(vt_ref[sl, :], x, preferred_element_type=F32) for x, sl in zip(pt, slots)]
        for h in heads:
            acc_scr[slots[h], :] = alpha[h] * acc_scr[slots[h], :] + pv[h]
            m_scr[h:h + 1, :] = m_new[h]

    last = last_ref[step] == 1
    first = (j == 0) & jnp.logical_not(last)

    @pl.when(last)
    def _():
        accumulate("full")

    @pl.when(first)
    def _():
        accumulate("front")

    @pl.when(jnp.logical_not(last | first))
    def _():
        accumulate("none")

    @pl.when(last)
    def _():
        for sl in slots:
            acc = acc_scr[sl, :]
            o_ref[:, sl] = (acc / acc[D_V:D_V + 1, :]).T.astype(BF16)


def _flash_attention(q, k, vt, tq, tk):
    n = q.shape[0]
    nq = n // tq
    n_kblocks = [-(-(i + 1) * tq // tk) for i in range(nq)]
    assert k.shape[0] >= n_kblocks[-1] * tk
    qi = np.concatenate([np.full(c, i, np.int32) for i, c in enumerate(n_kblocks)])
    kj = np.concatenate([np.arange(c, dtype=np.int32) for c in n_kblocks])
    last = np.concatenate([np.arange(c, dtype=np.int32) == c - 1 for c in n_kblocks]).astype(np.int32)
    grid_spec = pltpu.PrefetchScalarGridSpec(
        num_scalar_prefetch=3,
        grid=(len(qi),),
        in_specs=[pl.BlockSpec((tq, D_ATT), lambda s, qi, kj, last: (qi[s], 0)),
                  pl.BlockSpec((tk, D_ATT), lambda s, qi, kj, last: (kj[s], 0)),
                  pl.BlockSpec((D_ATT, tk), lambda s, qi, kj, last: (0, kj[s]))],
        out_specs=pl.BlockSpec((tq, D_ATT), lambda s, qi, kj, last: (qi[s], 0)),
        scratch_shapes=[pltpu.VMEM((MLA_HEADS, tq), F32), pltpu.VMEM((D_ATT, tq), F32)],
    )
    return pl.pallas_call(
        functools.partial(_flash_kernel, tq=tq, tk=tk),
        grid_spec=grid_spec,
        out_shape=jax.ShapeDtypeStruct((n, D_ATT), BF16),
        compiler_params=_cparams(("arbitrary",)),
        name="flash_attention",
    )(jnp.asarray(qi), jnp.asarray(kj), jnp.asarray(last), q, k, vt)


def _sample_attn_kernel(q_ref, kc_ref, wabs_ref, wuv_ref, o_ref, *, n_keys):
    kc = kc_ref[0]
    l = q_ref.shape[1]
    slots = [slice(h * HEAD_SLOT, (h + 1) * HEAD_SLOT) for h in range(MLA_HEADS)]
    qa = jnp.concatenate([_mm(q_ref[0, :, sl], wabs_ref[h], _NN, True) for h, sl in enumerate(slots)],
                         axis=0)
    keep = lax.broadcasted_iota(jnp.int32, (MLA_HEADS * l, kc.shape[0]), 1) < n_keys
    s = jnp.where(keep, _mm(qa, kc, _NT, True), NEG)
    p = jnp.exp(s - jnp.max(s, axis=-1, keepdims=True))
    p = p / jnp.sum(p, axis=-1, keepdims=True)
    pc = _mm(p, kc[:, :D_C], _NN, True)
    for h, sl in enumerate(slots):
        o_ref[0, :, sl] = _mm(pc[h * l:(h + 1) * l], wuv_ref[h], _NN, True)


def _sample_attention(q, kc, wabs, wuv, n_keys):
    b, l, _ = q.shape
    nk, dk = kc.shape[1:]
    return pl.pallas_call(
        functools.partial(_sample_attn_kernel, n_keys=n_keys),
        grid=(b,),
        in_specs=[pl.BlockSpec((1, l, D_ATT), lambda i: (i, 0, 0)),
                  pl.BlockSpec((1, nk, dk), lambda i: (i, 0, 0)),
                  _full(wabs.shape), _full(wuv.shape)],
        out_specs=pl.BlockSpec((1, l, D_ATT), lambda i: (i, 0, 0)),
        out_shape=jax.ShapeDtypeStruct((b, l, D_ATT), F32),
        compiler_params=_cparams(("arbitrary",)),
        name="sample_attention",
    )(q, kc, wabs, wuv)


def _expm1(x):
    series = x * (1.0 + x * (0.5 + x * (1.0 / 6.0 + x * (1.0 / 24.0 + x * (1.0 / 120.0)))))
    return jnp.where(jnp.abs(x) < 0.05, series, jnp.exp(x) - 1.0)


def _gelu_tanh(x):
    return 0.5 * x * (1.0 + jnp.tanh(0.7978845608028654 * (x + 0.044715 * x * x * x)))


def _rglru_kernel(xr_ref, yg_ref, cw_ref, cb_ref, wrg_ref, brg_ref, wig_ref, big_ref, sp_ref,
                  buf0_ref, h0_ref, rnn_ref, tailx_ref, tailh_ref, prev_scr, h_scr,
                  *, tm, start, end):
    t = pl.program_id(1)

    @pl.when(t == 0)
    def _():
        prev_scr[...] = buf0_ref[0]
        h_scr[...] = jnp.broadcast_to(h0_ref[0], h_scr.shape)

    x = xr_ref[0]
    ext = jnp.concatenate([prev_scr[...], x], axis=0)
    cw = cw_ref[...]
    xc = cb_ref[...] + cw[CONV_W - 1:CONV_W] * x
    for d in range(1, CONV_W):
        xc = xc + cw[CONV_W - 1 - d:CONV_W - d] * pltpu.roll(ext, d, 0)[SUBLANES:]
    prev_scr[...] = x[tm - SUBLANES:]

    r = jax.nn.sigmoid(_wdot(xc, wrg_ref[...]) + brg_ref[...])
    ig = jax.nn.sigmoid(_wdot(xc, wig_ref[...]) + big_ref[...])
    log_a = -LRU_C * r * sp_ref[...]
    a = jnp.exp(log_a)
    b = jnp.sqrt(-_expm1(2.0 * log_a)) * (ig * xc)
    row = lax.broadcasted_iota(jnp.int32, (tm, D_RNN), 0)
    if start > 0:
        live = (t * tm + row) >= start
        a = jnp.where(live, a, 1.0)
        b = jnp.where(live, b, 0.0)
    d = 1
    while d < tm:
        b = a * jnp.where(row >= d, pltpu.roll(b, d, 0), 0.0) + b
        a = a * jnp.where(row >= d, pltpu.roll(a, d, 0), 1.0)
        d *= 2
    h = a * h_scr[0:1] + b
    h_scr[...] = jnp.broadcast_to(h[tm - 1:tm], h_scr.shape)
    rnn_ref[0] = (h * _gelu_tanh(yg_ref[0])).astype(rnn_ref.dtype)

    t_end = (end - 1) // tm
    el = end - t_end * tm

    @pl.when(t == t_end)
    def _():
        tailx_ref[0] = ext[el:el + SUBLANES]
        tailh_ref[0] = h[el - SUBLANES:el]


def _rglru(xr, yg, cw, cb, wrg, brg, wig, big, sp, buf0, h0, start, end):
    b, l, _ = xr.shape
    tm = _row_tile(l, 512)
    seq = pl.BlockSpec((1, tm, D_RNN), lambda i, t: (i, t, 0))
    per_b = lambda r: pl.BlockSpec((1, r, D_RNN), lambda i, t: (i, 0, 0))
    return pl.pallas_call(
        functools.partial(_rglru_kernel, tm=tm, start=start, end=end),
        grid=(b, l // tm),
        in_specs=[seq, seq, _full(cw.shape), _full(cb.shape), _full(wrg.shape), _full(brg.shape),
                  _full(wig.shape), _full(big.shape), _full(sp.shape), per_b(SUBLANES), per_b(1)],
        out_specs=[seq, per_b(SUBLANES), per_b(SUBLANES)],
        out_shape=[jax.ShapeDtypeStruct((b, l, D_RNN), _act_dtype(wrg)),
                   jax.ShapeDtypeStruct((b, SUBLANES, D_RNN), F32),
                   jax.ShapeDtypeStruct((b, SUBLANES, D_RNN), F32)],
        scratch_shapes=[pltpu.VMEM((SUBLANES, D_RNN), F32), pltpu.VMEM((SUBLANES, D_RNN), F32)],
        compiler_params=_cparams(("arbitrary", "arbitrary")),
        name="rglru",
    )(xr, yg, cw, cb, wrg, brg, wig, big, sp, buf0, h0)


def _mix_out_kernel(attn_ref, rnn_ref, x_ref, wa_ref, wr_ref, g_ref, b_ref, o_ref):
    mix = _wdot(attn_ref[...], wa_ref[...]) + _wdot(rnn_ref[...], wr_ref[...])
    o_ref[...] = _ln(ALPHA * x_ref[...] + mix, g_ref[...], b_ref[...])


def _mix_out(attn, rnn, x, wa, wr, g, b):
    n = x.shape[0]
    tm = _row_tile(n, 512)
    row = lambda w: pl.BlockSpec((tm, w), lambda i: (i, 0))
    return pl.pallas_call(
        _mix_out_kernel,
        grid=(n // tm,),
        in_specs=[row(D_ATT), row(D_RNN), row(D_MODEL), _full(wa.shape), _full(wr.shape),
                  _full(g.shape), _full(b.shape)],
        out_specs=row(D_MODEL),
        out_shape=jax.ShapeDtypeStruct((n, D_MODEL), F32),
        compiler_params=_cparams(("arbitrary",)),
        name="mix_out",
    )(attn, rnn, x, wa, wr, g, b)


def _first_argmax(vals, lane):
    m = jnp.max(vals, axis=-1, keepdims=True)
    idx = jnp.min(jnp.where(vals == m, lane, N_EXPERTS), axis=-1, keepdims=True)
    return m, idx


def _router_top2(x, rw, rb):
    logits = _mm(x, rw, _NN, True)
    s = jax.nn.sigmoid(logits)
    sel = s + rb
    lane = lax.broadcasted_iota(jnp.int32, sel.shape, 1)
    grp = lane >> 2
    best = None
    g_best = None
    for g in range(N_GROUPS):
        vals = jnp.where(grp == g, sel, NEG)
        m1, i1 = _first_argmax(vals, lane)
        m2, _ = _first_argmax(jnp.where(lane == i1, NEG, vals), lane)
        score = m1 + m2
        if g == 0:
            best, g_best = score, jnp.zeros_like(i1)
        else:
            upd = score > best
            g_best = jnp.where(upd, g, g_best)
            best = jnp.where(upd, score, best)
    vals = jnp.where(grp == g_best, sel, NEG)
    _, i1 = _first_argmax(vals, lane)
    _, i2 = _first_argmax(jnp.where(lane == i1, NEG, vals), lane)
    w1 = jnp.sum(jnp.where(lane == i1, s, 0.0), axis=-1, keepdims=True)
    w2 = jnp.sum(jnp.where(lane == i2, s, 0.0), axis=-1, keepdims=True)
    den = w1 + w2
    return lane, i1, i2, w1 / den, w2 / den


def _router_gate(x, rw, rb):
    lane, i1, i2, g1, g2 = _router_top2(x, rw, rb)
    return jnp.where(lane == i1, g1, 0.0) + jnp.where(lane == i2, g2, 0.0)


def _moe_kernel(x_ref, rw_ref, rb_ref, wg_ref, wu_ref, wd_ref, g_ref, b_ref, o_ref,
                gate_scr, xb_scr, acc_scr):
    e = pl.program_id(1)

    @pl.when(e == 0)
    def _():
        x = x_ref[...]
        gate = _router_gate(x, rw_ref[...], rb_ref[...])
        for k in range(N_EXPERTS):
            gate_scr[k] = jnp.broadcast_to(gate[:, k:k + 1], gate_scr.shape[1:])
        xb_scr[...] = x.astype(BF16)
        acc_scr[...] = jnp.zeros(acc_scr.shape, F32)

    xb = xb_scr[...]
    hg = jnp.dot(xb, wg_ref[0].astype(BF16), preferred_element_type=F32)
    hu = jnp.dot(xb, wu_ref[0].astype(BF16), preferred_element_type=F32)
    gate_e = gate_scr[e]
    h = jax.nn.silu(hg) * hu * jnp.concatenate([gate_e] * (D_EXPERT // LANES), axis=1)
    acc_scr[...] += jnp.dot(h.astype(BF16), wd_ref[0].astype(BF16), preferred_element_type=F32)

    @pl.when(e == N_EXPERTS - 1)
    def _():
        o_ref[...] = _ln(ALPHA * x_ref[...] + acc_scr[...], g_ref[...], b_ref[...])


def _moe(x, rw, rb, wg, wu, wd, e0, g, b):
    n = x.shape[0]
    tm = _row_tile(n, 512)
    row = pl.BlockSpec((tm, D_MODEL), lambda i, e: (i, 0))
    return pl.pallas_call(
        _moe_kernel,
        grid=(n // tm, N_EXPERTS),
        in_specs=[row, _full(rw.shape), _full(rb.shape),
                  pl.BlockSpec((1, D_MODEL, D_EXPERT), lambda i, e: (e0 + e, 0, 0)),
                  pl.BlockSpec((1, D_MODEL, D_EXPERT), lambda i, e: (e0 + e, 0, 0)),
                  pl.BlockSpec((1, D_EXPERT, D_MODEL), lambda i, e: (e0 + e, 0, 0)),
                  _full(g.shape), _full(b.shape)],
        out_specs=row,
        out_shape=jax.ShapeDtypeStruct((n, D_MODEL), F32),
        scratch_shapes=[pltpu.VMEM((N_EXPERTS, tm, LANES), F32), pltpu.VMEM((tm, D_MODEL), BF16),
                        pltpu.VMEM((tm, D_MODEL), F32)],
        compiler_params=_cparams(("arbitrary", "arbitrary")),
        name="moe",
    )(x, rw, rb, wg, wu, wd, g, b)


M_I1, M_I2, M_R1, M_R2, M_G1, M_G2, M_COLS = 0, 1, 2, 3, 4, 5, 8


def _first_argmax_rows(vals, row):
    m = jnp.max(vals, axis=0, keepdims=True)
    idx = jnp.min(jnp.where(vals == m, row, N_EXPERTS), axis=0, keepdims=True)
    return m, idx


HI16 = -65536


def _pack_bf16_pairs(x):
    w = x.shape[1] // 2
    hi = lax.bitcast_convert_type(x[:, :w].astype(BF16).astype(F32), jnp.int32)
    lo = lax.bitcast_convert_type(x[:, w:].astype(BF16).astype(F32), jnp.int32)
    return (hi & HI16) | lax.shift_right_logical(lo, 16)


def _unpack_bf16_pairs(p):
    hi = lax.bitcast_convert_type(p & HI16, F32)
    lo = lax.bitcast_convert_type(lax.shift_left(p, 16), F32)
    return jnp.concatenate([hi, lo], axis=1)


def _route_kernel(x_ref, rwt_ref, rbc_ref, meta_ref, cnt_ref, xpk_ref, carry_scr, *, tm):
    @pl.when(pl.program_id(0) == 0)
    def _():
        carry_scr[...] = jnp.zeros(carry_scr.shape, F32)

    logits = _mm(rwt_ref[...], x_ref[...], _NT, True)
    s = jax.nn.sigmoid(logits)
    sel = s + rbc_ref[...]
    row = lax.broadcasted_iota(jnp.int32, sel.shape, 0)
    grp = row >> 2
    best = None
    g_best = None
    for g in range(N_GROUPS):
        vals = jnp.where(grp == g, sel, NEG)
        m1, i1 = _first_argmax_rows(vals, row)
        m2, _ = _first_argmax_rows(jnp.where(row == i1, NEG, vals), row)
        score = m1 + m2
        if g == 0:
            best, g_best = score, jnp.zeros_like(i1)
        else:
            upd = score > best
            g_best = jnp.where(upd, g, g_best)
            best = jnp.where(upd, score, best)
    vals = jnp.where(grp == g_best, sel, NEG)
    _, i1 = _first_argmax_rows(vals, row)
    _, i2 = _first_argmax_rows(jnp.where(row == i1, NEG, vals), row)
    w1 = jnp.sum(jnp.where(row == i1, s, 0.0), axis=0, keepdims=True)
    w2 = jnp.sum(jnp.where(row == i2, s, 0.0), axis=0, keepdims=True)
    den = w1 + w2

    chosen = jnp.where((row == i1) | (row == i2), 1.0, 0.0)
    earlier = (lax.broadcasted_iota(jnp.int32, (tm, tm), 0)
               < lax.broadcasted_iota(jnp.int32, (tm, tm), 1)).astype(BF16)
    seen = jnp.dot(chosen.astype(BF16), earlier, preferred_element_type=F32) + carry_scr[:, 0:1]
    r1 = jnp.sum(jnp.where(row == i1, seen, 0.0), axis=0, keepdims=True)
    r2 = jnp.sum(jnp.where(row == i2, seen, 0.0), axis=0, keepdims=True)
    carry_scr[...] = carry_scr[...] + jnp.sum(chosen, axis=1, keepdims=True)
    mrow = lax.broadcasted_iota(jnp.int32, (M_COLS, tm), 0)
    meta = jnp.zeros((M_COLS, tm), F32)
    for c, val in ((M_I1, i1.astype(F32)), (M_I2, i2.astype(F32)), (M_R1, r1), (M_R2, r2),
                   (M_G1, w1 / den), (M_G2, w2 / den)):
        meta = jnp.where(mrow == c, val, meta)
    meta_ref[...] = meta
    cnt_ref[...] = carry_scr[...]
    xpk_ref[...] = _pack_bf16_pairs(x_ref[...])


def _route(x, rwt, rbc):
    n = x.shape[0]
    tm = _row_tile(n, 512)
    return pl.pallas_call(
        functools.partial(_route_kernel, tm=tm),
        grid=(n // tm,),
        in_specs=[pl.BlockSpec((tm, D_MODEL), lambda i: (i, 0)), _full(rwt.shape), _full(rbc.shape)],
        out_specs=[pl.BlockSpec((M_COLS, tm), lambda i: (0, i)), _full((N_EXPERTS, LANES)),
                   pl.BlockSpec((tm, D_MODEL // 2), lambda i: (i, 0))],
        out_shape=[jax.ShapeDtypeStruct((M_COLS, n), F32), jax.ShapeDtypeStruct((N_EXPERTS, LANES), F32),
                   jax.ShapeDtypeStruct((n, D_MODEL // 2), jnp.int32)],
        scratch_shapes=[pltpu.VMEM((N_EXPERTS, LANES), F32)],
        compiler_params=_cparams(("arbitrary",)),
        name="moe_route",
    )(x, rwt, rbc)


def _sc_chunk(per_worker):
    for c in (64, 48, 32, 16, 8):
        if per_worker % c == 0:
            return c
    raise ValueError(per_worker)


def _sc_mesh():
    return plsc.VectorSubcoreMesh(core_axis_name="c", subcore_axis_name="s")


def _sc_scatter2(x, idx1, idx2, n_out):
    n, d = x.shape
    per_w = n // SC_WORKERS
    assert per_w * SC_WORKERS == n
    chunk = _sc_chunk(per_w)

    @functools.partial(
        pl.kernel, mesh=_sc_mesh(), out_type=jax.ShapeDtypeStruct((n_out, d), x.dtype),
        scratch_types=[pltpu.VMEM((chunk,), jnp.int32), pltpu.VMEM((chunk,), jnp.int32),
                       pltpu.VMEM((chunk, d), x.dtype), pltpu.SemaphoreType.DMA])
    def scatter(x_hbm, i1_hbm, i2_hbm, out_hbm, i1_v, i2_v, rows_v, sem):
        base = (lax.axis_index("s") * SC_CORES + lax.axis_index("c")) * per_w

        @pl.loop(0, per_w // chunk)
        def _(c):
            off = pl.multiple_of(base + c * chunk, SUBLANES)
            pltpu.sync_copy(i1_hbm.at[pl.ds(off, chunk)], i1_v)
            pltpu.sync_copy(i2_hbm.at[pl.ds(off, chunk)], i2_v)
            pltpu.sync_copy(x_hbm.at[pl.ds(off, chunk)], rows_v)
            pltpu.async_copy(rows_v, out_hbm.at[i1_v], sem).wait()
            pltpu.async_copy(rows_v, out_hbm.at[i2_v], sem).wait()

    return scatter(x, idx1, idx2)


def _sc_gather(y, idx):
    n = idx.shape[0]
    d = y.shape[1]
    per_w = n // SC_WORKERS
    assert per_w * SC_WORKERS == n
    chunk = _sc_chunk(per_w)

    @functools.partial(
        pl.kernel, mesh=_sc_mesh(), out_type=jax.ShapeDtypeStruct((n, d), y.dtype),
        scratch_types=[pltpu.VMEM((chunk,), jnp.int32), pltpu.VMEM((chunk, d), y.dtype),
                       pltpu.SemaphoreType.DMA])
    def gather(y_hbm, idx_hbm, out_hbm, idx_v, rows_v, sem):
        base = (lax.axis_index("s") * SC_CORES + lax.axis_index("c")) * per_w

        @pl.loop(0, per_w // chunk)
        def _(c):
            off = pl.multiple_of(base + c * chunk, SUBLANES)
            pltpu.sync_copy(idx_hbm.at[pl.ds(off, chunk)], idx_v)
            pltpu.async_copy(y_hbm.at[idx_v], rows_v, sem).wait()
            pltpu.sync_copy(rows_v, out_hbm.at[pl.ds(off, chunk)])

    return gather(y, idx)


def _experts_kernel(te_ref, used_ref, x_ref, wg_ref, wu_ref, wd_ref, o_ref):
    @pl.when(pl.program_id(0) < used_ref[0])
    def _():
        xb = _unpack_bf16_pairs(x_ref[...]).astype(BF16)
        hg = jnp.dot(xb, wg_ref[0].astype(BF16), preferred_element_type=F32)
        hu = jnp.dot(xb, wu_ref[0].astype(BF16), preferred_element_type=F32)
        h = jax.nn.silu(hg) * hu
        y = jnp.dot(h.astype(BF16), wd_ref[0].astype(BF16), preferred_element_type=F32)
        o_ref[...] = _pack_bf16_pairs(y)


def _experts(xg, tile_expert, n_used, wg, wu, wd):
    n_tiles = xg.shape[0] // MOE_TILE
    row = pl.BlockSpec((MOE_TILE, D_MODEL // 2), lambda i, te, used: (i, 0))
    grid_spec = pltpu.PrefetchScalarGridSpec(
        num_scalar_prefetch=2,
        grid=(n_tiles,),
        in_specs=[row,
                  pl.BlockSpec((1, D_MODEL, D_EXPERT), lambda i, te, used: (te[i], 0, 0)),
                  pl.BlockSpec((1, D_MODEL, D_EXPERT), lambda i, te, used: (te[i], 0, 0)),
                  pl.BlockSpec((1, D_EXPERT, D_MODEL), lambda i, te, used: (te[i], 0, 0))],
        out_specs=row,
    )
    return pl.pallas_call(
        _experts_kernel,
        grid_spec=grid_spec,
        out_shape=jax.ShapeDtypeStruct(xg.shape, jnp.int32),
        compiler_params=_cparams(("arbitrary",)),
        name="moe_experts",
    )(tile_expert, n_used, xg, wg, wu, wd)


def _combine_kernel(x_ref, y1_ref, y2_ref, meta_ref, g_ref, b_ref, o_ref):
    meta = meta_ref[...]
    moe = (meta[:, M_G1:M_G1 + 1] * _unpack_bf16_pairs(y1_ref[...])
           + meta[:, M_G2:M_G2 + 1] * _unpack_bf16_pairs(y2_ref[...]))
    o_ref[...] = _ln(ALPHA * x_ref[...] + moe, g_ref[...], b_ref[...])


def _combine(x, y1, y2, meta, g, b):
    n = x.shape[0]
    tm = _row_tile(n, 512)
    row = pl.BlockSpec((tm, D_MODEL), lambda i: (i, 0))
    half = pl.BlockSpec((tm, D_MODEL // 2), lambda i: (i, 0))
    return pl.pallas_call(
        _combine_kernel,
        grid=(n // tm,),
        in_specs=[row, half, half, pl.BlockSpec((tm, M_COLS), lambda i: (i, 0)), _full(g.shape), _full(b.shape)],
        out_specs=row,
        out_shape=jax.ShapeDtypeStruct((n, D_MODEL), F32),
        compiler_params=_cparams(("arbitrary",)),
        name="moe_combine",
    )(x, y1, y2, meta, g, b)


def _moe_sparse(x, rw, rb, wg, wu, wd, e0, g, b):
    n = x.shape[0]
    meta_t, counts, x_packed = _route(x, rw.T, rb.reshape(N_EXPERTS, 1))
    cnt = counts[:, 0].astype(jnp.int32)
    padded = (cnt + MOE_TILE - 1) // MOE_TILE * MOE_TILE
    seg_end = jnp.cumsum(padded)
    seg_start = seg_end - padded
    experts = jnp.arange(N_EXPERTS, dtype=jnp.int32)[:, None]
    start_of = lambda e: jnp.sum(jnp.where(experts == e[None, :], seg_start[:, None], 0), axis=0)
    e1, e2 = meta_t[M_I1].astype(jnp.int32), meta_t[M_I2].astype(jnp.int32)
    pos1 = start_of(e1) + meta_t[M_R1].astype(jnp.int32)
    pos2 = start_of(e2) + meta_t[M_R2].astype(jnp.int32)
    meta = meta_t.T
    n_tiles = -(-2 * n // MOE_TILE) + N_EXPERTS
    tile_start = jnp.arange(n_tiles, dtype=jnp.int32) * MOE_TILE
    tile_expert = e0 + jnp.minimum(jnp.sum(tile_start[:, None] >= seg_end[None, :], axis=1),
                                   N_EXPERTS - 1).astype(jnp.int32)
    n_used = (seg_end[-1:] // MOE_TILE).astype(jnp.int32)
    xg = _sc_scatter2(x_packed, pos1, pos2, n_tiles * MOE_TILE)
    yg = _experts(xg, tile_expert, n_used, wg, wu, wd)
    return _combine(x, _sc_gather(yg, pos1), _sc_gather(yg, pos2), meta, g, b)


def _head_sum(z, ones):
    hi, lo = _split2(z)
    parts = []
    for g in range(D_MODEL // LANES):
        sl = slice(g * LANES, (g + 1) * LANES)
        parts.append(jnp.dot(hi[:, sl], ones, preferred_element_type=F32)
                     + jnp.dot(lo[:, sl], ones, preferred_element_type=F32))
    return jnp.concatenate(parts, axis=1)


def _rwkv_proj_kernel(x_ref, sh0_ref, mu_ref, wr_ref, wk_ref, wv_ref, w0_ref, w1_ref, w2_ref,
                      a0_ref, a1_ref, a2_ref, g1_ref, g2_ref, kkw_ref, kaw_ref, ones_ref,
                      r_ref, lw_ref, k_ref, v_ref, kk_ref, a_ref, g_ref, prev_scr,
                      *, tm, start, end):
    t = pl.program_id(1)

    @pl.when(t == 0)
    def _():
        prev_scr[...] = jnp.zeros(prev_scr.shape, F32)

    x = x_ref[0]
    ext = jnp.concatenate([prev_scr[...], x], axis=0)
    x_prev = pltpu.roll(ext, 1, 0)[SUBLANES:]
    grow = t * tm + lax.broadcasted_iota(jnp.int32, (tm, D_MODEL), 0)
    x_prev = jnp.where(grow == start, sh0_ref[0], x_prev)
    prev_scr[...] = x[tm - SUBLANES:]
    xx = x_prev - x
    mu = mu_ref[...]
    xr, xw, xk, xv, xa, xg = (x + xx * mu[n:n + 1] for n in range(6))
    r = _bdot(xr, wr_ref[...])
    k = _bdot(xk, wk_ref[...])
    v = _bdot(xv, wv_ref[...])
    log_w = -DECAY_SCALE * jax.nn.sigmoid(w0_ref[...] + _bdot(jnp.tanh(_bdot(xw, w1_ref[...])), w2_ref[...]))
    a = jax.nn.sigmoid(a0_ref[...] + _bdot(_bdot(xa, a1_ref[...]), a2_ref[...]))
    g = _bdot(jax.nn.sigmoid(_bdot(xg, g1_ref[...])), g2_ref[...])
    kk = k * kkw_ref[...]
    kk = kk * lax.rsqrt(jnp.maximum(_head_sum(kk * kk, ones_ref[...]), 1e-24))
    k = k * (1.0 + (a - 1.0) * kaw_ref[...])
    r_ref[0] = r.astype(r_ref.dtype)
    lw_ref[0] = log_w
    k_ref[0] = k.astype(k_ref.dtype)
    v_ref[0] = v.astype(v_ref.dtype)
    kk_ref[0] = kk.astype(kk_ref.dtype)
    a_ref[0] = a.astype(a_ref.dtype)
    g_ref[0] = g.astype(g_ref.dtype)

    @pl.when((t * tm < start) | ((t + 1) * tm > end))
    def _():
        live = (grow >= start) & (grow < end)
        lw_ref[0] = jnp.where(live, log_w, 0.0)
        k_ref[0] = jnp.where(live, k, 0.0).astype(k_ref.dtype)
        kk_ref[0] = jnp.where(live, kk, 0.0).astype(kk_ref.dtype)


def _rwkv_proj(x, sh0, od, start, end, act_dtype):
    b, l, _ = x.shape
    tm = _row_tile(l, 256)
    dtypes = [act_dtype, F32] + [act_dtype] * 5
    seq = pl.BlockSpec((1, tm, D_MODEL), lambda i, t: (i, t, 0))
    ws = [od[n] for n in ("mu", "w_r", "w_k", "w_v", "w0", "w1", "w2", "a0", "a1", "a2", "g1", "g2",
                          "k_k", "k_a", "ones")]
    return pl.pallas_call(
        functools.partial(_rwkv_proj_kernel, tm=tm, start=start, end=end),
        grid=(b, l // tm),
        in_specs=[seq, pl.BlockSpec((1, 1, D_MODEL), lambda i, t: (i, 0, 0))] + [_full(w.shape) for w in ws],
        out_specs=[seq] * 7,
        out_shape=[jax.ShapeDtypeStruct((b, l, D_MODEL), dt) for dt in dtypes],
        scratch_shapes=[pltpu.VMEM((SUBLANES, D_MODEL), F32)],
        compiler_params=_cparams(("arbitrary", "arbitrary")),
        name="rwkv_proj",
    )(x, sh0, *ws)


def _wkv_kernel(r_ref, lw_ref, k_ref, v_ref, kk_ref, a_ref, s0_ref, o_ref, sout_ref, s_scr,
                *, c, exact):
    t = pl.program_id(1)

    @pl.when(t == 0)
    def _():
        s_scr[...] = s0_ref[0]

    head0 = lax.broadcasted_iota(jnp.int32, (c, LANES), 1) < RWKV_HEAD
    c2 = 2 * c
    row = lax.broadcasted_iota(jnp.int32, (c2, c2), 0)
    col = lax.broadcasted_iota(jnp.int32, (c2, c2), 1)
    row_hi = jnp.where(row >= c, c, 0)
    col_hi = jnp.where(col >= c, c, 0)
    same = row_hi == col_hi
    rr = row - row_hi
    cc = col - col_hi
    strict = same & (rr > cc)
    incl = same & (rr >= cc)
    eye = (row == col).astype(F32)
    crow = lax.broadcasted_iota(jnp.int32, (c, LANES), 0)

    def cumsum_rows(x):
        d = 1
        while d < c:
            x = x + jnp.where(crow >= d, pltpu.roll(x, d, 0), 0.0)
            d *= 2
        return x

    def stack(x):
        return jnp.concatenate([jnp.where(head0, x, 0.0), jnp.where(head0, 0.0, x)], axis=0)

    mm = functools.partial(_mm, exact=exact)
    pairs = range(RWKV_HEADS // 2)
    for sub in range(r_ref.shape[1] // c):
        rows = slice(sub * c, (sub + 1) * c)
        load = lambda ref: [ref[0, rows, p * LANES:(p + 1) * LANES].astype(F32) for p in pairs]
        r, lw, k, v, kk, a = (load(ref) for ref in (r_ref, lw_ref, k_ref, v_ref, kk_ref, a_ref))
        lc = [cumsum_rows(x) for x in lw]
        lc_end = [x[c - 1:c] for x in lc]
        b = [x * y for x, y in zip(kk, a)]
        lhs = [jnp.concatenate([stack(-kk[p] * jnp.exp(lc[p] - lw[p])), stack(r[p] * jnp.exp(lc[p]))], axis=0)
               for p in pairs]
        g_inv = [jnp.exp(-x) for x in lc]
        rhs = [jnp.concatenate([stack(b[p] * g_inv[p]), stack(k[p] * g_inv[p])], axis=0) for p in pairs]
        pm = [mm(x, y, _NT) for x, y in zip(lhs, rhs)]
        l_ab = [jnp.where(strict, x[:c2, :c2], 0.0) for x in pm]
        l_ak = [jnp.where(strict, x[:c2, c2:], 0.0) for x in pm]
        m_rb = [jnp.where(incl, x[c2:, :c2], 0.0) for x in pm]
        m_rk = [jnp.where(incl, x[c2:, c2:], 0.0) for x in pm]
        vs = [stack(x) for x in v]
        lakv = [mm(x, y, _NN) for x, y in zip(l_ak, vs)]
        mrkv = [mm(x, y, _NN) for x, y in zip(m_rk, vs)]
        tinv = [eye + x for x in l_ab]
        lp = l_ab
        n = 2
        while n < c:
            lp = [mm(x, x, _NN) for x in lp]
            tinv = [x + mm(x, y, _NN) for x, y in zip(tinv, lp)]
            n *= 2
        s = [s_scr[p] for p in pairs]
        xs = [mm(x, y, _NT) for x, y in zip(lhs, s)]
        u = [mm(tinv[p], xs[p][:c2] + lakv[p], _NN) for p in pairs]
        for p in pairs:
            os_ = xs[p][c2:] + mm(m_rb[p], u[p], _NN) + mrkv[p]
            o_ref[0, rows, p * LANES:(p + 1) * LANES] = os_[:c] + os_[c:]
        for p in pairs:
            g_rem = jnp.exp(lc_end[p] - lc[p])
            uv = jnp.concatenate([u[p], vs[p]], axis=0)
            bk = jnp.concatenate([stack(b[p] * g_rem), stack(k[p] * g_rem)], axis=0)
            s_scr[p] = s[p] * jnp.exp(lc_end[p]) + mm(uv, bk, _TN)

    @pl.when(t == pl.num_programs(1) - 1)
    def _():
        sout_ref[0] = s_scr[...]


def _wkv(r, lw, k, v, kk, a, s0, c, exact):
    b, l, _ = r.shape
    rows = 2 * c if l % (2 * c) == 0 else c
    seq = pl.BlockSpec((1, rows, D_MODEL), lambda i, t: (i, t, 0))
    st = pl.BlockSpec((1, RWKV_HEADS // 2, LANES, LANES), lambda i, t: (i, 0, 0, 0))
    return pl.pallas_call(
        functools.partial(_wkv_kernel, c=c, exact=exact),
        grid=(b, l // rows),
        in_specs=[seq] * 6 + [st],
        out_specs=[seq, st],
        out_shape=[jax.ShapeDtypeStruct((b, l, D_MODEL), F32),
                   jax.ShapeDtypeStruct((b, RWKV_HEADS // 2, LANES, LANES), F32)],
        scratch_shapes=[pltpu.VMEM((RWKV_HEADS // 2, LANES, LANES), F32)],
        compiler_params=_cparams(("arbitrary", "arbitrary")),
        name="wkv",
    )(r, lw, k, v, kk, a, s0)


def _rwkv_out_kernel(o_ref, r_ref, k_ref, v_ref, g_ref, x_ref, rk_ref, gng_ref, gnb_ref, wo_ref,
                     ones_ref, lg_ref, lb_ref, y_ref):
    ones = ones_ref[...]
    o = o_ref[...]
    inv = 1.0 / RWKV_HEAD
    mu = _head_sum(o, ones) * inv
    oc = o - mu
    var = _head_sum(oc * oc, ones) * inv
    on = oc * lax.rsqrt(var + GN_EPS) * gng_ref[...] + gnb_ref[...]
    r, k, v, g = (ref[...].astype(F32) for ref in (r_ref, k_ref, v_ref, g_ref))
    on = on + _head_sum(r * k * rk_ref[...], ones) * v
    out = _bdot(on * g, wo_ref[...])
    y_ref[...] = _ln(ALPHA * x_ref[...] + out, lg_ref[...], lb_ref[...])


def _rwkv_out(o, r, k, v, g, x, od, lg, lb):
    n = x.shape[0]
    tm = _row_tile(n, 512)
    row = pl.BlockSpec((tm, D_MODEL), lambda i: (i, 0))
    ws = [od["r_k"], od["ln_g"], od["ln_b"], od["w_o"], od["ones"], lg, lb]
    return pl.pallas_call(
        _rwkv_out_kernel,
        grid=(n // tm,),
        in_specs=[row] * 6 + [_full(w.shape) for w in ws],
        out_specs=row,
        out_shape=jax.ShapeDtypeStruct((n, D_MODEL), F32),
        compiler_params=_cparams(("arbitrary",)),
        name="rwkv_out",
    )(o, r, k, v, g, x, *ws)


def _rope_tables(pos):
    half = D_ROPE // 2
    freq = ROPE_BASE ** (-jnp.arange(half, dtype=F32) / half)
    ang_t = freq[:, None] * pos.astype(F32)[None, :]
    cos_t, sin_t = lax.optimization_barrier((jnp.cos(ang_t), jnp.sin(ang_t)))
    cos, sin = cos_t.T, sin_t.T
    n = pos.shape[0]
    ones = jnp.ones((n, D_NOPE), F32)
    zeros = jnp.zeros((n, D_NOPE), F32)
    z16 = jnp.zeros((n, half), F32)
    tail1 = jnp.ones((n, LANES - D_NOPE - D_ROPE), F32)
    tail0 = jnp.zeros((n, LANES - D_NOPE - D_ROPE), F32)
    c = jnp.concatenate([ones, cos, cos, tail1], axis=1)
    sa = jnp.concatenate([zeros, -sin, z16, tail0], axis=1)
    sb = jnp.concatenate([zeros, z16, sin, tail0], axis=1)
    return c, sa, sb


def _slot_cols(w, width):
    k, h, _ = w.shape
    return jnp.pad(w, ((0, 0), (0, 0), (0, HEAD_SLOT - width))).reshape(k, h * HEAD_SLOT)


def _block_diag(w):
    n, c, d = w.shape
    eye = jnp.eye(n, dtype=w.dtype)
    return (eye[:, None, :, None] * w[:, :, None, :]).reshape(n * c, n * d)


def _row2(v):
    return v.reshape(1, -1).astype(F32)


def _prep_even(w_in, g_q, w_uq, g_kv, w_uk, w_uv, conv_w, conv_b, w_rg, b_rg, w_ig, b_ig, lam, w_out):
    off_ckv, off_kr = D_CQ, D_CQ + D_C
    off_xr = off_kr + D_ROPE
    off_y = off_xr + D_RNN
    kr_cols = jnp.pad(w_in[:, off_kr:off_xr], ((0, 0), (D_NOPE, LANES - D_NOPE - D_ROPE)))
    w_in, w_uq, w_uk, w_uv, w_rg, w_ig, w_out = (
        w.astype(F32) for w in (w_in, w_uq, w_uk, w_uv, w_rg, w_ig, w_out))
    w1 = jnp.concatenate([w_in[:, :off_ckv], w_in[:, off_ckv:off_kr], w_in[:, off_xr:off_y],
                          w_in[:, off_y:], kr_cols], axis=1)
    wuq = _slot_cols(w_uq, D_NOPE + D_ROPE)
    wukv = jnp.concatenate([_slot_cols(w_uk, D_NOPE), _slot_cols(w_uv, D_V)], axis=1)
    place = np.zeros((LANES, D_ATT), np.float32)
    for h in range(MLA_HEADS):
        for cidx in range(D_ROPE):
            place[D_NOPE + cidx, h * HEAD_SLOT + D_NOPE + cidx] = 1.0
    wa = jnp.pad(w_out[:MLA_HEADS * D_V].reshape(MLA_HEADS, D_V, D_MODEL),
                 ((0, 0), (0, HEAD_SLOT - D_V), (0, 0))).reshape(D_ATT, D_MODEL)
    wr = w_out[MLA_HEADS * D_V:]
    keep_rope = np.zeros((HEAD_SLOT, LANES), np.float32)
    keep_rope[D_NOPE:D_NOPE + D_ROPE, D_NOPE:D_NOPE + D_ROPE] = np.eye(D_ROPE, dtype=np.float32)
    uk_t = jnp.pad(jnp.transpose(w_uk, (1, 2, 0)), ((0, 0), (0, HEAD_SLOT - D_NOPE), (0, 0)))
    wabs = jnp.concatenate([uk_t, jnp.broadcast_to(keep_rope, (MLA_HEADS, HEAD_SLOT, LANES))], axis=2)
    wuv = jnp.pad(jnp.transpose(w_uv, (1, 0, 2)), ((0, 0), (0, 0), (0, HEAD_SLOT - D_V)))
    return dict(
        wabs=wabs, wuv=wuv,
        w1=w1, gq=_row2(g_q), wuq=wuq, gkv=_row2(g_kv), wukv=wukv, place=jnp.asarray(place, F32),
        cw=conv_w.astype(F32), cb=_row2(conv_b), wrg=_block_diag(w_rg), brg=_row2(b_rg),
        wig=_block_diag(w_ig), big=_row2(b_ig), sp=_row2(jax.nn.softplus(-lam.astype(F32))),
        wa=wa, wr=wr)


_EVEN_MATMUL_WEIGHTS = ("w1", "wuq", "wukv", "place", "wrg", "wig", "wa", "wr")


def _single_pass(ev):
    return {n: (w.astype(BF16) if n in _EVEN_MATMUL_WEIGHTS else w) for n, w in ev.items()}


def _prep_odd(mu, w_r, w_k, w_v, w0, w1, w2, a0, a1, a2, g1, g2, k_k, k_a, r_k, ln_g, ln_b, w_o):
    ones = np.zeros((LANES, LANES), np.float32)
    ones[:RWKV_HEAD, :RWKV_HEAD] = 1.0
    ones[RWKV_HEAD:, RWKV_HEAD:] = 1.0
    return dict(
        mu=jnp.pad(mu.astype(F32), ((0, SUBLANES - mu.shape[0]), (0, 0))),
        w_r=w_r.astype(BF16), w_k=w_k.astype(BF16), w_v=w_v.astype(BF16), w0=_row2(w0),
        w1=w1.astype(BF16), w2=w2.astype(BF16), a0=_row2(a0), a1=a1.astype(BF16), a2=a2.astype(BF16),
        g1=g1.astype(BF16), g2=g2.astype(BF16), k_k=_row2(k_k), k_a=_row2(k_a), r_k=_row2(r_k),
        ln_g=_row2(ln_g), ln_b=_row2(ln_b), w_o=w_o.astype(BF16), ones=jnp.asarray(ones, BF16))


def _pair_states(s):
    b = s.shape[0]
    s = s.reshape(b, RWKV_HEADS // 2, 2, RWKV_HEAD, RWKV_HEAD).astype(F32)
    eye = jnp.eye(2, dtype=F32)
    out = s[:, :, :, :, None, :] * eye[None, None, :, None, :, None]
    return out.reshape(b, RWKV_HEADS // 2, LANES, LANES)


def _unpair_states(s):
    b = s.shape[0]
    s = s.reshape(b, RWKV_HEADS // 2, 2, RWKV_HEAD, 2, RWKV_HEAD)
    return jnp.stack([s[:, :, 0, :, 0, :], s[:, :, 1, :, 1, :]], axis=2).reshape(
        b, RWKV_HEADS, RWKV_HEAD, RWKV_HEAD)


def _round_up(n, m):
    return -(-n // m) * m


def kernel(x_prompt, x_sample, cache_ckv, cache_krope, state_conv, state_lru, state_shift, state_wkv,
           meta_tokens, ev_w_in, ev_g_q, ev_w_uq, ev_g_kv, ev_w_uk, ev_w_uv, ev_conv_w, ev_conv_b,
           ev_w_rg, ev_b_rg, ev_w_ig, ev_b_ig, ev_lru_lambda, ev_w_out, od_mu, od_w_r, od_w_k, od_w_v,
           od_w0, od_w1, od_w2, od_a0, od_a1, od_a2, od_g1, od_g2, od_k_k, od_k_a, od_r_k, od_ln_g,
           od_ln_b, od_w_o, ln_g, ln_b, router_w, router_b, exp_w_gate, exp_w_up, exp_w_down):
    assert x_prompt.shape[0] == 1 and x_prompt.shape[2] == D_MODEL
    seq = x_prompt.shape[1]
    assert seq % CHUNK == 0
    bs, ls, _ = x_sample.shape
    past = cache_ckv.shape[2]
    ns = bs * ls
    end = ROW0 + seq
    tp = _round_up(end, 512)

    ev = _prep_even(ev_w_in[0], ev_g_q[0], ev_w_uq[0], ev_g_kv[0], ev_w_uk[0], ev_w_uv[0], ev_conv_w[0],
                    ev_conv_b[0], ev_w_rg[0], ev_b_rg[0], ev_w_ig[0], ev_b_ig[0], ev_lru_lambda[0],
                    ev_w_out[0])
    od = _prep_odd(od_mu[0], od_w_r[0], od_w_k[0], od_w_v[0], od_w0[0], od_w1[0], od_w2[0], od_a0[0],
                   od_a1[0], od_a2[0], od_g1[0], od_g2[0], od_k_k[0], od_k_a[0], od_r_k[0], od_ln_g[0],
                   od_ln_b[0], od_w_o[0])
    rw = router_w.astype(F32)
    rb = _row2(router_b)
    wg, wu, wd = (w.reshape((DEPTH * N_EXPERTS,) + w.shape[2:]) for w in (exp_w_gate, exp_w_up, exp_w_down))
    lng = ln_g.astype(F32)[:, :, None, :]
    lnb = ln_b.astype(F32)[:, :, None, :]

    def moe(x, layer):
        sparse = x.shape[0] % (SC_WORKERS * SUBLANES) == 0 and x.shape[0] >= SPARSE_MIN_ROWS
        fn = _moe_sparse if sparse else _moe
        return fn(x, rw, rb, wg, wu, wd, layer * N_EXPERTS, lng[layer, 1], lnb[layer, 1])

    xp = jnp.concatenate([jnp.zeros((PAD_FRONT, D_MODEL), F32), meta_tokens.astype(F32),
                          x_prompt[0].astype(F32), jnp.zeros((tp - end, D_MODEL), F32)], axis=0)
    tabs_p = _rope_tables(jnp.maximum(jnp.arange(tp) - PAD_FRONT, 0))
    evb = _single_pass(ev)
    q_p, ckv_p, kr_p, xr_p, yg_p = _even_proj(xp, evb["w1"], ev["gq"], evb["wuq"], ev["gkv"], tabs_p,
                                              MLA_SCALE * LOG2E)
    ones_col = np.zeros((D_ATT, 1), np.float32)
    ones_col[D_V::HEAD_SLOT] = 1.0
    k_p, vt_p = _kv_proj_t(ckv_p, kr_p, evb["wukv"][:, :D_ATT], evb["place"], evb["wukv"][:, D_ATT:].T,
                           jnp.asarray(ones_col), _round_up(tp, FLASH_TK))
    attn_p = _flash_attention(q_p, k_p, vt_p, FLASH_TQ, FLASH_TK)
    rnn_p, tailx_p, tailh_p = _rglru(
        xr_p[None], yg_p[None], ev["cw"], ev["cb"], evb["wrg"], ev["brg"], evb["wig"], ev["big"], ev["sp"],
        jnp.zeros((1, SUBLANES, D_RNN), F32), jnp.zeros((1, 1, D_RNN), F32), PAD_FRONT, end)
    x1_p = _mix_out(attn_p, rnn_p[0], xp, evb["wa"], evb["wr"], lng[0, 0], lnb[0, 0])
    x2_p = moe(x1_p, 0)

    xs = x_sample.reshape(ns, D_MODEL).astype(F32)
    pos_s = jnp.tile(N_META + past + jnp.arange(ls), bs)
    q_s, ckv_s, kr_s, xr_s, yg_s = _even_proj(xs, ev["w1"], ev["gq"], ev["wuq"], ev["gkv"], _rope_tables(pos_s),
                                              MLA_SCALE)
    _, ckv_m, kr_m, _, _ = _even_proj(meta_tokens.astype(F32), ev["w1"], ev["gq"], ev["wuq"], ev["gkv"],
                                      _rope_tables(jnp.arange(N_META)), MLA_SCALE)
    n_keys = N_META + past + ls
    nk_pad = _round_up(n_keys, LANES)
    meta_ckv = jnp.broadcast_to(ckv_m[None], (bs, N_META, D_C))
    meta_kr = jnp.broadcast_to(kr_m[None], (bs, N_META, LANES))
    cache_kr = jnp.pad(cache_krope[0].astype(F32), ((0, 0), (0, 0), (D_NOPE, LANES - D_NOPE - D_ROPE)))
    all_ckv = jnp.concatenate([meta_ckv, cache_ckv[0].astype(F32), ckv_s.reshape(bs, ls, D_C),
                               jnp.zeros((bs, nk_pad - n_keys, D_C), F32)], axis=1)
    all_kr = jnp.concatenate([meta_kr, cache_kr, kr_s.reshape(bs, ls, LANES),
                              jnp.zeros((bs, nk_pad - n_keys, LANES), F32)], axis=1)
    attn_s = _sample_attention(q_s.reshape(bs, ls, D_ATT), jnp.concatenate([all_ckv, all_kr], axis=2),
                               ev["wabs"], ev["wuv"], n_keys)
    buf0_s = jnp.pad(state_conv[0].astype(F32), ((0, 0), (SUBLANES - (CONV_W - 1), 0), (0, 0)))
    rnn_s, tailx_s, tailh_s = _rglru(
        xr_s.reshape(bs, ls, D_RNN), yg_s.reshape(bs, ls, D_RNN), ev["cw"], ev["cb"], ev["wrg"], ev["brg"],
        ev["wig"], ev["big"], ev["sp"], buf0_s, state_lru[0].astype(F32)[:, None, :], 0, ls)
    x1_s = _mix_out(attn_s.reshape(ns, D_ATT), rnn_s.reshape(ns, D_RNN), xs, ev["wa"], ev["wr"],
                    lng[0, 0], lnb[0, 0])
    x2_s = moe(x1_s, 0)

    r_p, lw_p, kk_in_p, v1_p, kkn_p, a_p, g_p = _rwkv_proj(
        x2_p[None], jnp.zeros((1, 1, D_MODEL), F32), od, PAD_FRONT, end, BF16)
    o_p, s_p = _wkv(r_p, lw_p, kk_in_p, v1_p, kkn_p, a_p,
                    jnp.zeros((1, RWKV_HEADS // 2, LANES, LANES), F32), CHUNK, False)
    x3_p = _rwkv_out(o_p[0], r_p[0], kk_in_p[0], v1_p[0], g_p[0], x2_p, od, lng[1, 0], lnb[1, 0])
    x4_p = moe(x3_p, 1)

    x2_s3 = x2_s.reshape(bs, ls, D_MODEL)
    r_s, lw_s, kk_in_s, v1_s, kkn_s, a_s, g_s = _rwkv_proj(
        x2_s3, state_shift[0].astype(F32)[:, None, :], od, 0, ls, F32)
    to_chunk = lambda z: jnp.pad(z, ((0, 0), (0, _round_up(ls, CHUNK) - ls), (0, 0)))
    o_s, s_s = _wkv(*(to_chunk(z) for z in (r_s, lw_s, kk_in_s, v1_s, kkn_s, a_s)),
                    _pair_states(state_wkv[0]), CHUNK, True)
    o_s = o_s[:, :ls]
    flat = lambda z: z.reshape(ns, D_MODEL)
    x3_s = _rwkv_out(flat(o_s), flat(r_s), flat(kk_in_s), flat(v1_s), flat(g_s), x2_s, od,
                     lng[1, 0], lnb[1, 0])
    x4_s = moe(x3_s, 1)

    dt = x_prompt.dtype
    nb = CONV_W - 1
    return (
        x4_p[ROW0:end][None].astype(dt),
        x4_s.reshape(bs, ls, D_MODEL).astype(dt),
        ckv_p[PAD_FRONT:end][None, None].astype(dt),
        kr_p[PAD_FRONT:end, D_NOPE:D_NOPE + D_ROPE][None, None].astype(dt),
        tailx_p[:, SUBLANES - nb:][None].astype(dt),
        tailh_p[:, SUBLANES - 1][None].astype(dt),
        x2_p[end - 1][None, None].astype(dt),
        _unpair_states(s_p)[None].astype(dt),
        ckv_s.reshape(bs, ls, D_C)[None].astype(dt),
        kr_s.reshape(bs, ls, LANES)[:, :, D_NOPE:D_NOPE + D_ROPE][None].astype(dt),
        tailx_s[:, SUBLANES - nb:][None].astype(dt),
        tailh_s[:, SUBLANES - 1][None].astype(dt),
        x2_s3[:, ls - 1][None].astype(dt),
        _unpair_states(s_s)[None].astype(dt),
    )
```
